```python
import math
import jax, jax.numpy as jnp
from jax import lax
import numpy as np

D_MODEL = 1024
BATCH = 8
SEQ = 2048
DEPTH = 1
DEC_BATCH = 32
DEC_SEQ = 4
PAST_LEN = 8192
PAGE_SIZE = 128

DIL_GROUPS = ((128, 1), (512, 4), (2048, 16))
DIL_HPG = 4
DIL_N_HEADS = DIL_HPG * len(DIL_GROUPS)
DIL_HEAD_DIM = 64
DIL_QKV_W = DIL_N_HEADS * DIL_HEAD_DIM
DIL_Q_BLOCK = 128
REL_BUCKETS = 32
REL_MAX_DIST = 2048
GDN_HEADS = 8
GDN_HEAD_DIM = 128
GDN_W = GDN_HEADS * GDN_HEAD_DIM
GDN_QKV_W = 3 * GDN_W
GDN_CONV = 4
GDN_CHUNK = 64
MEM_LEN = 256
MEM_HEADS = 4
MEM_HEAD_DIM = 128
MEM_W = MEM_HEADS * MEM_HEAD_DIM
FFN_DIM = 2816
FFN_CONV = 3
EPS = 1e-6
SPLIT_SIZES = (DIL_QKV_W, DIL_QKV_W, DIL_QKV_W, GDN_QKV_W, GDN_W, GDN_HEADS, GDN_HEADS, D_MODEL, D_MODEL)
SPLIT_IDX = tuple(int(i) for i in np.cumsum(SPLIT_SIZES)[:-1])
N_IN = sum(SPLIT_SIZES)

kernel_name = "hybrid_dilated_deltanet_decoder_step"


def rmsnorm(x, gain):
    xf = x.astype(jnp.float32)
    y = xf * lax.rsqrt(jnp.mean(xf * xf, axis=-1, keepdims=True) + EPS)
    return (y * gain.astype(jnp.float32)).astype(x.dtype)


def l2norm(x):
    return x * lax.rsqrt(jnp.sum(x * x, axis=-1, keepdims=True) + EPS)


def rel_bucket(dist):
    exact = REL_BUCKETS // 2
    d = jnp.maximum(dist, 1).astype(jnp.float32)
    large = exact + (jnp.log(d / exact) / math.log(REL_MAX_DIST / exact) * (REL_BUCKETS - exact)).astype(jnp.int32)
    return jnp.where(dist < exact, dist, jnp.minimum(large, REL_BUCKETS - 1))


def dilation_biases(rel_bias):
    biases = []
    for g, (window, dil) in enumerate(DIL_GROUPS):
        dist = dil * jnp.arange(window // dil + 1, dtype=jnp.int32)
        tab = rel_bias[rel_bucket(dist)]
        biases.append(tab[:, g * DIL_HPG:(g + 1) * DIL_HPG].T.astype(jnp.float32))
    return biases


def dilated_group_attend(q, kv, qpos, base, dil, bias):
    nk = bias.shape[-1]
    kpos = qpos[:, None] - dil * jnp.arange(nk, dtype=jnp.int32)[None, :]
    valid = kpos >= 0
    idx = jnp.clip(kpos - base, 0, kv.shape[1] - 1)
    kvg = jnp.take(kv, idx, axis=1)
    s = jnp.einsum('bqhd,bqjhd->bhqj', q, kvg[:, :, :, 0]).astype(jnp.float32) * (DIL_HEAD_DIM ** -0.5)
    s = jnp.where(valid[None, None], s + bias[None, :, None, :], -jnp.inf)
    lse = jax.nn.logsumexp(s, axis=-1)
    p = jnp.exp(s - lse[..., None]).astype(kv.dtype)
    return jnp.einsum('bhqj,bqjhd->bqhd', p, kvg[:, :, :, 1]), lse


def dilated_mixture(q, kvs, qpos, bases, biases):
    outs, lses = [], []
    for g, (_, dil) in enumerate(DIL_GROUPS):
        o, lse = dilated_group_attend(q[:, :, g], kvs[g], qpos, bases[g], dil, biases[g])
        outs.append(o)
        lses.append(lse)
    wts = jax.nn.softmax(jnp.stack(lses), axis=0)
    return jnp.einsum('gbhq,gbqhd->bqhd', wts.astype(q.dtype), jnp.stack(outs))


def dilated_prompt(q, k, v, biases):
    B, S = q.shape[:2]
    kvs = [jnp.stack([k[:, :, g], v[:, :, g]], axis=2) for g in range(len(DIL_GROUPS))]
    bases = (0,) * len(DIL_GROUPS)

    def block(start):
        qb = lax.dynamic_slice_in_dim(q, start, DIL_Q_BLOCK, axis=1)
        qpos = start + jnp.arange(DIL_Q_BLOCK, dtype=jnp.int32)
        return dilated_mixture(qb, kvs, qpos, bases, biases)

    starts = jnp.arange(S // DIL_Q_BLOCK, dtype=jnp.int32) * DIL_Q_BLOCK
    o = lax.map(block, starts)
    o = jnp.moveaxis(o, 0, 1).reshape(B, S, DIL_HPG, DIL_HEAD_DIM)
    new = [kv[:, S - min(w, S):] for kv, (w, _) in zip(kvs, DIL_GROUPS)]
    return o, new


def dilated_sample(q, k, v, caches, biases):
    T = q.shape[1]
    qpos = PAST_LEN + jnp.arange(T, dtype=jnp.int32)
    kvs, bases, new = [], [], []
    for g, buf in enumerate(caches):
        L = buf.shape[1]
        kv = jnp.concatenate([buf, jnp.stack([k[:, :, g], v[:, :, g]], axis=2)], axis=1)
        kvs.append(kv)
        bases.append(PAST_LEN - L)
        new.append(kv[:, kv.shape[1] - L:])
    return dilated_mixture(q, kvs, qpos, bases, biases), new


def causal_dwconv(x, buf, w):
    T, K = x.shape[1], w.shape[0]
    xp = jnp.concatenate([buf.astype(x.dtype), x], axis=1)
    y = xp[:, :T] * w[0]
    for j in range(1, K):
        y = y + xp[:, j:j + T] * w[j]
    return y, xp[:, T:]


def gated_delta_chunked(q, k, v, beta, g, s0):
    B, T, H, Dk = k.shape
    Dv = v.shape[-1]
    C = min(GDN_CHUNK, T)
    pad = (-T) % C
    if pad:
        pw = ((0, 0), (0, pad), (0, 0))
        q, k, v = (jnp.pad(a, pw + ((0, 0),)) for a in (q, k, v))
        beta, g = jnp.pad(beta, pw), jnp.pad(g, pw)
    N = (T + pad) // C

    def blk(a):
        return jnp.moveaxis(a.reshape((B, N, C) + a.shape[2:]), 3, 1)

    qc, kc, vc, bc, gc = (blk(a) for a in (q, k, v, beta, g))
    G = jnp.cumsum(gc, axis=-1)
    ii = jnp.arange(C)
    incl = ii[:, None] >= ii[None, :]
    strict = ii[:, None] > ii[None, :]
    gamma = jnp.exp(jnp.where(incl, G[..., :, None] - G[..., None, :], -jnp.inf))
    kk = jnp.einsum('bhnck,bhnek->bhnce', kc, kc)
    m = jnp.where(strict, bc[..., :, None] * kk * gamma, 0.0)
    rhs = jnp.concatenate([vc * bc[..., None], kc * (bc * jnp.exp(G))[..., None]], axis=-1)
    sol = lax.linalg.triangular_solve(m, rhs, left_side=True, lower=True, unit_diagonal=True)
    u0, w = sol[..., :Dv], sol[..., Dv:]
    a_intra = jnp.einsum('bhnck,bhnek->bhnce', qc, kc) * gamma
    q_dec = qc * jnp.exp(G)[..., None]
    k_dec = kc * jnp.exp(G[..., -1:] - G)[..., None]
    decay_tot = jnp.exp(G[..., -1])
    xs = tuple(jnp.moveaxis(a, 2, 0) for a in (u0, w, a_intra, q_dec, k_dec, decay_tot))

    def step(s, inp):
        u0_n, w_n, a_n, qd_n, kd_n, dt_n = inp
        v_new = u0_n - jnp.einsum('bhck,bhkv->bhcv', w_n, s)
        o_n = jnp.einsum('bhck,bhkv->bhcv', qd_n, s) + jnp.einsum('bhce,bhev->bhcv', a_n, v_new)
        s = s * dt_n[..., None, None] + jnp.einsum('bhck,bhcv->bhkv', kd_n, v_new)
        return s, o_n

    s_fin, o = lax.scan(step, s0, xs)
    o = jnp.transpose(o, (1, 0, 3, 2, 4)).reshape(B, N * C, H, Dv)[:, :T]
    return o, s_fin


def gdn_branch(qkv, z, beta_raw, a_raw, conv_buf, s0, w_conv, a_log, dt_bias, norm_out):
    B, T, _ = qkv.shape
    hs = (B, T, GDN_HEADS, GDN_HEAD_DIM)
    qkv_c, conv_new = causal_dwconv(qkv, conv_buf, w_conv)
    q, k, v = jnp.split(jax.nn.silu(qkv_c.astype(jnp.float32)), 3, axis=-1)
    q = l2norm(q.reshape(hs)) * (GDN_HEAD_DIM ** -0.5)
    k = l2norm(k.reshape(hs))
    beta = jax.nn.sigmoid(beta_raw.astype(jnp.float32))
    g = -jnp.exp(a_log.astype(jnp.float32)) * jax.nn.softplus(a_raw.astype(jnp.float32) + dt_bias.astype(jnp.float32))
    o, s_new = gated_delta_chunked(q, k, v.reshape(hs), beta, g, s0.astype(jnp.float32))
    o = rmsnorm(o, norm_out) * jax.nn.silu(z.astype(jnp.float32).reshape(hs))
    return o.reshape(B, T, GDN_W).astype(qkv.dtype), conv_new, s_new.astype(s0.dtype)


def token_mix(u, dilated_fn, conv_buf, delta_state, w_in, w_conv_delta, a_log, dt_bias, norm_delta_out,
              w_branch_a, w_branch_b, w_out):
    B, T, _ = u.shape
    qa, ka, va, qkv_b, z_b, beta_raw, a_raw, gate_a, gate_b = jnp.split(u @ w_in, SPLIT_IDX, axis=-1)
    ashape = (B, T, len(DIL_GROUPS), DIL_HPG, DIL_HEAD_DIM)
    o_a, dil_state = dilated_fn(qa.reshape(ashape), ka.reshape(ashape), va.reshape(ashape))
    o_b, conv_new, delta_new = gdn_branch(qkv_b, z_b, beta_raw, a_raw, conv_buf, delta_state, w_conv_delta,
                                          a_log, dt_bias, norm_delta_out)
    merged = (jax.nn.sigmoid(gate_a) * (o_a.reshape(B, T, DIL_HPG * DIL_HEAD_DIM) @ w_branch_a)
              + jax.nn.sigmoid(gate_b) * (o_b @ w_branch_b))
    return merged @ w_out, dil_state, conv_new, delta_new


def memory_kv(mem, norm_mem_kv, w_mem_kv):
    B, M, _ = mem.shape
    k, v = jnp.split(rmsnorm(mem, norm_mem_kv) @ w_mem_kv, 2, axis=-1)
    return k.reshape(B, M, MEM_HEADS, MEM_HEAD_DIM), v.reshape(B, M, MEM_HEADS, MEM_HEAD_DIM)


def memory_attend(u, mem_k, mem_v, w_mem_q, w_mem_o):
    B, T, _ = u.shape
    q = (u @ w_mem_q).reshape(B, T, MEM_HEADS, MEM_HEAD_DIM)
    s = jnp.einsum('bthd,bmhd->bhtm', q, mem_k.astype(q.dtype)).astype(jnp.float32) * (MEM_HEAD_DIM ** -0.5)
    p = jax.nn.softmax(s, axis=-1).astype(q.dtype)
    o = jnp.einsum('bhtm,bmhd->bthd', p, mem_v.astype(q.dtype)).reshape(B, T, MEM_W)
    return o @ w_mem_o


def conv_ffn(u, conv_buf, w_up, w_conv, b_conv, w_down):
    gate, up = jnp.split(u @ w_up, 2, axis=-1)
    gate_c, conv_new = causal_dwconv(gate, conv_buf, w_conv)
    return (jax.nn.silu(gate_c + b_conv) * up) @ w_down, conv_new


def decoder_layer(h, dilated_fn, delta_conv_buf, delta_state, mem_k, mem_v, ffn_conv_buf,
                  norm_mix, w_in, w_conv_delta, a_log, dt_bias, norm_delta_out, w_branch_a, w_branch_b, w_out,
                  norm_mem_q, w_mem_q, w_mem_o, norm_ffn, w_ffn_up, w_ffn_conv, b_ffn_conv, w_ffn_down):
    mix, dil_state, dconv_new, delta_new = token_mix(rmsnorm(h, norm_mix), dilated_fn, delta_conv_buf, delta_state,
                                                     w_in, w_conv_delta, a_log, dt_bias, norm_delta_out,
                                                     w_branch_a, w_branch_b, w_out)
    h = h + mix
    h = h + memory_attend(rmsnorm(h, norm_mem_q), mem_k, mem_v, w_mem_q, w_mem_o)
    f, fconv_new = conv_ffn(rmsnorm(h, norm_ffn), ffn_conv_buf, w_ffn_up, w_ffn_conv, b_ffn_conv, w_ffn_down)
    return h + f, dil_state, dconv_new, delta_new, fconv_new


def setup_inputs(seed: int = 0) -> dict:
    key = jax.random.key(seed)
    ks = iter(jax.random.split(key, 48))
    nrm = lambda shape, scale: scale * jax.random.normal(next(ks), shape, jnp.float32)
    gain = lambda shape: 1.0 + 0.02 * jax.random.normal(next(ks), shape, jnp.float32)
    L = DEPTH
    dl = [min(w, PAST_LEN) for w, _ in DIL_GROUPS]
    a_log = jnp.log(jax.random.uniform(next(ks), (L, GDN_HEADS), jnp.float32, 1.0, 16.0))
    dt = jnp.exp(jax.random.uniform(next(ks), (L, GDN_HEADS), jnp.float32, math.log(1e-3), math.log(1e-1)))
    dt_bias = dt + jnp.log(-jnp.expm1(-dt))
    return {
        'x_prompt': nrm((BATCH, SEQ, D_MODEL), 1.0),
        'x_sample': nrm((DEC_BATCH, DEC_SEQ, D_MODEL), 1.0),
        'cache_dil0_kv': nrm((L, DEC_BATCH, dl[0], 2, DIL_HPG, DIL_HEAD_DIM), 1.0),
        'cache_dil1_kv': nrm((L, DEC_BATCH, dl[1], 2, DIL_HPG, DIL_HEAD_DIM), 1.0),
        'cache_dil2_kv': nrm((L, DEC_BATCH, dl[2], 2, DIL_HPG, DIL_HEAD_DIM), 1.0),
        'state_delta': nrm((L, DEC_BATCH, GDN_HEADS, GDN_HEAD_DIM, GDN_HEAD_DIM), 0.1),
        'state_delta_conv': nrm((L, DEC_BATCH, GDN_CONV - 1, GDN_QKV_W), 1.0),
        'cache_mem_k': nrm((L, DEC_BATCH, MEM_LEN, MEM_HEADS, MEM_HEAD_DIM), 1.0),
        'cache_mem_v': nrm((L, DEC_BATCH, MEM_LEN, MEM_HEADS, MEM_HEAD_DIM), 1.0),
        'state_ffn_conv': nrm((L, DEC_BATCH, FFN_CONV - 1, FFN_DIM), 1.0),
        'mem_prompt': nrm((BATCH, MEM_LEN, D_MODEL), 1.0),
        'rel_bias': nrm((REL_BUCKETS, DIL_N_HEADS), 0.2),
        'norm_mix': gain((L, D_MODEL)),
        'w_in': nrm((L, D_MODEL, N_IN), D_MODEL ** -0.5),
        'w_conv_delta': nrm((L, GDN_CONV, GDN_QKV_W), GDN_CONV ** -0.5),
        'a_log': a_log,
        'dt_bias': dt_bias,
        'norm_delta_out': gain((L, GDN_HEAD_DIM)),
        'w_branch_a': nrm((L, DIL_HPG * DIL_HEAD_DIM, D_MODEL), (DIL_HPG * DIL_HEAD_DIM) ** -0.5),
        'w_branch_b': nrm((L, GDN_W, D_MODEL), GDN_W ** -0.5),
        'w_out': nrm((L, D_MODEL, D_MODEL), D_MODEL ** -0.5),
        'norm_mem_q': gain((L, D_MODEL)),
        'norm_mem_kv': gain((L, D_MODEL)),
        'w_mem_q': nrm((L, D_MODEL, MEM_W), D_MODEL ** -0.5),
        'w_mem_kv': nrm((L, D_MODEL, 2 * MEM_W), D_MODEL ** -0.5),
        'w_mem_o': nrm((L, MEM_W, D_MODEL), MEM_W ** -0.5),
        'norm_ffn': gain((L, D_MODEL)),
        'w_ffn_up': nrm((L, D_MODEL, 2 * FFN_DIM), D_MODEL ** -0.5),
        'w_ffn_conv': nrm((L, FFN_CONV, FFN_DIM), FFN_CONV ** -0.5),
        'b_ffn_conv': nrm((L, FFN_DIM), 0.02),
        'w_ffn_down': nrm((L, FFN_DIM, D_MODEL), FFN_DIM ** -0.5),
        'norm_final': gain((D_MODEL,)),
    }


def reference(x_prompt, x_sample, cache_dil0_kv, cache_dil1_kv, cache_dil2_kv, state_delta, state_delta_conv,
              cache_mem_k, cache_mem_v, state_ffn_conv, mem_prompt, rel_bias, norm_mix, w_in, w_conv_delta,
              a_log, dt_bias, norm_delta_out, w_branch_a, w_branch_b, w_out, norm_mem_q, norm_mem_kv, w_mem_q,
              w_mem_kv, w_mem_o, norm_ffn, w_ffn_up, w_ffn_conv, b_ffn_conv, w_ffn_down, norm_final):
    biases = dilation_biases(rel_bias)
    B, dt = x_prompt.shape[0], x_prompt.dtype
    zero_dconv = jnp.zeros((B, GDN_CONV - 1, GDN_QKV_W), dt)
    zero_delta = jnp.zeros((B, GDN_HEADS, GDN_HEAD_DIM, GDN_HEAD_DIM), dt)
    zero_fconv = jnp.zeros((B, FFN_CONV - 1, FFN_DIM), dt)
    h_p, h_s = x_prompt, x_sample
    p_states, s_states = [], []
    for l in range(DEPTH):
        lw = (norm_mix[l], w_in[l], w_conv_delta[l], a_log[l], dt_bias[l], norm_delta_out[l], w_branch_a[l],
              w_branch_b[l], w_out[l], norm_mem_q[l], w_mem_q[l], w_mem_o[l], norm_ffn[l], w_ffn_up[l],
              w_ffn_conv[l], b_ffn_conv[l], w_ffn_down[l])
        mk_p, mv_p = memory_kv(mem_prompt, norm_mem_kv[l], w_mem_kv[l])
        h_p, dil_p, dconv_p, delta_p, fconv_p = decoder_layer(
            h_p, lambda q, k, v: dilated_prompt(q, k, v, biases), zero_dconv, zero_delta, mk_p, mv_p, zero_fconv, *lw)
        p_states.append((dil_p[0], dil_p[1], dil_p[2], delta_p, dconv_p, mk_p, mv_p, fconv_p))
        caches_l = (cache_dil0_kv[l], cache_dil1_kv[l], cache_dil2_kv[l])
        h_s, dil_s, dconv_s, delta_s, fconv_s = decoder_layer(
            h_s, lambda q, k, v, c=caches_l: dilated_sample(q, k, v, c, biases), state_delta_conv[l],
            state_delta[l], cache_mem_k[l], cache_mem_v[l], state_ffn_conv[l], *lw)
        s_states.append((dil_s[0], dil_s[1], dil_s[2], delta_s, dconv_s, fconv_s))
    y_prompt = rmsnorm(h_p, norm_final)
    y_sample = rmsnorm(h_s, norm_final)
    (dil0_p, dil1_p, dil2_p, delta_p, dconv_p, mem_k_p, mem_v_p, fconv_p) = [jnp.stack(t, axis=0) for t in zip(*p_states)]
    (dil0_s, dil1_s, dil2_s, delta_s, dconv_s, fconv_s) = [jnp.stack(t, axis=0) for t in zip(*s_states)]
    return (y_prompt, y_sample, dil0_p, dil1_p, dil2_p, delta_p, dconv_p, mem_k_p, mem_v_p, fconv_p,
            dil0_s, dil1_s, dil2_s, delta_s, dconv_s, fconv_s)
```

```python
import functools
import math

import jax
import jax.numpy as jnp
import numpy as np
from jax import lax
from jax.experimental import pallas as pl
from jax.experimental.pallas import tpu as pltpu

F32 = jnp.float32
BF16 = jnp.bfloat16

PAST_LEN = 8192
DIL_GROUPS = ((128, 1), (512, 4), (2048, 16))
DIL_HPG = 4
DIL_HEAD_DIM = 64
DIL_GW = DIL_HPG * DIL_HEAD_DIM
DIL_NK = 129
REL_BUCKETS = 32
REL_MAX_DIST = 2048
GDN_HEADS = 8
GDN_HEAD_DIM = 128
GDN_W = GDN_HEADS * GDN_HEAD_DIM
GDN_CHUNK = 64
MEM_HEADS = 4
MEM_HEAD_DIM = 128
EPS = 1e-6
NEG = -1e30

LANES = 128
SUBLANES = 8
TILE_Q = 128
SAMPLE_ROWS = SUBLANES
VMEM_LIMIT = 56 * 1024 * 1024


def _cparams(sem):
    return pltpu.CompilerParams(dimension_semantics=sem, vmem_limit_bytes=VMEM_LIMIT)


def _resident(shape):
    nd = len(shape)
    return pl.BlockSpec(shape, lambda *_: (0,) * nd, pipeline_mode=pl.Buffered(1))


def _rms(x, gain_row):
    return x * lax.rsqrt(jnp.mean(x * x, axis=-1, keepdims=True) + EPS) * gain_row


def _dot(a, b):
    return jnp.dot(a.astype(BF16), b.astype(BF16), preferred_element_type=F32)


def _dot_nt(a, b):
    return lax.dot_general(a.astype(BF16), b.astype(BF16), (((1,), (1,)), ((), ())), preferred_element_type=F32)


def _dot_tn(a, b):
    return lax.dot_general(a.astype(BF16), b.astype(BF16), (((0,), (0,)), ((), ())), preferred_element_type=F32)


def _split3(x):
    hi = x.astype(BF16)
    r1 = x - hi.astype(F32)
    mid = r1.astype(BF16)
    lo = (r1 - mid.astype(F32)).astype(BF16)
    return hi, mid, lo


def _dot_hp(a, b):
    ah, am, al = _split3(a)
    bh, bm, bl = _split3(b)
    d = functools.partial(jnp.dot, preferred_element_type=F32)
    return d(ah, bh) + (d(ah, bm) + d(am, bh)) + (d(am, bm) + d(ah, bl) + d(al, bh))


def _sigmoid(x):
    return 1.0 / (1.0 + jnp.exp(-x))


def _silu(x):
    return x * _sigmoid(x)


def _softplus(x):
    return jnp.maximum(x, 0.0) + jnp.log(1.0 + jnp.exp(-jnp.abs(x)))


IN_SEGS = (("q", 3 * DIL_GW), ("kv0", 2 * DIL_GW), ("kv1", 2 * DIL_GW), ("kv2", 2 * DIL_GW),
           ("gq", 3 * GDN_W), ("z", GDN_W), ("ba", LANES), ("ga", 1024), ("gb", 1024))


def _arrange_w_in(w_in):
    o = 0
    qa = w_in[:, o:o + 768]; o += 768
    ka = w_in[:, o:o + 768]; o += 768
    va = w_in[:, o:o + 768]; o += 768
    gq = w_in[:, o:o + 3 * GDN_W]; o += 3 * GDN_W
    z = w_in[:, o:o + GDN_W]; o += GDN_W
    beta = w_in[:, o:o + GDN_HEADS]; o += GDN_HEADS
    a = w_in[:, o:o + GDN_HEADS]; o += GDN_HEADS
    ga = w_in[:, o:o + 1024]; o += 1024
    gb = w_in[:, o:o + 1024]
    kv = [jnp.concatenate([ka[:, g * DIL_GW:(g + 1) * DIL_GW], va[:, g * DIL_GW:(g + 1) * DIL_GW]], axis=1)
          for g in range(3)]
    ba = jnp.concatenate([beta, a, jnp.zeros((w_in.shape[0], LANES - 2 * GDN_HEADS), w_in.dtype)], axis=1)
    return jnp.concatenate([qa] + kv + [gq, z, ba, ga, gb], axis=1).astype(BF16)


def _in_proj_kernel(x_ref, g_ref, w_ref, q_ref, kv0_ref, kv1_ref, kv2_ref, gq_ref, z_ref, ba_ref, ga_ref, gb_ref):
    u = _rms(x_ref[...], g_ref[...]).astype(BF16)
    off = 0

    def seg(n):
        nonlocal off
        r = jnp.dot(u, w_ref[:, off:off + n], preferred_element_type=F32)
        off += n
        return r

    q_ref[...] = (seg(3 * DIL_GW) * (DIL_HEAD_DIM ** -0.5)).astype(q_ref.dtype)
    kv0_ref[...] = seg(2 * DIL_GW)
    kv1_ref[...] = seg(2 * DIL_GW)
    kv2_ref[...] = seg(2 * DIL_GW)
    for c in range(3):
        gq_ref[:, c * GDN_W:(c + 1) * GDN_W] = seg(GDN_W)
    z_ref[...] = seg(GDN_W)
    ba_ref[...] = seg(LANES)
    ga_ref[...] = seg(1024)
    gb_ref[...] = seg(1024)


def _in_proj(x2d, gain, w_arr, tm):
    rows, d = x2d.shape
    widths = [n for _, n in IN_SEGS]
    dtypes = [BF16] + [F32] * (len(widths) - 1)
    return pl.pallas_call(
        _in_proj_kernel,
        grid=(rows // tm,),
        in_specs=[pl.BlockSpec((tm, d), lambda i: (i, 0)), _resident((1, d)), _resident(w_arr.shape)],
        out_specs=[pl.BlockSpec((tm, n), lambda i: (i, 0)) for n in widths],
        out_shape=[jax.ShapeDtypeStruct((rows, n), dt) for n, dt in zip(widths, dtypes)],
        compiler_params=_cparams(("parallel",)),
        name="in_proj",
    )(x2d, gain.reshape(1, d), w_arr)


def _rel_bucket(dist):
    exact = REL_BUCKETS // 2
    d = jnp.maximum(dist, 1).astype(F32)
    large = exact + (jnp.log(d / exact) / math.log(REL_MAX_DIST / exact) * (REL_BUCKETS - exact)).astype(jnp.int32)
    return jnp.where(dist < exact, dist, jnp.minimum(large, REL_BUCKETS - 1))


def _group_bias(rel_bias, g):
    dil = DIL_GROUPS[g][1]
    dist = dil * jnp.arange(DIL_NK, dtype=jnp.int32)
    tab = rel_bias[_rel_bucket(dist)]
    return tab[:, g * DIL_HPG:(g + 1) * DIL_HPG].T.astype(F32)


def _prompt_bias_tables(rel_bias):
    i = jnp.arange(TILE_Q, dtype=jnp.int32)[:, None]
    c = jnp.arange(TILE_Q, dtype=jnp.int32)[None, :]
    cur_j, prev_j = i - c, i + TILE_Q - c
    cur, prev = [], []
    for g in range(3):
        bg = _group_bias(rel_bias, g)
        cur.append(jnp.where(cur_j >= 0, bg[:, jnp.clip(cur_j, 0, DIL_NK - 1)], NEG))
        prev.append(jnp.where(prev_j <= DIL_NK - 1, bg[:, jnp.clip(prev_j, 0, DIL_NK - 1)], NEG))
    return cur, prev


def _sample_bias_tables(rel_bias, t_real):
    m = jnp.arange(TILE_Q, dtype=jnp.int32)
    u = jnp.arange(SAMPLE_ROWS, dtype=jnp.int32)
    tabc, tabn = [], []
    for g in range(3):
        bg = _group_bias(rel_bias, g)
        rc, rn = [], []
        for t in range(t_real):
            if g == 0:
                jc = TILE_Q + t - m
                c_ok = m >= t
                jn = t - u
                n_ok = u <= t
            else:
                jc = TILE_Q - m
                c_ok = m >= 0
                jn = jnp.zeros_like(u)
                n_ok = u == t
            tc = jnp.where(c_ok[None, :], bg[:, jnp.clip(jc, 0, DIL_NK - 1)], NEG)
            tn = jnp.where(n_ok[None, :], bg[:, jnp.clip(jn, 0, DIL_NK - 1)], NEG)
            rc.append(jnp.concatenate([tc, jnp.zeros_like(tc)], axis=0))
            rn.append(jnp.concatenate([tn, jnp.zeros_like(tn)], axis=0))
        tabc.append(jnp.stack(rc))
        tabn.append(jnp.stack(rn))
    return jnp.stack(tabc), jnp.stack(tabn)


def _dil_prompt_kernel(*refs, has_prev):
    if has_prev:
        q_ref, kc_ref, vc_ref, kp_ref, vp_ref, tc_ref, tp_ref, o_ref, l_ref = refs
    else:
        q_ref, kc_ref, vc_ref, tc_ref, o_ref, l_ref = refs
    first = pl.program_id(2) == 0
    q = q_ref[...]
    kc = kc_ref[...].astype(BF16)
    vc = vc_ref[...].astype(BF16)
    if has_prev:
        kp = kp_ref[...].astype(BF16)
        vp = vp_ref[...].astype(BF16)
    outs, lses = [], []
    for h in range(DIL_HPG):
        sl = slice(h * DIL_HEAD_DIM, (h + 1) * DIL_HEAD_DIM)
        s_c = _dot_nt(q[:, sl], kc[:, sl]) + tc_ref[h]
        m = jnp.max(s_c, axis=-1, keepdims=True)
        if has_prev:
            s_p = jnp.where(first, NEG, _dot_nt(q[:, sl], kp[:, sl]) + tp_ref[h])
            m = jnp.maximum(m, jnp.max(s_p, axis=-1, keepdims=True))
        p_c = jnp.exp(s_c - m)
        l = jnp.sum(p_c, axis=-1, keepdims=True)
        acc = _dot(p_c, vc[:, sl])
        if has_prev:
            p_p = jnp.exp(s_p - m)
            l = l + jnp.sum(p_p, axis=-1, keepdims=True)
            acc = acc + _dot(p_p, vp[:, sl])
        outs.append(acc / l)
        lses.append(jnp.broadcast_to(m + jnp.log(l), (TILE_Q, DIL_HEAD_DIM)))
    o_ref[...] = jnp.concatenate(outs, axis=-1)
    l_ref[...] = jnp.concatenate(lses, axis=-1)


def _dil_prompt(q, kv, g, t_cur, t_prev):
    B, S, _ = q.shape
    dil = DIL_GROUPS[g][1]
    sub = S // dil
    nb = sub // TILE_Q
    has_prev = nb > 1
    qr = q.reshape(B, sub, dil * 3 * DIL_GW)
    kvr = kv.reshape(B, sub, dil * 2 * DIL_GW)
    blk = (None, TILE_Q, DIL_GW)
    prev = lambda t: jnp.maximum(t - 1, 0)
    in_specs = [pl.BlockSpec(blk, lambda b, r, t: (b, t, r * 3 + g)),
                pl.BlockSpec(blk, lambda b, r, t: (b, t, r * 2)),
                pl.BlockSpec(blk, lambda b, r, t: (b, t, r * 2 + 1))]
    args = [qr, kvr, kvr]
    if has_prev:
        in_specs += [pl.BlockSpec(blk, lambda b, r, t: (b, prev(t), r * 2)),
                     pl.BlockSpec(blk, lambda b, r, t: (b, prev(t), r * 2 + 1))]
        args += [kvr, kvr]
    in_specs.append(_resident(t_cur.shape))
    args.append(t_cur)
    if has_prev:
        in_specs.append(_resident(t_prev.shape))
        args.append(t_prev)
    out_spec = pl.BlockSpec(blk, lambda b, r, t: (b, t, r))
    o, lse = pl.pallas_call(
        functools.partial(_dil_prompt_kernel, has_prev=has_prev),
        grid=(B, dil, nb),
        in_specs=in_specs,
        out_specs=[out_spec, out_spec],
        out_shape=[jax.ShapeDtypeStruct((B, sub, dil * DIL_GW), F32)] * 2,
        compiler_params=_cparams(("parallel", "parallel", "arbitrary")),
        name=f"dil_prompt_g{g}",
    )(*args)
    return o.reshape(B * S, DIL_GW), lse.reshape(B * S, DIL_GW)


def _dil_sample_kernel(q_ref, n0_ref, n1_ref, n2_ref, c0_ref, c1_ref, c2_ref, tabc_ref, tabn_ref, o_ref, l_ref,
                       *, t_real):
    rows = lax.broadcasted_iota(jnp.int32, (SAMPLE_ROWS, DIL_GW), 0)
    lanes = lax.broadcasted_iota(jnp.int32, (SAMPLE_ROWS, DIL_GW), 1)
    head_mask = (lanes // DIL_HEAD_DIM) == rows
    new_refs = (n0_ref, n1_ref, n2_ref)
    cache_refs = (c0_ref, c1_ref, c2_ref)
    for g in range(3):
        kn = new_refs[g][:, :DIL_GW].astype(BF16)
        vn = new_refs[g][:, DIL_GW:].astype(BF16)
        o_rows, l_rows = [], []
        for t in range(t_real):
            col = 0 if g == 0 else t * 2 * DIL_GW
            kc = cache_refs[g][:, col:col + DIL_GW].astype(BF16)
            vc = cache_refs[g][:, col + DIL_GW:col + 2 * DIL_GW].astype(BF16)
            q_t = q_ref[t:t + 1, g * DIL_GW:(g + 1) * DIL_GW]
            q_bd = jnp.where(head_mask, jnp.broadcast_to(q_t, (SAMPLE_ROWS, DIL_GW)), 0.0)
            s_c = _dot_nt(q_bd, kc) + tabc_ref[g, t]
            s_n = _dot_nt(q_bd, kn) + tabn_ref[g, t]
            m = jnp.maximum(jnp.max(s_c, axis=-1, keepdims=True), jnp.max(s_n, axis=-1, keepdims=True))
            p_c = jnp.exp(s_c - m)
            p_n = jnp.exp(s_n - m)
            l = jnp.sum(p_c, axis=-1, keepdims=True) + jnp.sum(p_n, axis=-1, keepdims=True)
            acc = (_dot(p_c, vc) + _dot(p_n, vn)) / l
            o_rows.append(jnp.sum(jnp.where(head_mask, acc, 0.0), axis=0, keepdims=True))
            lse = jnp.broadcast_to(m + jnp.log(l), (SAMPLE_ROWS, DIL_GW))
            l_rows.append(jnp.sum(jnp.where(head_mask, lse, 0.0), axis=0, keepdims=True))
        pad = jnp.zeros((SAMPLE_ROWS - t_real, DIL_GW), F32)
        o_ref[:, g * DIL_GW:(g + 1) * DIL_GW] = jnp.concatenate(o_rows + [pad], axis=0)
        l_ref[:, g * DIL_GW:(g + 1) * DIL_GW] = jnp.concatenate(l_rows + [pad], axis=0)


def _dil_sample(q, kvn, caches, tabc, tabn, t_real):
    B = q.shape[0]
    row = lambda n: pl.BlockSpec((None, SAMPLE_ROWS, n), lambda b: (b, 0, 0))
    cr, cspecs = [], []
    for g, (w, dil) in enumerate(DIL_GROUPS):
        assert caches[g].shape[1] == w and w // dil == TILE_Q and t_real <= dil * SUBLANES
        cr.append(caches[g].reshape(B, TILE_Q, dil * 2 * DIL_GW))
        cols = 2 * DIL_GW if g == 0 else t_real * 2 * DIL_GW
        cspecs.append(pl.BlockSpec((None, TILE_Q, cols), lambda b: (b, 0, 0)))
    out_spec = pl.BlockSpec((None, SAMPLE_ROWS, 3 * DIL_GW), lambda b: (b, 0, 0))
    o, lse = pl.pallas_call(
        functools.partial(_dil_sample_kernel, t_real=t_real),
        grid=(B,),
        in_specs=[row(3 * DIL_GW)] + [row(2 * DIL_GW)] * 3 + cspecs + [_resident(tabc.shape), _resident(tabn.shape)],
        out_specs=[out_spec, out_spec],
        out_shape=[jax.ShapeDtypeStruct((B, SAMPLE_ROWS, 3 * DIL_GW), F32)] * 2,
        compiler_params=_cparams(("parallel",)),
        name="dil_sample",
    )(q, *kvn, *cr, tabc, tabn)
    o = o.reshape(B * SAMPLE_ROWS, 3 * DIL_GW)
    lse = lse.reshape(B * SAMPLE_ROWS, 3 * DIL_GW)
    return ([o[:, g * DIL_GW:(g + 1) * DIL_GW] for g in range(3)],
            [lse[:, g * DIL_GW:(g + 1) * DIL_GW] for g in range(3)])


def _gdn_kernel(q_ref, k_ref, v_ref, bq_ref, bk_ref, bv_ref, wq_ref, wk_ref, wv_ref, ba_ref, alog_ref, dtb_ref,
                z_ref, gn_ref, s0_ref, o_ref, s_ref, xq_s, xk_s, xv_s, *, t_real):
    T = q_ref.shape[0]
    h = pl.program_id(1)
    C = GDN_CHUNK
    PAD = SUBLANES
    for x_s, x_ref, b_ref in ((xq_s, q_ref, bq_ref), (xk_s, k_ref, bk_ref), (xv_s, v_ref, bv_ref)):
        x_s[PAD - 3:PAD, :] = b_ref[...]
        x_s[PAD:PAD + T, :] = x_ref[...]

    ri = lax.broadcasted_iota(jnp.int32, (TILE_Q, TILE_Q), 0)
    ci = lax.broadcasted_iota(jnp.int32, (TILE_Q, TILE_Q), 1)
    same = (ri // C) == (ci // C)
    incl = same & (ri >= ci)
    strict = same & (ri > ci)
    eye = (ri == ci).astype(F32)
    tri = incl.astype(F32)
    lane = lax.broadcasted_iota(jnp.int32, (TILE_Q, LANES), 1)
    row = lax.broadcasted_iota(jnp.int32, (TILE_Q, LANES), 0)
    gain = gn_ref[...]

    def conv(x_s, w_ref, r0):
        y = x_s[pl.ds(r0 + PAD - 3, TILE_Q), :] * w_ref[0:1, :]
        for j in range(1, 4):
            y = y + x_s[pl.ds(r0 + PAD - 3 + j, TILE_Q), :] * w_ref[j:j + 1, :]
        return _silu(y)

    def l2n(x):
        return x * lax.rsqrt(jnp.sum(x * x, axis=-1, keepdims=True) + EPS)

    def tile(i, S):
        r0 = pl.multiple_of(i * TILE_Q, TILE_Q)
        q = l2n(conv(xq_s, wq_ref, r0)) * (GDN_HEAD_DIM ** -0.5)
        k = l2n(conv(xk_s, wk_ref, r0))
        v = conv(xv_s, wv_ref, r0)
        ba = ba_ref[pl.ds(r0, TILE_Q), :]
        live = (row + r0) < t_real
        beta_all = _sigmoid(ba)
        g_all = -jnp.exp(alog_ref[...]) * _softplus(ba + dtb_ref[...])
        bc = jnp.sum(jnp.where((lane == h) & live, beta_all, 0.0), axis=-1, keepdims=True)
        gc = jnp.sum(jnp.where((lane == h + GDN_HEADS) & live, g_all, 0.0), axis=-1, keepdims=True)
        Gc = _dot_hp(tri, jnp.broadcast_to(gc, (TILE_Q, LANES)))
        Gr = Gc.T
        gamma = jnp.exp(jnp.where(incl, Gc - Gr, NEG))
        kk = _dot_nt(k, k)
        qk = _dot_nt(q, k)
        Mm = jnp.where(strict, bc * kk * gamma, 0.0)
        P = eye - Mm
        X = Mm
        for _ in range(int(math.log2(C)) - 1):
            X = _dot_hp(X, X)
            P = P + _dot_hp(P, X)
        eG = jnp.exp(Gc)
        rhs = jnp.concatenate([v * bc, k * (bc * eG)], axis=-1)
        sol = _dot_hp(P, rhs)
        u0, w = sol[:, :GDN_HEAD_DIM], sol[:, GDN_HEAD_DIM:]
        a_in = qk * gamma
        q_dec = q * eG
        G_last = jnp.where(row < C, Gc[C - 1:C, :], Gc[2 * C - 1:2 * C, :])
        k_dec = k * jnp.exp(G_last - Gc)
        o_parts = []
        for c in range(TILE_Q // C):
            sl = slice(c * C, (c + 1) * C)
            v_new = u0[sl] - _dot(w[sl], S)
            o_parts.append(_dot(q_dec[sl], S) + _dot(a_in[sl, sl], v_new))
            dt = jnp.exp(Gc[(c + 1) * C - 1:(c + 1) * C, :])
            S = S * dt + _dot_tn(k_dec[sl], v_new)
        o = jnp.concatenate(o_parts, axis=0)
        o = _rms(o, gain) * _silu(z_ref[pl.ds(r0, TILE_Q), :])
        o_ref[pl.ds(r0, TILE_Q), :] = o.astype(o_ref.dtype)
        return S

    s_ref[...] = lax.fori_loop(0, T // TILE_Q, tile, s0_ref[...])


def _gdn(gq, ba, z, conv_buf, s0, w_conv, a_log, dt_bias, norm_out, t_real):
    B, T, _ = gq.shape
    H = GDN_HEADS
    pad16 = lambda x: jnp.concatenate([jnp.zeros((H,), F32), x.astype(F32), jnp.zeros((LANES - 2 * H,), F32)])
    col = lambda off: pl.BlockSpec((None, T, LANES), lambda b, h: (b, 0, off + h))
    bcol = lambda off: pl.BlockSpec((None, 3, LANES), lambda b, h: (b, 0, off + h))
    wcol = lambda off: pl.BlockSpec((4, LANES), lambda b, h: (0, off + h))
    o, s_new = pl.pallas_call(
        functools.partial(_gdn_kernel, t_real=t_real),
        grid=(B, H),
        in_specs=[col(0), col(H), col(2 * H), bcol(0), bcol(H), bcol(2 * H), wcol(0), wcol(H), wcol(2 * H),
                  pl.BlockSpec((None, T, LANES), lambda b, h: (b, 0, 0)),
                  _resident((1, LANES)), _resident((1, LANES)),
                  col(0), _resident((1, LANES)),
                  pl.BlockSpec((None, None, GDN_HEAD_DIM, GDN_HEAD_DIM), lambda b, h: (b, h, 0, 0))],
        out_specs=[col(0), pl.BlockSpec((None, None, GDN_HEAD_DIM, GDN_HEAD_DIM), lambda b, h: (b, h, 0, 0))],
        out_shape=[jax.ShapeDtypeStruct((B, T, GDN_W), BF16),
                   jax.ShapeDtypeStruct((B, H, GDN_HEAD_DIM, GDN_HEAD_DIM), F32)],
        scratch_shapes=[pltpu.VMEM((T + SUBLANES, LANES), F32)] * 3,
        compiler_params=_cparams(("parallel", "parallel")),
        name="gdn",
    )(gq, gq, gq, conv_buf, conv_buf, conv_buf, w_conv, w_conv, w_conv, ba,
      pad16(a_log).reshape(1, LANES), pad16(dt_bias).reshape(1, LANES), z, norm_out.reshape(1, LANES), s0)
    return o, s_new


def _mix_kernel(o0_ref, o1_ref, o2_ref, l0_ref, l1_ref, l2_ref, ob_ref, ga_ref, gb_ref, h_ref,
                wa_ref, wb_ref, wo_ref, gq_ref, wq_ref, h1_ref, qm_ref):
    l0, l1, l2 = l0_ref[...], l1_ref[...], l2_ref[...]
    mx = jnp.maximum(jnp.maximum(l0, l1), l2)
    e0, e1, e2 = jnp.exp(l0 - mx), jnp.exp(l1 - mx), jnp.exp(l2 - mx)
    o_a = (e0 * o0_ref[...] + e1 * o1_ref[...] + e2 * o2_ref[...]) / (e0 + e1 + e2)
    a = _dot(o_a, wa_ref[...])
    b = jnp.dot(ob_ref[...], wb_ref[...], preferred_element_type=F32)
    merged = _sigmoid(ga_ref[...]) * a + _sigmoid(gb_ref[...]) * b
    h1 = h_ref[...] + _dot(merged, wo_ref[...])
    h1_ref[...] = h1
    qm_ref[...] = _dot(_rms(h1, gq_ref[...]), wq_ref[...]).astype(qm_ref.dtype)


def _mix(o_g, l_g, o_b, ga, gb, h, w_a, w_b, w_o, norm_mem_q, w_mem_q, tm):
    rows, d = h.shape
    rt = lambda n: pl.BlockSpec((tm, n), lambda i: (i, 0))
    return pl.pallas_call(
        _mix_kernel,
        grid=(rows // tm,),
        in_specs=[rt(DIL_GW)] * 6 + [rt(GDN_W), rt(d), rt(d), rt(d),
                                     _resident(w_a.shape), _resident(w_b.shape), _resident(w_o.shape),
                                     _resident((1, d)), _resident(w_mem_q.shape)],
        out_specs=[rt(d), rt(w_mem_q.shape[1])],
        out_shape=[jax.ShapeDtypeStruct((rows, d), F32), jax.ShapeDtypeStruct((rows, w_mem_q.shape[1]), BF16)],
        compiler_params=_cparams(("parallel",)),
        name="mix",
    )(*o_g, *l_g, o_b, ga, gb, h, w_a, w_b, w_o, norm_mem_q.reshape(1, d), w_mem_q)


def _mem_kv_kernel(x_ref, g_ref, w_ref, k_ref, v_ref):
    u = _rms(x_ref[...], g_ref[...]).astype(BF16)
    n = k_ref.shape[-1]
    k_ref[...] = jnp.dot(u, w_ref[:, :n], preferred_element_type=F32)
    v_ref[...] = jnp.dot(u, w_ref[:, n:], preferred_element_type=F32)


def _mem_kv(mem2d, gain, w, tm):
    rows, d = mem2d.shape
    n = w.shape[1] // 2
    return pl.pallas_call(
        _mem_kv_kernel,
        grid=(rows // tm,),
        in_specs=[pl.BlockSpec((tm, d), lambda i: (i, 0)), _resident((1, d)), _resident(w.shape)],
        out_specs=[pl.BlockSpec((tm, n), lambda i: (i, 0))] * 2,
        out_shape=[jax.ShapeDtypeStruct((rows, n), F32)] * 2,
        compiler_params=_cparams(("parallel",)),
        name="mem_kv",
    )(mem2d, gain.reshape(1, d), w)


def _mem_attn_kernel(q_ref, k_ref, v_ref, o_ref):
    q = q_ref[...]
    k = k_ref[...].astype(BF16)
    v = v_ref[...].astype(BF16)
    outs = []
    for h in range(MEM_HEADS):
        sl = slice(h * MEM_HEAD_DIM, (h + 1) * MEM_HEAD_DIM)
        s = _dot_nt(q[:, sl], k[:, sl]) * (MEM_HEAD_DIM ** -0.5)
        p = jnp.exp(s - jnp.max(s, axis=-1, keepdims=True))
        outs.append(_dot(p, v[:, sl]) / jnp.sum(p, axis=-1, keepdims=True))
    o_ref[...] = jnp.concatenate(outs, axis=-1).astype(o_ref.dtype)


def _mem_attn(qm, mem_k, mem_v, tm):
    B, T, w = qm.shape
    M = mem_k.shape[1]
    return pl.pallas_call(
        _mem_attn_kernel,
        grid=(B, T // tm),
        in_specs=[pl.BlockSpec((None, tm, w), lambda b, j: (b, j, 0)),
                  pl.BlockSpec((None, M, w), lambda b, j: (b, 0, 0)),
                  pl.BlockSpec((None, M, w), lambda b, j: (b, 0, 0))],
        out_specs=pl.BlockSpec((None, tm, w), lambda b, j: (b, j, 0)),
        out_shape=jax.ShapeDtypeStruct((B, T, w), F32),
        compiler_params=_cparams(("parallel", "parallel")),
        name="mem_attn",
    )(qm, mem_k, mem_v)


def _ffn_kernel(*refs, inject, emit_gate):
    if inject:
        (h1_ref, om_ref, init_ref, fill_ref, wmo_ref, gf_ref, wup_ref, wc_ref, bc_ref, wd_ref, gfin_ref,
         y_ref, fc_ref, gs) = refs
    else:
        (h1_ref, om_ref, init_ref, wmo_ref, gf_ref, wup_ref, wc_ref, bc_ref, wd_ref, gfin_ref,
         y_ref, fc_ref, gs) = refs
    tm = h1_ref.shape[0]
    F = wd_ref.shape[0]
    PAD = SUBLANES

    @pl.when(pl.program_id(1) == 0)
    def _():
        gs[PAD - 2:PAD, :] = init_ref[...]

    h2 = h1_ref[...] + _dot(om_ref[...], wmo_ref[...])
    n = _rms(h2, gf_ref[...]).astype(BF16)
    gate = jnp.dot(n, wup_ref[:, :F], preferred_element_type=F32)
    if inject:
        r = lax.broadcasted_iota(jnp.int32, (tm, 1), 0)
        gate = jnp.where((r % SAMPLE_ROWS) >= SAMPLE_ROWS - 2, fill_ref[...], gate)
    gs[PAD:PAD + tm, :] = gate
    conv = (gs[PAD - 2:PAD - 2 + tm, :] * wc_ref[0:1, :] + gs[PAD - 1:PAD - 1 + tm, :] * wc_ref[1:2, :]
            + gate * wc_ref[2:3, :])
    last2 = gs[PAD + tm - 2:PAD + tm, :]
    gs[PAD - 2:PAD, :] = last2
    if emit_gate:
        fc_ref[...] = gate
    else:
        fc_ref[...] = last2
    up = jnp.dot(n, wup_ref[:, F:], preferred_element_type=F32)
    act = _silu(conv + bc_ref[...]) * up
    y = h2 + _dot(act, wd_ref[...])
    y_ref[...] = _rms(y, gfin_ref[...])


def _ffn(h1, om, init, fill, w_mo, norm_ffn, w_up, w_conv, b_conv, w_down, norm_final, tm, emit_gate):
    B, T, d = h1.shape
    F = w_down.shape[0]
    inject = fill is not None
    rt = lambda n: pl.BlockSpec((None, tm, n), lambda b, j: (b, j, 0))
    in_specs = [rt(d), rt(om.shape[-1]), pl.BlockSpec((None, 2, F), lambda b, j: (b, 0, 0))]
    args = [h1, om, init]
    if inject:
        in_specs.append(rt(F))
        args.append(fill)
    in_specs += [_resident(w_mo.shape), _resident((1, d)), _resident(w_up.shape), _resident(w_conv.shape),
                 _resident((1, F)), _resident(w_down.shape), _resident((1, d))]
    args += [w_mo, norm_ffn.reshape(1, d), w_up, w_conv, b_conv.reshape(1, F), w_down, norm_final.reshape(1, d)]
    if emit_gate:
        fc_spec, fc_shape = rt(F), jax.ShapeDtypeStruct((B, T, F), F32)
    else:
        fc_spec = pl.BlockSpec((None, 2, F), lambda b, j: (b, 0, 0))
        fc_shape = jax.ShapeDtypeStruct((B, 2, F), F32)
    return pl.pallas_call(
        functools.partial(_ffn_kernel, inject=inject, emit_gate=emit_gate),
        grid=(B, T // tm),
        in_specs=in_specs,
        out_specs=[rt(d), fc_spec],
        out_shape=[jax.ShapeDtypeStruct((B, T, d), F32), fc_shape],
        scratch_shapes=[pltpu.VMEM((tm + SUBLANES, F), F32)],
        compiler_params=_cparams(("parallel", "arbitrary")),
        name="ffn",
    )(*args)


def kernel(x_prompt, x_sample, cache_dil0_kv, cache_dil1_kv, cache_dil2_kv, state_delta, state_delta_conv, cache_mem_k, cache_mem_v, state_ffn_conv, mem_prompt, rel_bias, norm_mix, w_in, w_conv_delta, a_log, dt_bias, norm_delta_out, w_branch_a, w_branch_b, w_out, norm_mem_q, norm_mem_kv, w_mem_q, w_mem_kv, w_mem_o, norm_ffn, w_ffn_up, w_ffn_conv, b_ffn_conv, w_ffn_down, norm_final):
    B, S, D = x_prompt.shape
    Bs, Ts, _ = x_sample.shape
    depth = w_in.shape[0]
    assert depth == 1 and Ts <= SAMPLE_ROWS - 2 and Ts >= 3 and S % (16 * TILE_Q) == 0
    F = w_ffn_down.shape[1]
    M = mem_prompt.shape[1]
    l = 0
    w_arr = _arrange_w_in(w_in[l])
    w_a, w_b, w_o = (w.astype(BF16) for w in (w_branch_a[l], w_branch_b[l], w_out[l]))
    w_mq, w_mkv, w_mo = (w.astype(BF16) for w in (w_mem_q[l], w_mem_kv[l], w_mem_o[l]))
    w_up, w_dn = w_ffn_up[l].astype(BF16), w_ffn_down[l].astype(BF16)
    t_cur, t_prev = _prompt_bias_tables(rel_bias)

    xp = x_prompt.reshape(B * S, D)
    q, kv0, kv1, kv2, gq, z, ba, ga, gb = _in_proj(xp, norm_mix[l], w_arr, 256)
    kvs = [kv.reshape(B, S, 2 * DIL_GW) for kv in (kv0, kv1, kv2)]
    q3 = q.reshape(B, S, 3 * DIL_GW)
    o_g, l_g = zip(*[_dil_prompt(q3, kvs[g], g, t_cur[g], t_prev[g]) for g in range(3)])
    o_b, delta_p = _gdn(gq.reshape(B, S, -1), ba.reshape(B, S, LANES), z.reshape(B, S, GDN_W),
                        jnp.zeros((B, 3, 3 * GDN_W), F32), jnp.zeros((B, GDN_HEADS, GDN_HEAD_DIM, GDN_HEAD_DIM), F32),
                        w_conv_delta[l], a_log[l], dt_bias[l], norm_delta_out[l], S)
    h1, qm = _mix(o_g, l_g, o_b.reshape(B * S, GDN_W), ga, gb, xp, w_a, w_b, w_o, norm_mem_q[l], w_mq, 256)
    mk_p, mv_p = _mem_kv(mem_prompt.reshape(B * M, D), norm_mem_kv[l], w_mkv, 256)
    mk_p, mv_p = mk_p.reshape(B, M, -1), mv_p.reshape(B, M, -1)
    om = _mem_attn(qm.reshape(B, S, -1), mk_p, mv_p, 512)
    y_p, fconv_p = _ffn(h1.reshape(B, S, D), om, jnp.zeros((B, 2, F), F32), None, w_mo, norm_ffn[l], w_up,
                        w_ffn_conv[l], b_ffn_conv[l], w_dn, norm_final, 256, False)
    gq3 = gq.reshape(B, S, -1)
    p_out = ([kvs[g][:, S - min(w, S):].reshape(1, B, min(w, S), 2, DIL_HPG, DIL_HEAD_DIM)
              for g, (w, _) in enumerate(DIL_GROUPS)]
             + [delta_p[None], gq3[:, S - 3:][None], mk_p.reshape(1, B, M, MEM_HEADS, MEM_HEAD_DIM),
                mv_p.reshape(1, B, M, MEM_HEADS, MEM_HEAD_DIM), fconv_p[None]])

    R = SAMPLE_ROWS
    xs = jnp.pad(x_sample, ((0, 0), (0, R - Ts), (0, 0))).reshape(Bs * R, D)
    q, kv0, kv1, kv2, gq, z, ba, ga, gb = _in_proj(xs, norm_mix[l], w_arr, Bs * R)
    kvn = [kv.reshape(Bs, R, 2 * DIL_GW) for kv in (kv0, kv1, kv2)]
    caches = [c[l].reshape(Bs, c.shape[2], 2 * DIL_GW) for c in (cache_dil0_kv, cache_dil1_kv, cache_dil2_kv)]
    tabc, tabn = _sample_bias_tables(rel_bias, Ts)
    o_g, l_g = _dil_sample(q.astype(F32).reshape(Bs, R, 3 * DIL_GW), kvn, caches, tabc, tabn, Ts)
    padt = lambda a: jnp.pad(a.reshape(Bs, R, -1), ((0, 0), (0, TILE_Q - R), (0, 0)))
    o_b, delta_s = _gdn(padt(gq), padt(ba), padt(z), state_delta_conv[l], state_delta[l],
                        w_conv_delta[l], a_log[l], dt_bias[l], norm_delta_out[l], Ts)
    o_b = o_b[:, :R].reshape(Bs * R, GDN_W)
    h1, qm = _mix(o_g, l_g, o_b, ga, gb, xs, w_a, w_b, w_o, norm_mem_q[l], w_mq, Bs * R)
    om = _mem_attn(qm.reshape(Bs, R, -1), cache_mem_k[l].reshape(Bs, M, -1), cache_mem_v[l].reshape(Bs, M, -1), R)
    fst = state_ffn_conv[l]
    fill = jnp.concatenate([jnp.zeros((Bs, R - 2, F), F32),
                            jnp.concatenate([fst[1:], jnp.zeros((1, 2, F), F32)], axis=0)], axis=1)
    y_s, gate_s = _ffn(h1.reshape(1, Bs * R, D), om.reshape(1, Bs * R, -1), fst[:1], fill.reshape(1, Bs * R, F),
                       w_mo, norm_ffn[l], w_up, w_ffn_conv[l], b_ffn_conv[l], w_dn, norm_final, Bs * R, True)
    y_s = y_s.reshape(Bs, R, D)[:, :Ts]
    gq3 = gq.reshape(Bs, R, -1)
    s_out = ([jnp.concatenate([caches[g][:, Ts:], kvn[g][:, :Ts]], axis=1)
              .reshape(1, Bs, caches[g].shape[1], 2, DIL_HPG, DIL_HEAD_DIM) for g in range(3)]
             + [delta_s[None], gq3[:, Ts - 3:Ts][None], gate_s.reshape(Bs, R, F)[:, Ts - 2:Ts][None]])

    return (y_p.reshape(B, S, D), y_s, *p_out, *s_out)
```

```python
import functools
import math

import jax
import jax.numpy as jnp
import numpy as np
from jax import lax
from jax.experimental import pallas as pl
from jax.experimental.pallas import tpu as pltpu

F32 = jnp.float32
BF16 = jnp.bfloat16

PAST_LEN = 8192
DIL_GROUPS = ((128, 1), (512, 4), (2048, 16))
DIL_HPG = 4
DIL_HEAD_DIM = 64
DIL_GW = DIL_HPG * DIL_HEAD_DIM
DIL_NK = 129
REL_BUCKETS = 32
REL_MAX_DIST = 2048
GDN_HEADS = 8
GDN_HEAD_DIM = 128
GDN_W = GDN_HEADS * GDN_HEAD_DIM
GDN_CHUNK = 64
MEM_HEADS = 4
MEM_HEAD_DIM = 128
EPS = 1e-6
NEG = -1e30

LANES = 128
SUBLANES = 8
TILE_Q = 128
SAMPLE_ROWS = SUBLANES
VMEM_LIMIT = 56 * 1024 * 1024


def _cparams(sem):
    return pltpu.CompilerParams(dimension_semantics=sem, vmem_limit_bytes=VMEM_LIMIT)


def _resident(shape):
    nd = len(shape)
    return pl.BlockSpec(shape, lambda *_: (0,) * nd, pipeline_mode=pl.Buffered(1))


def _rms(x, gain_row):
    return x * lax.rsqrt(jnp.mean(x * x, axis=-1, keepdims=True) + EPS) * gain_row


def _dot(a, b):
    return jnp.dot(a.astype(BF16), b.astype(BF16), preferred_element_type=F32)


def _dot_nt(a, b):
    return lax.dot_general(a.astype(BF16), b.astype(BF16), (((1,), (1,)), ((), ())), preferred_element_type=F32)


def _dot_tn(a, b):
    return lax.dot_general(a.astype(BF16), b.astype(BF16), (((0,), (0,)), ((), ())), preferred_element_type=F32)


def _split3(x):
    hi = x.astype(BF16)
    r1 = x - hi.astype(F32)
    mid = r1.astype(BF16)
    lo = (r1 - mid.astype(F32)).astype(BF16)
    return hi, mid, lo


def _dot_hp(a, b):
    ah, am, al = _split3(a)
    bh, bm, bl = _split3(b)
    d = functools.partial(jnp.dot, preferred_element_type=F32)
    return d(ah, bh) + (d(ah, bm) + d(am, bh)) + (d(am, bm) + d(ah, bl) + d(al, bh))


def _sigmoid(x):
    return 1.0 / (1.0 + jnp.exp(-x))


def _silu(x):
    return x * _sigmoid(x)


def _softplus(x):
    return jnp.maximum(x, 0.0) + jnp.log(1.0 + jnp.exp(-jnp.abs(x)))


IN_SEGS = (("q", 3 * DIL_GW), ("kv0", 2 * DIL_GW), ("kv1", 2 * DIL_GW), ("kv2", 2 * DIL_GW),
           ("gq", 3 * GDN_W), ("z", GDN_W), ("ba", LANES), ("ga", 1024), ("gb", 1024))


def _arrange_w_in(w_in):
    o = 0
    qa = w_in[:, o:o + 768]; o += 768
    ka = w_in[:, o:o + 768]; o += 768
    va = w_in[:, o:o + 768]; o += 768
    gq = w_in[:, o:o + 3 * GDN_W]; o += 3 * GDN_W
    z = w_in[:, o:o + GDN_W]; o += GDN_W
    beta = w_in[:, o:o + GDN_HEADS]; o += GDN_HEADS
    a = w_in[:, o:o + GDN_HEADS]; o += GDN_HEADS
    ga = w_in[:, o:o + 1024]; o += 1024
    gb = w_in[:, o:o + 1024]
    kv = [jnp.concatenate([ka[:, g * DIL_GW:(g + 1) * DIL_GW], va[:, g * DIL_GW:(g + 1) * DIL_GW]], axis=1)
          for g in range(3)]
    ba = jnp.concatenate([beta, a, jnp.zeros((w_in.shape[0], LANES - 2 * GDN_HEADS), w_in.dtype)], axis=1)
    return jnp.concatenate([qa] + kv + [gq, z, ba, ga, gb], axis=1).astype(BF16)


def _in_proj_kernel(x_ref, g_ref, w_ref, q_ref, kv0_ref, kv1_ref, kv2_ref, gq_ref, z_ref, ba_ref, ga_ref, gb_ref):
    u = _rms(x_ref[...], g_ref[...]).astype(BF16)
    off = 0

    def seg(n):
        nonlocal off
        r = jnp.dot(u, w_ref[:, off:off + n], preferred_element_type=F32)
        off += n
        return r

    q_ref[...] = (seg(3 * DIL_GW) * (DIL_HEAD_DIM ** -0.5)).astype(q_ref.dtype)
    kv0_ref[...] = seg(2 * DIL_GW)
    kv1_ref[...] = seg(2 * DIL_GW)
    kv2_ref[...] = seg(2 * DIL_GW)
    for c in range(3):
        gq_ref[:, c * GDN_W:(c + 1) * GDN_W] = seg(GDN_W)
    z_ref[...] = seg(GDN_W)
    ba_ref[...] = seg(LANES)
    ga_ref[...] = seg(1024)
    gb_ref[...] = seg(1024)


def _in_proj(x2d, gain, w_arr, tm):
    rows, d = x2d.shape
    widths = [n for _, n in IN_SEGS]
    dtypes = [BF16] + [F32] * (len(widths) - 1)
    return pl.pallas_call(
        _in_proj_kernel,
        grid=(rows // tm,),
        in_specs=[pl.BlockSpec((tm, d), lambda i: (i, 0)), _resident((1, d)), _resident(w_arr.shape)],
        out_specs=[pl.BlockSpec((tm, n), lambda i: (i, 0)) for n in widths],
        out_shape=[jax.ShapeDtypeStruct((rows, n), dt) for n, dt in zip(widths, dtypes)],
        compiler_params=_cparams(("parallel",)),
        name="in_proj",
    )(x2d, gain.reshape(1, d), w_arr)


def _rel_bucket(dist):
    exact = REL_BUCKETS // 2
    d = jnp.maximum(dist, 1).astype(F32)
    large = exact + (jnp.log(d / exact) / math.log(REL_MAX_DIST / exact) * (REL_BUCKETS - exact)).astype(jnp.int32)
    return jnp.where(dist < exact, dist, jnp.minimum(large, REL_BUCKETS - 1))


def _group_bias(rel_bias, g):
    dil = DIL_GROUPS[g][1]
    dist = dil * jnp.arange(DIL_NK, dtype=jnp.int32)
    tab = rel_bias[_rel_bucket(dist)]
    return tab[:, g * DIL_HPG:(g + 1) * DIL_HPG].T.astype(F32)


def _toeplitz(v, n, width):
    h, L = v.shape
    return jnp.tile(v, (1, n))[:, :n * (L - 1)].reshape(h, n, L - 1)[:, :, :width]


def _prompt_bias_tables(rel_bias):
    cur, prev = [], []
    for g in range(3):
        bg = _group_bias(rel_bias, g)
        v = jnp.concatenate([bg[:, ::-1], jnp.full((DIL_HPG, 3 * TILE_Q - DIL_NK), NEG, F32)], axis=1)
        t = _toeplitz(v, TILE_Q, 2 * TILE_Q)
        prev.append(t[:, :, :TILE_Q])
        cur.append(t[:, :, TILE_Q:])
    return cur, prev


def _sample_bias_tables(rel_bias, t_real):
    m = jnp.arange(TILE_Q, dtype=jnp.int32)
    u = jnp.arange(SAMPLE_ROWS, dtype=jnp.int32)
    tabc, tabn = [], []
    for g in range(3):
        bg = _group_bias(rel_bias, g)
        rc, rn = [], []
        for t in range(t_real):
            if g == 0:
                jc = TILE_Q + t - m
                c_ok = m >= t
                jn = t - u
                n_ok = u <= t
            else:
                jc = TILE_Q - m
                c_ok = m >= 0
                jn = jnp.zeros_like(u)
                n_ok = u == t
            tc = jnp.where(c_ok[None, :], bg[:, jnp.clip(jc, 0, DIL_NK - 1)], NEG)
            tn = jnp.where(n_ok[None, :], bg[:, jnp.clip(jn, 0, DIL_NK - 1)], NEG)
            rc.append(jnp.concatenate([tc, jnp.zeros_like(tc)], axis=0))
            rn.append(jnp.concatenate([tn, jnp.zeros_like(tn)], axis=0))
        tabc.append(jnp.stack(rc))
        tabn.append(jnp.stack(rn))
    return jnp.stack(tabc), jnp.stack(tabn)


def _dil_prompt_kernel(*refs, has_prev):
    if has_prev:
        q_ref, kc_ref, vc_ref, kp_ref, vp_ref, tc_ref, tp_ref, o_ref, l_ref = refs
    else:
        q_ref, kc_ref, vc_ref, tc_ref, o_ref, l_ref = refs
    first = pl.program_id(2) == 0
    q = q_ref[...]
    kc = kc_ref[...].astype(BF16)
    vc = vc_ref[...].astype(BF16)
    if has_prev:
        kp = kp_ref[...].astype(BF16)
        vp = vp_ref[...].astype(BF16)
    outs, lses = [], []
    for h in range(DIL_HPG):
        sl = slice(h * DIL_HEAD_DIM, (h + 1) * DIL_HEAD_DIM)
        s_c = _dot_nt(q[:, sl], kc[:, sl]) + tc_ref[h]
        m = jnp.max(s_c, axis=-1, keepdims=True)
        if has_prev:
            s_p = jnp.where(first, NEG, _dot_nt(q[:, sl], kp[:, sl]) + tp_ref[h])
            m = jnp.maximum(m, jnp.max(s_p, axis=-1, keepdims=True))
        p_c = jnp.exp(s_c - m)
        l = jnp.sum(p_c, axis=-1, keepdims=True)
        acc = _dot(p_c, vc[:, sl])
        if has_prev:
            p_p = jnp.exp(s_p - m)
            l = l + jnp.sum(p_p, axis=-1, keepdims=True)
            acc = acc + _dot(p_p, vp[:, sl])
        outs.append(acc / l)
        lses.append(jnp.broadcast_to(m + jnp.log(l), (TILE_Q, DIL_HEAD_DIM)))
    o_ref[...] = jnp.concatenate(outs, axis=-1)
    l_ref[...] = jnp.concatenate(lses, axis=-1)


def _dil_prompt(q, kv, g, t_cur, t_prev):
    B, S, _ = q.shape
    dil = DIL_GROUPS[g][1]
    sub = S // dil
    nb = sub // TILE_Q
    has_prev = nb > 1
    qr = q.reshape(B, sub, dil * 3 * DIL_GW)
    kvr = kv.reshape(B, sub, dil * 2 * DIL_GW)
    blk = (None, TILE_Q, DIL_GW)
    prev = lambda t: jnp.maximum(t - 1, 0)
    in_specs = [pl.BlockSpec(blk, lambda b, r, t: (b, t, r * 3 + g)),
                pl.BlockSpec(blk, lambda b, r, t: (b, t, r * 2)),
                pl.BlockSpec(blk, lambda b, r, t: (b, t, r * 2 + 1))]
    args = [qr, kvr, kvr]
    if has_prev:
        in_specs += [pl.BlockSpec(blk, lambda b, r, t: (b, prev(t), r * 2)),
                     pl.BlockSpec(blk, lambda b, r, t: (b, prev(t), r * 2 + 1))]
        args += [kvr, kvr]
    in_specs.append(_resident(t_cur.shape))
    args.append(t_cur)
    if has_prev:
        in_specs.append(_resident(t_prev.shape))
        args.append(t_prev)
    out_spec = pl.BlockSpec(blk, lambda b, r, t: (b, t, r))
    o, lse = pl.pallas_call(
        functools.partial(_dil_prompt_kernel, has_prev=has_prev),
        grid=(B, dil, nb),
        in_specs=in_specs,
        out_specs=[out_spec, out_spec],
        out_shape=[jax.ShapeDtypeStruct((B, sub, dil * DIL_GW), F32)] * 2,
        compiler_params=_cparams(("parallel", "parallel", "arbitrary")),
        name=f"dil_prompt_g{g}",
    )(*args)
    return o.reshape(B * S, DIL_GW), lse.reshape(B * S, DIL_GW)


def _dil_sample_kernel(q_ref, n0_ref, n1_ref, n2_ref, c0_ref, c1_ref, c2_ref, tabc_ref, tabn_ref, o_ref, l_ref,
                       *, t_real):
    rows = lax.broadcasted_iota(jnp.int32, (SAMPLE_ROWS, DIL_GW), 0)
    lanes = lax.broadcasted_iota(jnp.int32, (SAMPLE_ROWS, DIL_GW), 1)
    head_mask = (lanes // DIL_HEAD_DIM) == rows
    new_refs = (n0_ref, n1_ref, n2_ref)
    cache_refs = (c0_ref, c1_ref, c2_ref)
    for g in range(3):
        kn = new_refs[g][:, :DIL_GW].astype(BF16)
        vn = new_refs[g][:, DIL_GW:].astype(BF16)
        o_rows, l_rows = [], []
        for t in range(t_real):
            col = 0 if g == 0 else t * 2 * DIL_GW
            kc = cache_refs[g][:, col:col + DIL_GW].astype(BF16)
            vc = cache_refs[g][:, col + DIL_GW:col + 2 * DIL_GW].astype(BF16)
            q_t = q_ref[t:t + 1, g * DIL_GW:(g + 1) * DIL_GW]
            q_bd = jnp.where(head_mask, jnp.broadcast_to(q_t, (SAMPLE_ROWS, DIL_GW)), 0.0)
            s_c = _dot_nt(q_bd, kc) + tabc_ref[g, t]
            s_n = _dot_nt(q_bd, kn) + tabn_ref[g, t]
            m = jnp.maximum(jnp.max(s_c, axis=-1, keepdims=True), jnp.max(s_n, axis=-1, keepdims=True))
            p_c = jnp.exp(s_c - m)
            p_n = jnp.exp(s_n - m)
            l = jnp.sum(p_c, axis=-1, keepdims=True) + jnp.sum(p_n, axis=-1, keepdims=True)
            acc = (_dot(p_c, vc) + _dot(p_n, vn)) / l
            o_rows.append(jnp.sum(jnp.where(head_mask, acc, 0.0), axis=0, keepdims=True))
            lse = jnp.broadcast_to(m + jnp.log(l), (SAMPLE_ROWS, DIL_GW))
            l_rows.append(jnp.sum(jnp.where(head_mask, lse, 0.0), axis=0, keepdims=True))
        pad = jnp.zeros((SAMPLE_ROWS - t_real, DIL_GW), F32)
        o_ref[:, g * DIL_GW:(g + 1) * DIL_GW] = jnp.concatenate(o_rows + [pad], axis=0)
        l_ref[:, g * DIL_GW:(g + 1) * DIL_GW] = jnp.concatenate(l_rows + [pad], axis=0)


def _dil_sample(q, kvn, caches, tabc, tabn, t_real):
    B = q.shape[0]
    row = lambda n: pl.BlockSpec((None, SAMPLE_ROWS, n), lambda b: (b, 0, 0))
    cr, cspecs = [], []
    for g, (w, dil) in enumerate(DIL_GROUPS):
        assert caches[g].shape[1] == w and w // dil == TILE_Q and t_real <= dil * SUBLANES
        cr.append(caches[g].reshape(B, TILE_Q, dil * 2 * DIL_GW))
        cols = 2 * DIL_GW if g == 0 else t_real * 2 * DIL_GW
        cspecs.append(pl.BlockSpec((None, TILE_Q, cols), lambda b: (b, 0, 0)))
    out_spec = pl.BlockSpec((None, SAMPLE_ROWS, 3 * DIL_GW), lambda b: (b, 0, 0))
    o, lse = pl.pallas_call(
        functools.partial(_dil_sample_kernel, t_real=t_real),
        grid=(B,),
        in_specs=[row(3 * DIL_GW)] + [row(2 * DIL_GW)] * 3 + cspecs + [_resident(tabc.shape), _resident(tabn.shape)],
        out_specs=[out_spec, out_spec],
        out_shape=[jax.ShapeDtypeStruct((B, SAMPLE_ROWS, 3 * DIL_GW), F32)] * 2,
        compiler_params=_cparams(("parallel",)),
        name="dil_sample",
    )(q, *kvn, *cr, tabc, tabn)
    o = o.reshape(B * SAMPLE_ROWS, 3 * DIL_GW)
    lse = lse.reshape(B * SAMPLE_ROWS, 3 * DIL_GW)
    return ([o[:, g * DIL_GW:(g + 1) * DIL_GW] for g in range(3)],
            [lse[:, g * DIL_GW:(g + 1) * DIL_GW] for g in range(3)])


GDN_HPS = 4
GDN_SW = GDN_HPS * GDN_HEAD_DIM
CONV_PAD = SUBLANES


def _gdn_kernel(q_ref, k_ref, v_ref, bq_ref, bk_ref, bv_ref, wq_ref, wk_ref, wv_ref, ba_ref, alog_ref, dtb_ref,
                z_ref, gn_ref, s0_ref, o_ref, s_ref, xq_s, xk_s, xv_s, *, t_real):
    T = q_ref.shape[0]
    hg = pl.program_id(1)
    C = GDN_CHUNK
    D = GDN_HEAD_DIM
    ri = lax.broadcasted_iota(jnp.int32, (TILE_Q, TILE_Q), 0)
    ci = lax.broadcasted_iota(jnp.int32, (TILE_Q, TILE_Q), 1)
    same = (ri // C) == (ci // C)
    incl = same & (ri >= ci)
    strict = same & (ri > ci)
    eye = (ri == ci).astype(F32)
    tri = incl.astype(BF16)
    lane, row = ci, ri
    gain = gn_ref[...]
    zpad = jnp.zeros((C, D), F32)

    HH = range(GDN_HPS)
    hsl = [slice(hh * D, (hh + 1) * D) for hh in HH]

    def conv(x_s, hh, base, w_ref):
        y = x_s[hh, pl.ds(base, TILE_Q), :] * w_ref[0:1, hsl[hh]]
        for j in range(1, 4):
            y = y + x_s[hh, pl.ds(base + j, TILE_Q), :] * w_ref[j:j + 1, hsl[hh]]
        return _silu(y)

    def l2n(x):
        return x * lax.rsqrt(jnp.sum(x * x, axis=-1, keepdims=True) + EPS)

    def tile(i, S):
        r0 = pl.multiple_of(i * TILE_Q, TILE_Q)
        for x_s, x_ref, b_ref in ((xq_s, q_ref, bq_ref), (xk_s, k_ref, bk_ref), (xv_s, v_ref, bv_ref)):
            @pl.when(i == 0)
            def _():
                for hh in HH:
                    x_s[hh, CONV_PAD - 3:CONV_PAD, :] = b_ref[:, hsl[hh]]

            @pl.when(i > 0)
            def _():
                for hh in HH:
                    x_s[hh, CONV_PAD - 3:CONV_PAD, :] = x_s[hh, CONV_PAD + TILE_Q - 3:CONV_PAD + TILE_Q, :]

            for hh in HH:
                x_s[hh, CONV_PAD:CONV_PAD + TILE_Q, :] = x_ref[pl.ds(r0, TILE_Q), hsl[hh]]
        base = CONV_PAD - 3 + jnp.minimum(i, 0)
        q4 = [conv(xq_s, hh, base, wq_ref) for hh in HH]
        k4 = [conv(xk_s, hh, base, wk_ref) for hh in HH]
        v4 = [conv(xv_s, hh, base, wv_ref) for hh in HH]
        ba = ba_ref[pl.ds(r0, TILE_Q), :]
        live = (row + r0) < t_real
        beta_all = jnp.where(live, _sigmoid(ba), 0.0)
        g_all = jnp.where(live, -jnp.exp(alog_ref[...]) * _softplus(ba + dtb_ref[...]), 0.0)
        gh, gm, gl = _split3(g_all)
        G_all = (jnp.dot(tri, gh, preferred_element_type=F32) + jnp.dot(tri, gm, preferred_element_type=F32)
                 + jnp.dot(tri, gl, preferred_element_type=F32))
        head = [hg * GDN_HPS + hh for hh in HH]
        q = [l2n(q4[hh]) * (D ** -0.5) for hh in HH]
        k = [l2n(k4[hh]) for hh in HH]
        bc = [jnp.sum(jnp.where(lane == head[hh], beta_all, 0.0), axis=-1, keepdims=True) for hh in HH]
        Gc = [jnp.broadcast_to(jnp.sum(jnp.where(lane == head[hh] + GDN_HEADS, G_all, 0.0), axis=-1, keepdims=True),
                               (TILE_Q, TILE_Q)) for hh in HH]
        gamma = [jnp.exp(jnp.where(incl, Gc[hh] - Gc[hh].T, NEG)) for hh in HH]
        kk = [_dot_nt(k[hh], k[hh]) for hh in HH]
        qk = [_dot_nt(q[hh], k[hh]) for hh in HH]
        X = [jnp.where(strict, bc[hh] * kk[hh] * gamma[hh], 0.0) for hh in HH]
        P = [eye - X[hh] for hh in HH]
        for _ in range(int(math.log2(C)) - 1):
            X = [_dot(X[hh], X[hh]) for hh in HH]
            P = [P[hh] + _dot(P[hh], X[hh]) for hh in HH]
        eG = [jnp.exp(Gc[hh]) for hh in HH]
        rhs = [jnp.concatenate([v4[hh] * bc[hh], k[hh] * (bc[hh] * eG[hh])], axis=-1) for hh in HH]
        sol = [rhs[hh] + _dot(P[hh] - eye, rhs[hh]) for hh in HH]
        a_in = [qk[hh] * gamma[hh] for hh in HH]
        q_dec = [q[hh] * eG[hh] for hh in HH]
        kdT = [(k[hh] * jnp.exp(jnp.where(row < C, Gc[hh][C - 1:C, :], Gc[hh][2 * C - 1:2 * C, :]) - Gc[hh])).T
               for hh in HH]
        S = list(S)
        oq, vn = [[] for _ in HH], [[] for _ in HH]
        for c in range(TILE_Q // C):
            cs = slice(c * C, (c + 1) * C)
            r = [_dot(jnp.concatenate([sol[hh][cs, D:], q_dec[hh][cs]], axis=0), S[hh]) for hh in HH]
            for hh in HH:
                v_new = sol[hh][cs, :D] - r[hh][:C]
                oq[hh].append(r[hh][C:])
                vn[hh].append(v_new)
            vpad = [jnp.concatenate([vn[hh][c], zpad] if c == 0 else [zpad, vn[hh][c]], axis=0) for hh in HH]
            S = [S[hh] * jnp.exp(Gc[hh][(c + 1) * C - 1:(c + 1) * C, :]) + _dot(kdT[hh], vpad[hh]) for hh in HH]
        o = [jnp.concatenate(oq[hh], axis=0) + _dot(a_in[hh], jnp.concatenate(vn[hh], axis=0)) for hh in HH]
        outs = [_rms(o[hh], gain) * _silu(z_ref[pl.ds(r0, TILE_Q), hsl[hh]]) for hh in HH]
        o_ref[pl.ds(r0, TILE_Q), :] = jnp.concatenate(outs, axis=-1).astype(o_ref.dtype)
        return tuple(S)

    S = lax.fori_loop(0, T // TILE_Q, tile, tuple(s0_ref[hh] for hh in HH))
    for hh in HH:
        s_ref[hh] = S[hh]


def _gdn(gq, ba, z, conv_buf, s0, w_conv, a_log, dt_bias, norm_out, t_real):
    B, T, _ = gq.shape
    H = GDN_HEADS
    pad16 = lambda x: jnp.concatenate([jnp.zeros((H,), F32), x.astype(F32), jnp.zeros((LANES - 2 * H,), F32)])
    ng = H // GDN_HPS
    col = lambda off: pl.BlockSpec((None, T, GDN_SW), lambda b, g: (b, 0, off * ng + g))
    bcol = lambda off: pl.BlockSpec((None, 3, GDN_SW), lambda b, g: (b, 0, off * ng + g))
    wcol = lambda off: pl.BlockSpec((4, GDN_SW), lambda b, g: (0, off * ng + g))
    sblk = pl.BlockSpec((None, GDN_HPS, GDN_HEAD_DIM, GDN_HEAD_DIM), lambda b, g: (b, g, 0, 0))
    o, s_new = pl.pallas_call(
        functools.partial(_gdn_kernel, t_real=t_real),
        grid=(B, ng),
        in_specs=[col(0), col(1), col(2), bcol(0), bcol(1), bcol(2), wcol(0), wcol(1), wcol(2),
                  pl.BlockSpec((None, T, LANES), lambda b, g: (b, 0, 0)),
                  _resident((1, LANES)), _resident((1, LANES)),
                  col(0), _resident((1, LANES)), sblk],
        out_specs=[col(0), sblk],
        out_shape=[jax.ShapeDtypeStruct((B, T, GDN_W), BF16),
                   jax.ShapeDtypeStruct((B, H, GDN_HEAD_DIM, GDN_HEAD_DIM), F32)],
        scratch_shapes=[pltpu.VMEM((GDN_HPS, TILE_Q + CONV_PAD, LANES), F32)] * 3,
        compiler_params=_cparams(("parallel", "parallel")),
        name="gdn",
    )(gq, gq, gq, conv_buf, conv_buf, conv_buf, w_conv, w_conv, w_conv, ba,
      pad16(a_log).reshape(1, LANES), pad16(dt_bias).reshape(1, LANES), z, norm_out.reshape(1, LANES), s0)
    return o, s_new


def _mix_kernel(o0_ref, o1_ref, o2_ref, l0_ref, l1_ref, l2_ref, ob_ref, ga_ref, gb_ref, h_ref,
                wa_ref, wb_ref, wo_ref, gq_ref, wq_ref, h1_ref, qm_ref):
    l0, l1, l2 = l0_ref[...], l1_ref[...], l2_ref[...]
    mx = jnp.maximum(jnp.maximum(l0, l1), l2)
    e0, e1, e2 = jnp.exp(l0 - mx), jnp.exp(l1 - mx), jnp.exp(l2 - mx)
    o_a = (e0 * o0_ref[...] + e1 * o1_ref[...] + e2 * o2_ref[...]) / (e0 + e1 + e2)
    a = _dot(o_a, wa_ref[...])
    b = jnp.dot(ob_ref[...], wb_ref[...], preferred_element_type=F32)
    merged = _sigmoid(ga_ref[...]) * a + _sigmoid(gb_ref[...]) * b
    h1 = h_ref[...] + _dot(merged, wo_ref[...])
    h1_ref[...] = h1
    qm_ref[...] = _dot(_rms(h1, gq_ref[...]), wq_ref[...]).astype(qm_ref.dtype)


def _mix(o_g, l_g, o_b, ga, gb, h, w_a, w_b, w_o, norm_mem_q, w_mem_q, tm):
    rows, d = h.shape
    rt = lambda n: pl.BlockSpec((tm, n), lambda i: (i, 0))
    return pl.pallas_call(
        _mix_kernel,
        grid=(rows // tm,),
        in_specs=[rt(DIL_GW)] * 6 + [rt(GDN_W), rt(d), rt(d), rt(d),
                                     _resident(w_a.shape), _resident(w_b.shape), _resident(w_o.shape),
                                     _resident((1, d)), _resident(w_mem_q.shape)],
        out_specs=[rt(d), rt(w_mem_q.shape[1])],
        out_shape=[jax.ShapeDtypeStruct((rows, d), F32), jax.ShapeDtypeStruct((rows, w_mem_q.shape[1]), BF16)],
        compiler_params=_cparams(("parallel",)),
        name="mix",
    )(*o_g, *l_g, o_b, ga, gb, h, w_a, w_b, w_o, norm_mem_q.reshape(1, d), w_mem_q)


def _mem_kv_kernel(x_ref, g_ref, w_ref, k_ref, v_ref):
    u = _rms(x_ref[...], g_ref[...]).astype(BF16)
    n = k_ref.shape[-1]
    k_ref[...] = jnp.dot(u, w_ref[:, :n], preferred_element_type=F32)
    v_ref[...] = jnp.dot(u, w_ref[:, n:], preferred_element_type=F32)


def _mem_kv(mem2d, gain, w, tm):
    rows, d = mem2d.shape
    n = w.shape[1] // 2
    return pl.pallas_call(
        _mem_kv_kernel,
        grid=(rows // tm,),
        in_specs=[pl.BlockSpec((tm, d), lambda i: (i, 0)), _resident((1, d)), _resident(w.shape)],
        out_specs=[pl.BlockSpec((tm, n), lambda i: (i, 0))] * 2,
        out_shape=[jax.ShapeDtypeStruct((rows, n), F32)] * 2,
        compiler_params=_cparams(("parallel",)),
        name="mem_kv",
    )(mem2d, gain.reshape(1, d), w)


def _mem_attn_kernel(q_ref, k_ref, v_ref, o_ref):
    q = q_ref[...]
    k = k_ref[...].astype(BF16)
    v = v_ref[...].astype(BF16)
    outs = []
    for h in range(MEM_HEADS):
        sl = slice(h * MEM_HEAD_DIM, (h + 1) * MEM_HEAD_DIM)
        s = _dot_nt(q[:, sl], k[:, sl]) * (MEM_HEAD_DIM ** -0.5)
        p = jnp.exp(s - jnp.max(s, axis=-1, keepdims=True))
        outs.append(_dot(p, v[:, sl]) / jnp.sum(p, axis=-1, keepdims=True))
    o_ref[...] = jnp.concatenate(outs, axis=-1).astype(o_ref.dtype)


def _mem_attn(qm, mem_k, mem_v, tm):
    B, T, w = qm.shape
    M = mem_k.shape[1]
    return pl.pallas_call(
        _mem_attn_kernel,
        grid=(B, T // tm),
        in_specs=[pl.BlockSpec((None, tm, w), lambda b, j: (b, j, 0)),
                  pl.BlockSpec((None, M, w), lambda b, j: (b, 0, 0)),
                  pl.BlockSpec((None, M, w), lambda b, j: (b, 0, 0))],
        out_specs=pl.BlockSpec((None, tm, w), lambda b, j: (b, j, 0)),
        out_shape=jax.ShapeDtypeStruct((B, T, w), F32),
        compiler_params=_cparams(("parallel", "parallel")),
        name="mem_attn",
    )(qm, mem_k, mem_v)


def _ffn_kernel(*refs, inject, emit_gate):
    if inject:
        (h1_ref, om_ref, init_ref, fill_ref, wmo_ref, gf_ref, wup_ref, wc_ref, bc_ref, wd_ref, gfin_ref,
         y_ref, fc_ref, gs) = refs
    else:
        (h1_ref, om_ref, init_ref, wmo_ref, gf_ref, wup_ref, wc_ref, bc_ref, wd_ref, gfin_ref,
         y_ref, fc_ref, gs) = refs
    tm = h1_ref.shape[0]
    F = wd_ref.shape[0]
    PAD = SUBLANES

    @pl.when(pl.program_id(1) == 0)
    def _():
        gs[PAD - 2:PAD, :] = init_ref[...]

    h2 = h1_ref[...] + _dot(om_ref[...], wmo_ref[...])
    n = _rms(h2, gf_ref[...]).astype(BF16)
    gate = jnp.dot(n, wup_ref[:, :F], preferred_element_type=F32)
    if inject:
        r = lax.broadcasted_iota(jnp.int32, (tm, 1), 0)
        gate = jnp.where((r % SAMPLE_ROWS) >= SAMPLE_ROWS - 2, fill_ref[...], gate)
    gs[PAD:PAD + tm, :] = gate
    conv = (gs[PAD - 2:PAD - 2 + tm, :] * wc_ref[0:1, :] + gs[PAD - 1:PAD - 1 + tm, :] * wc_ref[1:2, :]
            + gate * wc_ref[2:3, :])
    last2 = gs[PAD + tm - 2:PAD + tm, :]
    gs[PAD - 2:PAD, :] = last2
    if emit_gate:
        fc_ref[...] = gate
    else:
        fc_ref[...] = last2
    up = jnp.dot(n, wup_ref[:, F:], preferred_element_type=F32)
    act = _silu(conv + bc_ref[...]) * up
    y = h2 + _dot(act, wd_ref[...])
    y_ref[...] = _rms(y, gfin_ref[...])


def _ffn(h1, om, init, fill, w_mo, norm_ffn, w_up, w_conv, b_conv, w_down, norm_final, tm, emit_gate):
    B, T, d = h1.shape
    F = w_down.shape[0]
    inject = fill is not None
    rt = lambda n: pl.BlockSpec((None, tm, n), lambda b, j: (b, j, 0))
    in_specs = [rt(d), rt(om.shape[-1]), pl.BlockSpec((None, 2, F), lambda b, j: (b, 0, 0))]
    args = [h1, om, init]
    if inject:
        in_specs.append(rt(F))
        args.append(fill)
    in_specs += [_resident(w_mo.shape), _resident((1, d)), _resident(w_up.shape), _resident(w_conv.shape),
                 _resident((1, F)), _resident(w_down.shape), _resident((1, d))]
    args += [w_mo, norm_ffn.reshape(1, d), w_up, w_conv, b_conv.reshape(1, F), w_down, norm_final.reshape(1, d)]
    if emit_gate:
        fc_spec, fc_shape = rt(F), jax.ShapeDtypeStruct((B, T, F), F32)
    else:
        fc_spec = pl.BlockSpec((None, 2, F), lambda b, j: (b, 0, 0))
        fc_shape = jax.ShapeDtypeStruct((B, 2, F), F32)
    return pl.pallas_call(
        functools.partial(_ffn_kernel, inject=inject, emit_gate=emit_gate),
        grid=(B, T // tm),
        in_specs=in_specs,
        out_specs=[rt(d), fc_spec],
        out_shape=[jax.ShapeDtypeStruct((B, T, d), F32), fc_shape],
        scratch_shapes=[pltpu.VMEM((tm + SUBLANES, F), F32)],
        compiler_params=_cparams(("parallel", "arbitrary")),
        name="ffn",
    )(*args)


def kernel(x_prompt, x_sample, cache_dil0_kv, cache_dil1_kv, cache_dil2_kv, state_delta, state_delta_conv, cache_mem_k, cache_mem_v, state_ffn_conv, mem_prompt, rel_bias, norm_mix, w_in, w_conv_delta, a_log, dt_bias, norm_delta_out, w_branch_a, w_branch_b, w_out, norm_mem_q, norm_mem_kv, w_mem_q, w_mem_kv, w_mem_o, norm_ffn, w_ffn_up, w_ffn_conv, b_ffn_conv, w_ffn_down, norm_final):
    B, S, D = x_prompt.shape
    Bs, Ts, _ = x_sample.shape
    depth = w_in.shape[0]
    assert depth == 1 and Ts <= SAMPLE_ROWS - 2 and Ts >= 3 and S % (16 * TILE_Q) == 0
    F = w_ffn_down.shape[1]
    M = mem_prompt.shape[1]
    l = 0
    w_arr = _arrange_w_in(w_in[l])
    w_a, w_b, w_o = (w.astype(BF16) for w in (w_branch_a[l], w_branch_b[l], w_out[l]))
    w_mq, w_mkv, w_mo = (w.astype(BF16) for w in (w_mem_q[l], w_mem_kv[l], w_mem_o[l]))
    w_up, w_dn = w_ffn_up[l].astype(BF16), w_ffn_down[l].astype(BF16)
    t_cur, t_prev = _prompt_bias_tables(rel_bias)

    xp = x_prompt.reshape(B * S, D)
    q, kv0, kv1, kv2, gq, z, ba, ga, gb = _in_proj(xp, norm_mix[l], w_arr, 256)
    kvs = [kv.reshape(B, S, 2 * DIL_GW) for kv in (kv0, kv1, kv2)]
    q3 = q.reshape(B, S, 3 * DIL_GW)
    o_g, l_g = zip(*[_dil_prompt(q3, kvs[g], g, t_cur[g], t_prev[g]) for g in range(3)])
    o_b, delta_p = _gdn(gq.reshape(B, S, -1), ba.reshape(B, S, LANES), z.reshape(B, S, GDN_W),
                        jnp.zeros((B, 3, 3 * GDN_W), F32), jnp.zeros((B, GDN_HEADS, GDN_HEAD_DIM, GDN_HEAD_DIM), F32),
                        w_conv_delta[l], a_log[l], dt_bias[l], norm_delta_out[l], S)
    h1, qm = _mix(o_g, l_g, o_b.reshape(B * S, GDN_W), ga, gb, xp, w_a, w_b, w_o, norm_mem_q[l], w_mq, 256)
    mk_p, mv_p = _mem_kv(mem_prompt.reshape(B * M, D), norm_mem_kv[l], w_mkv, 256)
    mk_p, mv_p = mk_p.reshape(B, M, -1), mv_p.reshape(B, M, -1)
    om = _mem_attn(qm.reshape(B, S, -1), mk_p, mv_p, 512)
    y_p, fconv_p = _ffn(h1.reshape(B, S, D), om, jnp.zeros((B, 2, F), F32), None, w_mo, norm_ffn[l], w_up,
                        w_ffn_conv[l], b_ffn_conv[l], w_dn, norm_final, 256, False)
    gq3 = gq.reshape(B, S, -1)
    p_out = ([kvs[g][:, S - min(w, S):].reshape(1, B, min(w, S), 2, DIL_HPG, DIL_HEAD_DIM)
              for g, (w, _) in enumerate(DIL_GROUPS)]
             + [delta_p[None], gq3[:, S - 3:][None], mk_p.reshape(1, B, M, MEM_HEADS, MEM_HEAD_DIM),
                mv_p.reshape(1, B, M, MEM_HEADS, MEM_HEAD_DIM), fconv_p[None]])

    R = SAMPLE_ROWS
    xs = jnp.pad(x_sample, ((0, 0), (0, R - Ts), (0, 0))).reshape(Bs * R, D)
    q, kv0, kv1, kv2, gq, z, ba, ga, gb = _in_proj(xs, norm_mix[l], w_arr, Bs * R)
    kvn = [kv.reshape(Bs, R, 2 * DIL_GW) for kv in (kv0, kv1, kv2)]
    caches = [c[l].reshape(Bs, c.shape[2], 2 * DIL_GW) for c in (cache_dil0_kv, cache_dil1_kv, cache_dil2_kv)]
    tabc, tabn = _sample_bias_tables(rel_bias, Ts)
    o_g, l_g = _dil_sample(q.astype(F32).reshape(Bs, R, 3 * DIL_GW), kvn, caches, tabc, tabn, Ts)
    padt = lambda a: jnp.pad(a.reshape(Bs, R, -1), ((0, 0), (0, TILE_Q - R), (0, 0)))
    o_b, delta_s = _gdn(padt(gq), padt(ba), padt(z), state_delta_conv[l], state_delta[l],
                        w_conv_delta[l], a_log[l], dt_bias[l], norm_delta_out[l], Ts)
    o_b = o_b[:, :R].reshape(Bs * R, GDN_W)
    h1, qm = _mix(o_g, l_g, o_b, ga, gb, xs, w_a, w_b, w_o, norm_mem_q[l], w_mq, Bs * R)
    om = _mem_attn(qm.reshape(Bs, R, -1), cache_mem_k[l].reshape(Bs, M, -1), cache_mem_v[l].reshape(Bs, M, -1), R)
    fst = state_ffn_conv[l]
    fill = jnp.concatenate([jnp.zeros((Bs, R - 2, F), F32),
                            jnp.concatenate([fst[1:], jnp.zeros((1, 2, F), F32)], axis=0)], axis=1)
    y_s, gate_s = _ffn(h1.reshape(1, Bs * R, D), om.reshape(1, Bs * R, -1), fst[:1], fill.reshape(1, Bs * R, F),
                       w_mo, norm_ffn[l], w_up, w_ffn_conv[l], b_ffn_conv[l], w_dn, norm_final, Bs * R, True)
    y_s = y_s.reshape(Bs, R, D)[:, :Ts]
    gq3 = gq.reshape(Bs, R, -1)
    s_out = ([jnp.concatenate([caches[g][:, Ts:], kvn[g][:, :Ts]], axis=1)
              .reshape(1, Bs, caches[g].shape[1], 2, DIL_HPG, DIL_HEAD_DIM) for g in range(3)]
             + [delta_s[None], gq3[:, Ts - 3:Ts][None], gate_s.reshape(Bs, R, F)[:, Ts - 2:Ts][None]])

    return (y_p.reshape(B, S, D), y_s, *p_out, *s_out)
```

```python
import functools
import math

import jax
import jax.numpy as jnp
import numpy as np
from jax import lax
from jax.experimental import pallas as pl
from jax.experimental.pallas import tpu as pltpu

F32 = jnp.float32
BF16 = jnp.bfloat16

PAST_LEN = 8192
DIL_GROUPS = ((128, 1), (512, 4), (2048, 16))
DIL_HPG = 4
DIL_HEAD_DIM = 64
DIL_GW = DIL_HPG * DIL_HEAD_DIM
DIL_NK = 129
REL_BUCKETS = 32
REL_MAX_DIST = 2048
GDN_HEADS = 8
GDN_HEAD_DIM = 128
GDN_W = GDN_HEADS * GDN_HEAD_DIM
GDN_CHUNK = 64
MEM_HEADS = 4
MEM_HEAD_DIM = 128
EPS = 1e-6
NEG = -1e30

LANES = 128
SUBLANES = 8
TILE_Q = 128
SAMPLE_ROWS = SUBLANES
VMEM_LIMIT = 56 * 1024 * 1024


def _cparams(sem):
    return pltpu.CompilerParams(dimension_semantics=sem, vmem_limit_bytes=VMEM_LIMIT)


def _resident(shape):
    nd = len(shape)
    return pl.BlockSpec(shape, lambda *_: (0,) * nd, pipeline_mode=pl.Buffered(1))


def _rms(x, gain_row):
    return x * lax.rsqrt(jnp.mean(x * x, axis=-1, keepdims=True) + EPS) * gain_row


def _dot(a, b):
    return jnp.dot(a.astype(BF16), b.astype(BF16), preferred_element_type=F32)


def _dot_nt(a, b):
    return lax.dot_general(a.astype(BF16), b.astype(BF16), (((1,), (1,)), ((), ())), preferred_element_type=F32)


def _dot_tn(a, b):
    return lax.dot_general(a.astype(BF16), b.astype(BF16), (((0,), (0,)), ((), ())), preferred_element_type=F32)


def _split3(x):
    hi = x.astype(BF16)
    r1 = x - hi.astype(F32)
    mid = r1.astype(BF16)
    lo = (r1 - mid.astype(F32)).astype(BF16)
    return hi, mid, lo


def _dot_hp(a, b):
    ah, am, al = _split3(a)
    bh, bm, bl = _split3(b)
    d = functools.partial(jnp.dot, preferred_element_type=F32)
    return d(ah, bh) + (d(ah, bm) + d(am, bh)) + (d(am, bm) + d(ah, bl) + d(al, bh))


def _sigmoid(x):
    return 1.0 / (1.0 + jnp.exp(-x))


def _silu(x):
    return x * _sigmoid(x)


def _softplus(x):
    return jnp.maximum(x, 0.0) + jnp.log(1.0 + jnp.exp(-jnp.abs(x)))


IN_SEGS = (("q", 3 * DIL_GW), ("kv0", 2 * DIL_GW), ("kv1", 2 * DIL_GW), ("kv2", 2 * DIL_GW),
           ("gq", 3 * GDN_W), ("z", GDN_W), ("ba", LANES), ("ga", 1024), ("gb", 1024))


def _arrange_w_in(w_in):
    o = 0
    qa = w_in[:, o:o + 768]; o += 768
    ka = w_in[:, o:o + 768]; o += 768
    va = w_in[:, o:o + 768]; o += 768
    gq = w_in[:, o:o + 3 * GDN_W]; o += 3 * GDN_W
    z = w_in[:, o:o + GDN_W]; o += GDN_W
    beta = w_in[:, o:o + GDN_HEADS]; o += GDN_HEADS
    a = w_in[:, o:o + GDN_HEADS]; o += GDN_HEADS
    ga = w_in[:, o:o + 1024]; o += 1024
    gb = w_in[:, o:o + 1024]
    kv = [jnp.concatenate([ka[:, g * DIL_GW:(g + 1) * DIL_GW], va[:, g * DIL_GW:(g + 1) * DIL_GW]], axis=1)
          for g in range(3)]
    ba = jnp.concatenate([beta, a, jnp.zeros((w_in.shape[0], LANES - 2 * GDN_HEADS), w_in.dtype)], axis=1)
    return jnp.concatenate([qa] + kv + [gq, z, ba, ga, gb], axis=1).astype(BF16)


def _in_proj_kernel(x_ref, g_ref, w_ref, *out_refs, kv_t):
    if kv_t:
        q_ref, kv0_ref, kv1_ref, kv2_ref, kt0_ref, kt1_ref, kt2_ref, gq_ref, z_ref, ba_ref, ga_ref, gb_ref = out_refs
        kt_refs = (kt0_ref, kt1_ref, kt2_ref)
    else:
        q_ref, kv0_ref, kv1_ref, kv2_ref, gq_ref, z_ref, ba_ref, ga_ref, gb_ref = out_refs
    u = _rms(x_ref[...], g_ref[...]).astype(BF16)
    off = 0

    def seg(n):
        nonlocal off
        r = jnp.dot(u, w_ref[:, off:off + n], preferred_element_type=F32)
        off += n
        return r

    q_ref[...] = (seg(3 * DIL_GW) * (DIL_HEAD_DIM ** -0.5)).astype(q_ref.dtype)
    for g, kv_ref in enumerate((kv0_ref, kv1_ref, kv2_ref)):
        kv = seg(2 * DIL_GW)
        kv_ref[...] = kv.astype(kv_ref.dtype)
        if kv_t:
            kt_refs[g][...] = kv.T
    for c in range(3):
        gq_ref[:, c * GDN_W:(c + 1) * GDN_W] = seg(GDN_W)
    z_ref[...] = seg(GDN_W)
    ba_ref[...] = seg(LANES)
    ga_ref[...] = seg(1024)
    gb_ref[...] = seg(1024)


def _in_proj(x2d, gain, w_arr, tm, seq=None):
    rows, d = x2d.shape
    kv_t = seq is not None
    names = [n for n, _ in IN_SEGS]
    widths = dict(IN_SEGS)
    row_spec = lambda n: pl.BlockSpec((tm, n), lambda i: (i, 0))
    out_specs, out_shape = [], []
    for n in names:
        dt = BF16 if n == "q" or (kv_t and n.startswith("kv")) else F32
        out_specs.append(row_spec(widths[n]))
        out_shape.append(jax.ShapeDtypeStruct((rows, widths[n]), dt))
        if n == "kv2" and kv_t:
            nt = seq // tm
            for _ in range(3):
                out_specs.append(pl.BlockSpec((None, 2 * DIL_GW, tm), lambda i: (i // nt, 0, i % nt)))
                out_shape.append(jax.ShapeDtypeStruct((rows // seq, 2 * DIL_GW, seq), F32))
    return pl.pallas_call(
        functools.partial(_in_proj_kernel, kv_t=kv_t),
        grid=(rows // tm,),
        in_specs=[pl.BlockSpec((tm, d), lambda i: (i, 0)), _resident((1, d)), _resident(w_arr.shape)],
        out_specs=out_specs,
        out_shape=out_shape,
        compiler_params=_cparams(("parallel",)),
        name="in_proj",
    )(x2d, gain.reshape(1, d), w_arr)


def _rel_bucket(dist):
    exact = REL_BUCKETS // 2
    d = jnp.maximum(dist, 1).astype(F32)
    large = exact + (jnp.log(d / exact) / math.log(REL_MAX_DIST / exact) * (REL_BUCKETS - exact)).astype(jnp.int32)
    return jnp.where(dist < exact, dist, jnp.minimum(large, REL_BUCKETS - 1))


def _group_bias(rel_bias, g):
    dil = DIL_GROUPS[g][1]
    dist = dil * jnp.arange(DIL_NK, dtype=jnp.int32)
    tab = rel_bias[_rel_bucket(dist)]
    return tab[:, g * DIL_HPG:(g + 1) * DIL_HPG].T.astype(F32)


def _toeplitz(v, n, width):
    h, L = v.shape
    return jnp.tile(v, (1, n))[:, :n * (L - 1)].reshape(h, n, L - 1)[:, :, :width]


def _prompt_bias_tables(rel_bias):
    cur, prev = [], []
    for g in range(3):
        bg = _group_bias(rel_bias, g)
        v = jnp.concatenate([bg[:, ::-1], jnp.full((DIL_HPG, 3 * TILE_Q - DIL_NK), NEG, F32)], axis=1)
        t = _toeplitz(v, TILE_Q, 2 * TILE_Q)
        prev.append(t[:, :, :TILE_Q])
        cur.append(t[:, :, TILE_Q:])
    return cur, prev


def _dil_prompt_kernel(*refs, has_prev):
    if has_prev:
        q_ref, kc_ref, vc_ref, kp_ref, vp_ref, tc_ref, tp_ref, o_ref, l_ref = refs
    else:
        q_ref, kc_ref, vc_ref, tc_ref, o_ref, l_ref = refs
    first = pl.program_id(2) == 0
    q = q_ref[...]
    kc = kc_ref[...].astype(BF16)
    vc = vc_ref[...].astype(BF16)
    if has_prev:
        kp = kp_ref[...].astype(BF16)
        vp = vp_ref[...].astype(BF16)
    outs, lses = [], []
    for h in range(DIL_HPG):
        sl = slice(h * DIL_HEAD_DIM, (h + 1) * DIL_HEAD_DIM)
        s_c = _dot_nt(q[:, sl], kc[:, sl]) + tc_ref[h]
        m = jnp.max(s_c, axis=-1, keepdims=True)
        if has_prev:
            s_p = jnp.where(first, NEG, _dot_nt(q[:, sl], kp[:, sl]) + tp_ref[h])
            m = jnp.maximum(m, jnp.max(s_p, axis=-1, keepdims=True))
        p_c = jnp.exp(s_c - m)
        l = jnp.sum(p_c, axis=-1, keepdims=True)
        acc = _dot(p_c, vc[:, sl])
        if has_prev:
            p_p = jnp.exp(s_p - m)
            l = l + jnp.sum(p_p, axis=-1, keepdims=True)
            acc = acc + _dot(p_p, vp[:, sl])
        outs.append(acc / l)
        lses.append(jnp.broadcast_to(m + jnp.log(l), (TILE_Q, DIL_HEAD_DIM)))
    o_ref[...] = jnp.concatenate(outs, axis=-1)
    l_ref[...] = jnp.concatenate(lses, axis=-1)


def _dil_prompt(q, kv, g, t_cur, t_prev):
    B, S, _ = q.shape
    dil = DIL_GROUPS[g][1]
    sub = S // dil
    nb = sub // TILE_Q
    has_prev = nb > 1
    qr = q.reshape(B, sub, dil * 3 * DIL_GW)
    kvr = kv.reshape(B, sub, dil * 2 * DIL_GW)
    blk = (None, TILE_Q, DIL_GW)
    prev = lambda t: jnp.maximum(t - 1, 0)
    in_specs = [pl.BlockSpec(blk, lambda b, r, t: (b, t, r * 3 + g)),
                pl.BlockSpec(blk, lambda b, r, t: (b, t, r * 2)),
                pl.BlockSpec(blk, lambda b, r, t: (b, t, r * 2 + 1))]
    args = [qr, kvr, kvr]
    if has_prev:
        in_specs += [pl.BlockSpec(blk, lambda b, r, t: (b, prev(t), r * 2)),
                     pl.BlockSpec(blk, lambda b, r, t: (b, prev(t), r * 2 + 1))]
        args += [kvr, kvr]
    in_specs.append(_resident(t_cur.shape))
    args.append(t_cur)
    if has_prev:
        in_specs.append(_resident(t_prev.shape))
        args.append(t_prev)
    out_spec = pl.BlockSpec(blk, lambda b, r, t: (b, t, r))
    o, lse = pl.pallas_call(
        functools.partial(_dil_prompt_kernel, has_prev=has_prev),
        grid=(B, dil, nb),
        in_specs=in_specs,
        out_specs=[out_spec, out_spec],
        out_shape=[jax.ShapeDtypeStruct((B, sub, dil * DIL_GW), F32)] * 2,
        compiler_params=_cparams(("parallel", "parallel", "arbitrary")),
        name=f"dil_prompt_g{g}",
    )(*args)
    return o.reshape(B * S, DIL_GW), lse.reshape(B * S, DIL_GW)


def _sample_bias_tables(rel_bias, t_real):
    R = SAMPLE_ROWS
    tabc, tabn = [], []
    t_i = np.arange(R)[:, None]
    u_i = np.arange(R)[None, :]
    for g, (w, dil) in enumerate(DIL_GROUPS):
        bg = _group_bias(rel_bias, g)
        base = bg[:, ::-1][:, :TILE_Q]
        t0 = jnp.concatenate([base[:, :, None], jnp.full((DIL_HPG, TILE_Q, dil - 1), NEG, F32)], axis=2)
        t0 = t0.reshape(DIL_HPG, w)
        rows = [jnp.concatenate([jnp.full((DIL_HPG, t), NEG, F32), t0[:, :w - t]], axis=1) for t in range(t_real)]
        rows += [jnp.zeros((DIL_HPG, w), F32)] * (R - t_real)
        tabc.append(jnp.stack(rows, axis=1).reshape(DIL_HPG * R, w))
        tn = jnp.full((DIL_HPG, R, R), NEG, F32)
        for j in range(-(-t_real // dil)):
            hit = (t_i - u_i == j * dil) & (t_i < t_real)
            tn = jnp.where(hit[None], bg[:, j][:, None, None], tn)
        tn = jnp.where((t_i >= t_real)[None], 0.0, tn)
        tabn.append(tn.reshape(DIL_HPG * R, R))
    return tabc, jnp.stack(tabn)


def _dil_sample_kernel(q_ref, n0_ref, n1_ref, n2_ref, c0_ref, c1_ref, c2_ref, tc0_ref, tc1_ref, tc2_ref, tn_ref,
                       o_ref, l_ref, oc0_ref, oc1_ref, oc2_ref, *, t_real):
    R = SAMPLE_ROWS
    rows = lax.broadcasted_iota(jnp.int32, (DIL_HPG * R, DIL_GW), 0)
    lanes = lax.broadcasted_iota(jnp.int32, (DIL_HPG * R, DIL_GW), 1)
    head_mask = (lanes // DIL_HEAD_DIM) == (rows // R)
    lane_f = lax.broadcasted_iota(jnp.int32, (2 * DIL_GW, LANES), 1)
    keep = lane_f < LANES - t_real
    sel_l = lax.broadcasted_iota(jnp.int32, (LANES, R), 0)
    sel_u = lax.broadcasted_iota(jnp.int32, (LANES, R), 1)
    selT = ((sel_l == sel_u + LANES - t_real) & (sel_u < t_real)).astype(BF16)
    groups = ((n0_ref, c0_ref, tc0_ref, oc0_ref), (n1_ref, c1_ref, tc1_ref, oc1_ref), (n2_ref, c2_ref, tc2_ref, oc2_ref))

    def fold_heads(x):
        x = jnp.where(head_mask, x, 0.0)
        return x[0:R] + x[R:2 * R] + x[2 * R:3 * R] + x[3 * R:4 * R]

    for g, (n_ref, c_ref, tc_ref, oc_ref) in enumerate(groups):
        W = c_ref.shape[1]
        kvn = n_ref[...]
        q_g = q_ref[:, g * DIL_GW:(g + 1) * DIL_GW]
        q_bd = jnp.where(head_mask, jnp.concatenate([q_g] * DIL_HPG, axis=0), 0.0)
        s_c = _dot(q_bd, c_ref[:DIL_GW, :]) + tc_ref[...]
        s_n = _dot_nt(q_bd, kvn[:, :DIL_GW]) + tn_ref[g]
        m = jnp.maximum(jnp.max(s_c, axis=-1, keepdims=True), jnp.max(s_n, axis=-1, keepdims=True))
        p_c = jnp.exp(s_c - m)
        p_n = jnp.exp(s_n - m)
        l = jnp.sum(p_c, axis=-1, keepdims=True) + jnp.sum(p_n, axis=-1, keepdims=True)
        acc = (_dot_nt(p_c, c_ref[DIL_GW:, :]) + _dot(p_n, kvn[:, DIL_GW:])) / l
        o_ref[:, g * DIL_GW:(g + 1) * DIL_GW] = fold_heads(acc)
        l_ref[:, g * DIL_GW:(g + 1) * DIL_GW] = fold_heads(jnp.broadcast_to(m + jnp.log(l), acc.shape))
        hi, mid, lo = _split3(kvn)
        tail = (jnp.dot(selT, hi, preferred_element_type=F32) + jnp.dot(selT, mid, preferred_element_type=F32)
                + jnp.dot(selT, lo, preferred_element_type=F32)).T
        nxt = pltpu.roll(c_ref[:, 0:LANES], LANES - t_real, axis=1)
        for c in range(W // LANES):
            cur = nxt
            nxt = (pltpu.roll(c_ref[:, (c + 1) * LANES:(c + 2) * LANES], LANES - t_real, axis=1)
                   if (c + 1) * LANES < W else tail)
            oc_ref[:, c * LANES:(c + 1) * LANES] = jnp.where(keep, cur, nxt)


def _dil_sample(q, kvn, caches_t, tabc, tabn, t_real):
    B = q.shape[0]
    row = lambda n: pl.BlockSpec((None, SAMPLE_ROWS, n), lambda b: (b, 0, 0))
    cspecs = [pl.BlockSpec((None,) + c.shape[1:], lambda b: (b, 0, 0)) for c in caches_t]
    for g, (w, dil) in enumerate(DIL_GROUPS):
        assert caches_t[g].shape == (B, 2 * DIL_GW, w) and w // dil == TILE_Q
    out_spec = row(3 * DIL_GW)
    o, lse, *new_caches = pl.pallas_call(
        functools.partial(_dil_sample_kernel, t_real=t_real),
        grid=(B,),
        in_specs=([row(3 * DIL_GW)] + [row(2 * DIL_GW)] * 3 + cspecs + [_resident(t.shape) for t in tabc]
                  + [_resident(tabn.shape)]),
        out_specs=[out_spec, out_spec] + cspecs,
        out_shape=([jax.ShapeDtypeStruct((B, SAMPLE_ROWS, 3 * DIL_GW), F32)] * 2
                   + [jax.ShapeDtypeStruct(c.shape, F32) for c in caches_t]),
        compiler_params=_cparams(("parallel",)),
        name="dil_sample",
    )(q, *kvn, *caches_t, *tabc, tabn)
    o = o.reshape(B * SAMPLE_ROWS, 3 * DIL_GW)
    lse = lse.reshape(B * SAMPLE_ROWS, 3 * DIL_GW)
    return ([o[:, g * DIL_GW:(g + 1) * DIL_GW] for g in range(3)],
            [lse[:, g * DIL_GW:(g + 1) * DIL_GW] for g in range(3)], new_caches)


GDN_HPS = 4
GDN_SW = GDN_HPS * GDN_HEAD_DIM
CONV_PAD = SUBLANES


def _gdn_kernel(q_ref, k_ref, v_ref, bq_ref, bk_ref, bv_ref, wq_ref, wk_ref, wv_ref, ba_ref, alog_ref, dtb_ref,
                z_ref, gn_ref, s0_ref, o_ref, s_ref, xq_s, xk_s, xv_s, *, t_real):
    T = q_ref.shape[0]
    hg = pl.program_id(1)
    C = GDN_CHUNK
    D = GDN_HEAD_DIM
    ri = lax.broadcasted_iota(jnp.int32, (TILE_Q, TILE_Q), 0)
    ci = lax.broadcasted_iota(jnp.int32, (TILE_Q, TILE_Q), 1)
    same = (ri // C) == (ci // C)
    incl = same & (ri >= ci)
    strict = same & (ri > ci)
    eye = (ri == ci).astype(F32)
    tri = incl.astype(BF16)
    lane, row = ci, ri
    gain = gn_ref[...]
    zpad = jnp.zeros((C, D), F32)

    HH = range(GDN_HPS)
    hsl = [slice(hh * D, (hh + 1) * D) for hh in HH]

    def conv(x_s, hh, base, w_ref):
        y = x_s[hh, pl.ds(base, TILE_Q), :] * w_ref[0:1, hsl[hh]]
        for j in range(1, 4):
            y = y + x_s[hh, pl.ds(base + j, TILE_Q), :] * w_ref[j:j + 1, hsl[hh]]
        return _silu(y)

    def l2n(x):
        return x * lax.rsqrt(jnp.sum(x * x, axis=-1, keepdims=True) + EPS)

    def tile(i, S):
        r0 = pl.multiple_of(i * TILE_Q, TILE_Q)
        for x_s, x_ref, b_ref in ((xq_s, q_ref, bq_ref), (xk_s, k_ref, bk_ref), (xv_s, v_ref, bv_ref)):
            @pl.when(i == 0)
            def _():
                for hh in HH:
                    x_s[hh, CONV_PAD - 3:CONV_PAD, :] = b_ref[:, hsl[hh]]

            @pl.when(i > 0)
            def _():
                for hh in HH:
                    x_s[hh, CONV_PAD - 3:CONV_PAD, :] = x_s[hh, CONV_PAD + TILE_Q - 3:CONV_PAD + TILE_Q, :]

            for hh in HH:
                x_s[hh, CONV_PAD:CONV_PAD + TILE_Q, :] = x_ref[pl.ds(r0, TILE_Q), hsl[hh]]
        base = CONV_PAD - 3 + jnp.minimum(i, 0)
        q4 = [conv(xq_s, hh, base, wq_ref) for hh in HH]
        k4 = [conv(xk_s, hh, base, wk_ref) for hh in HH]
        v4 = [conv(xv_s, hh, base, wv_ref) for hh in HH]
        ba = ba_ref[pl.ds(r0, TILE_Q), :]
        live = (row + r0) < t_real
        beta_all = jnp.where(live, _sigmoid(ba), 0.0)
        g_all = jnp.where(live, -jnp.exp(alog_ref[...]) * _softplus(ba + dtb_ref[...]), 0.0)
        gh, gm, gl = _split3(g_all)
        G_all = (jnp.dot(tri, gh, preferred_element_type=F32) + jnp.dot(tri, gm, preferred_element_type=F32)
                 + jnp.dot(tri, gl, preferred_element_type=F32))
        head = [hg * GDN_HPS + hh for hh in HH]
        q = [l2n(q4[hh]) * (D ** -0.5) for hh in HH]
        k = [l2n(k4[hh]) for hh in HH]
        bc = [jnp.sum(jnp.where(lane == head[hh], beta_all, 0.0), axis=-1, keepdims=True) for hh in HH]
        Gc = [jnp.broadcast_to(jnp.sum(jnp.where(lane == head[hh] + GDN_HEADS, G_all, 0.0), axis=-1, keepdims=True),
                               (TILE_Q, TILE_Q)) for hh in HH]
        gamma = [jnp.exp(jnp.where(incl, Gc[hh] - Gc[hh].T, NEG)) for hh in HH]
        kk = [_dot_nt(k[hh], k[hh]) for hh in HH]
        qk = [_dot_nt(q[hh], k[hh]) for hh in HH]
        X = [jnp.where(strict, bc[hh] * kk[hh] * gamma[hh], 0.0) for hh in HH]
        P = [eye - X[hh] for hh in HH]
        for _ in range(int(math.log2(C)) - 1):
            X = [_dot(X[hh], X[hh]) for hh in HH]
            P = [P[hh] + _dot(P[hh], X[hh]) for hh in HH]
        eG = [jnp.exp(Gc[hh]) for hh in HH]
        rhs = [jnp.concatenate([v4[hh] * bc[hh], k[hh] * (bc[hh] * eG[hh])], axis=-1) for hh in HH]
        sol = [rhs[hh] + _dot(P[hh] - eye, rhs[hh]) for hh in HH]
        a_in = [qk[hh] * gamma[hh] for hh in HH]
        q_dec = [q[hh] * eG[hh] for hh in HH]
        kdT = [(k[hh] * jnp.exp(jnp.where(row < C, Gc[hh][C - 1:C, :], Gc[hh][2 * C - 1:2 * C, :]) - Gc[hh])).T
               for hh in HH]
        S = list(S)
        oq, vn = [[] for _ in HH], [[] for _ in HH]
        for c in range(TILE_Q // C):
            cs = slice(c * C, (c + 1) * C)
            r = [_dot(jnp.concatenate([sol[hh][cs, D:], q_dec[hh][cs]], axis=0), S[hh]) for hh in HH]
            for hh in HH:
                v_new = sol[hh][cs, :D] - r[hh][:C]
                oq[hh].append(r[hh][C:])
                vn[hh].append(v_new)
            vpad = [jnp.concatenate([vn[hh][c], zpad] if c == 0 else [zpad, vn[hh][c]], axis=0) for hh in HH]
            S = [S[hh] * jnp.exp(Gc[hh][(c + 1) * C - 1:(c + 1) * C, :]) + _dot(kdT[hh], vpad[hh]) for hh in HH]
        o = [jnp.concatenate(oq[hh], axis=0) + _dot(a_in[hh], jnp.concatenate(vn[hh], axis=0)) for hh in HH]
        outs = [_rms(o[hh], gain) * _silu(z_ref[pl.ds(r0, TILE_Q), hsl[hh]]) for hh in HH]
        o_ref[pl.ds(r0, TILE_Q), :] = jnp.concatenate(outs, axis=-1).astype(o_ref.dtype)
        return tuple(S)

    S = lax.fori_loop(0, T // TILE_Q, tile, tuple(s0_ref[hh] for hh in HH))
    for hh in HH:
        s_ref[hh] = S[hh]


def _gdn(gq, ba, z, conv_buf, s0, w_conv, a_log, dt_bias, norm_out, t_real):
    B, T, _ = gq.shape
    H = GDN_HEADS
    pad16 = lambda x: jnp.concatenate([jnp.zeros((H,), F32), x.astype(F32), jnp.zeros((LANES - 2 * H,), F32)])
    ng = H // GDN_HPS
    col = lambda off: pl.BlockSpec((None, T, GDN_SW), lambda b, g: (b, 0, off * ng + g))
    bcol = lambda off: pl.BlockSpec((None, 3, GDN_SW), lambda b, g: (b, 0, off * ng + g))
    wcol = lambda off: pl.BlockSpec((4, GDN_SW), lambda b, g: (0, off * ng + g))
    sblk = pl.BlockSpec((None, GDN_HPS, GDN_HEAD_DIM, GDN_HEAD_DIM), lambda b, g: (b, g, 0, 0))
    o, s_new = pl.pallas_call(
        functools.partial(_gdn_kernel, t_real=t_real),
        grid=(B, ng),
        in_specs=[col(0), col(1), col(2), bcol(0), bcol(1), bcol(2), wcol(0), wcol(1), wcol(2),
                  pl.BlockSpec((None, T, LANES), lambda b, g: (b, 0, 0)),
                  _resident((1, LANES)), _resident((1, LANES)),
                  col(0), _resident((1, LANES)), sblk],
        out_specs=[col(0), sblk],
        out_shape=[jax.ShapeDtypeStruct((B, T, GDN_W), BF16),
                   jax.ShapeDtypeStruct((B, H, GDN_HEAD_DIM, GDN_HEAD_DIM), F32)],
        scratch_shapes=[pltpu.VMEM((GDN_HPS, TILE_Q + CONV_PAD, LANES), F32)] * 3,
        compiler_params=_cparams(("parallel", "parallel")),
        name="gdn",
    )(gq, gq, gq, conv_buf, conv_buf, conv_buf, w_conv, w_conv, w_conv, ba,
      pad16(a_log).reshape(1, LANES), pad16(dt_bias).reshape(1, LANES), z, norm_out.reshape(1, LANES), s0)
    return o, s_new


def _mix_kernel(o0_ref, o1_ref, o2_ref, l0_ref, l1_ref, l2_ref, ob_ref, ga_ref, gb_ref, h_ref,
                wa_ref, wb_ref, wo_ref, gq_ref, wq_ref, h1_ref, qm_ref):
    l0, l1, l2 = l0_ref[...], l1_ref[...], l2_ref[...]
    mx = jnp.maximum(jnp.maximum(l0, l1), l2)
    e0, e1, e2 = jnp.exp(l0 - mx), jnp.exp(l1 - mx), jnp.exp(l2 - mx)
    o_a = (e0 * o0_ref[...] + e1 * o1_ref[...] + e2 * o2_ref[...]) / (e0 + e1 + e2)
    a = _dot(o_a, wa_ref[...])
    b = jnp.dot(ob_ref[...], wb_ref[...], preferred_element_type=F32)
    merged = _sigmoid(ga_ref[...]) * a + _sigmoid(gb_ref[...]) * b
    h1 = h_ref[...] + _dot(merged, wo_ref[...])
    h1_ref[...] = h1
    qm_ref[...] = _dot(_rms(h1, gq_ref[...]), wq_ref[...]).astype(qm_ref.dtype)


def _mix(o_g, l_g, o_b, ga, gb, h, w_a, w_b, w_o, norm_mem_q, w_mem_q, tm):
    rows, d = h.shape
    rt = lambda n: pl.BlockSpec((tm, n), lambda i: (i, 0))
    return pl.pallas_call(
        _mix_kernel,
        grid=(rows // tm,),
        in_specs=[rt(DIL_GW)] * 6 + [rt(GDN_W), rt(d), rt(d), rt(d),
                                     _resident(w_a.shape), _resident(w_b.shape), _resident(w_o.shape),
                                     _resident((1, d)), _resident(w_mem_q.shape)],
        out_specs=[rt(d), rt(w_mem_q.shape[1])],
        out_shape=[jax.ShapeDtypeStruct((rows, d), F32), jax.ShapeDtypeStruct((rows, w_mem_q.shape[1]), BF16)],
        compiler_params=_cparams(("parallel",)),
        name="mix",
    )(*o_g, *l_g, o_b, ga, gb, h, w_a, w_b, w_o, norm_mem_q.reshape(1, d), w_mem_q)


def _mem_kv_kernel(x_ref, g_ref, w_ref, k_ref, v_ref):
    u = _rms(x_ref[...], g_ref[...]).astype(BF16)
    n = k_ref.shape[-1]
    k_ref[...] = jnp.dot(u, w_ref[:, :n], preferred_element_type=F32)
    v_ref[...] = jnp.dot(u, w_ref[:, n:], preferred_element_type=F32)


def _mem_kv(mem2d, gain, w, tm):
    rows, d = mem2d.shape
    n = w.shape[1] // 2
    return pl.pallas_call(
        _mem_kv_kernel,
        grid=(rows // tm,),
        in_specs=[pl.BlockSpec((tm, d), lambda i: (i, 0)), _resident((1, d)), _resident(w.shape)],
        out_specs=[pl.BlockSpec((tm, n), lambda i: (i, 0))] * 2,
        out_shape=[jax.ShapeDtypeStruct((rows, n), F32)] * 2,
        compiler_params=_cparams(("parallel",)),
        name="mem_kv",
    )(mem2d, gain.reshape(1, d), w)


def _mem_attn_kernel(q_ref, k_ref, v_ref, o_ref):
    q = q_ref[...]
    k = k_ref[...].astype(BF16)
    v = v_ref[...].astype(BF16)
    outs = []
    for h in range(MEM_HEADS):
        sl = slice(h * MEM_HEAD_DIM, (h + 1) * MEM_HEAD_DIM)
        s = _dot_nt(q[:, sl], k[:, sl]) * (MEM_HEAD_DIM ** -0.5)
        p = jnp.exp(s - jnp.max(s, axis=-1, keepdims=True))
        outs.append(_dot(p, v[:, sl]) / jnp.sum(p, axis=-1, keepdims=True))
    o_ref[...] = jnp.concatenate(outs, axis=-1).astype(o_ref.dtype)


def _mem_attn(qm, mem_k, mem_v, tm):
    B, T, w = qm.shape
    M = mem_k.shape[1]
    return pl.pallas_call(
        _mem_attn_kernel,
        grid=(B, T // tm),
        in_specs=[pl.BlockSpec((None, tm, w), lambda b, j: (b, j, 0)),
                  pl.BlockSpec((None, M, w), lambda b, j: (b, 0, 0)),
                  pl.BlockSpec((None, M, w), lambda b, j: (b, 0, 0))],
        out_specs=pl.BlockSpec((None, tm, w), lambda b, j: (b, j, 0)),
        out_shape=jax.ShapeDtypeStruct((B, T, w), F32),
        compiler_params=_cparams(("parallel", "parallel")),
        name="mem_attn",
    )(qm, mem_k, mem_v)


def _ffn_kernel(*refs, inject, emit_gate):
    if inject:
        (h1_ref, om_ref, init_ref, fill_ref, wmo_ref, gf_ref, wup_ref, wc_ref, bc_ref, wd_ref, gfin_ref,
         y_ref, fc_ref, gs) = refs
    else:
        (h1_ref, om_ref, init_ref, wmo_ref, gf_ref, wup_ref, wc_ref, bc_ref, wd_ref, gfin_ref,
         y_ref, fc_ref, gs) = refs
    tm = h1_ref.shape[0]
    F = wd_ref.shape[0]
    PAD = SUBLANES

    @pl.when(pl.program_id(1) == 0)
    def _():
        gs[PAD - 2:PAD, :] = init_ref[...]

    h2 = h1_ref[...] + _dot(om_ref[...], wmo_ref[...])
    n = _rms(h2, gf_ref[...]).astype(BF16)
    gate = jnp.dot(n, wup_ref[:, :F], preferred_element_type=F32)
    if inject:
        r = lax.broadcasted_iota(jnp.int32, (tm, 1), 0)
        gate = jnp.where((r % SAMPLE_ROWS) >= SAMPLE_ROWS - 2, fill_ref[...], gate)
    gs[PAD:PAD + tm, :] = gate
    conv = (gs[PAD - 2:PAD - 2 + tm, :] * wc_ref[0:1, :] + gs[PAD - 1:PAD - 1 + tm, :] * wc_ref[1:2, :]
            + gate * wc_ref[2:3, :])
    last2 = gs[PAD + tm - 2:PAD + tm, :]
    gs[PAD - 2:PAD, :] = last2
    if emit_gate:
        fc_ref[...] = gate
    else:
        fc_ref[...] = last2
    up = jnp.dot(n, wup_ref[:, F:], preferred_element_type=F32)
    act = _silu(conv + bc_ref[...]) * up
    y = h2 + _dot(act, wd_ref[...])
    y_ref[...] = _rms(y, gfin_ref[...])


def _ffn(h1, om, init, fill, w_mo, norm_ffn, w_up, w_conv, b_conv, w_down, norm_final, tm, emit_gate):
    B, T, d = h1.shape
    F = w_down.shape[0]
    inject = fill is not None
    rt = lambda n: pl.BlockSpec((None, tm, n), lambda b, j: (b, j, 0))
    in_specs = [rt(d), rt(om.shape[-1]), pl.BlockSpec((None, 2, F), lambda b, j: (b, 0, 0))]
    args = [h1, om, init]
    if inject:
        in_specs.append(rt(F))
        args.append(fill)
    in_specs += [_resident(w_mo.shape), _resident((1, d)), _resident(w_up.shape), _resident(w_conv.shape),
                 _resident((1, F)), _resident(w_down.shape), _resident((1, d))]
    args += [w_mo, norm_ffn.reshape(1, d), w_up, w_conv, b_conv.reshape(1, F), w_down, norm_final.reshape(1, d)]
    if emit_gate:
        fc_spec, fc_shape = rt(F), jax.ShapeDtypeStruct((B, T, F), F32)
    else:
        fc_spec = pl.BlockSpec((None, 2, F), lambda b, j: (b, 0, 0))
        fc_shape = jax.ShapeDtypeStruct((B, 2, F), F32)
    return pl.pallas_call(
        functools.partial(_ffn_kernel, inject=inject, emit_gate=emit_gate),
        grid=(B, T // tm),
        in_specs=in_specs,
        out_specs=[rt(d), fc_spec],
        out_shape=[jax.ShapeDtypeStruct((B, T, d), F32), fc_shape],
        scratch_shapes=[pltpu.VMEM((tm + SUBLANES, F), F32)],
        compiler_params=_cparams(("parallel", "arbitrary")),
        name="ffn",
    )(*args)


def kernel(x_prompt, x_sample, cache_dil0_kv, cache_dil1_kv, cache_dil2_kv, state_delta, state_delta_conv, cache_mem_k, cache_mem_v, state_ffn_conv, mem_prompt, rel_bias, norm_mix, w_in, w_conv_delta, a_log, dt_bias, norm_delta_out, w_branch_a, w_branch_b, w_out, norm_mem_q, norm_mem_kv, w_mem_q, w_mem_kv, w_mem_o, norm_ffn, w_ffn_up, w_ffn_conv, b_ffn_conv, w_ffn_down, norm_final):
    B, S, D = x_prompt.shape
    Bs, Ts, _ = x_sample.shape
    depth = w_in.shape[0]
    assert depth == 1 and Ts <= SAMPLE_ROWS - 2 and Ts >= 3 and S % (16 * TILE_Q) == 0
    F = w_ffn_down.shape[1]
    M = mem_prompt.shape[1]
    l = 0
    w_arr = _arrange_w_in(w_in[l])
    w_a, w_b, w_o = (w.astype(BF16) for w in (w_branch_a[l], w_branch_b[l], w_out[l]))
    w_mq, w_mkv, w_mo = (w.astype(BF16) for w in (w_mem_q[l], w_mem_kv[l], w_mem_o[l]))
    w_up, w_dn = w_ffn_up[l].astype(BF16), w_ffn_down[l].astype(BF16)
    t_cur, t_prev = _prompt_bias_tables(rel_bias)

    xp = x_prompt.reshape(B * S, D)
    q, kv0, kv1, kv2, kt0, kt1, kt2, gq, z, ba, ga, gb = _in_proj(xp, norm_mix[l], w_arr, 256, seq=S)
    kvs = [kv.reshape(B, S, 2 * DIL_GW) for kv in (kv0, kv1, kv2)]
    q3 = q.reshape(B, S, 3 * DIL_GW)
    o_g, l_g = zip(*[_dil_prompt(q3, kvs[g], g, t_cur[g], t_prev[g]) for g in range(3)])
    o_b, delta_p = _gdn(gq.reshape(B, S, -1), ba.reshape(B, S, LANES), z.reshape(B, S, GDN_W),
                        jnp.zeros((B, 3, 3 * GDN_W), F32), jnp.zeros((B, GDN_HEADS, GDN_HEAD_DIM, GDN_HEAD_DIM), F32),
                        w_conv_delta[l], a_log[l], dt_bias[l], norm_delta_out[l], S)
    h1, qm = _mix(o_g, l_g, o_b.reshape(B * S, GDN_W), ga, gb, xp, w_a, w_b, w_o, norm_mem_q[l], w_mq, 256)
    mk_p, mv_p = _mem_kv(mem_prompt.reshape(B * M, D), norm_mem_kv[l], w_mkv, 256)
    mk_p, mv_p = mk_p.reshape(B, M, -1), mv_p.reshape(B, M, -1)
    om = _mem_attn(qm.reshape(B, S, -1), mk_p, mv_p, 512)
    y_p, fconv_p = _ffn(h1.reshape(B, S, D), om, jnp.zeros((B, 2, F), F32), None, w_mo, norm_ffn[l], w_up,
                        w_ffn_conv[l], b_ffn_conv[l], w_dn, norm_final, 256, False)
    gq3 = gq.reshape(B, S, -1)
    p_out = ([kt[:, :, S - min(w, S):].reshape(B, 2, DIL_HPG, DIL_HEAD_DIM, min(w, S)).transpose(0, 4, 1, 2, 3)[None]
              for kt, (w, _) in zip((kt0, kt1, kt2), DIL_GROUPS)]
             + [delta_p[None], gq3[:, S - 3:][None], mk_p.reshape(1, B, M, MEM_HEADS, MEM_HEAD_DIM),
                mv_p.reshape(1, B, M, MEM_HEADS, MEM_HEAD_DIM), fconv_p[None]])

    R = SAMPLE_ROWS
    xs = jnp.pad(x_sample, ((0, 0), (0, R - Ts), (0, 0))).reshape(Bs * R, D)
    q, kv0, kv1, kv2, gq, z, ba, ga, gb = _in_proj(xs, norm_mix[l], w_arr, Bs * R)
    kvn = [kv.reshape(Bs, R, 2 * DIL_GW) for kv in (kv0, kv1, kv2)]
    caches_t = [jnp.transpose(c[l], (0, 2, 3, 4, 1)).reshape(Bs, 2 * DIL_GW, c.shape[2])
                for c in (cache_dil0_kv, cache_dil1_kv, cache_dil2_kv)]
    tabc, tabn = _sample_bias_tables(rel_bias, Ts)
    o_g, l_g, new_caches = _dil_sample(q.astype(F32).reshape(Bs, R, 3 * DIL_GW), kvn, caches_t, tabc, tabn, Ts)
    padt = lambda a: jnp.pad(a.reshape(Bs, R, -1), ((0, 0), (0, TILE_Q - R), (0, 0)))
    o_b, delta_s = _gdn(padt(gq), padt(ba), padt(z), state_delta_conv[l], state_delta[l],
                        w_conv_delta[l], a_log[l], dt_bias[l], norm_delta_out[l], Ts)
    o_b = o_b[:, :R].reshape(Bs * R, GDN_W)
    h1, qm = _mix(o_g, l_g, o_b, ga, gb, xs, w_a, w_b, w_o, norm_mem_q[l], w_mq, Bs * R)
    om = _mem_attn(qm.reshape(Bs, R, -1), cache_mem_k[l].reshape(Bs, M, -1), cache_mem_v[l].reshape(Bs, M, -1), R)
    fst = state_ffn_conv[l]
    fill = jnp.concatenate([jnp.zeros((Bs, R - 2, F), F32),
                            jnp.concatenate([fst[1:], jnp.zeros((1, 2, F), F32)], axis=0)], axis=1)
    y_s, gate_s = _ffn(h1.reshape(1, Bs * R, D), om.reshape(1, Bs * R, -1), fst[:1], fill.reshape(1, Bs * R, F),
                       w_mo, norm_ffn[l], w_up, w_ffn_conv[l], b_ffn_conv[l], w_dn, norm_final, Bs * R, True)
    y_s = y_s.reshape(Bs, R, D)[:, :Ts]
    gq3 = gq.reshape(Bs, R, -1)
    s_out = ([nc.reshape(Bs, 2, DIL_HPG, DIL_HEAD_DIM, nc.shape[2]).transpose(0, 4, 1, 2, 3)[None]
              for nc in new_caches]
             + [delta_s[None], gq3[:, Ts - 3:Ts][None], gate_s.reshape(Bs, R, F)[:, Ts - 2:Ts][None]])

    return (y_p.reshape(B, S, D), y_s, *p_out, *s_out)
```

```python
import functools
import math

import jax
import jax.numpy as jnp
import numpy as np
from jax import lax
from jax.experimental import pallas as pl
from jax.experimental.pallas import tpu as pltpu

F32 = jnp.float32
BF16 = jnp.bfloat16

PAST_LEN = 8192
DIL_GROUPS = ((128, 1), (512, 4), (2048, 16))
DIL_HPG = 4
DIL_HEAD_DIM = 64
DIL_GW = DIL_HPG * DIL_HEAD_DIM
DIL_NK = 129
REL_BUCKETS = 32
REL_MAX_DIST = 2048
GDN_HEADS = 8
GDN_HEAD_DIM = 128
GDN_W = GDN_HEADS * GDN_HEAD_DIM
GDN_CHUNK = 64
MEM_HEADS = 4
MEM_HEAD_DIM = 128
EPS = 1e-6
NEG = -1e30

LANES = 128
SUBLANES = 8
TILE_Q = 128
SAMPLE_ROWS = SUBLANES
VMEM_LIMIT = 56 * 1024 * 1024


def _cparams(sem):
    return pltpu.CompilerParams(dimension_semantics=sem, vmem_limit_bytes=VMEM_LIMIT)


def _resident(shape):
    nd = len(shape)
    return pl.BlockSpec(shape, lambda *_: (0,) * nd, pipeline_mode=pl.Buffered(1))


def _rms(x, gain_row):
    return x * lax.rsqrt(jnp.mean(x * x, axis=-1, keepdims=True) + EPS) * gain_row


def _dot(a, b):
    return jnp.dot(a.astype(BF16), b.astype(BF16), preferred_element_type=F32)


def _dot_nt(a, b):
    return lax.dot_general(a.astype(BF16), b.astype(BF16), (((1,), (1,)), ((), ())), preferred_element_type=F32)


def _dot_tn(a, b):
    return lax.dot_general(a.astype(BF16), b.astype(BF16), (((0,), (0,)), ((), ())), preferred_element_type=F32)


def _split3(x):
    hi = x.astype(BF16)
    r1 = x - hi.astype(F32)
    mid = r1.astype(BF16)
    lo = (r1 - mid.astype(F32)).astype(BF16)
    return hi, mid, lo


def _dot_hp(a, b):
    ah, am, al = _split3(a)
    bh, bm, bl = _split3(b)
    d = functools.partial(jnp.dot, preferred_element_type=F32)
    return d(ah, bh) + (d(ah, bm) + d(am, bh)) + (d(am, bm) + d(ah, bl) + d(al, bh))


def _sigmoid(x):
    return 1.0 / (1.0 + jnp.exp(-x))


def _silu(x):
    return x * _sigmoid(x)


def _softplus(x):
    return jnp.maximum(x, 0.0) + jnp.log(1.0 + jnp.exp(-jnp.abs(x)))


IN_SEGS = (("q", 3 * DIL_GW), ("kv0", 2 * DIL_GW), ("kv1", 2 * DIL_GW), ("kv2", 2 * DIL_GW),
           ("gq", 3 * GDN_W), ("z", GDN_W), ("ba", LANES), ("ga", 1024), ("gb", 1024))


def _arrange_w_in(w_in):
    o = 0
    qa = w_in[:, o:o + 768]; o += 768
    ka = w_in[:, o:o + 768]; o += 768
    va = w_in[:, o:o + 768]; o += 768
    gq = w_in[:, o:o + 3 * GDN_W]; o += 3 * GDN_W
    z = w_in[:, o:o + GDN_W]; o += GDN_W
    beta = w_in[:, o:o + GDN_HEADS]; o += GDN_HEADS
    a = w_in[:, o:o + GDN_HEADS]; o += GDN_HEADS
    ga = w_in[:, o:o + 1024]; o += 1024
    gb = w_in[:, o:o + 1024]
    kv = [jnp.concatenate([ka[:, g * DIL_GW:(g + 1) * DIL_GW], va[:, g * DIL_GW:(g + 1) * DIL_GW]], axis=1)
          for g in range(3)]
    ba = jnp.concatenate([beta, a, jnp.zeros((w_in.shape[0], LANES - 2 * GDN_HEADS), w_in.dtype)], axis=1)
    return jnp.concatenate([qa] + kv + [gq, z, ba, ga, gb], axis=1).astype(BF16)


def _in_proj_kernel(x_ref, g_ref, w_ref, *out_refs, kv_t):
    if kv_t:
        q_ref, kv0_ref, kv1_ref, kv2_ref, kt0_ref, kt1_ref, kt2_ref, gq_ref, z_ref, ba_ref, ga_ref, gb_ref = out_refs
        kt_refs = (kt0_ref, kt1_ref, kt2_ref)
    else:
        q_ref, kv0_ref, kv1_ref, kv2_ref, gq_ref, z_ref, ba_ref, ga_ref, gb_ref = out_refs
    u = _rms(x_ref[...], g_ref[...]).astype(BF16)
    off = 0

    def seg(n):
        nonlocal off
        r = jnp.dot(u, w_ref[:, off:off + n], preferred_element_type=F32)
        off += n
        return r

    q_ref[...] = (seg(3 * DIL_GW) * (DIL_HEAD_DIM ** -0.5)).astype(q_ref.dtype)
    for g, kv_ref in enumerate((kv0_ref, kv1_ref, kv2_ref)):
        kv = seg(2 * DIL_GW)
        kv_ref[...] = kv.astype(kv_ref.dtype)
        if kv_t:
            kt_refs[g][...] = kv.T
    for c in range(3):
        gq_ref[:, c * GDN_W:(c + 1) * GDN_W] = seg(GDN_W)
    z_ref[...] = seg(GDN_W)
    ba_ref[...] = seg(LANES)
    ga_ref[...] = seg(1024)
    gb_ref[...] = seg(1024)


def _in_proj(x2d, gain, w_arr, tm, seq=None):
    rows, d = x2d.shape
    kv_t = seq is not None
    names = [n for n, _ in IN_SEGS]
    widths = dict(IN_SEGS)
    row_spec = lambda n: pl.BlockSpec((tm, n), lambda i: (i, 0))
    out_specs, out_shape = [], []
    for n in names:
        out_specs.append(row_spec(widths[n]))
        out_shape.append(jax.ShapeDtypeStruct((rows, widths[n]), F32))
        if n == "kv2" and kv_t:
            nt = seq // tm
            for _ in range(3):
                out_specs.append(pl.BlockSpec((None, 2 * DIL_GW, tm), lambda i: (i // nt, 0, i % nt)))
                out_shape.append(jax.ShapeDtypeStruct((rows // seq, 2 * DIL_GW, seq), F32))
    return pl.pallas_call(
        functools.partial(_in_proj_kernel, kv_t=kv_t),
        grid=(rows // tm,),
        in_specs=[pl.BlockSpec((tm, d), lambda i: (i, 0)), _resident((1, d)), _resident(w_arr.shape)],
        out_specs=out_specs,
        out_shape=out_shape,
        compiler_params=_cparams(("parallel",)),
        name="in_proj",
    )(x2d, gain.reshape(1, d), w_arr)


def _rel_bucket(dist):
    exact = REL_BUCKETS // 2
    d = jnp.maximum(dist, 1).astype(F32)
    large = exact + (jnp.log(d / exact) / math.log(REL_MAX_DIST / exact) * (REL_BUCKETS - exact)).astype(jnp.int32)
    return jnp.where(dist < exact, dist, jnp.minimum(large, REL_BUCKETS - 1))


def _group_bias(rel_bias, g):
    dil = DIL_GROUPS[g][1]
    dist = dil * jnp.arange(DIL_NK, dtype=jnp.int32)
    tab = rel_bias[_rel_bucket(dist)]
    return tab[:, g * DIL_HPG:(g + 1) * DIL_HPG].T.astype(F32)


def _toeplitz(v, n, width):
    h, L = v.shape
    return jnp.tile(v, (1, n))[:, :n * (L - 1)].reshape(h, n, L - 1)[:, :, :width]


def _prompt_bias_tables(rel_bias):
    cat, cur = [], []
    for g in range(3):
        bg = _group_bias(rel_bias, g)
        v = jnp.concatenate([bg[:, ::-1], jnp.full((DIL_HPG, 3 * TILE_Q - DIL_NK), NEG, F32)], axis=1)
        t = _toeplitz(v, TILE_Q, 2 * TILE_Q)
        cat.append(t)
        cur.append(t[:, :, TILE_Q:])
    return cat, cur


DIL_TIF = 2
DIL_SLABS = DIL_GW // LANES


def _dil_prompt_kernel(q0_ref, q1_ref, k0_ref, k1_ref, v0_ref, v1_ref, tcat_ref, tcur_ref,
                       o0_ref, o1_ref, l0_ref, l1_ref, *, dil):
    S = q0_ref.shape[0]
    nb = S // dil // TILE_Q
    q_refs, k_refs, v_refs = (q0_ref, q1_ref), (k0_ref, k1_ref), (v0_ref, v1_ref)
    o_refs, l_refs = (o0_ref, o1_ref), (l0_ref, l1_ref)
    even = lax.broadcasted_iota(jnp.int32, (TILE_Q, LANES), 1) < DIL_HEAD_DIM

    def rows(r, t):
        start = r + dil * TILE_Q * t
        return pl.ds(start, TILE_Q, stride=dil) if dil > 1 else pl.ds(start, TILE_Q)

    tiles = [(r, t) for r in range(dil) for t in range(nb)]
    for i0 in range(0, len(tiles), DIL_TIF):
        grp = tiles[i0:i0 + DIL_TIF]
        qm, kc, vc = {}, {}, {}
        for ti, (r, t) in enumerate(grp):
            for sl in range(DIL_SLABS):
                qf = q_refs[sl][rows(r, t), :]
                qm[ti, 2 * sl] = jnp.where(even, qf, 0.0).astype(BF16)
                qm[ti, 2 * sl + 1] = jnp.where(even, 0.0, qf).astype(BF16)
                kc[ti, sl] = k_refs[sl][rows(r, t), :].astype(BF16)
                vc[ti, sl] = v_refs[sl][rows(r, t), :].astype(BF16)
                if t > 0:
                    kc[ti, sl] = jnp.concatenate([k_refs[sl][rows(r, t - 1), :].astype(BF16), kc[ti, sl]], axis=0)
                    vc[ti, sl] = jnp.concatenate([v_refs[sl][rows(r, t - 1), :].astype(BF16), vc[ti, sl]], axis=0)
        units = [(ti, h) for ti in range(len(grp)) for h in range(DIL_HPG)]
        s = [_dot_nt(qm[ti, h], kc[ti, h // 2]) + (tcat_ref[h] if grp[ti][1] > 0 else tcur_ref[h]) for ti, h in units]
        m = [jnp.max(x, axis=-1, keepdims=True) for x in s]
        p = [jnp.exp(x - mx) for x, mx in zip(s, m)]
        l = [jnp.sum(x, axis=-1, keepdims=True) for x in p]
        pv = [jnp.dot(p[u].astype(BF16), vc[ti, h // 2], preferred_element_type=F32) for u, (ti, h) in enumerate(units)]
        o = [pv[u] / l[u] for u in range(len(units))]
        lse = [m[u] + jnp.log(l[u]) for u in range(len(units))]
        for ti, (r, t) in enumerate(grp):
            for sl in range(DIL_SLABS):
                ue, uo = ti * DIL_HPG + 2 * sl, ti * DIL_HPG + 2 * sl + 1
                o_refs[sl][rows(r, t), :] = jnp.where(even, o[ue], o[uo])
                l_refs[sl][rows(r, t), :] = jnp.where(even, lse[ue], lse[uo])


def _dil_prompt(q, kv, g, t_cat, t_cur):
    B, S, _ = q.shape
    dil = DIL_GROUPS[g][1]
    slab = lambda c: pl.BlockSpec((None, S, LANES), lambda b: (b, 0, c))
    nq, nk = g * DIL_SLABS, 0
    outs = pl.pallas_call(
        functools.partial(_dil_prompt_kernel, dil=dil),
        grid=(B,),
        in_specs=[slab(nq), slab(nq + 1), slab(nk), slab(nk + 1), slab(nk + 2), slab(nk + 3),
                  _resident(t_cat.shape), _resident(t_cur.shape)],
        out_specs=[slab(0)] * 4,
        out_shape=[jax.ShapeDtypeStruct((B, S, LANES), F32)] * 4,
        compiler_params=_cparams(("parallel",)),
        name=f"dil_prompt_g{g}",
    )(q, q, kv, kv, kv, kv, t_cat, t_cur)
    o0, o1, l0, l1 = (x.reshape(B * S, LANES) for x in outs)
    return [o0, o1], [l0, l1]


def _sample_bias_tables(rel_bias, t_real):
    R = SAMPLE_ROWS
    tabc, tabn = [], []
    t_i = np.arange(R)[:, None]
    u_i = np.arange(R)[None, :]
    for g, (w, dil) in enumerate(DIL_GROUPS):
        bg = _group_bias(rel_bias, g)
        base = bg[:, ::-1][:, :TILE_Q]
        t0 = jnp.concatenate([base[:, :, None], jnp.full((DIL_HPG, TILE_Q, dil - 1), NEG, F32)], axis=2)
        t0 = t0.reshape(DIL_HPG, w)
        rows = [jnp.concatenate([jnp.full((DIL_HPG, t), NEG, F32), t0[:, :w - t]], axis=1) for t in range(t_real)]
        rows += [jnp.zeros((DIL_HPG, w), F32)] * (R - t_real)
        tabc.append(jnp.stack(rows, axis=1).reshape(DIL_HPG * R, w))
        tn = jnp.full((DIL_HPG, R, R), NEG, F32)
        for j in range(-(-t_real // dil)):
            hit = (t_i - u_i == j * dil) & (t_i < t_real)
            tn = jnp.where(hit[None], bg[:, j][:, None, None], tn)
        tn = jnp.where((t_i >= t_real)[None], 0.0, tn)
        tabn.append(tn.reshape(DIL_HPG * R, R))
    return tabc, jnp.stack(tabn)


def _dil_sample_kernel(q_ref, n0_ref, n1_ref, n2_ref, c0_ref, c1_ref, c2_ref, tc0_ref, tc1_ref, tc2_ref, tn_ref,
                       o_ref, l_ref, oc0_ref, oc1_ref, oc2_ref, *, t_real):
    R = SAMPLE_ROWS
    rows = lax.broadcasted_iota(jnp.int32, (DIL_HPG * R, DIL_GW), 0)
    lanes = lax.broadcasted_iota(jnp.int32, (DIL_HPG * R, DIL_GW), 1)
    head_mask = (lanes // DIL_HEAD_DIM) == (rows // R)
    lane_f = lax.broadcasted_iota(jnp.int32, (2 * DIL_GW, LANES), 1)
    keep = lane_f < LANES - t_real
    sel_l = lax.broadcasted_iota(jnp.int32, (LANES, R), 0)
    sel_u = lax.broadcasted_iota(jnp.int32, (LANES, R), 1)
    selT = ((sel_l == sel_u + LANES - t_real) & (sel_u < t_real)).astype(BF16)
    groups = ((n0_ref, c0_ref, tc0_ref, oc0_ref), (n1_ref, c1_ref, tc1_ref, oc1_ref), (n2_ref, c2_ref, tc2_ref, oc2_ref))

    def fold_heads(x):
        x = jnp.where(head_mask, x, 0.0)
        return x[0:R] + x[R:2 * R] + x[2 * R:3 * R] + x[3 * R:4 * R]

    for g, (n_ref, c_ref, tc_ref, oc_ref) in enumerate(groups):
        W = c_ref.shape[1]
        kvn = n_ref[...]
        q_g = q_ref[:, g * DIL_GW:(g + 1) * DIL_GW]
        q_bd = jnp.where(head_mask, jnp.concatenate([q_g] * DIL_HPG, axis=0), 0.0)
        s_c = _dot(q_bd, c_ref[:DIL_GW, :]) + tc_ref[...]
        s_n = _dot_nt(q_bd, kvn[:, :DIL_GW]) + tn_ref[g]
        m = jnp.maximum(jnp.max(s_c, axis=-1, keepdims=True), jnp.max(s_n, axis=-1, keepdims=True))
        p_c = jnp.exp(s_c - m)
        p_n = jnp.exp(s_n - m)
        l = jnp.sum(p_c, axis=-1, keepdims=True) + jnp.sum(p_n, axis=-1, keepdims=True)
        acc = (_dot_nt(p_c, c_ref[DIL_GW:, :]) + _dot(p_n, kvn[:, DIL_GW:])) / l
        o_ref[:, g * DIL_GW:(g + 1) * DIL_GW] = fold_heads(acc)
        l_ref[:, g * DIL_GW:(g + 1) * DIL_GW] = fold_heads(jnp.broadcast_to(m + jnp.log(l), acc.shape))
        hi, mid, lo = _split3(kvn)
        tail = (jnp.dot(selT, hi, preferred_element_type=F32) + jnp.dot(selT, mid, preferred_element_type=F32)
                + jnp.dot(selT, lo, preferred_element_type=F32)).T
        nxt = pltpu.roll(c_ref[:, 0:LANES], LANES - t_real, axis=1)
        for c in range(W // LANES):
            cur = nxt
            nxt = (pltpu.roll(c_ref[:, (c + 1) * LANES:(c + 2) * LANES], LANES - t_real, axis=1)
                   if (c + 1) * LANES < W else tail)
            oc_ref[:, c * LANES:(c + 1) * LANES] = jnp.where(keep, cur, nxt)


def _dil_sample(q, kvn, caches_t, tabc, tabn, t_real):
    B = q.shape[0]
    row = lambda n: pl.BlockSpec((None, SAMPLE_ROWS, n), lambda b: (b, 0, 0))
    cspecs = [pl.BlockSpec((None,) + c.shape[1:], lambda b: (b, 0, 0)) for c in caches_t]
    for g, (w, dil) in enumerate(DIL_GROUPS):
        assert caches_t[g].shape == (B, 2 * DIL_GW, w) and w // dil == TILE_Q
    out_spec = row(3 * DIL_GW)
    o, lse, *new_caches = pl.pallas_call(
        functools.partial(_dil_sample_kernel, t_real=t_real),
        grid=(B,),
        in_specs=([row(3 * DIL_GW)] + [row(2 * DIL_GW)] * 3 + cspecs + [_resident(t.shape) for t in tabc]
                  + [_resident(tabn.shape)]),
        out_specs=[out_spec, out_spec] + cspecs,
        out_shape=([jax.ShapeDtypeStruct((B, SAMPLE_ROWS, 3 * DIL_GW), F32)] * 2
                   + [jax.ShapeDtypeStruct(c.shape, F32) for c in caches_t]),
        compiler_params=_cparams(("parallel",)),
        name="dil_sample",
    )(q, *kvn, *caches_t, *tabc, tabn)
    o = o.reshape(B * SAMPLE_ROWS, 3 * DIL_GW)
    lse = lse.reshape(B * SAMPLE_ROWS, 3 * DIL_GW)
    n = 3 * DIL_SLABS
    return ([o[:, i * LANES:(i + 1) * LANES] for i in range(n)],
            [lse[:, i * LANES:(i + 1) * LANES] for i in range(n)], new_caches)


GDN_HPS = 4
GDN_SW = GDN_HPS * GDN_HEAD_DIM
CONV_PAD = SUBLANES
GDN_TPI = 4


def _gdn_kernel(q_ref, k_ref, v_ref, bq_ref, bk_ref, bv_ref, wq_ref, wk_ref, wv_ref, ba_ref, alog_ref, dtb_ref,
                z_ref, gn_ref, s0_ref, o_ref, s_ref, xq_s, xk_s, xv_s, *, t_real, tpi):
    T = q_ref.shape[0]
    hg = pl.program_id(1)
    C = GDN_CHUNK
    D = GDN_HEAD_DIM
    ri = lax.broadcasted_iota(jnp.int32, (TILE_Q, TILE_Q), 0)
    ci = lax.broadcasted_iota(jnp.int32, (TILE_Q, TILE_Q), 1)
    same = (ri // C) == (ci // C)
    incl = same & (ri >= ci)
    strict = same & (ri > ci)
    eye = (ri == ci).astype(F32)
    tri = incl.astype(BF16)
    lane, row = ci, ri
    gain = gn_ref[...]
    zpad = jnp.zeros((C, D), F32)

    HH = range(GDN_HPS)
    hsl = [slice(hh * D, (hh + 1) * D) for hh in HH]

    def conv(x_s, hh, base, w_ref):
        y = x_s[hh, pl.ds(base, TILE_Q), :] * w_ref[0:1, hsl[hh]]
        for j in range(1, 4):
            y = y + x_s[hh, pl.ds(base + j, TILE_Q), :] * w_ref[j:j + 1, hsl[hh]]
        return _silu(y)

    def l2n(x):
        return x * lax.rsqrt(jnp.sum(x * x, axis=-1, keepdims=True) + EPS)

    RW = tpi * TILE_Q
    UU = [(tt, hh) for tt in range(tpi) for hh in HH]
    UI = range(len(UU))

    def tile(i, S):
        r0 = pl.multiple_of(i * RW, RW)
        for x_s, x_ref, b_ref in ((xq_s, q_ref, bq_ref), (xk_s, k_ref, bk_ref), (xv_s, v_ref, bv_ref)):
            @pl.when(i == 0)
            def _():
                for hh in HH:
                    x_s[hh, CONV_PAD - 3:CONV_PAD, :] = b_ref[:, hsl[hh]]

            @pl.when(i > 0)
            def _():
                for hh in HH:
                    x_s[hh, CONV_PAD - 3:CONV_PAD, :] = x_s[hh, CONV_PAD + RW - 3:CONV_PAD + RW, :]

            for hh in HH:
                x_s[hh, CONV_PAD:CONV_PAD + RW, :] = x_ref[pl.ds(r0, RW), hsl[hh]]
        base = CONV_PAD - 3 + jnp.minimum(i, 0)
        q4 = [conv(xq_s, hh, base + tt * TILE_Q, wq_ref) for tt, hh in UU]
        k4 = [conv(xk_s, hh, base + tt * TILE_Q, wk_ref) for tt, hh in UU]
        v4 = [conv(xv_s, hh, base + tt * TILE_Q, wv_ref) for tt, hh in UU]
        beta_all, G_all = [], []
        for tt in range(tpi):
            ba = ba_ref[pl.ds(r0 + tt * TILE_Q, TILE_Q), :]
            live = (row + (r0 + tt * TILE_Q)) < t_real
            beta_all.append(jnp.where(live, _sigmoid(ba), 0.0))
            g_all = jnp.where(live, -jnp.exp(alog_ref[...]) * _softplus(ba + dtb_ref[...]), 0.0)
            gh, gm, gl = _split3(g_all)
            G_all.append(jnp.dot(tri, gh, preferred_element_type=F32) + jnp.dot(tri, gm, preferred_element_type=F32)
                         + jnp.dot(tri, gl, preferred_element_type=F32))
        head = [hg * GDN_HPS + hh for _, hh in UU]
        q = [l2n(q4[u]) * (D ** -0.5) for u in UI]
        k = [l2n(k4[u]) for u in UI]
        bc = [jnp.sum(jnp.where(lane == head[u], beta_all[UU[u][0]], 0.0), axis=-1, keepdims=True) for u in UI]
        Gc = [jnp.broadcast_to(jnp.sum(jnp.where(lane == head[u] + GDN_HEADS, G_all[UU[u][0]], 0.0), axis=-1,
                                       keepdims=True), (TILE_Q, TILE_Q)) for u in UI]
        gamma = [jnp.exp(jnp.where(incl, Gc[u] - Gc[u].T, NEG)) for u in UI]
        kk = [_dot_nt(k[u], k[u]) for u in UI]
        qk = [_dot_nt(q[u], k[u]) for u in UI]
        X = [jnp.where(strict, bc[u] * kk[u] * gamma[u], 0.0) for u in UI]
        P = [eye - X[u] for u in UI]
        for _ in range(int(math.log2(C)) - 1):
            X = [_dot(X[u], X[u]) for u in UI]
            P = [P[u] + _dot(P[u], X[u]) for u in UI]
        eG = [jnp.exp(Gc[u]) for u in UI]
        rhs = [jnp.concatenate([v4[u] * bc[u], k[u] * (bc[u] * eG[u])], axis=-1) for u in UI]
        sol = [rhs[u] + _dot(P[u] - eye, rhs[u]) for u in UI]
        a_in = [qk[u] * gamma[u] for u in UI]
        q_dec = [q[u] * eG[u] for u in UI]
        kdT = [(k[u] * jnp.exp(jnp.where(row < C, Gc[u][C - 1:C, :], Gc[u][2 * C - 1:2 * C, :]) - Gc[u])).T
               for u in UI]
        S = list(S)
        for tt in range(tpi):
            us = [tt * GDN_HPS + hh for hh in HH]
            oq, vn = [[] for _ in HH], [[] for _ in HH]
            for c in range(TILE_Q // C):
                cs = slice(c * C, (c + 1) * C)
                r = [_dot(jnp.concatenate([sol[us[hh]][cs, D:], q_dec[us[hh]][cs]], axis=0), S[hh]) for hh in HH]
                for hh in HH:
                    oq[hh].append(r[hh][C:])
                    vn[hh].append(sol[us[hh]][cs, :D] - r[hh][:C])
                vpad = [jnp.concatenate([vn[hh][c], zpad] if c == 0 else [zpad, vn[hh][c]], axis=0) for hh in HH]
                S = [S[hh] * jnp.exp(Gc[us[hh]][(c + 1) * C - 1:(c + 1) * C, :]) + _dot(kdT[us[hh]], vpad[hh])
                     for hh in HH]
            o = [jnp.concatenate(oq[hh], axis=0) + _dot(a_in[us[hh]], jnp.concatenate(vn[hh], axis=0)) for hh in HH]
            rt = r0 + tt * TILE_Q
            outs = [_rms(o[hh], gain) * _silu(z_ref[pl.ds(rt, TILE_Q), hsl[hh]]) for hh in HH]
            o_ref[pl.ds(rt, TILE_Q), :] = jnp.concatenate(outs, axis=-1).astype(o_ref.dtype)
        return tuple(S)

    S = lax.fori_loop(0, T // RW, tile, tuple(s0_ref[hh] for hh in HH))
    for hh in HH:
        s_ref[hh] = S[hh]


def _gdn(gq, ba, z, conv_buf, s0, w_conv, a_log, dt_bias, norm_out, t_real):
    B, T, _ = gq.shape
    H = GDN_HEADS
    pad16 = lambda x: jnp.concatenate([jnp.zeros((H,), F32), x.astype(F32), jnp.zeros((LANES - 2 * H,), F32)])
    ng = H // GDN_HPS
    tpi = GDN_TPI if (T // TILE_Q) % GDN_TPI == 0 else 1
    col = lambda off: pl.BlockSpec((None, T, GDN_SW), lambda b, g: (b, 0, off * ng + g))
    bcol = lambda off: pl.BlockSpec((None, 3, GDN_SW), lambda b, g: (b, 0, off * ng + g))
    wcol = lambda off: pl.BlockSpec((4, GDN_SW), lambda b, g: (0, off * ng + g))
    sblk = pl.BlockSpec((None, GDN_HPS, GDN_HEAD_DIM, GDN_HEAD_DIM), lambda b, g: (b, g, 0, 0))
    o, s_new = pl.pallas_call(
        functools.partial(_gdn_kernel, t_real=t_real, tpi=tpi),
        grid=(B, ng),
        in_specs=[col(0), col(1), col(2), bcol(0), bcol(1), bcol(2), wcol(0), wcol(1), wcol(2),
                  pl.BlockSpec((None, T, LANES), lambda b, g: (b, 0, 0)),
                  _resident((1, LANES)), _resident((1, LANES)),
                  col(0), _resident((1, LANES)), sblk],
        out_specs=[col(0), sblk],
        out_shape=[jax.ShapeDtypeStruct((B, T, GDN_W), BF16),
                   jax.ShapeDtypeStruct((B, H, GDN_HEAD_DIM, GDN_HEAD_DIM), F32)],
        scratch_shapes=[pltpu.VMEM((GDN_HPS, tpi * TILE_Q + CONV_PAD, LANES), F32)] * 3,
        compiler_params=_cparams(("parallel", "parallel")),
        name="gdn",
    )(gq, gq, gq, conv_buf, conv_buf, conv_buf, w_conv, w_conv, w_conv, ba,
      pad16(a_log).reshape(1, LANES), pad16(dt_bias).reshape(1, LANES), z, norm_out.reshape(1, LANES), s0)
    return o, s_new


def _mix_kernel(*refs):
    n_og = 3 * DIL_SLABS
    o_refs, l_refs = refs[:n_og], refs[n_og:2 * n_og]
    ob_ref, ga_ref, gb_ref, h_ref, wa_ref, wb_ref, wo_ref, gq_ref, wq_ref, h1_ref, qm_ref = refs[2 * n_og:]
    slabs = []
    for sl in range(DIL_SLABS):
        l0, l1, l2 = (l_refs[g * DIL_SLABS + sl][...] for g in range(3))
        o0, o1, o2 = (o_refs[g * DIL_SLABS + sl][...] for g in range(3))
        mx = jnp.maximum(jnp.maximum(l0, l1), l2)
        e0, e1, e2 = jnp.exp(l0 - mx), jnp.exp(l1 - mx), jnp.exp(l2 - mx)
        slabs.append((e0 * o0 + e1 * o1 + e2 * o2) / (e0 + e1 + e2))
    o_a = jnp.concatenate(slabs, axis=-1)
    a = _dot(o_a, wa_ref[...])
    b = jnp.dot(ob_ref[...], wb_ref[...], preferred_element_type=F32)
    merged = _sigmoid(ga_ref[...]) * a + _sigmoid(gb_ref[...]) * b
    h1 = h_ref[...] + _dot(merged, wo_ref[...])
    h1_ref[...] = h1
    qm_ref[...] = _dot(_rms(h1, gq_ref[...]), wq_ref[...]).astype(qm_ref.dtype)


def _mix(o_g, l_g, o_b, ga, gb, h, w_a, w_b, w_o, norm_mem_q, w_mem_q, tm):
    rows, d = h.shape
    rt = lambda n: pl.BlockSpec((tm, n), lambda i: (i, 0))
    assert len(o_g) == len(l_g) == 3 * DIL_SLABS
    return pl.pallas_call(
        _mix_kernel,
        grid=(rows // tm,),
        in_specs=[rt(LANES)] * (6 * DIL_SLABS) + [rt(GDN_W), rt(d), rt(d), rt(d),
                                     _resident(w_a.shape), _resident(w_b.shape), _resident(w_o.shape),
                                     _resident((1, d)), _resident(w_mem_q.shape)],
        out_specs=[rt(d), rt(w_mem_q.shape[1])],
        out_shape=[jax.ShapeDtypeStruct((rows, d), F32), jax.ShapeDtypeStruct((rows, w_mem_q.shape[1]), BF16)],
        compiler_params=_cparams(("parallel",)),
        name="mix",
    )(*o_g, *l_g, o_b, ga, gb, h, w_a, w_b, w_o, norm_mem_q.reshape(1, d), w_mem_q)


def _mem_kv_kernel(x_ref, g_ref, w_ref, k_ref, v_ref):
    u = _rms(x_ref[...], g_ref[...]).astype(BF16)
    n = k_ref.shape[-1]
    k_ref[...] = jnp.dot(u, w_ref[:, :n], preferred_element_type=F32)
    v_ref[...] = jnp.dot(u, w_ref[:, n:], preferred_element_type=F32)


def _mem_kv(mem2d, gain, w, tm):
    rows, d = mem2d.shape
    n = w.shape[1] // 2
    return pl.pallas_call(
        _mem_kv_kernel,
        grid=(rows // tm,),
        in_specs=[pl.BlockSpec((tm, d), lambda i: (i, 0)), _resident((1, d)), _resident(w.shape)],
        out_specs=[pl.BlockSpec((tm, n), lambda i: (i, 0))] * 2,
        out_shape=[jax.ShapeDtypeStruct((rows, n), F32)] * 2,
        compiler_params=_cparams(("parallel",)),
        name="mem_kv",
    )(mem2d, gain.reshape(1, d), w)


def _mem_attn_kernel(q_ref, k_ref, v_ref, o_ref):
    q = q_ref[...]
    k = k_ref[...].astype(BF16)
    v = v_ref[...].astype(BF16)
    outs = []
    for h in range(MEM_HEADS):
        sl = slice(h * MEM_HEAD_DIM, (h + 1) * MEM_HEAD_DIM)
        s = _dot_nt(q[:, sl], k[:, sl]) * (MEM_HEAD_DIM ** -0.5)
        p = jnp.exp(s - jnp.max(s, axis=-1, keepdims=True))
        outs.append(_dot(p, v[:, sl]) / jnp.sum(p, axis=-1, keepdims=True))
    o_ref[...] = jnp.concatenate(outs, axis=-1).astype(o_ref.dtype)


def _mem_attn(qm, mem_k, mem_v, tm):
    B, T, w = qm.shape
    M = mem_k.shape[1]
    return pl.pallas_call(
        _mem_attn_kernel,
        grid=(B, T // tm),
        in_specs=[pl.BlockSpec((None, tm, w), lambda b, j: (b, j, 0)),
                  pl.BlockSpec((None, M, w), lambda b, j: (b, 0, 0)),
                  pl.BlockSpec((None, M, w), lambda b, j: (b, 0, 0))],
        out_specs=pl.BlockSpec((None, tm, w), lambda b, j: (b, j, 0)),
        out_shape=jax.ShapeDtypeStruct((B, T, w), F32),
        compiler_params=_cparams(("parallel", "parallel")),
        name="mem_attn",
    )(qm, mem_k, mem_v)


def _ffn_kernel(*refs, inject, emit_gate):
    if inject:
        (h1_ref, om_ref, init_ref, fill_ref, wmo_ref, gf_ref, wup_ref, wc_ref, bc_ref, wd_ref, gfin_ref,
         y_ref, fc_ref, gs) = refs
    else:
        (h1_ref, om_ref, init_ref, wmo_ref, gf_ref, wup_ref, wc_ref, bc_ref, wd_ref, gfin_ref,
         y_ref, fc_ref, gs) = refs
    tm = h1_ref.shape[0]
    F = wd_ref.shape[0]
    PAD = SUBLANES

    @pl.when(pl.program_id(1) == 0)
    def _():
        gs[PAD - 2:PAD, :] = init_ref[...]

    h2 = h1_ref[...] + _dot(om_ref[...], wmo_ref[...])
    n = _rms(h2, gf_ref[...]).astype(BF16)
    gate = jnp.dot(n, wup_ref[:, :F], preferred_element_type=F32)
    if inject:
        r = lax.broadcasted_iota(jnp.int32, (tm, 1), 0)
        gate = jnp.where((r % SAMPLE_ROWS) >= SAMPLE_ROWS - 2, fill_ref[...], gate)
    gs[PAD:PAD + tm, :] = gate
    conv = (gs[PAD - 2:PAD - 2 + tm, :] * wc_ref[0:1, :] + gs[PAD - 1:PAD - 1 + tm, :] * wc_ref[1:2, :]
            + gate * wc_ref[2:3, :])
    last2 = gs[PAD + tm - 2:PAD + tm, :]
    gs[PAD - 2:PAD, :] = last2
    if emit_gate:
        fc_ref[...] = gate
    else:
        fc_ref[...] = last2
    up = jnp.dot(n, wup_ref[:, F:], preferred_element_type=F32)
    act = _silu(conv + bc_ref[...]) * up
    y = h2 + _dot(act, wd_ref[...])
    y_ref[...] = _rms(y, gfin_ref[...])


def _ffn(h1, om, init, fill, w_mo, norm_ffn, w_up, w_conv, b_conv, w_down, norm_final, tm, emit_gate):
    B, T, d = h1.shape
    F = w_down.shape[0]
    inject = fill is not None
    rt = lambda n: pl.BlockSpec((None, tm, n), lambda b, j: (b, j, 0))
    in_specs = [rt(d), rt(om.shape[-1]), pl.BlockSpec((None, 2, F), lambda b, j: (b, 0, 0))]
    args = [h1, om, init]
    if inject:
        in_specs.append(rt(F))
        args.append(fill)
    in_specs += [_resident(w_mo.shape), _resident((1, d)), _resident(w_up.shape), _resident(w_conv.shape),
                 _resident((1, F)), _resident(w_down.shape), _resident((1, d))]
    args += [w_mo, norm_ffn.reshape(1, d), w_up, w_conv, b_conv.reshape(1, F), w_down, norm_final.reshape(1, d)]
    if emit_gate:
        fc_spec, fc_shape = rt(F), jax.ShapeDtypeStruct((B, T, F), F32)
    else:
        fc_spec = pl.BlockSpec((None, 2, F), lambda b, j: (b, 0, 0))
        fc_shape = jax.ShapeDtypeStruct((B, 2, F), F32)
    return pl.pallas_call(
        functools.partial(_ffn_kernel, inject=inject, emit_gate=emit_gate),
        grid=(B, T // tm),
        in_specs=in_specs,
        out_specs=[rt(d), fc_spec],
        out_shape=[jax.ShapeDtypeStruct((B, T, d), F32), fc_shape],
        scratch_shapes=[pltpu.VMEM((tm + SUBLANES, F), F32)],
        compiler_params=_cparams(("parallel", "arbitrary")),
        name="ffn",
    )(*args)


def kernel(x_prompt, x_sample, cache_dil0_kv, cache_dil1_kv, cache_dil2_kv, state_delta, state_delta_conv, cache_mem_k, cache_mem_v, state_ffn_conv, mem_prompt, rel_bias, norm_mix, w_in, w_conv_delta, a_log, dt_bias, norm_delta_out, w_branch_a, w_branch_b, w_out, norm_mem_q, norm_mem_kv, w_mem_q, w_mem_kv, w_mem_o, norm_ffn, w_ffn_up, w_ffn_conv, b_ffn_conv, w_ffn_down, norm_final):
    B, S, D = x_prompt.shape
    Bs, Ts, _ = x_sample.shape
    depth = w_in.shape[0]
    assert depth == 1 and Ts <= SAMPLE_ROWS - 2 and Ts >= 3 and S % (16 * TILE_Q) == 0
    F = w_ffn_down.shape[1]
    M = mem_prompt.shape[1]
    l = 0
    w_arr = _arrange_w_in(w_in[l])
    w_a, w_b, w_o = (w.astype(BF16) for w in (w_branch_a[l], w_branch_b[l], w_out[l]))
    w_mq, w_mkv, w_mo = (w.astype(BF16) for w in (w_mem_q[l], w_mem_kv[l], w_mem_o[l]))
    w_up, w_dn = w_ffn_up[l].astype(BF16), w_ffn_down[l].astype(BF16)
    t_cat, t_cur = _prompt_bias_tables(rel_bias)

    xp = x_prompt.reshape(B * S, D)
    q, kv0, kv1, kv2, kt0, kt1, kt2, gq, z, ba, ga, gb = _in_proj(xp, norm_mix[l], w_arr, 256, seq=S)
    kvs = [kv.reshape(B, S, 2 * DIL_GW) for kv in (kv0, kv1, kv2)]
    q3 = q.reshape(B, S, 3 * DIL_GW)
    o_g, l_g = [], []
    for g in range(3):
        o_sl, l_sl = _dil_prompt(q3, kvs[g], g, t_cat[g], t_cur[g])
        o_g += o_sl
        l_g += l_sl
    o_b, delta_p = _gdn(gq.reshape(B, S, -1), ba.reshape(B, S, LANES), z.reshape(B, S, GDN_W),
                        jnp.zeros((B, 3, 3 * GDN_W), F32), jnp.zeros((B, GDN_HEADS, GDN_HEAD_DIM, GDN_HEAD_DIM), F32),
                        w_conv_delta[l], a_log[l], dt_bias[l], norm_delta_out[l], S)
    h1, qm = _mix(o_g, l_g, o_b.reshape(B * S, GDN_W), ga, gb, xp, w_a, w_b, w_o, norm_mem_q[l], w_mq, 256)
    mk_p, mv_p = _mem_kv(mem_prompt.reshape(B * M, D), norm_mem_kv[l], w_mkv, 256)
    mk_p, mv_p = mk_p.reshape(B, M, -1), mv_p.reshape(B, M, -1)
    om = _mem_attn(qm.reshape(B, S, -1), mk_p, mv_p, 512)
    y_p, fconv_p = _ffn(h1.reshape(B, S, D), om, jnp.zeros((B, 2, F), F32), None, w_mo, norm_ffn[l], w_up,
                        w_ffn_conv[l], b_ffn_conv[l], w_dn, norm_final, 256, False)
    gq3 = gq.reshape(B, S, -1)
    p_out = ([kt[:, :, S - min(w, S):].reshape(B, 2, DIL_HPG, DIL_HEAD_DIM, min(w, S)).transpose(0, 4, 1, 2, 3)[None]
              for kt, (w, _) in zip((kt0, kt1, kt2), DIL_GROUPS)]
             + [delta_p[None], gq3[:, S - 3:][None], mk_p.reshape(1, B, M, MEM_HEADS, MEM_HEAD_DIM),
                mv_p.reshape(1, B, M, MEM_HEADS, MEM_HEAD_DIM), fconv_p[None]])

    R = SAMPLE_ROWS
    xs = jnp.pad(x_sample, ((0, 0), (0, R - Ts), (0, 0))).reshape(Bs * R, D)
    q, kv0, kv1, kv2, gq, z, ba, ga, gb = _in_proj(xs, norm_mix[l], w_arr, Bs * R)
    kvn = [kv.reshape(Bs, R, 2 * DIL_GW) for kv in (kv0, kv1, kv2)]
    caches_t = [jnp.transpose(c[l], (0, 2, 3, 4, 1)).reshape(Bs, 2 * DIL_GW, c.shape[2])
                for c in (cache_dil0_kv, cache_dil1_kv, cache_dil2_kv)]
    tabc, tabn = _sample_bias_tables(rel_bias, Ts)
    o_g, l_g, new_caches = _dil_sample(q.reshape(Bs, R, 3 * DIL_GW), kvn, caches_t, tabc, tabn, Ts)
    padt = lambda a: jnp.pad(a.reshape(Bs, R, -1), ((0, 0), (0, TILE_Q - R), (0, 0)))
    o_b, delta_s = _gdn(padt(gq), padt(ba), padt(z), state_delta_conv[l], state_delta[l],
                        w_conv_delta[l], a_log[l], dt_bias[l], norm_delta_out[l], Ts)
    o_b = o_b[:, :R].reshape(Bs * R, GDN_W)
    h1, qm = _mix(o_g, l_g, o_b, ga, gb, xs, w_a, w_b, w_o, norm_mem_q[l], w_mq, Bs * R)
    om = _mem_attn(qm.reshape(Bs, R, -1), cache_mem_k[l].reshape(Bs, M, -1), cache_mem_v[l].reshape(Bs, M, -1), R)
    fst = state_ffn_conv[l]
    fill = jnp.concatenate([jnp.zeros((Bs, R - 2, F), F32),
                            jnp.concatenate([fst[1:], jnp.zeros((1, 2, F), F32)], axis=0)], axis=1)
    y_s, gate_s = _ffn(h1.reshape(1, Bs * R, D), om.reshape(1, Bs * R, -1), fst[:1], fill.reshape(1, Bs * R, F),
                       w_mo, norm_ffn[l], w_up, w_ffn_conv[l], b_ffn_conv[l], w_dn, norm_final, Bs * R, True)
    y_s = y_s.reshape(Bs, R, D)[:, :Ts]
    gq3 = gq.reshape(Bs, R, -1)
    s_out = ([nc.reshape(Bs, 2, DIL_HPG, DIL_HEAD_DIM, nc.shape[2]).transpose(0, 4, 1, 2, 3)[None]
              for nc in new_caches]
             + [delta_s[None], gq3[:, Ts - 3:Ts][None], gate_s.reshape(Bs, R, F)[:, Ts - 2:Ts][None]])

    return (y_p.reshape(B, S, D), y_s, *p_out, *s_out)
```

```python
import functools
import math

import jax
import jax.numpy as jnp
import numpy as np
from jax import lax
from jax.experimental import pallas as pl
from jax.experimental.pallas import tpu as pltpu

F32 = jnp.float32
BF16 = jnp.bfloat16

PAST_LEN = 8192
DIL_GROUPS = ((128, 1), (512, 4), (2048, 16))
DIL_HPG = 4
DIL_HEAD_DIM = 64
DIL_GW = DIL_HPG * DIL_HEAD_DIM
DIL_NK = 129
REL_BUCKETS = 32
REL_MAX_DIST = 2048
GDN_HEADS = 8
GDN_HEAD_DIM = 128
GDN_W = GDN_HEADS * GDN_HEAD_DIM
GDN_CHUNK = 64
MEM_HEADS = 4
MEM_HEAD_DIM = 128
EPS = 1e-6
NEG = -1e30

LANES = 128
SUBLANES = 8
TILE_Q = 128
SAMPLE_ROWS = SUBLANES
VMEM_LIMIT = 56 * 1024 * 1024


def _cparams(sem):
    return pltpu.CompilerParams(dimension_semantics=sem, vmem_limit_bytes=VMEM_LIMIT)


def _resident(shape):
    nd = len(shape)
    return pl.BlockSpec(shape, lambda *_: (0,) * nd, pipeline_mode=pl.Buffered(1))


def _rms(x, gain_row):
    return x * lax.rsqrt(jnp.mean(x * x, axis=-1, keepdims=True) + EPS) * gain_row


def _dot(a, b):
    return jnp.dot(a.astype(BF16), b.astype(BF16), preferred_element_type=F32)


def _dot_nt(a, b):
    return lax.dot_general(a.astype(BF16), b.astype(BF16), (((1,), (1,)), ((), ())), preferred_element_type=F32)


def _dot_tn(a, b):
    return lax.dot_general(a.astype(BF16), b.astype(BF16), (((0,), (0,)), ((), ())), preferred_element_type=F32)


def _split3(x):
    hi = x.astype(BF16)
    r1 = x - hi.astype(F32)
    mid = r1.astype(BF16)
    lo = (r1 - mid.astype(F32)).astype(BF16)
    return hi, mid, lo


def _dot_hp(a, b):
    ah, am, al = _split3(a)
    bh, bm, bl = _split3(b)
    d = functools.partial(jnp.dot, preferred_element_type=F32)
    return d(ah, bh) + (d(ah, bm) + d(am, bh)) + (d(am, bm) + d(ah, bl) + d(al, bh))


def _sigmoid(x):
    return 1.0 / (1.0 + jnp.exp(-x))


def _silu(x):
    return x * _sigmoid(x)


def _softplus(x):
    return jnp.maximum(x, 0.0) + jnp.log(1.0 + jnp.exp(-jnp.abs(x)))


IN_SEGS = (("q", 3 * DIL_GW), ("kv0", 2 * DIL_GW), ("kv1", 2 * DIL_GW), ("kv2", 2 * DIL_GW),
           ("gq", 3 * GDN_W), ("z", GDN_W), ("ba", LANES), ("ga", 1024), ("gb", 1024))


OFF_Q, OFF_K, OFF_V = 0, 3 * DIL_GW, 6 * DIL_GW
OFF_GQ = 9 * DIL_GW
OFF_Z = OFF_GQ + 3 * GDN_W
OFF_BA = OFF_Z + GDN_W
OFF_GATES = OFF_BA + 2 * GDN_HEADS


def _arrange_w_in(w_in):
    w = w_in.astype(BF16)
    return w, w[:, OFF_GATES:]


def _in_proj_kernel(x_ref, g_ref, w_ref, wg_ref, *out_refs, kv_t):
    if kv_t:
        q_ref, kv0_ref, kv1_ref, kv2_ref, kt0_ref, kt1_ref, kt2_ref, gq_ref, z_ref, ba_ref, ga_ref, gb_ref = out_refs
        kt_refs = (kt0_ref, kt1_ref, kt2_ref)
    else:
        q_ref, kv0_ref, kv1_ref, kv2_ref, gq_ref, z_ref, ba_ref, ga_ref, gb_ref = out_refs
    u = _rms(x_ref[...], g_ref[...]).astype(BF16)

    def seg(ref, off, n):
        return jnp.dot(u, ref[:, off:off + n], preferred_element_type=F32)

    q_ref[...] = seg(w_ref, OFF_Q, 3 * DIL_GW) * (DIL_HEAD_DIM ** -0.5)
    for g, kv_ref in enumerate((kv0_ref, kv1_ref, kv2_ref)):
        for part, off in enumerate((OFF_K, OFF_V)):
            r = seg(w_ref, off + g * DIL_GW, DIL_GW)
            kv_ref[:, part * DIL_GW:(part + 1) * DIL_GW] = r
            if kv_t:
                kt_refs[g][part * DIL_GW:(part + 1) * DIL_GW, :] = r.T
    for c in range(3):
        gq_ref[:, c * GDN_W:(c + 1) * GDN_W] = seg(w_ref, OFF_GQ + c * GDN_W, GDN_W)
    z_ref[...] = seg(w_ref, OFF_Z, GDN_W)
    ba_ref[...] = seg(w_ref, OFF_BA, LANES)
    ga_ref[...] = seg(wg_ref, 0, 1024)
    gb_ref[...] = seg(wg_ref, 1024, 1024)


def _in_proj(x2d, gain, w_arr, tm, seq=None):
    rows, d = x2d.shape
    kv_t = seq is not None
    names = [n for n, _ in IN_SEGS]
    widths = dict(IN_SEGS)
    row_spec = lambda n: pl.BlockSpec((tm, n), lambda i: (i, 0))
    out_specs, out_shape = [], []
    for n in names:
        out_specs.append(row_spec(widths[n]))
        out_shape.append(jax.ShapeDtypeStruct((rows, widths[n]), F32))
        if n == "kv2" and kv_t:
            nt = seq // tm
            for _ in range(3):
                out_specs.append(pl.BlockSpec((None, 2 * DIL_GW, tm), lambda i: (i // nt, 0, i % nt)))
                out_shape.append(jax.ShapeDtypeStruct((rows // seq, 2 * DIL_GW, seq), F32))
    return pl.pallas_call(
        functools.partial(_in_proj_kernel, kv_t=kv_t),
        grid=(rows // tm,),
        in_specs=[pl.BlockSpec((tm, d), lambda i: (i, 0)), _resident((1, d)), _resident(w_arr[0].shape),
                  _resident(w_arr[1].shape)],
        out_specs=out_specs,
        out_shape=out_shape,
        compiler_params=_cparams(("parallel",)),
        name="in_proj",
    )(x2d, gain.reshape(1, d), *w_arr)


def _rel_bucket(dist):
    exact = REL_BUCKETS // 2
    d = jnp.maximum(dist, 1).astype(F32)
    large = exact + (jnp.log(d / exact) / math.log(REL_MAX_DIST / exact) * (REL_BUCKETS - exact)).astype(jnp.int32)
    return jnp.where(dist < exact, dist, jnp.minimum(large, REL_BUCKETS - 1))


def _group_bias(rel_bias, g):
    dil = DIL_GROUPS[g][1]
    dist = dil * jnp.arange(DIL_NK, dtype=jnp.int32)
    tab = rel_bias[_rel_bucket(dist)]
    return tab[:, g * DIL_HPG:(g + 1) * DIL_HPG].T.astype(F32)


def _toeplitz(v, n, width):
    h, L = v.shape
    return jnp.tile(v, (1, n))[:, :n * (L - 1)].reshape(h, n, L - 1)[:, :, :width]


def _prompt_bias_tables(rel_bias):
    cat, cur = [], []
    for g in range(3):
        bg = _group_bias(rel_bias, g)
        v = jnp.concatenate([bg[:, ::-1], jnp.full((DIL_HPG, 3 * TILE_Q - DIL_NK), NEG, F32)], axis=1)
        t = _toeplitz(v, TILE_Q, 2 * TILE_Q)
        cat.append(t)
        cur.append(t[:, :, TILE_Q:])
    return cat, cur


DIL_TIF = 2
DIL_SLABS = DIL_GW // LANES


def _dil_prompt_kernel(q0_ref, q1_ref, k0_ref, k1_ref, v0_ref, v1_ref, tcat_ref, tcur_ref,
                       o0_ref, o1_ref, l0_ref, l1_ref, *, dil):
    S = q0_ref.shape[0]
    nb = S // dil // TILE_Q
    q_refs, k_refs, v_refs = (q0_ref, q1_ref), (k0_ref, k1_ref), (v0_ref, v1_ref)
    o_refs, l_refs = (o0_ref, o1_ref), (l0_ref, l1_ref)
    even = lax.broadcasted_iota(jnp.int32, (TILE_Q, LANES), 1) < DIL_HEAD_DIM

    def rows(r, t):
        start = r + dil * TILE_Q * t
        return pl.ds(start, TILE_Q, stride=dil) if dil > 1 else pl.ds(start, TILE_Q)

    tiles = [(r, t) for r in range(dil) for t in range(nb)]
    for i0 in range(0, len(tiles), DIL_TIF):
        grp = tiles[i0:i0 + DIL_TIF]
        qm, kc, vc = {}, {}, {}
        for ti, (r, t) in enumerate(grp):
            for sl in range(DIL_SLABS):
                qf = q_refs[sl][rows(r, t), :]
                qm[ti, 2 * sl] = jnp.where(even, qf, 0.0).astype(BF16)
                qm[ti, 2 * sl + 1] = jnp.where(even, 0.0, qf).astype(BF16)
                kc[ti, sl] = k_refs[sl][rows(r, t), :].astype(BF16)
                vc[ti, sl] = v_refs[sl][rows(r, t), :].astype(BF16)
                if t > 0:
                    kc[ti, sl] = jnp.concatenate([k_refs[sl][rows(r, t - 1), :].astype(BF16), kc[ti, sl]], axis=0)
                    vc[ti, sl] = jnp.concatenate([v_refs[sl][rows(r, t - 1), :].astype(BF16), vc[ti, sl]], axis=0)
        units = [(ti, h) for ti in range(len(grp)) for h in range(DIL_HPG)]
        s = [_dot_nt(qm[ti, h], kc[ti, h // 2]) + (tcat_ref[h] if grp[ti][1] > 0 else tcur_ref[h]) for ti, h in units]
        m = [jnp.max(x, axis=-1, keepdims=True) for x in s]
        p = [jnp.exp(x - mx) for x, mx in zip(s, m)]
        l = [jnp.sum(x, axis=-1, keepdims=True) for x in p]
        pv = [jnp.dot(p[u].astype(BF16), vc[ti, h // 2], preferred_element_type=F32) for u, (ti, h) in enumerate(units)]
        o = [pv[u] / l[u] for u in range(len(units))]
        lse = [m[u] + jnp.log(l[u]) for u in range(len(units))]
        for ti, (r, t) in enumerate(grp):
            for sl in range(DIL_SLABS):
                ue, uo = ti * DIL_HPG + 2 * sl, ti * DIL_HPG + 2 * sl + 1
                o_refs[sl][rows(r, t), :] = jnp.where(even, o[ue], o[uo])
                l_refs[sl][rows(r, t), :] = jnp.where(even, lse[ue], lse[uo])


def _dil_prompt(q, kv, g, t_cat, t_cur):
    B, S, _ = q.shape
    dil = DIL_GROUPS[g][1]
    slab = lambda c: pl.BlockSpec((None, S, LANES), lambda b: (b, 0, c))
    nq, nk = g * DIL_SLABS, 0
    outs = pl.pallas_call(
        functools.partial(_dil_prompt_kernel, dil=dil),
        grid=(B,),
        in_specs=[slab(nq), slab(nq + 1), slab(nk), slab(nk + 1), slab(nk + 2), slab(nk + 3),
                  _resident(t_cat.shape), _resident(t_cur.shape)],
        out_specs=[slab(0)] * 4,
        out_shape=[jax.ShapeDtypeStruct((B, S, LANES), F32)] * 4,
        compiler_params=_cparams(("parallel",)),
        name=f"dil_prompt_g{g}",
    )(q, q, kv, kv, kv, kv, t_cat, t_cur)
    o0, o1, l0, l1 = (x.reshape(B * S, LANES) for x in outs)
    return [o0, o1], [l0, l1]


def _sample_bias_tables(rel_bias, t_real):
    R = SAMPLE_ROWS
    tabc, tabn = [], []
    t_i = np.arange(R)[:, None]
    u_i = np.arange(R)[None, :]
    for g, (w, dil) in enumerate(DIL_GROUPS):
        bg = _group_bias(rel_bias, g)
        base = bg[:, ::-1][:, :TILE_Q]
        t0 = jnp.concatenate([base[:, :, None], jnp.full((DIL_HPG, TILE_Q, dil - 1), NEG, F32)], axis=2)
        t0 = t0.reshape(DIL_HPG, w)
        rows = [jnp.concatenate([jnp.full((DIL_HPG, t), NEG, F32), t0[:, :w - t]], axis=1) for t in range(t_real)]
        rows += [jnp.zeros((DIL_HPG, w), F32)] * (R - t_real)
        tabc.append(jnp.stack(rows, axis=1).reshape(DIL_HPG * R, w))
        tn = jnp.full((DIL_HPG, R, R), NEG, F32)
        for j in range(-(-t_real // dil)):
            hit = (t_i - u_i == j * dil) & (t_i < t_real)
            tn = jnp.where(hit[None], bg[:, j][:, None, None], tn)
        tn = jnp.where((t_i >= t_real)[None], 0.0, tn)
        tabn.append(tn.reshape(DIL_HPG * R, R))
    return tabc, jnp.stack(tabn)


def _dil_sample_kernel(q_ref, n0_ref, n1_ref, n2_ref, c0_ref, c1_ref, c2_ref, tc0_ref, tc1_ref, tc2_ref, tn_ref,
                       o_ref, l_ref, oc0_ref, oc1_ref, oc2_ref, *, t_real):
    R = SAMPLE_ROWS
    rows = lax.broadcasted_iota(jnp.int32, (DIL_HPG * R, DIL_GW), 0)
    lanes = lax.broadcasted_iota(jnp.int32, (DIL_HPG * R, DIL_GW), 1)
    head_mask = (lanes // DIL_HEAD_DIM) == (rows // R)
    lane_f = lax.broadcasted_iota(jnp.int32, (2 * DIL_GW, LANES), 1)
    keep = lane_f < LANES - t_real
    sel_l = lax.broadcasted_iota(jnp.int32, (LANES, R), 0)
    sel_u = lax.broadcasted_iota(jnp.int32, (LANES, R), 1)
    selT = ((sel_l == sel_u + LANES - t_real) & (sel_u < t_real)).astype(BF16)
    groups = ((n0_ref, c0_ref, tc0_ref, oc0_ref), (n1_ref, c1_ref, tc1_ref, oc1_ref), (n2_ref, c2_ref, tc2_ref, oc2_ref))

    def fold_heads(x):
        x = jnp.where(head_mask, x, 0.0)
        return x[0:R] + x[R:2 * R] + x[2 * R:3 * R] + x[3 * R:4 * R]

    for g, (n_ref, c_ref, tc_ref, oc_ref) in enumerate(groups):
        W = c_ref.shape[1]
        kvn = n_ref[...]
        q_g = q_ref[:, g * DIL_GW:(g + 1) * DIL_GW]
        q_bd = jnp.where(head_mask, jnp.concatenate([q_g] * DIL_HPG, axis=0), 0.0)
        s_c = _dot(q_bd, c_ref[:DIL_GW, :]) + tc_ref[...]
        s_n = _dot_nt(q_bd, kvn[:, :DIL_GW]) + tn_ref[g]
        m = jnp.maximum(jnp.max(s_c, axis=-1, keepdims=True), jnp.max(s_n, axis=-1, keepdims=True))
        p_c = jnp.exp(s_c - m)
        p_n = jnp.exp(s_n - m)
        l = jnp.sum(p_c, axis=-1, keepdims=True) + jnp.sum(p_n, axis=-1, keepdims=True)
        acc = (_dot_nt(p_c, c_ref[DIL_GW:, :]) + _dot(p_n, kvn[:, DIL_GW:])) / l
        o_ref[:, g * DIL_GW:(g + 1) * DIL_GW] = fold_heads(acc)
        l_ref[:, g * DIL_GW:(g + 1) * DIL_GW] = fold_heads(jnp.broadcast_to(m + jnp.log(l), acc.shape))
        hi, mid, lo = _split3(kvn)
        tail = (jnp.dot(selT, hi, preferred_element_type=F32) + jnp.dot(selT, mid, preferred_element_type=F32)
                + jnp.dot(selT, lo, preferred_element_type=F32)).T
        nxt = pltpu.roll(c_ref[:, 0:LANES], LANES - t_real, axis=1)
        for c in range(W // LANES):
            cur = nxt
            nxt = (pltpu.roll(c_ref[:, (c + 1) * LANES:(c + 2) * LANES], LANES - t_real, axis=1)
                   if (c + 1) * LANES < W else tail)
            oc_ref[:, c * LANES:(c + 1) * LANES] = jnp.where(keep, cur, nxt)


def _dil_sample(q, kvn, caches_t, tabc, tabn, t_real):
    B = q.shape[0]
    row = lambda n: pl.BlockSpec((None, SAMPLE_ROWS, n), lambda b: (b, 0, 0))
    cspecs = [pl.BlockSpec((None,) + c.shape[1:], lambda b: (b, 0, 0)) for c in caches_t]
    for g, (w, dil) in enumerate(DIL_GROUPS):
        assert caches_t[g].shape == (B, 2 * DIL_GW, w) and w // dil == TILE_Q
    out_spec = row(3 * DIL_GW)
    o, lse, *new_caches = pl.pallas_call(
        functools.partial(_dil_sample_kernel, t_real=t_real),
        grid=(B,),
        in_specs=([row(3 * DIL_GW)] + [row(2 * DIL_GW)] * 3 + cspecs + [_resident(t.shape) for t in tabc]
                  + [_resident(tabn.shape)]),
        out_specs=[out_spec, out_spec] + cspecs,
        out_shape=([jax.ShapeDtypeStruct((B, SAMPLE_ROWS, 3 * DIL_GW), F32)] * 2
                   + [jax.ShapeDtypeStruct(c.shape, F32) for c in caches_t]),
        compiler_params=_cparams(("parallel",)),
        name="dil_sample",
    )(q, *kvn, *caches_t, *tabc, tabn)
    o = o.reshape(B * SAMPLE_ROWS, 3 * DIL_GW)
    lse = lse.reshape(B * SAMPLE_ROWS, 3 * DIL_GW)
    n = 3 * DIL_SLABS
    return ([o[:, i * LANES:(i + 1) * LANES] for i in range(n)],
            [lse[:, i * LANES:(i + 1) * LANES] for i in range(n)], new_caches)


GDN_HPS = 4
GDN_SW = GDN_HPS * GDN_HEAD_DIM
CONV_PAD = SUBLANES
GDN_TPI = 4


def _gdn_kernel(q_ref, k_ref, v_ref, bq_ref, bk_ref, bv_ref, wq_ref, wk_ref, wv_ref, ba_ref, alog_ref, dtb_ref,
                z_ref, gn_ref, s0_ref, o_ref, s_ref, xq_s, xk_s, xv_s, *, t_real, tpi):
    T = q_ref.shape[0]
    hg = pl.program_id(1)
    C = GDN_CHUNK
    D = GDN_HEAD_DIM
    ri = lax.broadcasted_iota(jnp.int32, (TILE_Q, TILE_Q), 0)
    ci = lax.broadcasted_iota(jnp.int32, (TILE_Q, TILE_Q), 1)
    same = (ri // C) == (ci // C)
    incl = same & (ri >= ci)
    strict = same & (ri > ci)
    eye = (ri == ci).astype(F32)
    tri = incl.astype(BF16)
    lane, row = ci, ri
    gain = gn_ref[...]
    zpad = jnp.zeros((C, D), F32)

    HH = range(GDN_HPS)
    hsl = [slice(hh * D, (hh + 1) * D) for hh in HH]

    def conv(x_s, hh, base, w_ref):
        y = x_s[hh, pl.ds(base, TILE_Q), :] * w_ref[0:1, hsl[hh]]
        for j in range(1, 4):
            y = y + x_s[hh, pl.ds(base + j, TILE_Q), :] * w_ref[j:j + 1, hsl[hh]]
        return _silu(y)

    def l2n(x):
        return x * lax.rsqrt(jnp.sum(x * x, axis=-1, keepdims=True) + EPS)

    RW = tpi * TILE_Q
    UU = [(tt, hh) for tt in range(tpi) for hh in HH]
    UI = range(len(UU))

    def tile(i, S):
        r0 = pl.multiple_of(i * RW, RW)
        for x_s, x_ref, b_ref in ((xq_s, q_ref, bq_ref), (xk_s, k_ref, bk_ref), (xv_s, v_ref, bv_ref)):
            @pl.when(i == 0)
            def _():
                for hh in HH:
                    x_s[hh, CONV_PAD - 3:CONV_PAD, :] = b_ref[:, hsl[hh]]

            @pl.when(i > 0)
            def _():
                for hh in HH:
                    x_s[hh, CONV_PAD - 3:CONV_PAD, :] = x_s[hh, CONV_PAD + RW - 3:CONV_PAD + RW, :]

            for hh in HH:
                x_s[hh, CONV_PAD:CONV_PAD + RW, :] = x_ref[pl.ds(r0, RW), hsl[hh]]
        base = CONV_PAD - 3 + jnp.minimum(i, 0)
        q4 = [conv(xq_s, hh, base + tt * TILE_Q, wq_ref) for tt, hh in UU]
        k4 = [conv(xk_s, hh, base + tt * TILE_Q, wk_ref) for tt, hh in UU]
        v4 = [conv(xv_s, hh, base + tt * TILE_Q, wv_ref) for tt, hh in UU]
        beta_all, G_all = [], []
        for tt in range(tpi):
            ba = ba_ref[pl.ds(r0 + tt * TILE_Q, TILE_Q), :]
            live = (row + (r0 + tt * TILE_Q)) < t_real
            beta_all.append(jnp.where(live, _sigmoid(ba), 0.0))
            g_all = jnp.where(live, -jnp.exp(alog_ref[...]) * _softplus(ba + dtb_ref[...]), 0.0)
            gh, gm, gl = _split3(g_all)
            G_all.append(jnp.dot(tri, gh, preferred_element_type=F32) + jnp.dot(tri, gm, preferred_element_type=F32)
                         + jnp.dot(tri, gl, preferred_element_type=F32))
        head = [hg * GDN_HPS + hh for _, hh in UU]
        q = [l2n(q4[u]) * (D ** -0.5) for u in UI]
        k = [l2n(k4[u]) for u in UI]
        bc = [jnp.sum(jnp.where(lane == head[u], beta_all[UU[u][0]], 0.0), axis=-1, keepdims=True) for u in UI]
        Gc = [jnp.broadcast_to(jnp.sum(jnp.where(lane == head[u] + GDN_HEADS, G_all[UU[u][0]], 0.0), axis=-1,
                                       keepdims=True), (TILE_Q, TILE_Q)) for u in UI]
        gamma = [jnp.exp(jnp.where(incl, Gc[u] - Gc[u].T, NEG)) for u in UI]
        kk = [_dot_nt(k[u], k[u]) for u in UI]
        qk = [_dot_nt(q[u], k[u]) for u in UI]
        X = [jnp.where(strict, bc[u] * kk[u] * gamma[u], 0.0) for u in UI]
        P = [eye - X[u] for u in UI]
        for _ in range(int(math.log2(C)) - 1):
            X = [_dot(X[u], X[u]) for u in UI]
            P = [P[u] + _dot(P[u], X[u]) for u in UI]
        eG = [jnp.exp(Gc[u]) for u in UI]
        rhs = [jnp.concatenate([v4[u] * bc[u], k[u] * (bc[u] * eG[u])], axis=-1) for u in UI]
        sol = [rhs[u] + _dot(P[u] - eye, rhs[u]) for u in UI]
        a_in = [qk[u] * gamma[u] for u in UI]
        q_dec = [q[u] * eG[u] for u in UI]
        kdT = [(k[u] * jnp.exp(jnp.where(row < C, Gc[u][C - 1:C, :], Gc[u][2 * C - 1:2 * C, :]) - Gc[u])).T
               for u in UI]
        S = list(S)
        for tt in range(tpi):
            us = [tt * GDN_HPS + hh for hh in HH]
            oq, vn = [[] for _ in HH], [[] for _ in HH]
            for c in range(TILE_Q // C):
                cs = slice(c * C, (c + 1) * C)
                r = [_dot(jnp.concatenate([sol[us[hh]][cs, D:], q_dec[us[hh]][cs]], axis=0), S[hh]) for hh in HH]
                for hh in HH:
                    oq[hh].append(r[hh][C:])
                    vn[hh].append(sol[us[hh]][cs, :D] - r[hh][:C])
                vpad = [jnp.concatenate([vn[hh][c], zpad] if c == 0 else [zpad, vn[hh][c]], axis=0) for hh in HH]
                S = [S[hh] * jnp.exp(Gc[us[hh]][(c + 1) * C - 1:(c + 1) * C, :]) + _dot(kdT[us[hh]], vpad[hh])
                     for hh in HH]
            o = [jnp.concatenate(oq[hh], axis=0) + _dot(a_in[us[hh]], jnp.concatenate(vn[hh], axis=0)) for hh in HH]
            rt = r0 + tt * TILE_Q
            outs = [_rms(o[hh], gain) * _silu(z_ref[pl.ds(rt, TILE_Q), hsl[hh]]) for hh in HH]
            o_ref[pl.ds(rt, TILE_Q), :] = jnp.concatenate(outs, axis=-1).astype(o_ref.dtype)
        return tuple(S)

    S = lax.fori_loop(0, T // RW, tile, tuple(s0_ref[hh] for hh in HH))
    for hh in HH:
        s_ref[hh] = S[hh]


def _gdn(gq, ba, z, conv_buf, s0, w_conv, a_log, dt_bias, norm_out, t_real):
    B, T, _ = gq.shape
    H = GDN_HEADS
    pad16 = lambda x: jnp.concatenate([jnp.zeros((H,), F32), x.astype(F32), jnp.zeros((LANES - 2 * H,), F32)])
    ng = H // GDN_HPS
    tpi = GDN_TPI if (T // TILE_Q) % GDN_TPI == 0 else 1
    col = lambda off: pl.BlockSpec((None, T, GDN_SW), lambda b, g: (b, 0, off * ng + g))
    bcol = lambda off: pl.BlockSpec((None, 3, GDN_SW), lambda b, g: (b, 0, off * ng + g))
    wcol = lambda off: pl.BlockSpec((4, GDN_SW), lambda b, g: (0, off * ng + g))
    sblk = pl.BlockSpec((None, GDN_HPS, GDN_HEAD_DIM, GDN_HEAD_DIM), lambda b, g: (b, g, 0, 0))
    o, s_new = pl.pallas_call(
        functools.partial(_gdn_kernel, t_real=t_real, tpi=tpi),
        grid=(B, ng),
        in_specs=[col(0), col(1), col(2), bcol(0), bcol(1), bcol(2), wcol(0), wcol(1), wcol(2),
                  pl.BlockSpec((None, T, LANES), lambda b, g: (b, 0, 0)),
                  _resident((1, LANES)), _resident((1, LANES)),
                  col(0), _resident((1, LANES)), sblk],
        out_specs=[col(0), sblk],
        out_shape=[jax.ShapeDtypeStruct((B, T, GDN_W), BF16),
                   jax.ShapeDtypeStruct((B, H, GDN_HEAD_DIM, GDN_HEAD_DIM), F32)],
        scratch_shapes=[pltpu.VMEM((GDN_HPS, tpi * TILE_Q + CONV_PAD, LANES), F32)] * 3,
        compiler_params=_cparams(("parallel", "parallel")),
        name="gdn",
    )(gq, gq, gq, conv_buf, conv_buf, conv_buf, w_conv, w_conv, w_conv, ba,
      pad16(a_log).reshape(1, LANES), pad16(dt_bias).reshape(1, LANES), z, norm_out.reshape(1, LANES), s0)
    return o, s_new


def _gdn_sample_kernel(x_ref, b_ref, w_ref, ba_ref, alog_ref, dtb_ref, z_ref, gn_ref, s0_ref, o_ref, s_ref, xs,
                       *, t_real):
    R, D, H = SAMPLE_ROWS, GDN_HEAD_DIM, GDN_HEADS
    xs[CONV_PAD - 3:CONV_PAD, :] = b_ref[...]
    xs[CONV_PAD:CONV_PAD + R, :] = x_ref[...]
    y = xs[CONV_PAD - 3:CONV_PAD - 3 + R, :] * w_ref[0:1, :]
    for j in range(1, 4):
        y = y + xs[CONV_PAD - 3 + j:CONV_PAD - 3 + j + R, :] * w_ref[j:j + 1, :]
    y = _silu(y)
    ri = lax.broadcasted_iota(jnp.int32, (R, R), 0)
    ci = lax.broadcasted_iota(jnp.int32, (R, R), 1)
    incl, strict = ri >= ci, ri > ci
    eye = (ri == ci).astype(F32)
    tri = incl.astype(BF16)
    er = lax.broadcasted_iota(jnp.int32, (LANES, LANES), 0)
    ec = lax.broadcasted_iota(jnp.int32, (LANES, LANES), 1)
    eye_l = (er == ec).astype(BF16)
    live = lax.broadcasted_iota(jnp.int32, (R, LANES), 0) < t_real
    ba = ba_ref[...]
    beta_all = jnp.where(live, _sigmoid(ba), 0.0)
    g_all = jnp.where(live, -jnp.exp(alog_ref[...]) * _softplus(ba + dtb_ref[...]), 0.0)
    d32 = functools.partial(jnp.dot, preferred_element_type=F32)
    gh, gm, gl = _split3(g_all)
    G_all = d32(tri, gh) + d32(tri, gm) + d32(tri, gl)
    th, tm, tl = _split3(G_all)
    nt = lambda a, b: lax.dot_general(a, b, (((1,), (1,)), ((), ())), preferred_element_type=F32)
    G_allT = nt(eye_l, th) + nt(eye_l, tm) + nt(eye_l, tl)
    gain = gn_ref[...]
    HR = range(H)

    def l2n(x):
        return x * lax.rsqrt(jnp.sum(x * x, axis=-1, keepdims=True) + EPS)

    q = [l2n(y[:, h * D:(h + 1) * D]) * (D ** -0.5) for h in HR]
    k = [l2n(y[:, GDN_W + h * D:GDN_W + (h + 1) * D]) for h in HR]
    v = [y[:, 2 * GDN_W + h * D:2 * GDN_W + (h + 1) * D] for h in HR]
    bc = [beta_all[:, h:h + 1] for h in HR]
    Gc = [G_all[:, H + h:H + h + 1] for h in HR]
    gamma = [jnp.exp(jnp.where(incl, Gc[h] - G_allT[H + h:H + h + 1, :], NEG)) for h in HR]
    kk = [_dot_nt(k[h], k[h]) for h in HR]
    qk = [_dot_nt(q[h], k[h]) for h in HR]
    X = [jnp.where(strict, bc[h] * kk[h] * gamma[h], 0.0) for h in HR]
    P = [eye - X[h] for h in HR]
    for _ in range(int(math.log2(R)) - 1):
        X = [_dot(X[h], X[h]) for h in HR]
        P = [P[h] + _dot(P[h], X[h]) for h in HR]
    eG = [jnp.exp(Gc[h]) for h in HR]
    rhs = [jnp.concatenate([v[h] * bc[h], k[h] * (bc[h] * eG[h])], axis=-1) for h in HR]
    sol = [rhs[h] + _dot(P[h] - eye, rhs[h]) for h in HR]
    S = [s0_ref[h] for h in HR]
    r = [_dot(jnp.concatenate([sol[h][:, D:], q[h] * eG[h]], axis=0), S[h]) for h in HR]
    v_new = [sol[h][:, :D] - r[h][:R] for h in HR]
    o = [r[h][R:] + _dot(qk[h] * gamma[h], v_new[h]) for h in HR]
    kdT = [nt(eye_l, (k[h] * jnp.exp(Gc[h][R - 1:R, :] - Gc[h])).astype(BF16)) for h in HR]
    for h in HR:
        s_ref[h] = S[h] * jnp.exp(Gc[h][R - 1:R, :]) + _dot(kdT[h], v_new[h])
    o_ref[...] = jnp.concatenate([_rms(o[h], gain) * _silu(z_ref[:, h * D:(h + 1) * D]) for h in HR], axis=-1)


def _gdn_sample(gq, ba, z, conv_buf, s0, w_conv, a_log, dt_bias, norm_out, t_real):
    B, R, _ = gq.shape
    H = GDN_HEADS
    pad16 = lambda x: jnp.concatenate([jnp.zeros((H,), F32), x.astype(F32), jnp.zeros((LANES - 2 * H,), F32)])
    blk = lambda *s: pl.BlockSpec((None,) + s, lambda b: (b,) + (0,) * len(s))
    return pl.pallas_call(
        functools.partial(_gdn_sample_kernel, t_real=t_real),
        grid=(B,),
        in_specs=[blk(R, 3 * GDN_W), blk(3, 3 * GDN_W), _resident(w_conv.shape), blk(R, LANES),
                  _resident((1, LANES)), _resident((1, LANES)), blk(R, GDN_W), _resident((1, LANES)),
                  blk(H, GDN_HEAD_DIM, GDN_HEAD_DIM)],
        out_specs=[blk(R, GDN_W), blk(H, GDN_HEAD_DIM, GDN_HEAD_DIM)],
        out_shape=[jax.ShapeDtypeStruct((B, R, GDN_W), F32),
                   jax.ShapeDtypeStruct((B, H, GDN_HEAD_DIM, GDN_HEAD_DIM), F32)],
        scratch_shapes=[pltpu.VMEM((CONV_PAD + R, 3 * GDN_W), F32)],
        compiler_params=_cparams(("parallel",)),
        name="gdn_sample",
    )(gq, conv_buf, w_conv, ba, pad16(a_log).reshape(1, LANES), pad16(dt_bias).reshape(1, LANES), z,
      norm_out.reshape(1, LANES), s0)


def _mix_kernel(*refs):
    n_og = 3 * DIL_SLABS
    o_refs, l_refs = refs[:n_og], refs[n_og:2 * n_og]
    ob_ref, ga_ref, gb_ref, h_ref, wa_ref, wb_ref, wo_ref, gq_ref, wq_ref, h1_ref, qm_ref = refs[2 * n_og:]
    slabs = []
    for sl in range(DIL_SLABS):
        l0, l1, l2 = (l_refs[g * DIL_SLABS + sl][...] for g in range(3))
        o0, o1, o2 = (o_refs[g * DIL_SLABS + sl][...] for g in range(3))
        mx = jnp.maximum(jnp.maximum(l0, l1), l2)
        e0, e1, e2 = jnp.exp(l0 - mx), jnp.exp(l1 - mx), jnp.exp(l2 - mx)
        slabs.append((e0 * o0 + e1 * o1 + e2 * o2) / (e0 + e1 + e2))
    o_a = jnp.concatenate(slabs, axis=-1)
    a = _dot(o_a, wa_ref[...])
    b = jnp.dot(ob_ref[...], wb_ref[...], preferred_element_type=F32)
    merged = _sigmoid(ga_ref[...]) * a + _sigmoid(gb_ref[...]) * b
    h1 = h_ref[...] + _dot(merged, wo_ref[...])
    h1_ref[...] = h1
    qm_ref[...] = _dot(_rms(h1, gq_ref[...]), wq_ref[...]).astype(qm_ref.dtype)


def _mix(o_g, l_g, o_b, ga, gb, h, w_a, w_b, w_o, norm_mem_q, w_mem_q, tm):
    rows, d = h.shape
    rt = lambda n: pl.BlockSpec((tm, n), lambda i: (i, 0))
    assert len(o_g) == len(l_g) == 3 * DIL_SLABS
    return pl.pallas_call(
        _mix_kernel,
        grid=(rows // tm,),
        in_specs=[rt(LANES)] * (6 * DIL_SLABS) + [rt(GDN_W), rt(d), rt(d), rt(d),
                                     _resident(w_a.shape), _resident(w_b.shape), _resident(w_o.shape),
                                     _resident((1, d)), _resident(w_mem_q.shape)],
        out_specs=[rt(d), rt(w_mem_q.shape[1])],
        out_shape=[jax.ShapeDtypeStruct((rows, d), F32), jax.ShapeDtypeStruct((rows, w_mem_q.shape[1]), BF16)],
        compiler_params=_cparams(("parallel",)),
        name="mix",
    )(*o_g, *l_g, o_b, ga, gb, h, w_a, w_b, w_o, norm_mem_q.reshape(1, d), w_mem_q)


def _mem_kv_kernel(x_ref, g_ref, w_ref, k_ref, v_ref):
    u = _rms(x_ref[...], g_ref[...]).astype(BF16)
    tm = x_ref.shape[0]
    n = MEM_HEADS * MEM_HEAD_DIM
    for o_ref, off in ((k_ref, 0), (v_ref, n)):
        r = jnp.dot(u, w_ref[:, off:off + n], preferred_element_type=F32)
        for h in range(MEM_HEADS):
            o_ref[pl.ds(h, tm, stride=MEM_HEADS), :] = r[:, h * MEM_HEAD_DIM:(h + 1) * MEM_HEAD_DIM]


def _mem_kv(mem2d, gain, w, tm):
    rows, d = mem2d.shape
    return pl.pallas_call(
        _mem_kv_kernel,
        grid=(rows // tm,),
        in_specs=[pl.BlockSpec((tm, d), lambda i: (i, 0)), _resident((1, d)), _resident(w.shape)],
        out_specs=[pl.BlockSpec((tm * MEM_HEADS, MEM_HEAD_DIM), lambda i: (i, 0))] * 2,
        out_shape=[jax.ShapeDtypeStruct((rows * MEM_HEADS, MEM_HEAD_DIM), F32)] * 2,
        compiler_params=_cparams(("parallel",)),
        name="mem_kv",
    )(mem2d, gain.reshape(1, d), w)


def _mem_attn_kernel(q_ref, k_ref, v_ref, o_ref):
    M = k_ref.shape[1] // MEM_HEADS
    for b in range(q_ref.shape[0]):
        q = q_ref[b]
        outs = []
        for h in range(MEM_HEADS):
            k = k_ref[b, pl.ds(h, M, stride=MEM_HEADS), :]
            v = v_ref[b, pl.ds(h, M, stride=MEM_HEADS), :]
            s = _dot_nt(q[:, h * MEM_HEAD_DIM:(h + 1) * MEM_HEAD_DIM], k) * (MEM_HEAD_DIM ** -0.5)
            p = jnp.exp(s - jnp.max(s, axis=-1, keepdims=True))
            outs.append(_dot(p, v) / jnp.sum(p, axis=-1, keepdims=True))
        o_ref[b] = jnp.concatenate(outs, axis=-1).astype(o_ref.dtype)


def _mem_attn(qm, mem_k, mem_v, tm, nb):
    B, T, w = qm.shape
    kv_spec = pl.BlockSpec((nb,) + mem_k.shape[1:], lambda b, j: (b, 0, 0))
    return pl.pallas_call(
        _mem_attn_kernel,
        grid=(B // nb, T // tm),
        in_specs=[pl.BlockSpec((nb, tm, w), lambda b, j: (b, j, 0)), kv_spec, kv_spec],
        out_specs=pl.BlockSpec((nb, tm, w), lambda b, j: (b, j, 0)),
        out_shape=jax.ShapeDtypeStruct((B, T, w), F32),
        compiler_params=_cparams(("parallel", "parallel")),
        name="mem_attn",
    )(qm, mem_k, mem_v)


def _ffn_kernel(*refs, inject, emit_gate):
    if inject:
        (h1_ref, om_ref, init_ref, fill_ref, wmo_ref, gf_ref, wup_ref, wc_ref, bc_ref, wd_ref, gfin_ref,
         y_ref, fc_ref, gs) = refs
    else:
        (h1_ref, om_ref, init_ref, wmo_ref, gf_ref, wup_ref, wc_ref, bc_ref, wd_ref, gfin_ref,
         y_ref, fc_ref, gs) = refs
    tm = h1_ref.shape[0]
    F = wd_ref.shape[0]
    PAD = SUBLANES

    @pl.when(pl.program_id(1) == 0)
    def _():
        gs[PAD - 2:PAD, :] = init_ref[...]

    h2 = h1_ref[...] + _dot(om_ref[...], wmo_ref[...])
    n = _rms(h2, gf_ref[...]).astype(BF16)
    gate = jnp.dot(n, wup_ref[:, :F], preferred_element_type=F32)
    if inject:
        r = lax.broadcasted_iota(jnp.int32, (tm, 1), 0)
        gate = jnp.where((r % SAMPLE_ROWS) >= SAMPLE_ROWS - 2, fill_ref[...], gate)
    gs[PAD:PAD + tm, :] = gate
    conv = (gs[PAD - 2:PAD - 2 + tm, :] * wc_ref[0:1, :] + gs[PAD - 1:PAD - 1 + tm, :] * wc_ref[1:2, :]
            + gate * wc_ref[2:3, :])
    last2 = gs[PAD + tm - 2:PAD + tm, :]
    gs[PAD - 2:PAD, :] = last2
    if emit_gate:
        fc_ref[...] = gate
    else:
        fc_ref[...] = last2
    up = jnp.dot(n, wup_ref[:, F:], preferred_element_type=F32)
    act = _silu(conv + bc_ref[...]) * up
    y = h2 + _dot(act, wd_ref[...])
    y_ref[...] = _rms(y, gfin_ref[...])


def _ffn(h1, om, init, fill, w_mo, norm_ffn, w_up, w_conv, b_conv, w_down, norm_final, tm, emit_gate):
    B, T, d = h1.shape
    F = w_down.shape[0]
    inject = fill is not None
    rt = lambda n: pl.BlockSpec((None, tm, n), lambda b, j: (b, j, 0))
    in_specs = [rt(d), rt(om.shape[-1]), pl.BlockSpec((None, 2, F), lambda b, j: (b, 0, 0))]
    args = [h1, om, init]
    if inject:
        in_specs.append(rt(F))
        args.append(fill)
    in_specs += [_resident(w_mo.shape), _resident((1, d)), _resident(w_up.shape), _resident(w_conv.shape),
                 _resident((1, F)), _resident(w_down.shape), _resident((1, d))]
    args += [w_mo, norm_ffn.reshape(1, d), w_up, w_conv, b_conv.reshape(1, F), w_down, norm_final.reshape(1, d)]
    if emit_gate:
        fc_spec, fc_shape = rt(F), jax.ShapeDtypeStruct((B, T, F), F32)
    else:
        fc_spec = pl.BlockSpec((None, 2, F), lambda b, j: (b, 0, 0))
        fc_shape = jax.ShapeDtypeStruct((B, 2, F), F32)
    return pl.pallas_call(
        functools.partial(_ffn_kernel, inject=inject, emit_gate=emit_gate),
        grid=(B, T // tm),
        in_specs=in_specs,
        out_specs=[rt(d), fc_spec],
        out_shape=[jax.ShapeDtypeStruct((B, T, d), F32), fc_shape],
        scratch_shapes=[pltpu.VMEM((tm + SUBLANES, F), F32)],
        compiler_params=_cparams(("parallel", "arbitrary")),
        name="ffn",
    )(*args)


def kernel(x_prompt, x_sample, cache_dil0_kv, cache_dil1_kv, cache_dil2_kv, state_delta, state_delta_conv, cache_mem_k, cache_mem_v, state_ffn_conv, mem_prompt, rel_bias, norm_mix, w_in, w_conv_delta, a_log, dt_bias, norm_delta_out, w_branch_a, w_branch_b, w_out, norm_mem_q, norm_mem_kv, w_mem_q, w_mem_kv, w_mem_o, norm_ffn, w_ffn_up, w_ffn_conv, b_ffn_conv, w_ffn_down, norm_final):
    B, S, D = x_prompt.shape
    Bs, Ts, _ = x_sample.shape
    depth = w_in.shape[0]
    assert depth == 1 and Ts <= SAMPLE_ROWS - 2 and Ts >= 3 and S % (16 * TILE_Q) == 0
    F = w_ffn_down.shape[1]
    M = mem_prompt.shape[1]
    l = 0
    w_arr = _arrange_w_in(w_in[l])
    w_a, w_b, w_o = (w.astype(BF16) for w in (w_branch_a[l], w_branch_b[l], w_out[l]))
    w_mq, w_mkv, w_mo = (w.astype(BF16) for w in (w_mem_q[l], w_mem_kv[l], w_mem_o[l]))
    w_up, w_dn = w_ffn_up[l].astype(BF16), w_ffn_down[l].astype(BF16)
    t_cat, t_cur = _prompt_bias_tables(rel_bias)

    xp = x_prompt.reshape(B * S, D)
    q, kv0, kv1, kv2, kt0, kt1, kt2, gq, z, ba, ga, gb = _in_proj(xp, norm_mix[l], w_arr, 256, seq=S)
    kvs = [kv.reshape(B, S, 2 * DIL_GW) for kv in (kv0, kv1, kv2)]
    q3 = q.reshape(B, S, 3 * DIL_GW)
    o_g, l_g = [], []
    for g in range(3):
        o_sl, l_sl = _dil_prompt(q3, kvs[g], g, t_cat[g], t_cur[g])
        o_g += o_sl
        l_g += l_sl
    o_b, delta_p = _gdn(gq.reshape(B, S, -1), ba.reshape(B, S, LANES), z.reshape(B, S, GDN_W),
                        jnp.zeros((B, 3, 3 * GDN_W), F32), jnp.zeros((B, GDN_HEADS, GDN_HEAD_DIM, GDN_HEAD_DIM), F32),
                        w_conv_delta[l], a_log[l], dt_bias[l], norm_delta_out[l], S)
    h1, qm = _mix(o_g, l_g, o_b.reshape(B * S, GDN_W), ga, gb, xp, w_a, w_b, w_o, norm_mem_q[l], w_mq, 256)
    mk_p, mv_p = _mem_kv(mem_prompt.reshape(B * M, D), norm_mem_kv[l], w_mkv, 256)
    mk_p, mv_p = (x.reshape(B, M * MEM_HEADS, MEM_HEAD_DIM) for x in (mk_p, mv_p))
    om = _mem_attn(qm.reshape(B, S, -1), mk_p, mv_p, 512, 1)
    y_p, fconv_p = _ffn(h1.reshape(B, S, D), om, jnp.zeros((B, 2, F), F32), None, w_mo, norm_ffn[l], w_up,
                        w_ffn_conv[l], b_ffn_conv[l], w_dn, norm_final, 256, False)
    gq3 = gq.reshape(B, S, -1)
    p_out = ([kt[:, :, S - min(w, S):].reshape(B, 2, DIL_HPG, DIL_HEAD_DIM, min(w, S)).transpose(0, 4, 1, 2, 3)[None]
              for kt, (w, _) in zip((kt0, kt1, kt2), DIL_GROUPS)]
             + [delta_p[None], gq3[:, S - 3:][None], mk_p.reshape(1, B, M, MEM_HEADS, MEM_HEAD_DIM),
                mv_p.reshape(1, B, M, MEM_HEADS, MEM_HEAD_DIM), fconv_p[None]])

    R = SAMPLE_ROWS
    xs = jnp.pad(x_sample, ((0, 0), (0, R - Ts), (0, 0))).reshape(Bs * R, D)
    q, kv0, kv1, kv2, gq, z, ba, ga, gb = _in_proj(xs, norm_mix[l], w_arr, Bs * R)
    kvn = [kv.reshape(Bs, R, 2 * DIL_GW) for kv in (kv0, kv1, kv2)]
    caches_t = [jnp.transpose(c[l], (0, 2, 3, 4, 1)).reshape(Bs, 2 * DIL_GW, c.shape[2])
                for c in (cache_dil0_kv, cache_dil1_kv, cache_dil2_kv)]
    tabc, tabn = _sample_bias_tables(rel_bias, Ts)
    o_g, l_g, new_caches = _dil_sample(q.reshape(Bs, R, 3 * DIL_GW), kvn, caches_t, tabc, tabn, Ts)
    o_b, delta_s = _gdn_sample(gq.reshape(Bs, R, -1), ba.reshape(Bs, R, LANES), z.reshape(Bs, R, GDN_W),
                               state_delta_conv[l], state_delta[l], w_conv_delta[l], a_log[l], dt_bias[l],
                               norm_delta_out[l], Ts)
    o_b = o_b.reshape(Bs * R, GDN_W).astype(BF16)
    h1, qm = _mix(o_g, l_g, o_b, ga, gb, xs, w_a, w_b, w_o, norm_mem_q[l], w_mq, Bs * R)
    om = _mem_attn(qm.reshape(Bs, R, -1), cache_mem_k[l].reshape(Bs, M * MEM_HEADS, MEM_HEAD_DIM),
                   cache_mem_v[l].reshape(Bs, M * MEM_HEADS, MEM_HEAD_DIM), R, 4 if Bs % 4 == 0 else 1)
    fst = state_ffn_conv[l]
    fill = jnp.concatenate([jnp.zeros((Bs, R - 2, F), F32),
                            jnp.concatenate([fst[1:], jnp.zeros((1, 2, F), F32)], axis=0)], axis=1)
    y_s, gate_s = _ffn(h1.reshape(1, Bs * R, D), om.reshape(1, Bs * R, -1), fst[:1], fill.reshape(1, Bs * R, F),
                       w_mo, norm_ffn[l], w_up, w_ffn_conv[l], b_ffn_conv[l], w_dn, norm_final, Bs * R, True)
    y_s = y_s.reshape(Bs, R, D)[:, :Ts]
    gq3 = gq.reshape(Bs, R, -1)
    s_out = ([nc.reshape(Bs, 2, DIL_HPG, DIL_HEAD_DIM, nc.shape[2]).transpose(0, 4, 1, 2, 3)[None]
              for nc in new_caches]
             + [delta_s[None], gq3[:, Ts - 3:Ts][None], gate_s.reshape(Bs, R, F)[:, Ts - 2:Ts][None]])

    return (y_p.reshape(B, S, D), y_s, *p_out, *s_out)
```

```python
import functools
import math

import jax
import jax.numpy as jnp
import numpy as np
from jax import lax
from jax.experimental import pallas as pl
from jax.experimental.pallas import tpu as pltpu

F32 = jnp.float32
BF16 = jnp.bfloat16

PAST_LEN = 8192
DIL_GROUPS = ((128, 1), (512, 4), (2048, 16))
DIL_HPG = 4
DIL_HEAD_DIM = 64
DIL_GW = DIL_HPG * DIL_HEAD_DIM
DIL_NK = 129
REL_BUCKETS = 32
REL_MAX_DIST = 2048
GDN_HEADS = 8
GDN_HEAD_DIM = 128
GDN_W = GDN_HEADS * GDN_HEAD_DIM
GDN_CHUNK = 64
MEM_HEADS = 4
MEM_HEAD_DIM = 128
EPS = 1e-6
NEG = -1e30

LANES = 128
SUBLANES = 8
TILE_Q = 128
SAMPLE_ROWS = SUBLANES
CONV_PAD = SUBLANES
VMEM_LIMIT = 56 * 1024 * 1024


def _cparams(sem):
    return pltpu.CompilerParams(dimension_semantics=sem, vmem_limit_bytes=VMEM_LIMIT)


def _resident(shape):
    nd = len(shape)
    return pl.BlockSpec(shape, lambda *_: (0,) * nd, pipeline_mode=pl.Buffered(1))


def _rms(x, gain_row):
    return x * lax.rsqrt(jnp.mean(x * x, axis=-1, keepdims=True) + EPS) * gain_row


def _dot(a, b):
    return jnp.dot(a.astype(BF16), b.astype(BF16), preferred_element_type=F32)


def _dot_nt(a, b):
    return lax.dot_general(a.astype(BF16), b.astype(BF16), (((1,), (1,)), ((), ())), preferred_element_type=F32)


def _dot_tn(a, b):
    return lax.dot_general(a.astype(BF16), b.astype(BF16), (((0,), (0,)), ((), ())), preferred_element_type=F32)


def _split3(x):
    hi = x.astype(BF16)
    r1 = x - hi.astype(F32)
    mid = r1.astype(BF16)
    lo = (r1 - mid.astype(F32)).astype(BF16)
    return hi, mid, lo


def _dot_hp(a, b):
    ah, am, al = _split3(a)
    bh, bm, bl = _split3(b)
    d = functools.partial(jnp.dot, preferred_element_type=F32)
    return d(ah, bh) + (d(ah, bm) + d(am, bh)) + (d(am, bm) + d(ah, bl) + d(al, bh))


def _sigmoid(x):
    return 1.0 / (1.0 + jnp.exp(-x))


def _silu(x):
    return x * _sigmoid(x)


def _softplus(x):
    return jnp.maximum(x, 0.0) + jnp.log(1.0 + jnp.exp(-jnp.abs(x)))


IN_SEGS = (("q", 3 * DIL_GW), ("kv0", 2 * DIL_GW), ("kv1", 2 * DIL_GW), ("kv2", 2 * DIL_GW),
           ("gq", 3 * GDN_W), ("z", GDN_W), ("ba", LANES), ("ga", 1024), ("gb", 1024))


OFF_Q, OFF_K, OFF_V = 0, 3 * DIL_GW, 6 * DIL_GW
OFF_GQ = 9 * DIL_GW
OFF_Z = OFF_GQ + 3 * GDN_W
OFF_BA = OFF_Z + GDN_W
OFF_GATES = OFF_BA + 2 * GDN_HEADS


def _arrange_w_in(w_in):
    w = w_in.astype(BF16)
    return w, w[:, OFF_GATES:]


GQ_SLABS = 3 * GDN_W // LANES


def _l2n(x):
    return x * lax.rsqrt(jnp.sum(x * x, axis=-1, keepdims=True) + EPS)


def _in_proj_kernel(x_ref, g_ref, w_ref, wg_ref, *rest, seq_tiles):
    if seq_tiles:
        (cb_ref, wc_ref, q_ref, kv0_ref, kv1_ref, kv2_ref, kt0_ref, kt1_ref, kt2_ref, gq_ref, tail_ref, z_ref,
         ba_ref, ga_ref, gb_ref, cs) = rest
        kt_refs = (kt0_ref, kt1_ref, kt2_ref)
    else:
        q_ref, kv0_ref, kv1_ref, kv2_ref, gq_ref, z_ref, ba_ref, ga_ref, gb_ref = rest
    tm = x_ref.shape[0]
    u = _rms(x_ref[...], g_ref[...]).astype(BF16)

    def seg(ref, off, n):
        return jnp.dot(u, ref[:, off:off + n], preferred_element_type=F32)

    def q_seg():
        q_ref[...] = seg(w_ref, OFF_Q, 3 * DIL_GW) * (DIL_HEAD_DIM ** -0.5)

    def kv_seg(g, part):
        kv_ref = (kv0_ref, kv1_ref, kv2_ref)[g]
        r = seg(w_ref, (OFF_K, OFF_V)[part] + g * DIL_GW, DIL_GW)
        kv_ref[:, part * DIL_GW:(part + 1) * DIL_GW] = r
        if seq_tiles:
            kt_refs[g][part * DIL_GW:(part + 1) * DIL_GW, :] = r.T

    def ba_seg():
        ba_ref[...] = seg(w_ref, OFF_BA, LANES)

    def gate_seg(ref, off):
        ref[...] = seg(wg_ref, off, 1024)

    others = ([(q_seg, 3 * DIL_GW)] + [(functools.partial(kv_seg, g, p), DIL_GW) for g in range(3) for p in range(2)]
              + [(functools.partial(gate_seg, ga_ref, 0), 1024), (functools.partial(gate_seg, gb_ref, 1024), 1024),
                 (ba_seg, LANES)])
    if not seq_tiles:
        for f, _ in others:
            f()
        for c in range(3):
            gq_ref[:, c * GDN_W:(c + 1) * GDN_W] = seg(w_ref, OFF_GQ + c * GDN_W, GDN_W)
        z_ref[...] = seg(w_ref, OFF_Z, GDN_W)
        return
    i = pl.program_id(0)
    first = (i % seq_tiles) == 0

    @pl.when(first)
    def _():
        for s in range(GQ_SLABS):
            cs[s, CONV_PAD - 3:CONV_PAD, :] = cb_ref[:, s * LANES:(s + 1) * LANES]

    @pl.when(jnp.logical_not(first))
    def _():
        for s in range(GQ_SLABS):
            cs[s, CONV_PAD - 3:CONV_PAD, :] = cs[s, CONV_PAD + tm - 3:CONV_PAD + tm, :]

    for c in range(3):
        r = seg(w_ref, OFF_GQ + c * GDN_W, GDN_W)
        for hh in range(GDN_HEADS):
            cs[c * GDN_HEADS + hh, CONV_PAD:CONV_PAD + tm, :] = r[:, hh * LANES:(hh + 1) * LANES]
    base = CONV_PAD - 3 + jnp.minimum(i, 0)

    def conv_slab(s):
        ls = slice(s * LANES, (s + 1) * LANES)
        y = cs[s, pl.ds(base, tm), :] * wc_ref[0:1, ls]
        for j in range(1, 4):
            y = y + cs[s, pl.ds(base + j, tm), :] * wc_ref[j:j + 1, ls]
        y = _silu(y)
        if s < GDN_HEADS:
            y = _l2n(y) * (GDN_HEAD_DIM ** -0.5)
        elif s < 2 * GDN_HEADS:
            y = _l2n(y)
        gq_ref[:, ls] = y
        tail_ref[:, ls] = cs[s, tm:tm + SUBLANES, :]

    def z_seg():
        z_ref[...] = _silu(seg(w_ref, OFF_Z, GDN_W))

    others.insert(len(others) - 1, (z_seg, GDN_W))
    total = sum(n for _, n in others)
    done, cols = 0, 0
    for f, n in others:
        f()
        cols += n
        upto = min(GQ_SLABS, (cols * GQ_SLABS + total - 1) // total)
        for s in range(done, upto):
            conv_slab(s)
        done = upto
    assert done == GQ_SLABS


def _in_proj(x2d, gain, w_arr, tm, seq=None, conv_buf=None, w_conv=None):
    rows, d = x2d.shape
    fused = seq is not None
    nt = seq // tm if fused else None
    names = [n for n, _ in IN_SEGS]
    widths = dict(IN_SEGS)
    row_spec = lambda n: pl.BlockSpec((tm, n), lambda i: (i, 0))
    out_specs, out_shape = [], []
    for n in names:
        out_specs.append(row_spec(widths[n]))
        out_shape.append(jax.ShapeDtypeStruct((rows, widths[n]), F32))
        if n == "kv2" and fused:
            for _ in range(3):
                out_specs.append(pl.BlockSpec((None, 2 * DIL_GW, tm), lambda i: (i // nt, 0, i % nt)))
                out_shape.append(jax.ShapeDtypeStruct((rows // seq, 2 * DIL_GW, seq), F32))
        if n == "gq" and fused:
            out_specs.append(pl.BlockSpec((None, SUBLANES, 3 * GDN_W), lambda i: (i // nt, 0, 0)))
            out_shape.append(jax.ShapeDtypeStruct((rows // seq, SUBLANES, 3 * GDN_W), F32))
    in_specs = [pl.BlockSpec((tm, d), lambda i: (i, 0)), _resident((1, d)), _resident(w_arr[0].shape),
                _resident(w_arr[1].shape)]
    args = [x2d, gain.reshape(1, d), *w_arr]
    if fused:
        in_specs += [pl.BlockSpec((None, 3, 3 * GDN_W), lambda i: (i // nt, 0, 0)), _resident(w_conv.shape)]
        args += [conv_buf, w_conv]
    return pl.pallas_call(
        functools.partial(_in_proj_kernel, seq_tiles=nt),
        grid=(rows // tm,),
        in_specs=in_specs,
        out_specs=out_specs,
        out_shape=out_shape,
        scratch_shapes=[pltpu.VMEM((GQ_SLABS, CONV_PAD + tm, LANES), F32)] if fused else [],
        compiler_params=_cparams(("arbitrary",)),
        name="in_proj",
    )(*args)


def _rel_bucket(dist):
    exact = REL_BUCKETS // 2
    d = jnp.maximum(dist, 1).astype(F32)
    large = exact + (jnp.log(d / exact) / math.log(REL_MAX_DIST / exact) * (REL_BUCKETS - exact)).astype(jnp.int32)
    return jnp.where(dist < exact, dist, jnp.minimum(large, REL_BUCKETS - 1))


def _group_bias(rel_bias, g):
    dil = DIL_GROUPS[g][1]
    dist = dil * jnp.arange(DIL_NK, dtype=jnp.int32)
    tab = rel_bias[_rel_bucket(dist)]
    return tab[:, g * DIL_HPG:(g + 1) * DIL_HPG].T.astype(F32)


def _toeplitz(v, n, width):
    h, L = v.shape
    return jnp.tile(v, (1, n))[:, :n * (L - 1)].reshape(h, n, L - 1)[:, :, :width]


def _prompt_bias_tables(rel_bias):
    cat, cur = [], []
    for g in range(3):
        bg = _group_bias(rel_bias, g)
        v = jnp.concatenate([bg[:, ::-1], jnp.full((DIL_HPG, 3 * TILE_Q - DIL_NK), NEG, F32)], axis=1)
        t = _toeplitz(v, TILE_Q, 2 * TILE_Q)
        cat.append(t)
        cur.append(t[:, :, TILE_Q:])
    return cat, cur


DIL_TIF = 2
DIL_SLABS = DIL_GW // LANES


def _dil_prompt_kernel(q0_ref, q1_ref, k0_ref, k1_ref, v0_ref, v1_ref, tcat_ref, tcur_ref,
                       o0_ref, o1_ref, l0_ref, l1_ref, *, dil):
    S = q0_ref.shape[0]
    nb = S // dil // TILE_Q
    q_refs, k_refs, v_refs = (q0_ref, q1_ref), (k0_ref, k1_ref), (v0_ref, v1_ref)
    o_refs, l_refs = (o0_ref, o1_ref), (l0_ref, l1_ref)
    even = lax.broadcasted_iota(jnp.int32, (TILE_Q, LANES), 1) < DIL_HEAD_DIM

    def rows(r, t):
        start = r + dil * TILE_Q * t
        return pl.ds(start, TILE_Q, stride=dil) if dil > 1 else pl.ds(start, TILE_Q)

    tiles = [(r, t) for r in range(dil) for t in range(nb)]
    for i0 in range(0, len(tiles), DIL_TIF):
        grp = tiles[i0:i0 + DIL_TIF]
        qm, kc, vc = {}, {}, {}
        for ti, (r, t) in enumerate(grp):
            for sl in range(DIL_SLABS):
                qf = q_refs[sl][rows(r, t), :]
                qm[ti, 2 * sl] = jnp.where(even, qf, 0.0).astype(BF16)
                qm[ti, 2 * sl + 1] = jnp.where(even, 0.0, qf).astype(BF16)
                kc[ti, sl] = k_refs[sl][rows(r, t), :].astype(BF16)
                vc[ti, sl] = v_refs[sl][rows(r, t), :].astype(BF16)
                if t > 0:
                    kc[ti, sl] = jnp.concatenate([k_refs[sl][rows(r, t - 1), :].astype(BF16), kc[ti, sl]], axis=0)
                    vc[ti, sl] = jnp.concatenate([v_refs[sl][rows(r, t - 1), :].astype(BF16), vc[ti, sl]], axis=0)
        units = [(ti, h) for ti in range(len(grp)) for h in range(DIL_HPG)]
        s = [_dot_nt(qm[ti, h], kc[ti, h // 2]) + (tcat_ref[h] if grp[ti][1] > 0 else tcur_ref[h]) for ti, h in units]
        m = [jnp.max(x, axis=-1, keepdims=True) for x in s]
        p = [jnp.exp(x - mx) for x, mx in zip(s, m)]
        l = [jnp.sum(x, axis=-1, keepdims=True) for x in p]
        pv = [jnp.dot(p[u].astype(BF16), vc[ti, h // 2], preferred_element_type=F32) for u, (ti, h) in enumerate(units)]
        o = [pv[u] / l[u] for u in range(len(units))]
        lse = [m[u] + jnp.log(l[u]) for u in range(len(units))]
        for ti, (r, t) in enumerate(grp):
            for sl in range(DIL_SLABS):
                ue, uo = ti * DIL_HPG + 2 * sl, ti * DIL_HPG + 2 * sl + 1
                o_refs[sl][rows(r, t), :] = jnp.where(even, o[ue], o[uo])
                l_refs[sl][rows(r, t), :] = jnp.where(even, lse[ue], lse[uo])


def _dil_prompt(q, kv, g, t_cat, t_cur):
    B, S, _ = q.shape
    dil = DIL_GROUPS[g][1]
    slab = lambda c: pl.BlockSpec((None, S, LANES), lambda b: (b, 0, c))
    nq, nk = g * DIL_SLABS, 0
    outs = pl.pallas_call(
        functools.partial(_dil_prompt_kernel, dil=dil),
        grid=(B,),
        in_specs=[slab(nq), slab(nq + 1), slab(nk), slab(nk + 1), slab(nk + 2), slab(nk + 3),
                  _resident(t_cat.shape), _resident(t_cur.shape)],
        out_specs=[slab(0)] * 4,
        out_shape=[jax.ShapeDtypeStruct((B, S, LANES), F32)] * 4,
        compiler_params=_cparams(("parallel",)),
        name=f"dil_prompt_g{g}",
    )(q, q, kv, kv, kv, kv, t_cat, t_cur)
    o0, o1, l0, l1 = (x.reshape(B * S, LANES) for x in outs)
    return [o0, o1], [l0, l1]


def _sample_bias_tables(rel_bias, t_real):
    R = SAMPLE_ROWS
    tabc, tabn = [], []
    t_i = np.arange(R)[:, None]
    u_i = np.arange(R)[None, :]
    for g, (w, dil) in enumerate(DIL_GROUPS):
        bg = _group_bias(rel_bias, g)
        base = bg[:, ::-1][:, :TILE_Q]
        t0 = jnp.concatenate([base[:, :, None], jnp.full((DIL_HPG, TILE_Q, dil - 1), NEG, F32)], axis=2)
        t0 = t0.reshape(DIL_HPG, w)
        rows = [jnp.concatenate([jnp.full((DIL_HPG, t), NEG, F32), t0[:, :w - t]], axis=1) for t in range(t_real)]
        rows += [jnp.zeros((DIL_HPG, w), F32)] * (R - t_real)
        tabc.append(jnp.stack(rows, axis=1).reshape(DIL_HPG * R, w))
        tn = jnp.full((DIL_HPG, R, R), NEG, F32)
        for j in range(-(-t_real // dil)):
            hit = (t_i - u_i == j * dil) & (t_i < t_real)
            tn = jnp.where(hit[None], bg[:, j][:, None, None], tn)
        tn = jnp.where((t_i >= t_real)[None], 0.0, tn)
        tabn.append(tn.reshape(DIL_HPG * R, R))
    return tabc, jnp.stack(tabn)


def _dil_sample_kernel(q_ref, n0_ref, n1_ref, n2_ref, c0_ref, c1_ref, c2_ref, tc0_ref, tc1_ref, tc2_ref, tn_ref,
                       o_ref, l_ref, oc0_ref, oc1_ref, oc2_ref, *, t_real):
    R = SAMPLE_ROWS
    rows = lax.broadcasted_iota(jnp.int32, (DIL_HPG * R, DIL_GW), 0)
    lanes = lax.broadcasted_iota(jnp.int32, (DIL_HPG * R, DIL_GW), 1)
    head_mask = (lanes // DIL_HEAD_DIM) == (rows // R)
    lane_f = lax.broadcasted_iota(jnp.int32, (2 * DIL_GW, LANES), 1)
    keep = lane_f < LANES - t_real
    sel_l = lax.broadcasted_iota(jnp.int32, (LANES, R), 0)
    sel_u = lax.broadcasted_iota(jnp.int32, (LANES, R), 1)
    selT = ((sel_l == sel_u + LANES - t_real) & (sel_u < t_real)).astype(BF16)
    groups = ((n0_ref, c0_ref, tc0_ref, oc0_ref), (n1_ref, c1_ref, tc1_ref, oc1_ref), (n2_ref, c2_ref, tc2_ref, oc2_ref))

    def fold_heads(x):
        x = jnp.where(head_mask, x, 0.0)
        return x[0:R] + x[R:2 * R] + x[2 * R:3 * R] + x[3 * R:4 * R]

    for g, (n_ref, c_ref, tc_ref, oc_ref) in enumerate(groups):
        W = c_ref.shape[1]
        kvn = n_ref[...]
        q_g = q_ref[:, g * DIL_GW:(g + 1) * DIL_GW]
        q_bd = jnp.where(head_mask, jnp.concatenate([q_g] * DIL_HPG, axis=0), 0.0)
        s_c = _dot(q_bd, c_ref[:DIL_GW, :]) + tc_ref[...]
        s_n = _dot_nt(q_bd, kvn[:, :DIL_GW]) + tn_ref[g]
        m = jnp.maximum(jnp.max(s_c, axis=-1, keepdims=True), jnp.max(s_n, axis=-1, keepdims=True))
        p_c = jnp.exp(s_c - m)
        p_n = jnp.exp(s_n - m)
        l = jnp.sum(p_c, axis=-1, keepdims=True) + jnp.sum(p_n, axis=-1, keepdims=True)
        acc = (_dot_nt(p_c, c_ref[DIL_GW:, :]) + _dot(p_n, kvn[:, DIL_GW:])) / l
        o_ref[:, g * DIL_GW:(g + 1) * DIL_GW] = fold_heads(acc)
        l_ref[:, g * DIL_GW:(g + 1) * DIL_GW] = fold_heads(jnp.broadcast_to(m + jnp.log(l), acc.shape))
        hi, mid, lo = _split3(kvn)
        tail = (jnp.dot(selT, hi, preferred_element_type=F32) + jnp.dot(selT, mid, preferred_element_type=F32)
                + jnp.dot(selT, lo, preferred_element_type=F32)).T
        nxt = pltpu.roll(c_ref[:, 0:LANES], LANES - t_real, axis=1)
        for c in range(W // LANES):
            cur = nxt
            nxt = (pltpu.roll(c_ref[:, (c + 1) * LANES:(c + 2) * LANES], LANES - t_real, axis=1)
                   if (c + 1) * LANES < W else tail)
            oc_ref[:, c * LANES:(c + 1) * LANES] = jnp.where(keep, cur, nxt)


def _dil_sample(q, kvn, caches_t, tabc, tabn, t_real):
    B = q.shape[0]
    row = lambda n: pl.BlockSpec((None, SAMPLE_ROWS, n), lambda b: (b, 0, 0))
    cspecs = [pl.BlockSpec((None,) + c.shape[1:], lambda b: (b, 0, 0)) for c in caches_t]
    for g, (w, dil) in enumerate(DIL_GROUPS):
        assert caches_t[g].shape == (B, 2 * DIL_GW, w) and w // dil == TILE_Q
    out_spec = row(3 * DIL_GW)
    o, lse, *new_caches = pl.pallas_call(
        functools.partial(_dil_sample_kernel, t_real=t_real),
        grid=(B,),
        in_specs=([row(3 * DIL_GW)] + [row(2 * DIL_GW)] * 3 + cspecs + [_resident(t.shape) for t in tabc]
                  + [_resident(tabn.shape)]),
        out_specs=[out_spec, out_spec] + cspecs,
        out_shape=([jax.ShapeDtypeStruct((B, SAMPLE_ROWS, 3 * DIL_GW), F32)] * 2
                   + [jax.ShapeDtypeStruct(c.shape, F32) for c in caches_t]),
        compiler_params=_cparams(("parallel",)),
        name="dil_sample",
    )(q, *kvn, *caches_t, *tabc, tabn)
    o = o.reshape(B * SAMPLE_ROWS, 3 * DIL_GW)
    lse = lse.reshape(B * SAMPLE_ROWS, 3 * DIL_GW)
    n = 3 * DIL_SLABS
    return ([o[:, i * LANES:(i + 1) * LANES] for i in range(n)],
            [lse[:, i * LANES:(i + 1) * LANES] for i in range(n)], new_caches)


GDN_HPS = 4
GDN_SW = GDN_HPS * GDN_HEAD_DIM
GDN_TPI = 4


def _gdn_kernel(q_ref, k_ref, v_ref, ba_ref, alog_ref, dtb_ref, z_ref, gn_ref, tri_ref, s0_ref, o_ref, s_ref, *, tpi):
    T = q_ref.shape[0]
    hg = pl.program_id(1)
    C = GDN_CHUNK
    D = GDN_HEAD_DIM
    ri = lax.broadcasted_iota(jnp.int32, (TILE_Q, TILE_Q), 0)
    ci = lax.broadcasted_iota(jnp.int32, (TILE_Q, TILE_Q), 1)
    same = (ri // C) == (ci // C)
    incl = same & (ri >= ci)
    strict = same & (ri > ci)
    eye = (ri == ci).astype(F32)
    lane, row = ci, ri
    gain = gn_ref[...]
    zpad = jnp.zeros((C, D), F32)
    HH = range(GDN_HPS)
    hsl = [slice(hh * D, (hh + 1) * D) for hh in HH]
    RW = tpi * TILE_Q
    UU = [(tt, hh) for tt in range(tpi) for hh in HH]
    UI = range(len(UU))

    def tile(i, S):
        r0 = pl.multiple_of(i * RW, RW)
        rt = [r0 + tt * TILE_Q for tt in range(tpi)]
        q = [q_ref[pl.ds(rt[tt], TILE_Q), hsl[hh]] for tt, hh in UU]
        k = [k_ref[pl.ds(rt[tt], TILE_Q), hsl[hh]] for tt, hh in UU]
        v = [v_ref[pl.ds(rt[tt], TILE_Q), hsl[hh]] for tt, hh in UU]
        tri = tri_ref[...]
        beta_all, G_all = [], []
        for tt in range(tpi):
            ba = ba_ref[pl.ds(rt[tt], TILE_Q), :]
            beta_all.append(_sigmoid(ba))
            gh, gm, gl = _split3(-jnp.exp(alog_ref[...]) * _softplus(ba + dtb_ref[...]))
            G_all.append(jnp.dot(tri, gh, preferred_element_type=F32) + jnp.dot(tri, gm, preferred_element_type=F32)
                         + jnp.dot(tri, gl, preferred_element_type=F32))
        head = [hg * GDN_HPS + hh for _, hh in UU]
        bc = [jnp.sum(jnp.where(lane == head[u], beta_all[UU[u][0]], 0.0), axis=-1, keepdims=True) for u in UI]
        Gc = [jnp.broadcast_to(jnp.sum(jnp.where(lane == head[u] + GDN_HEADS, G_all[UU[u][0]], 0.0), axis=-1,
                                       keepdims=True), (TILE_Q, TILE_Q)) for u in UI]
        gamma = [jnp.exp(jnp.where(incl, Gc[u] - Gc[u].T, NEG)) for u in UI]
        kk = [_dot_nt(k[u], k[u]) for u in UI]
        qk = [_dot_nt(q[u], k[u]) for u in UI]
        X = [jnp.where(strict, bc[u] * kk[u] * gamma[u], 0.0) for u in UI]
        P = [eye - X[u] for u in UI]
        for _ in range(int(math.log2(C)) - 1):
            X = [_dot(X[u], X[u]) for u in UI]
            P = [P[u] + _dot(P[u], X[u]) for u in UI]
        eG = [jnp.exp(Gc[u]) for u in UI]
        rhs = [jnp.concatenate([v[u] * bc[u], k[u] * (bc[u] * eG[u])], axis=-1) for u in UI]
        sol = [rhs[u] + _dot(P[u] - eye, rhs[u]) for u in UI]
        a_in = [qk[u] * gamma[u] for u in UI]
        q_dec = [q[u] * eG[u] for u in UI]
        kdT = [(k[u] * jnp.exp(jnp.where(row < C, Gc[u][C - 1:C, :], Gc[u][2 * C - 1:2 * C, :]) - Gc[u])).T
               for u in UI]
        S = list(S)
        for tt in range(tpi):
            us = [tt * GDN_HPS + hh for hh in HH]
            oq, vn = [[] for _ in HH], [[] for _ in HH]
            for c in range(TILE_Q // C):
                cs = slice(c * C, (c + 1) * C)
                r = [_dot(jnp.concatenate([sol[us[hh]][cs, D:], q_dec[us[hh]][cs]], axis=0), S[hh]) for hh in HH]
                for hh in HH:
                    oq[hh].append(r[hh][C:])
                    vn[hh].append(sol[us[hh]][cs, :D] - r[hh][:C])
                vpad = [jnp.concatenate([vn[hh][c], zpad] if c == 0 else [zpad, vn[hh][c]], axis=0) for hh in HH]
                S = [S[hh] * jnp.exp(Gc[us[hh]][(c + 1) * C - 1:(c + 1) * C, :]) + _dot(kdT[us[hh]], vpad[hh])
                     for hh in HH]
            o = [jnp.concatenate(oq[hh], axis=0) + _dot(a_in[us[hh]], jnp.concatenate(vn[hh], axis=0)) for hh in HH]
            outs = [_rms(o[hh], gain) * z_ref[pl.ds(rt[tt], TILE_Q), hsl[hh]] for hh in HH]
            o_ref[pl.ds(rt[tt], TILE_Q), :] = jnp.concatenate(outs, axis=-1).astype(o_ref.dtype)
        return tuple(S)

    S = lax.fori_loop(0, T // RW, tile, tuple(s0_ref[hh] for hh in HH))
    for hh in HH:
        s_ref[hh] = S[hh]


def _gdn(gq, ba, z, s0, a_log, dt_bias, norm_out):
    B, T, _ = gq.shape
    H = GDN_HEADS
    pad16 = lambda x: jnp.concatenate([jnp.zeros((H,), F32), x.astype(F32), jnp.zeros((LANES - 2 * H,), F32)])
    ng = H // GDN_HPS
    tpi = GDN_TPI if (T // TILE_Q) % GDN_TPI == 0 else 1
    col = lambda off: pl.BlockSpec((None, T, GDN_SW), lambda b, g: (b, 0, off * ng + g))
    sblk = pl.BlockSpec((None, GDN_HPS, GDN_HEAD_DIM, GDN_HEAD_DIM), lambda b, g: (b, g, 0, 0))
    r = np.arange(TILE_Q)
    tri = jnp.asarray((r[:, None] >= r[None, :]) & (r[:, None] // GDN_CHUNK == r[None, :] // GDN_CHUNK), BF16)
    o, s_new = pl.pallas_call(
        functools.partial(_gdn_kernel, tpi=tpi),
        grid=(B, ng),
        in_specs=[col(0), col(1), col(2),
                  pl.BlockSpec((None, T, LANES), lambda b, g: (b, 0, 0)),
                  _resident((1, LANES)), _resident((1, LANES)),
                  col(0), _resident((1, LANES)), _resident((TILE_Q, TILE_Q)), sblk],
        out_specs=[col(0), sblk],
        out_shape=[jax.ShapeDtypeStruct((B, T, GDN_W), BF16),
                   jax.ShapeDtypeStruct((B, H, GDN_HEAD_DIM, GDN_HEAD_DIM), F32)],
        compiler_params=_cparams(("parallel", "parallel")),
        name="gdn",
    )(gq, gq, gq, ba, pad16(a_log).reshape(1, LANES), pad16(dt_bias).reshape(1, LANES), z,
      norm_out.reshape(1, LANES), tri, s0)
    return o, s_new


def _gdn_sample_kernel(x_ref, b_ref, w_ref, ba_ref, alog_ref, dtb_ref, z_ref, gn_ref, s0_ref, o_ref, s_ref, xs,
                       *, t_real):
    R, D, H = SAMPLE_ROWS, GDN_HEAD_DIM, GDN_HEADS
    xs[CONV_PAD - 3:CONV_PAD, :] = b_ref[...]
    xs[CONV_PAD:CONV_PAD + R, :] = x_ref[...]
    y = xs[CONV_PAD - 3:CONV_PAD - 3 + R, :] * w_ref[0:1, :]
    for j in range(1, 4):
        y = y + xs[CONV_PAD - 3 + j:CONV_PAD - 3 + j + R, :] * w_ref[j:j + 1, :]
    y = _silu(y)
    ri = lax.broadcasted_iota(jnp.int32, (R, R), 0)
    ci = lax.broadcasted_iota(jnp.int32, (R, R), 1)
    incl, strict = ri >= ci, ri > ci
    eye = (ri == ci).astype(F32)
    tri = incl.astype(BF16)
    er = lax.broadcasted_iota(jnp.int32, (LANES, LANES), 0)
    ec = lax.broadcasted_iota(jnp.int32, (LANES, LANES), 1)
    eye_l = (er == ec).astype(BF16)
    live = lax.broadcasted_iota(jnp.int32, (R, LANES), 0) < t_real
    ba = ba_ref[...]
    beta_all = jnp.where(live, _sigmoid(ba), 0.0)
    g_all = jnp.where(live, -jnp.exp(alog_ref[...]) * _softplus(ba + dtb_ref[...]), 0.0)
    d32 = functools.partial(jnp.dot, preferred_element_type=F32)
    gh, gm, gl = _split3(g_all)
    G_all = d32(tri, gh) + d32(tri, gm) + d32(tri, gl)
    th, tm, tl = _split3(G_all)
    nt = lambda a, b: lax.dot_general(a, b, (((1,), (1,)), ((), ())), preferred_element_type=F32)
    G_allT = nt(eye_l, th) + nt(eye_l, tm) + nt(eye_l, tl)
    gain = gn_ref[...]
    HR = range(H)

    def l2n(x):
        return x * lax.rsqrt(jnp.sum(x * x, axis=-1, keepdims=True) + EPS)

    q = [l2n(y[:, h * D:(h + 1) * D]) * (D ** -0.5) for h in HR]
    k = [l2n(y[:, GDN_W + h * D:GDN_W + (h + 1) * D]) for h in HR]
    v = [y[:, 2 * GDN_W + h * D:2 * GDN_W + (h + 1) * D] for h in HR]
    bc = [beta_all[:, h:h + 1] for h in HR]
    Gc = [G_all[:, H + h:H + h + 1] for h in HR]
    gamma = [jnp.exp(jnp.where(incl, Gc[h] - G_allT[H + h:H + h + 1, :], NEG)) for h in HR]
    kk = [_dot_nt(k[h], k[h]) for h in HR]
    qk = [_dot_nt(q[h], k[h]) for h in HR]
    X = [jnp.where(strict, bc[h] * kk[h] * gamma[h], 0.0) for h in HR]
    P = [eye - X[h] for h in HR]
    for _ in range(int(math.log2(R)) - 1):
        X = [_dot(X[h], X[h]) for h in HR]
        P = [P[h] + _dot(P[h], X[h]) for h in HR]
    eG = [jnp.exp(Gc[h]) for h in HR]
    rhs = [jnp.concatenate([v[h] * bc[h], k[h] * (bc[h] * eG[h])], axis=-1) for h in HR]
    sol = [rhs[h] + _dot(P[h] - eye, rhs[h]) for h in HR]
    S = [s0_ref[h] for h in HR]
    r = [_dot(jnp.concatenate([sol[h][:, D:], q[h] * eG[h]], axis=0), S[h]) for h in HR]
    v_new = [sol[h][:, :D] - r[h][:R] for h in HR]
    o = [r[h][R:] + _dot(qk[h] * gamma[h], v_new[h]) for h in HR]
    kdT = [nt(eye_l, (k[h] * jnp.exp(Gc[h][R - 1:R, :] - Gc[h])).astype(BF16)) for h in HR]
    for h in HR:
        s_ref[h] = S[h] * jnp.exp(Gc[h][R - 1:R, :]) + _dot(kdT[h], v_new[h])
    o_ref[...] = jnp.concatenate([_rms(o[h], gain) * _silu(z_ref[:, h * D:(h + 1) * D]) for h in HR], axis=-1)


def _gdn_sample(gq, ba, z, conv_buf, s0, w_conv, a_log, dt_bias, norm_out, t_real):
    B, R, _ = gq.shape
    H = GDN_HEADS
    pad16 = lambda x: jnp.concatenate([jnp.zeros((H,), F32), x.astype(F32), jnp.zeros((LANES - 2 * H,), F32)])
    blk = lambda *s: pl.BlockSpec((None,) + s, lambda b: (b,) + (0,) * len(s))
    return pl.pallas_call(
        functools.partial(_gdn_sample_kernel, t_real=t_real),
        grid=(B,),
        in_specs=[blk(R, 3 * GDN_W), blk(3, 3 * GDN_W), _resident(w_conv.shape), blk(R, LANES),
                  _resident((1, LANES)), _resident((1, LANES)), blk(R, GDN_W), _resident((1, LANES)),
                  blk(H, GDN_HEAD_DIM, GDN_HEAD_DIM)],
        out_specs=[blk(R, GDN_W), blk(H, GDN_HEAD_DIM, GDN_HEAD_DIM)],
        out_shape=[jax.ShapeDtypeStruct((B, R, GDN_W), F32),
                   jax.ShapeDtypeStruct((B, H, GDN_HEAD_DIM, GDN_HEAD_DIM), F32)],
        scratch_shapes=[pltpu.VMEM((CONV_PAD + R, 3 * GDN_W), F32)],
        compiler_params=_cparams(("parallel",)),
        name="gdn_sample",
    )(gq, conv_buf, w_conv, ba, pad16(a_log).reshape(1, LANES), pad16(dt_bias).reshape(1, LANES), z,
      norm_out.reshape(1, LANES), s0)


def _mix_kernel(*refs):
    n_og = 3 * DIL_SLABS
    o_refs, l_refs = refs[:n_og], refs[n_og:2 * n_og]
    ob_ref, ga_ref, gb_ref, h_ref, wa_ref, wb_ref, wo_ref, gq_ref, wq_ref, h1_ref, qm_ref = refs[2 * n_og:]
    slabs = []
    for sl in range(DIL_SLABS):
        l0, l1, l2 = (l_refs[g * DIL_SLABS + sl][...] for g in range(3))
        o0, o1, o2 = (o_refs[g * DIL_SLABS + sl][...] for g in range(3))
        mx = jnp.maximum(jnp.maximum(l0, l1), l2)
        e0, e1, e2 = jnp.exp(l0 - mx), jnp.exp(l1 - mx), jnp.exp(l2 - mx)
        slabs.append((e0 * o0 + e1 * o1 + e2 * o2) / (e0 + e1 + e2))
    o_a = jnp.concatenate(slabs, axis=-1)
    a = _dot(o_a, wa_ref[...])
    b = jnp.dot(ob_ref[...], wb_ref[...], preferred_element_type=F32)
    merged = _sigmoid(ga_ref[...]) * a + _sigmoid(gb_ref[...]) * b
    h1 = h_ref[...] + _dot(merged, wo_ref[...])
    h1_ref[...] = h1
    qm_ref[...] = _dot(_rms(h1, gq_ref[...]), wq_ref[...]).astype(qm_ref.dtype)


def _mix(o_g, l_g, o_b, ga, gb, h, w_a, w_b, w_o, norm_mem_q, w_mem_q, tm):
    rows, d = h.shape
    rt = lambda n: pl.BlockSpec((tm, n), lambda i: (i, 0))
    assert len(o_g) == len(l_g) == 3 * DIL_SLABS
    return pl.pallas_call(
        _mix_kernel,
        grid=(rows // tm,),
        in_specs=[rt(LANES)] * (6 * DIL_SLABS) + [rt(GDN_W), rt(d), rt(d), rt(d),
                                     _resident(w_a.shape), _resident(w_b.shape), _resident(w_o.shape),
                                     _resident((1, d)), _resident(w_mem_q.shape)],
        out_specs=[rt(d), rt(w_mem_q.shape[1])],
        out_shape=[jax.ShapeDtypeStruct((rows, d), F32), jax.ShapeDtypeStruct((rows, w_mem_q.shape[1]), BF16)],
        compiler_params=_cparams(("parallel",)),
        name="mix",
    )(*o_g, *l_g, o_b, ga, gb, h, w_a, w_b, w_o, norm_mem_q.reshape(1, d), w_mem_q)


def _mem_kv_kernel(x_ref, g_ref, w_ref, k_ref, v_ref):
    u = _rms(x_ref[...], g_ref[...]).astype(BF16)
    tm = x_ref.shape[0]
    n = MEM_HEADS * MEM_HEAD_DIM
    for o_ref, off in ((k_ref, 0), (v_ref, n)):
        r = jnp.dot(u, w_ref[:, off:off + n], preferred_element_type=F32)
        for h in range(MEM_HEADS):
            o_ref[pl.ds(h, tm, stride=MEM_HEADS), :] = r[:, h * MEM_HEAD_DIM:(h + 1) * MEM_HEAD_DIM]


def _mem_kv(mem2d, gain, w, tm):
    rows, d = mem2d.shape
    return pl.pallas_call(
        _mem_kv_kernel,
        grid=(rows // tm,),
        in_specs=[pl.BlockSpec((tm, d), lambda i: (i, 0)), _resident((1, d)), _resident(w.shape)],
        out_specs=[pl.BlockSpec((tm * MEM_HEADS, MEM_HEAD_DIM), lambda i: (i, 0))] * 2,
        out_shape=[jax.ShapeDtypeStruct((rows * MEM_HEADS, MEM_HEAD_DIM), F32)] * 2,
        compiler_params=_cparams(("parallel",)),
        name="mem_kv",
    )(mem2d, gain.reshape(1, d), w)


def _mem_attn_kernel(q_ref, k_ref, v_ref, o_ref):
    M = k_ref.shape[1] // MEM_HEADS
    nb = q_ref.shape[0]
    units = [(b, h) for b in range(nb) for h in range(MEM_HEADS)]
    q = [q_ref[b] for b in range(nb)]
    s = [_dot_nt(q[b][:, h * MEM_HEAD_DIM:(h + 1) * MEM_HEAD_DIM], k_ref[b, pl.ds(h, M, stride=MEM_HEADS), :])
         * (MEM_HEAD_DIM ** -0.5) for b, h in units]
    p = [jnp.exp(x - jnp.max(x, axis=-1, keepdims=True)) for x in s]
    o = [_dot(p[u], v_ref[b, pl.ds(h, M, stride=MEM_HEADS), :]) / jnp.sum(p[u], axis=-1, keepdims=True)
         for u, (b, h) in enumerate(units)]
    for b in range(nb):
        o_ref[b] = jnp.concatenate(o[b * MEM_HEADS:(b + 1) * MEM_HEADS], axis=-1).astype(o_ref.dtype)


def _mem_attn(qm, mem_k, mem_v, tm, nb):
    B, T, w = qm.shape
    kv_spec = pl.BlockSpec((nb,) + mem_k.shape[1:], lambda b, j: (b, 0, 0))
    return pl.pallas_call(
        _mem_attn_kernel,
        grid=(B // nb, T // tm),
        in_specs=[pl.BlockSpec((nb, tm, w), lambda b, j: (b, j, 0)), kv_spec, kv_spec],
        out_specs=pl.BlockSpec((nb, tm, w), lambda b, j: (b, j, 0)),
        out_shape=jax.ShapeDtypeStruct((B, T, w), F32),
        compiler_params=_cparams(("parallel", "parallel")),
        name="mem_attn",
    )(qm, mem_k, mem_v)


def _ffn_kernel(*refs, inject, emit_gate):
    if inject:
        (h1_ref, om_ref, init_ref, fill_ref, wmo_ref, gf_ref, wup_ref, wc_ref, bc_ref, wd_ref, gfin_ref,
         y_ref, fc_ref, gs) = refs
    else:
        (h1_ref, om_ref, init_ref, wmo_ref, gf_ref, wup_ref, wc_ref, bc_ref, wd_ref, gfin_ref,
         y_ref, fc_ref, gs) = refs
    tm = h1_ref.shape[0]
    F = wd_ref.shape[0]
    PAD = SUBLANES

    @pl.when(pl.program_id(1) == 0)
    def _():
        gs[PAD - 2:PAD, :] = init_ref[...]

    h2 = h1_ref[...] + _dot(om_ref[...], wmo_ref[...])
    n = _rms(h2, gf_ref[...]).astype(BF16)
    gate = jnp.dot(n, wup_ref[:, :F], preferred_element_type=F32)
    if inject:
        r = lax.broadcasted_iota(jnp.int32, (tm, 1), 0)
        gate = jnp.where((r % SAMPLE_ROWS) >= SAMPLE_ROWS - 2, fill_ref[...], gate)
    gs[PAD:PAD + tm, :] = gate
    conv = (gs[PAD - 2:PAD - 2 + tm, :] * wc_ref[0:1, :] + gs[PAD - 1:PAD - 1 + tm, :] * wc_ref[1:2, :]
            + gate * wc_ref[2:3, :])
    last2 = gs[PAD + tm - 2:PAD + tm, :]
    gs[PAD - 2:PAD, :] = last2
    if emit_gate:
        fc_ref[...] = gate
    else:
        fc_ref[...] = last2
    up = jnp.dot(n, wup_ref[:, F:], preferred_element_type=F32)
    act = _silu(conv + bc_ref[...]) * up
    y = h2 + _dot(act, wd_ref[...])
    y_ref[...] = _rms(y, gfin_ref[...])


def _ffn(h1, om, init, fill, w_mo, norm_ffn, w_up, w_conv, b_conv, w_down, norm_final, tm, emit_gate):
    B, T, d = h1.shape
    F = w_down.shape[0]
    inject = fill is not None
    rt = lambda n: pl.BlockSpec((None, tm, n), lambda b, j: (b, j, 0))
    in_specs = [rt(d), rt(om.shape[-1]), pl.BlockSpec((None, 2, F), lambda b, j: (b, 0, 0))]
    args = [h1, om, init]
    if inject:
        in_specs.append(rt(F))
        args.append(fill)
    in_specs += [_resident(w_mo.shape), _resident((1, d)), _resident(w_up.shape), _resident(w_conv.shape),
                 _resident((1, F)), _resident(w_down.shape), _resident((1, d))]
    args += [w_mo, norm_ffn.reshape(1, d), w_up, w_conv, b_conv.reshape(1, F), w_down, norm_final.reshape(1, d)]
    if emit_gate:
        fc_spec, fc_shape = rt(F), jax.ShapeDtypeStruct((B, T, F), F32)
    else:
        fc_spec = pl.BlockSpec((None, 2, F), lambda b, j: (b, 0, 0))
        fc_shape = jax.ShapeDtypeStruct((B, 2, F), F32)
    return pl.pallas_call(
        functools.partial(_ffn_kernel, inject=inject, emit_gate=emit_gate),
        grid=(B, T // tm),
        in_specs=in_specs,
        out_specs=[rt(d), fc_spec],
        out_shape=[jax.ShapeDtypeStruct((B, T, d), F32), fc_shape],
        scratch_shapes=[pltpu.VMEM((tm + SUBLANES, F), F32)],
        compiler_params=_cparams(("parallel", "arbitrary")),
        name="ffn",
    )(*args)


def kernel(x_prompt, x_sample, cache_dil0_kv, cache_dil1_kv, cache_dil2_kv, state_delta, state_delta_conv, cache_mem_k, cache_mem_v, state_ffn_conv, mem_prompt, rel_bias, norm_mix, w_in, w_conv_delta, a_log, dt_bias, norm_delta_out, w_branch_a, w_branch_b, w_out, norm_mem_q, norm_mem_kv, w_mem_q, w_mem_kv, w_mem_o, norm_ffn, w_ffn_up, w_ffn_conv, b_ffn_conv, w_ffn_down, norm_final):
    B, S, D = x_prompt.shape
    Bs, Ts, _ = x_sample.shape
    depth = w_in.shape[0]
    assert depth == 1 and Ts <= SAMPLE_ROWS - 2 and Ts >= 3 and S % (16 * TILE_Q) == 0
    F = w_ffn_down.shape[1]
    M = mem_prompt.shape[1]
    l = 0
    w_arr = _arrange_w_in(w_in[l])
    w_a, w_b, w_o = (w.astype(BF16) for w in (w_branch_a[l], w_branch_b[l], w_out[l]))
    w_mq, w_mkv, w_mo = (w.astype(BF16) for w in (w_mem_q[l], w_mem_kv[l], w_mem_o[l]))
    w_up, w_dn = w_ffn_up[l].astype(BF16), w_ffn_down[l].astype(BF16)
    t_cat, t_cur = _prompt_bias_tables(rel_bias)

    xp = x_prompt.reshape(B * S, D)
    q, kv0, kv1, kv2, kt0, kt1, kt2, gq, gq_tail, z, ba, ga, gb = _in_proj(
        xp, norm_mix[l], w_arr, 256, seq=S, conv_buf=jnp.zeros((B, 3, 3 * GDN_W), F32), w_conv=w_conv_delta[l])
    kvs = [kv.reshape(B, S, 2 * DIL_GW) for kv in (kv0, kv1, kv2)]
    q3 = q.reshape(B, S, 3 * DIL_GW)
    o_g, l_g = [], []
    for g in range(3):
        o_sl, l_sl = _dil_prompt(q3, kvs[g], g, t_cat[g], t_cur[g])
        o_g += o_sl
        l_g += l_sl
    o_b, delta_p = _gdn(gq.reshape(B, S, -1), ba.reshape(B, S, LANES), z.reshape(B, S, GDN_W),
                        jnp.zeros((B, GDN_HEADS, GDN_HEAD_DIM, GDN_HEAD_DIM), F32), a_log[l], dt_bias[l],
                        norm_delta_out[l])
    h1, qm = _mix(o_g, l_g, o_b.reshape(B * S, GDN_W), ga, gb, xp, w_a, w_b, w_o, norm_mem_q[l], w_mq, 256)
    mk_p, mv_p = _mem_kv(mem_prompt.reshape(B * M, D), norm_mem_kv[l], w_mkv, 256)
    mk_p, mv_p = (x.reshape(B, M * MEM_HEADS, MEM_HEAD_DIM) for x in (mk_p, mv_p))
    om = _mem_attn(qm.reshape(B, S, -1), mk_p, mv_p, 512, 1)
    y_p, fconv_p = _ffn(h1.reshape(B, S, D), om, jnp.zeros((B, 2, F), F32), None, w_mo, norm_ffn[l], w_up,
                        w_ffn_conv[l], b_ffn_conv[l], w_dn, norm_final, 256, False)
    p_out = ([kt[:, :, S - min(w, S):].reshape(B, 2, DIL_HPG, DIL_HEAD_DIM, min(w, S)).transpose(0, 4, 1, 2, 3)[None]
              for kt, (w, _) in zip((kt0, kt1, kt2), DIL_GROUPS)]
             + [delta_p[None], gq_tail[:, SUBLANES - 3:][None], mk_p.reshape(1, B, M, MEM_HEADS, MEM_HEAD_DIM),
                mv_p.reshape(1, B, M, MEM_HEADS, MEM_HEAD_DIM), fconv_p[None]])

    R = SAMPLE_ROWS
    xs = jnp.pad(x_sample, ((0, 0), (0, R - Ts), (0, 0))).reshape(Bs * R, D)
    q, kv0, kv1, kv2, gq, z, ba, ga, gb = _in_proj(xs, norm_mix[l], w_arr, Bs * R)
    kvn = [kv.reshape(Bs, R, 2 * DIL_GW) for kv in (kv0, kv1, kv2)]
    caches_t = [jnp.transpose(c[l], (0, 2, 3, 4, 1)).reshape(Bs, 2 * DIL_GW, c.shape[2])
                for c in (cache_dil0_kv, cache_dil1_kv, cache_dil2_kv)]
    tabc, tabn = _sample_bias_tables(rel_bias, Ts)
    o_g, l_g, new_caches = _dil_sample(q.reshape(Bs, R, 3 * DIL_GW), kvn, caches_t, tabc, tabn, Ts)
    o_b, delta_s = _gdn_sample(gq.reshape(Bs, R, -1), ba.reshape(Bs, R, LANES), z.reshape(Bs, R, GDN_W),
                               state_delta_conv[l], state_delta[l], w_conv_delta[l], a_log[l], dt_bias[l],
                               norm_delta_out[l], Ts)
    o_b = o_b.reshape(Bs * R, GDN_W).astype(BF16)
    h1, qm = _mix(o_g, l_g, o_b, ga, gb, xs, w_a, w_b, w_o, norm_mem_q[l], w_mq, Bs * R)
    om = _mem_attn(qm.reshape(Bs, R, -1), cache_mem_k[l].reshape(Bs, M * MEM_HEADS, MEM_HEAD_DIM),
                   cache_mem_v[l].reshape(Bs, M * MEM_HEADS, MEM_HEAD_DIM), R, 4 if Bs % 4 == 0 else 1)
    fst = state_ffn_conv[l]
    fill = jnp.concatenate([jnp.zeros((Bs, R - 2, F), F32),
                            jnp.concatenate([fst[1:], jnp.zeros((1, 2, F), F32)], axis=0)], axis=1)
    y_s, gate_s = _ffn(h1.reshape(1, Bs * R, D), om.reshape(1, Bs * R, -1), fst[:1], fill.reshape(1, Bs * R, F),
                       w_mo, norm_ffn[l], w_up, w_ffn_conv[l], b_ffn_conv[l], w_dn, norm_final, Bs * R, True)
    y_s = y_s.reshape(Bs, R, D)[:, :Ts]
    gq3 = gq.reshape(Bs, R, -1)
    s_out = ([nc.reshape(Bs, 2, DIL_HPG, DIL_HEAD_DIM, nc.shape[2]).transpose(0, 4, 1, 2, 3)[None]
              for nc in new_caches]
             + [delta_s[None], gq3[:, Ts - 3:Ts][None], gate_s.reshape(Bs, R, F)[:, Ts - 2:Ts][None]])

    return (y_p.reshape(B, S, D), y_s, *p_out, *s_out)
```

```python
import functools
import math

import jax
import jax.numpy as jnp
import numpy as np
from jax import lax
from jax.experimental import pallas as pl
from jax.experimental.pallas import tpu as pltpu

F32 = jnp.float32
BF16 = jnp.bfloat16

PAST_LEN = 8192
DIL_GROUPS = ((128, 1), (512, 4), (2048, 16))
DIL_HPG = 4
DIL_HEAD_DIM = 64
DIL_GW = DIL_HPG * DIL_HEAD_DIM
DIL_NK = 129
REL_BUCKETS = 32
REL_MAX_DIST = 2048
GDN_HEADS = 8
GDN_HEAD_DIM = 128
GDN_W = GDN_HEADS * GDN_HEAD_DIM
GDN_CHUNK = 64
MEM_HEADS = 4
MEM_HEAD_DIM = 128
EPS = 1e-6
NEG = -1e30

LANES = 128
SUBLANES = 8
TILE_Q = 128
SAMPLE_ROWS = SUBLANES
CONV_PAD = SUBLANES
VMEM_LIMIT = 56 * 1024 * 1024


def _cparams(sem):
    return pltpu.CompilerParams(dimension_semantics=sem, vmem_limit_bytes=VMEM_LIMIT)


def _resident(shape):
    nd = len(shape)
    return pl.BlockSpec(shape, lambda *_: (0,) * nd, pipeline_mode=pl.Buffered(1))


def _rms(x, gain_row):
    return x * lax.rsqrt(jnp.mean(x * x, axis=-1, keepdims=True) + EPS) * gain_row


def _dot(a, b):
    return jnp.dot(a.astype(BF16), b.astype(BF16), preferred_element_type=F32)


def _dot_nt(a, b):
    return lax.dot_general(a.astype(BF16), b.astype(BF16), (((1,), (1,)), ((), ())), preferred_element_type=F32)


def _dot_tn(a, b):
    return lax.dot_general(a.astype(BF16), b.astype(BF16), (((0,), (0,)), ((), ())), preferred_element_type=F32)


def _split3(x):
    hi = x.astype(BF16)
    r1 = x - hi.astype(F32)
    mid = r1.astype(BF16)
    lo = (r1 - mid.astype(F32)).astype(BF16)
    return hi, mid, lo


def _dot_hp(a, b):
    ah, am, al = _split3(a)
    bh, bm, bl = _split3(b)
    d = functools.partial(jnp.dot, preferred_element_type=F32)
    return d(ah, bh) + (d(ah, bm) + d(am, bh)) + (d(am, bm) + d(ah, bl) + d(al, bh))


def _sigmoid(x):
    return 1.0 / (1.0 + jnp.exp(-x))


def _silu(x):
    return x * _sigmoid(x)


def _softplus(x):
    return jnp.maximum(x, 0.0) + jnp.log(1.0 + jnp.exp(-jnp.abs(x)))


IN_SEGS = (("q", 3 * DIL_GW), ("kv0", 2 * DIL_GW), ("kv1", 2 * DIL_GW), ("kv2", 2 * DIL_GW),
           ("gq", 3 * GDN_W), ("z", GDN_W), ("ba", LANES), ("ga", 1024), ("gb", 1024))


OFF_Q, OFF_K, OFF_V = 0, 3 * DIL_GW, 6 * DIL_GW
OFF_GQ = 9 * DIL_GW
OFF_Z = OFF_GQ + 3 * GDN_W
OFF_BA = OFF_Z + GDN_W
OFF_GATES = OFF_BA + 2 * GDN_HEADS


def _arrange_w_in(w_in):
    w = w_in.astype(BF16)
    return w, w[:, OFF_GATES:]


GQ_SLABS = 3 * GDN_W // LANES


def _l2n(x):
    return x * lax.rsqrt(jnp.sum(x * x, axis=-1, keepdims=True) + EPS)


def _in_proj_kernel(x_ref, g_ref, w_ref, wg_ref, *rest, seq_tiles):
    if seq_tiles:
        (cb_ref, wc_ref, q_ref, kv0_ref, kv1_ref, kv2_ref, kt0_ref, kt1_ref, kt2_ref, gq_ref, tail_ref, z_ref,
         ba_ref, ga_ref, gb_ref, cs) = rest
        kt_refs = (kt0_ref, kt1_ref, kt2_ref)
    else:
        q_ref, kv0_ref, kv1_ref, kv2_ref, gq_ref, z_ref, ba_ref, ga_ref, gb_ref = rest
    tm = x_ref.shape[0]
    u = _rms(x_ref[...], g_ref[...]).astype(BF16)

    def seg(ref, off, n):
        return jnp.dot(u, ref[:, off:off + n], preferred_element_type=F32)

    def q_seg():
        q_ref[...] = seg(w_ref, OFF_Q, 3 * DIL_GW) * (DIL_HEAD_DIM ** -0.5)

    def kv_seg(g, part):
        kv_ref = (kv0_ref, kv1_ref, kv2_ref)[g]
        r = seg(w_ref, (OFF_K, OFF_V)[part] + g * DIL_GW, DIL_GW)
        kv_ref[:, part * DIL_GW:(part + 1) * DIL_GW] = r
        if seq_tiles:
            kt_refs[g][part * DIL_GW:(part + 1) * DIL_GW, :] = r.T

    def ba_seg():
        ba_ref[...] = seg(w_ref, OFF_BA, LANES)

    def gate_seg(ref, off):
        ref[...] = seg(wg_ref, off, 1024)

    others = ([(q_seg, 3 * DIL_GW)] + [(functools.partial(kv_seg, g, p), DIL_GW) for g in range(3) for p in range(2)]
              + [(functools.partial(gate_seg, ga_ref, 0), 1024), (functools.partial(gate_seg, gb_ref, 1024), 1024),
                 (ba_seg, LANES)])
    if not seq_tiles:
        for f, _ in others:
            f()
        for c in range(3):
            gq_ref[:, c * GDN_W:(c + 1) * GDN_W] = seg(w_ref, OFF_GQ + c * GDN_W, GDN_W)
        z_ref[...] = seg(w_ref, OFF_Z, GDN_W)
        return
    i = pl.program_id(0)
    first = (i % seq_tiles) == 0

    @pl.when(first)
    def _():
        for s in range(GQ_SLABS):
            cs[s, CONV_PAD - 3:CONV_PAD, :] = cb_ref[:, s * LANES:(s + 1) * LANES]

    @pl.when(jnp.logical_not(first))
    def _():
        for s in range(GQ_SLABS):
            cs[s, CONV_PAD - 3:CONV_PAD, :] = cs[s, CONV_PAD + tm - 3:CONV_PAD + tm, :]

    for c in range(3):
        r = seg(w_ref, OFF_GQ + c * GDN_W, GDN_W)
        for hh in range(GDN_HEADS):
            cs[c * GDN_HEADS + hh, CONV_PAD:CONV_PAD + tm, :] = r[:, hh * LANES:(hh + 1) * LANES]
    base = CONV_PAD - 3 + jnp.minimum(i, 0)

    def conv_slab(s):
        ls = slice(s * LANES, (s + 1) * LANES)
        y = cs[s, pl.ds(base, tm), :] * wc_ref[0:1, ls]
        for j in range(1, 4):
            y = y + cs[s, pl.ds(base + j, tm), :] * wc_ref[j:j + 1, ls]
        y = _silu(y)
        if s < GDN_HEADS:
            y = _l2n(y) * (GDN_HEAD_DIM ** -0.5)
        elif s < 2 * GDN_HEADS:
            y = _l2n(y)
        gq_ref[:, ls] = y
        tail_ref[:, ls] = cs[s, tm:tm + SUBLANES, :]

    def z_seg():
        z_ref[...] = _silu(seg(w_ref, OFF_Z, GDN_W))

    others.insert(len(others) - 1, (z_seg, GDN_W))
    total = sum(n for _, n in others)
    done, cols = 0, 0
    for f, n in others:
        f()
        cols += n
        upto = min(GQ_SLABS, (cols * GQ_SLABS + total - 1) // total)
        for s in range(done, upto):
            conv_slab(s)
        done = upto
    assert done == GQ_SLABS


def _in_proj(x2d, gain, w_arr, tm, seq=None, conv_buf=None, w_conv=None):
    rows, d = x2d.shape
    fused = seq is not None
    nt = seq // tm if fused else None
    names = [n for n, _ in IN_SEGS]
    widths = dict(IN_SEGS)
    row_spec = lambda n: pl.BlockSpec((tm, n), lambda i: (i, 0))
    out_specs, out_shape = [], []
    for n in names:
        out_specs.append(row_spec(widths[n]))
        out_shape.append(jax.ShapeDtypeStruct((rows, widths[n]), F32))
        if n == "kv2" and fused:
            for _ in range(3):
                out_specs.append(pl.BlockSpec((None, 2 * DIL_GW, tm), lambda i: (i // nt, 0, i % nt)))
                out_shape.append(jax.ShapeDtypeStruct((rows // seq, 2 * DIL_GW, seq), F32))
        if n == "gq" and fused:
            out_specs.append(pl.BlockSpec((None, SUBLANES, 3 * GDN_W), lambda i: (i // nt, 0, 0)))
            out_shape.append(jax.ShapeDtypeStruct((rows // seq, SUBLANES, 3 * GDN_W), F32))
    in_specs = [pl.BlockSpec((tm, d), lambda i: (i, 0)), _resident((1, d)), _resident(w_arr[0].shape),
                _resident(w_arr[1].shape)]
    args = [x2d, gain.reshape(1, d), *w_arr]
    if fused:
        in_specs += [pl.BlockSpec((None, 3, 3 * GDN_W), lambda i: (i // nt, 0, 0)), _resident(w_conv.shape)]
        args += [conv_buf, w_conv]
    return pl.pallas_call(
        functools.partial(_in_proj_kernel, seq_tiles=nt),
        grid=(rows // tm,),
        in_specs=in_specs,
        out_specs=out_specs,
        out_shape=out_shape,
        scratch_shapes=[pltpu.VMEM((GQ_SLABS, CONV_PAD + tm, LANES), F32)] if fused else [],
        compiler_params=_cparams(("arbitrary",)),
        name="in_proj",
    )(*args)


def _rel_bucket(dist):
    exact = REL_BUCKETS // 2
    d = jnp.maximum(dist, 1).astype(F32)
    large = exact + (jnp.log(d / exact) / math.log(REL_MAX_DIST / exact) * (REL_BUCKETS - exact)).astype(jnp.int32)
    return jnp.where(dist < exact, dist, jnp.minimum(large, REL_BUCKETS - 1))


def _group_bias(rel_bias, g):
    dil = DIL_GROUPS[g][1]
    dist = dil * jnp.arange(DIL_NK, dtype=jnp.int32)
    tab = rel_bias[_rel_bucket(dist)]
    return tab[:, g * DIL_HPG:(g + 1) * DIL_HPG].T.astype(F32)


def _toeplitz(v, n, width):
    h, L = v.shape
    return jnp.tile(v, (1, n))[:, :n * (L - 1)].reshape(h, n, L - 1)[:, :, :width]


def _prompt_bias_tables(rel_bias):
    cat, cur = [], []
    for g in range(3):
        bg = _group_bias(rel_bias, g)
        v = jnp.concatenate([bg[:, ::-1], jnp.full((DIL_HPG, 3 * TILE_Q - DIL_NK), NEG, F32)], axis=1)
        t = _toeplitz(v, TILE_Q, 2 * TILE_Q)
        cat.append(t)
        cur.append(t[:, :, TILE_Q:])
    return cat, cur


DIL_TIF = 2
DIL_SLABS = DIL_GW // LANES


def _dil_prompt_kernel(q0_ref, q1_ref, k0_ref, k1_ref, v0_ref, v1_ref, tcat_ref, tcur_ref,
                       o0_ref, o1_ref, l0_ref, l1_ref, *, dil):
    S = q0_ref.shape[0]
    nb = S // dil // TILE_Q
    q_refs, k_refs, v_refs = (q0_ref, q1_ref), (k0_ref, k1_ref), (v0_ref, v1_ref)
    o_refs, l_refs = (o0_ref, o1_ref), (l0_ref, l1_ref)
    even = lax.broadcasted_iota(jnp.int32, (TILE_Q, LANES), 1) < DIL_HEAD_DIM

    def rows(r, t):
        start = r + dil * TILE_Q * t
        return pl.ds(start, TILE_Q, stride=dil) if dil > 1 else pl.ds(start, TILE_Q)

    tiles = [(r, t) for r in range(dil) for t in range(nb)]
    for i0 in range(0, len(tiles), DIL_TIF):
        grp = tiles[i0:i0 + DIL_TIF]
        qm, kc, vc = {}, {}, {}
        for ti, (r, t) in enumerate(grp):
            for sl in range(DIL_SLABS):
                qf = q_refs[sl][rows(r, t), :]
                qm[ti, 2 * sl] = jnp.where(even, qf, 0.0).astype(BF16)
                qm[ti, 2 * sl + 1] = jnp.where(even, 0.0, qf).astype(BF16)
                kc[ti, sl] = k_refs[sl][rows(r, t), :].astype(BF16)
                vc[ti, sl] = v_refs[sl][rows(r, t), :].astype(BF16)
                if t > 0:
                    kc[ti, sl] = jnp.concatenate([k_refs[sl][rows(r, t - 1), :].astype(BF16), kc[ti, sl]], axis=0)
                    vc[ti, sl] = jnp.concatenate([v_refs[sl][rows(r, t - 1), :].astype(BF16), vc[ti, sl]], axis=0)
        units = [(ti, h) for ti in range(len(grp)) for h in range(DIL_HPG)]
        s = [_dot_nt(qm[ti, h], kc[ti, h // 2]) + (tcat_ref[h] if grp[ti][1] > 0 else tcur_ref[h]) for ti, h in units]
        m = [jnp.max(x, axis=-1, keepdims=True) for x in s]
        p = [jnp.exp(x - mx) for x, mx in zip(s, m)]
        l = [jnp.sum(x, axis=-1, keepdims=True) for x in p]
        pv = [jnp.dot(p[u].astype(BF16), vc[ti, h // 2], preferred_element_type=F32) for u, (ti, h) in enumerate(units)]
        o = [pv[u] / l[u] for u in range(len(units))]
        lse = [m[u] + jnp.log(l[u]) for u in range(len(units))]
        for ti, (r, t) in enumerate(grp):
            for sl in range(DIL_SLABS):
                ue, uo = ti * DIL_HPG + 2 * sl, ti * DIL_HPG + 2 * sl + 1
                o_refs[sl][rows(r, t), :] = jnp.where(even, o[ue], o[uo])
                l_refs[sl][rows(r, t), :] = jnp.where(even, lse[ue], lse[uo])


def _dil_prompt(q, kv, g, t_cat, t_cur):
    B, S, _ = q.shape
    dil = DIL_GROUPS[g][1]
    slab = lambda c: pl.BlockSpec((None, S, LANES), lambda b: (b, 0, c))
    nq, nk = g * DIL_SLABS, 0
    outs = pl.pallas_call(
        functools.partial(_dil_prompt_kernel, dil=dil),
        grid=(B,),
        in_specs=[slab(nq), slab(nq + 1), slab(nk), slab(nk + 1), slab(nk + 2), slab(nk + 3),
                  _resident(t_cat.shape), _resident(t_cur.shape)],
        out_specs=[slab(0)] * 4,
        out_shape=[jax.ShapeDtypeStruct((B, S, LANES), F32)] * 4,
        compiler_params=_cparams(("parallel",)),
        name=f"dil_prompt_g{g}",
    )(q, q, kv, kv, kv, kv, t_cat, t_cur)
    o0, o1, l0, l1 = (x.reshape(B * S, LANES) for x in outs)
    return [o0, o1], [l0, l1]


def _sample_bias_tables(rel_bias, t_real):
    R = SAMPLE_ROWS
    tabc, tabn = [], []
    t_i = np.arange(R)[:, None]
    u_i = np.arange(R)[None, :]
    for g, (w, dil) in enumerate(DIL_GROUPS):
        bg = _group_bias(rel_bias, g)
        base = bg[:, ::-1][:, :TILE_Q]
        t0 = jnp.concatenate([base[:, :, None], jnp.full((DIL_HPG, TILE_Q, dil - 1), NEG, F32)], axis=2)
        t0 = t0.reshape(DIL_HPG, w)
        rows = [jnp.concatenate([jnp.full((DIL_HPG, t), NEG, F32), t0[:, :w - t]], axis=1) for t in range(t_real)]
        rows += [jnp.zeros((DIL_HPG, w), F32)] * (R - t_real)
        tabc.append(jnp.stack(rows, axis=1).reshape(DIL_HPG * R, w))
        tn = jnp.full((DIL_HPG, R, R), NEG, F32)
        for j in range(-(-t_real // dil)):
            hit = (t_i - u_i == j * dil) & (t_i < t_real)
            tn = jnp.where(hit[None], bg[:, j][:, None, None], tn)
        tn = jnp.where((t_i >= t_real)[None], 0.0, tn)
        tabn.append(tn.reshape(DIL_HPG * R, R))
    return tabc, jnp.stack(tabn)


def _dil_sample_kernel(q_ref, n0_ref, n1_ref, n2_ref, c0_ref, c1_ref, c2_ref, tc0_ref, tc1_ref, tc2_ref, tn_ref,
                       o_ref, l_ref, oc0_ref, oc1_ref, oc2_ref, *, t_real):
    R = SAMPLE_ROWS
    rows = lax.broadcasted_iota(jnp.int32, (DIL_HPG * R, DIL_GW), 0)
    lanes = lax.broadcasted_iota(jnp.int32, (DIL_HPG * R, DIL_GW), 1)
    head_mask = (lanes // DIL_HEAD_DIM) == (rows // R)
    lane_f = lax.broadcasted_iota(jnp.int32, (2 * DIL_GW, LANES), 1)
    keep = lane_f < LANES - t_real
    sel_l = lax.broadcasted_iota(jnp.int32, (LANES, R), 0)
    sel_u = lax.broadcasted_iota(jnp.int32, (LANES, R), 1)
    selT = ((sel_l == sel_u + LANES - t_real) & (sel_u < t_real)).astype(BF16)
    groups = ((n0_ref, c0_ref, tc0_ref, oc0_ref), (n1_ref, c1_ref, tc1_ref, oc1_ref), (n2_ref, c2_ref, tc2_ref, oc2_ref))

    def fold_heads(x):
        x = jnp.where(head_mask, x, 0.0)
        return x[0:R] + x[R:2 * R] + x[2 * R:3 * R] + x[3 * R:4 * R]

    for g, (n_ref, c_ref, tc_ref, oc_ref) in enumerate(groups):
        W = c_ref.shape[1]
        kvn = n_ref[...]
        q_g = q_ref[:, g * DIL_GW:(g + 1) * DIL_GW]
        q_bd = jnp.where(head_mask, jnp.concatenate([q_g] * DIL_HPG, axis=0), 0.0)
        s_c = _dot(q_bd, c_ref[:DIL_GW, :]) + tc_ref[...]
        s_n = _dot_nt(q_bd, kvn[:, :DIL_GW]) + tn_ref[g]
        m = jnp.maximum(jnp.max(s_c, axis=-1, keepdims=True), jnp.max(s_n, axis=-1, keepdims=True))
        p_c = jnp.exp(s_c - m)
        p_n = jnp.exp(s_n - m)
        l = jnp.sum(p_c, axis=-1, keepdims=True) + jnp.sum(p_n, axis=-1, keepdims=True)
        acc = (_dot_nt(p_c, c_ref[DIL_GW:, :]) + _dot(p_n, kvn[:, DIL_GW:])) / l
        o_ref[:, g * DIL_GW:(g + 1) * DIL_GW] = fold_heads(acc)
        l_ref[:, g * DIL_GW:(g + 1) * DIL_GW] = fold_heads(jnp.broadcast_to(m + jnp.log(l), acc.shape))
        hi, mid, lo = _split3(kvn)
        tail = (jnp.dot(selT, hi, preferred_element_type=F32) + jnp.dot(selT, mid, preferred_element_type=F32)
                + jnp.dot(selT, lo, preferred_element_type=F32)).T
        nxt = pltpu.roll(c_ref[:, 0:LANES], LANES - t_real, axis=1)
        for c in range(W // LANES):
            cur = nxt
            nxt = (pltpu.roll(c_ref[:, (c + 1) * LANES:(c + 2) * LANES], LANES - t_real, axis=1)
                   if (c + 1) * LANES < W else tail)
            oc_ref[:, c * LANES:(c + 1) * LANES] = jnp.where(keep, cur, nxt)


def _dil_sample(q, kvn, caches_t, tabc, tabn, t_real):
    B = q.shape[0]
    row = lambda n: pl.BlockSpec((None, SAMPLE_ROWS, n), lambda b: (b, 0, 0))
    cspecs = [pl.BlockSpec((None,) + c.shape[1:], lambda b: (b, 0, 0)) for c in caches_t]
    for g, (w, dil) in enumerate(DIL_GROUPS):
        assert caches_t[g].shape == (B, 2 * DIL_GW, w) and w // dil == TILE_Q
    out_spec = row(3 * DIL_GW)
    o, lse, *new_caches = pl.pallas_call(
        functools.partial(_dil_sample_kernel, t_real=t_real),
        grid=(B,),
        in_specs=([row(3 * DIL_GW)] + [row(2 * DIL_GW)] * 3 + cspecs + [_resident(t.shape) for t in tabc]
                  + [_resident(tabn.shape)]),
        out_specs=[out_spec, out_spec] + cspecs,
        out_shape=([jax.ShapeDtypeStruct((B, SAMPLE_ROWS, 3 * DIL_GW), F32)] * 2
                   + [jax.ShapeDtypeStruct(c.shape, F32) for c in caches_t]),
        compiler_params=_cparams(("parallel",)),
        name="dil_sample",
    )(q, *kvn, *caches_t, *tabc, tabn)
    o = o.reshape(B * SAMPLE_ROWS, 3 * DIL_GW)
    lse = lse.reshape(B * SAMPLE_ROWS, 3 * DIL_GW)
    n = 3 * DIL_SLABS
    return ([o[:, i * LANES:(i + 1) * LANES] for i in range(n)],
            [lse[:, i * LANES:(i + 1) * LANES] for i in range(n)], new_caches)


GDN_HPS = 4
GDN_SW = GDN_HPS * GDN_HEAD_DIM
GDN_TPI = 4


def _gdn_kernel(q_ref, k_ref, v_ref, ba_ref, alog_ref, dtb_ref, z_ref, gn_ref, tri_ref, s0_ref, o_ref, s_ref, *, tpi):
    T = q_ref.shape[0]
    hg = pl.program_id(1)
    C = GDN_CHUNK
    D = GDN_HEAD_DIM
    ri = lax.broadcasted_iota(jnp.int32, (TILE_Q, TILE_Q), 0)
    ci = lax.broadcasted_iota(jnp.int32, (TILE_Q, TILE_Q), 1)
    same = (ri // C) == (ci // C)
    incl = same & (ri >= ci)
    strict = same & (ri > ci)
    eye = (ri == ci).astype(F32)
    lane, row = ci, ri
    gain = gn_ref[...]
    zpad = jnp.zeros((C, D), F32)
    HH = range(GDN_HPS)
    hsl = [slice(hh * D, (hh + 1) * D) for hh in HH]
    RW = tpi * TILE_Q
    UU = [(tt, hh) for tt in range(tpi) for hh in HH]
    UI = range(len(UU))

    def tile(i, S):
        r0 = pl.multiple_of(i * RW, RW)
        rt = [r0 + tt * TILE_Q for tt in range(tpi)]
        q = [q_ref[pl.ds(rt[tt], TILE_Q), hsl[hh]] for tt, hh in UU]
        k = [k_ref[pl.ds(rt[tt], TILE_Q), hsl[hh]] for tt, hh in UU]
        v = [v_ref[pl.ds(rt[tt], TILE_Q), hsl[hh]] for tt, hh in UU]
        tri = tri_ref[...]
        beta_all, G_all = [], []
        for tt in range(tpi):
            ba = ba_ref[pl.ds(rt[tt], TILE_Q), :]
            beta_all.append(_sigmoid(ba))
            gh, gm, gl = _split3(-jnp.exp(alog_ref[...]) * _softplus(ba + dtb_ref[...]))
            G_all.append(jnp.dot(tri, gh, preferred_element_type=F32) + jnp.dot(tri, gm, preferred_element_type=F32)
                         + jnp.dot(tri, gl, preferred_element_type=F32))
        head = [hg * GDN_HPS + hh for _, hh in UU]
        bc = [jnp.sum(jnp.where(lane == head[u], beta_all[UU[u][0]], 0.0), axis=-1, keepdims=True) for u in UI]
        Gc = [jnp.broadcast_to(jnp.sum(jnp.where(lane == head[u] + GDN_HEADS, G_all[UU[u][0]], 0.0), axis=-1,
                                       keepdims=True), (TILE_Q, TILE_Q)) for u in UI]
        gamma = [jnp.exp(jnp.where(incl, Gc[u] - Gc[u].T, NEG)) for u in UI]
        kk = [_dot_nt(k[u], k[u]) for u in UI]
        qk = [_dot_nt(q[u], k[u]) for u in UI]
        X = [jnp.where(strict, bc[u] * kk[u] * gamma[u], 0.0) for u in UI]
        P = [eye - X[u] for u in UI]
        for _ in range(int(math.log2(C)) - 1):
            X = [_dot(X[u], X[u]) for u in UI]
            P = [P[u] + _dot(P[u], X[u]) for u in UI]
        eG = [jnp.exp(Gc[u]) for u in UI]
        rhs = [jnp.concatenate([v[u] * bc[u], k[u] * (bc[u] * eG[u])], axis=-1) for u in UI]
        sol = [rhs[u] + _dot(P[u] - eye, rhs[u]) for u in UI]
        a_in = [qk[u] * gamma[u] for u in UI]
        q_dec = [q[u] * eG[u] for u in UI]
        kdT = [(k[u] * jnp.exp(jnp.where(row < C, Gc[u][C - 1:C, :], Gc[u][2 * C - 1:2 * C, :]) - Gc[u])).T
               for u in UI]
        S = list(S)
        for tt in range(tpi):
            us = [tt * GDN_HPS + hh for hh in HH]
            oq, vn = [[] for _ in HH], [[] for _ in HH]
            for c in range(TILE_Q // C):
                cs = slice(c * C, (c + 1) * C)
                r = [_dot(jnp.concatenate([sol[us[hh]][cs, D:], q_dec[us[hh]][cs]], axis=0), S[hh]) for hh in HH]
                for hh in HH:
                    oq[hh].append(r[hh][C:])
                    vn[hh].append(sol[us[hh]][cs, :D] - r[hh][:C])
                vpad = [jnp.concatenate([vn[hh][c], zpad] if c == 0 else [zpad, vn[hh][c]], axis=0) for hh in HH]
                S = [S[hh] * jnp.exp(Gc[us[hh]][(c + 1) * C - 1:(c + 1) * C, :]) + _dot(kdT[us[hh]], vpad[hh])
                     for hh in HH]
            o = [jnp.concatenate(oq[hh], axis=0) + _dot(a_in[us[hh]], jnp.concatenate(vn[hh], axis=0)) for hh in HH]
            outs = [_rms(o[hh], gain) * z_ref[pl.ds(rt[tt], TILE_Q), hsl[hh]] for hh in HH]
            o_ref[pl.ds(rt[tt], TILE_Q), :] = jnp.concatenate(outs, axis=-1).astype(o_ref.dtype)
        return tuple(S)

    S = lax.fori_loop(0, T // RW, tile, tuple(s0_ref[hh] for hh in HH))
    for hh in HH:
        s_ref[hh] = S[hh]


def _gdn(gq, ba, z, s0, a_log, dt_bias, norm_out):
    B, T, _ = gq.shape
    H = GDN_HEADS
    pad16 = lambda x: jnp.concatenate([jnp.zeros((H,), F32), x.astype(F32), jnp.zeros((LANES - 2 * H,), F32)])
    ng = H // GDN_HPS
    tpi = GDN_TPI if (T // TILE_Q) % GDN_TPI == 0 else 1
    col = lambda off: pl.BlockSpec((None, T, GDN_SW), lambda b, g: (b, 0, off * ng + g))
    sblk = pl.BlockSpec((None, GDN_HPS, GDN_HEAD_DIM, GDN_HEAD_DIM), lambda b, g: (b, g, 0, 0))
    r = np.arange(TILE_Q)
    tri = jnp.asarray((r[:, None] >= r[None, :]) & (r[:, None] // GDN_CHUNK == r[None, :] // GDN_CHUNK), BF16)
    o, s_new = pl.pallas_call(
        functools.partial(_gdn_kernel, tpi=tpi),
        grid=(B, ng),
        in_specs=[col(0), col(1), col(2),
                  pl.BlockSpec((None, T, LANES), lambda b, g: (b, 0, 0)),
                  _resident((1, LANES)), _resident((1, LANES)),
                  col(0), _resident((1, LANES)), _resident((TILE_Q, TILE_Q)), sblk],
        out_specs=[col(0), sblk],
        out_shape=[jax.ShapeDtypeStruct((B, T, GDN_W), BF16),
                   jax.ShapeDtypeStruct((B, H, GDN_HEAD_DIM, GDN_HEAD_DIM), F32)],
        compiler_params=_cparams(("parallel", "parallel")),
        name="gdn",
    )(gq, gq, gq, ba, pad16(a_log).reshape(1, LANES), pad16(dt_bias).reshape(1, LANES), z,
      norm_out.reshape(1, LANES), tri, s0)
    return o, s_new


def _gdn_sample_kernel(x_ref, b_ref, w_ref, ba_ref, alog_ref, dtb_ref, z_ref, gn_ref, s0_ref, o_ref, s_ref, xs,
                       *, t_real):
    R, D, H = SAMPLE_ROWS, GDN_HEAD_DIM, GDN_HEADS
    xs[CONV_PAD - 3:CONV_PAD, :] = b_ref[...]
    xs[CONV_PAD:CONV_PAD + R, :] = x_ref[...]
    y = xs[CONV_PAD - 3:CONV_PAD - 3 + R, :] * w_ref[0:1, :]
    for j in range(1, 4):
        y = y + xs[CONV_PAD - 3 + j:CONV_PAD - 3 + j + R, :] * w_ref[j:j + 1, :]
    y = _silu(y)
    ri = lax.broadcasted_iota(jnp.int32, (R, R), 0)
    ci = lax.broadcasted_iota(jnp.int32, (R, R), 1)
    incl, strict = ri >= ci, ri > ci
    eye = (ri == ci).astype(F32)
    tri = incl.astype(BF16)
    er = lax.broadcasted_iota(jnp.int32, (LANES, LANES), 0)
    ec = lax.broadcasted_iota(jnp.int32, (LANES, LANES), 1)
    eye_l = (er == ec).astype(BF16)
    live = lax.broadcasted_iota(jnp.int32, (R, LANES), 0) < t_real
    ba = ba_ref[...]
    beta_all = jnp.where(live, _sigmoid(ba), 0.0)
    g_all = jnp.where(live, -jnp.exp(alog_ref[...]) * _softplus(ba + dtb_ref[...]), 0.0)
    d32 = functools.partial(jnp.dot, preferred_element_type=F32)
    gh, gm, gl = _split3(g_all)
    G_all = d32(tri, gh) + d32(tri, gm) + d32(tri, gl)
    th, tm, tl = _split3(G_all)
    nt = lambda a, b: lax.dot_general(a, b, (((1,), (1,)), ((), ())), preferred_element_type=F32)
    G_allT = nt(eye_l, th) + nt(eye_l, tm) + nt(eye_l, tl)
    gain = gn_ref[...]
    HR = range(H)

    def l2n(x):
        return x * lax.rsqrt(jnp.sum(x * x, axis=-1, keepdims=True) + EPS)

    q = [l2n(y[:, h * D:(h + 1) * D]) * (D ** -0.5) for h in HR]
    k = [l2n(y[:, GDN_W + h * D:GDN_W + (h + 1) * D]) for h in HR]
    v = [y[:, 2 * GDN_W + h * D:2 * GDN_W + (h + 1) * D] for h in HR]
    bc = [beta_all[:, h:h + 1] for h in HR]
    Gc = [G_all[:, H + h:H + h + 1] for h in HR]
    gamma = [jnp.exp(jnp.where(incl, Gc[h] - G_allT[H + h:H + h + 1, :], NEG)) for h in HR]
    kk = [_dot_nt(k[h], k[h]) for h in HR]
    qk = [_dot_nt(q[h], k[h]) for h in HR]
    X = [jnp.where(strict, bc[h] * kk[h] * gamma[h], 0.0) for h in HR]
    P = [eye - X[h] for h in HR]
    for _ in range(int(math.log2(R)) - 1):
        X = [_dot(X[h], X[h]) for h in HR]
        P = [P[h] + _dot(P[h], X[h]) for h in HR]
    eG = [jnp.exp(Gc[h]) for h in HR]
    rhs = [jnp.concatenate([v[h] * bc[h], k[h] * (bc[h] * eG[h])], axis=-1) for h in HR]
    sol = [rhs[h] + _dot(P[h] - eye, rhs[h]) for h in HR]
    S = [s0_ref[h] for h in HR]
    r = [_dot(jnp.concatenate([sol[h][:, D:], q[h] * eG[h]], axis=0), S[h]) for h in HR]
    v_new = [sol[h][:, :D] - r[h][:R] for h in HR]
    o = [r[h][R:] + _dot(qk[h] * gamma[h], v_new[h]) for h in HR]
    kdT = [nt(eye_l, (k[h] * jnp.exp(Gc[h][R - 1:R, :] - Gc[h])).astype(BF16)) for h in HR]
    for h in HR:
        s_ref[h] = S[h] * jnp.exp(Gc[h][R - 1:R, :]) + _dot(kdT[h], v_new[h])
    o_ref[...] = jnp.concatenate([_rms(o[h], gain) * _silu(z_ref[:, h * D:(h + 1) * D]) for h in HR], axis=-1)


def _gdn_sample(gq, ba, z, conv_buf, s0, w_conv, a_log, dt_bias, norm_out, t_real):
    B, R, _ = gq.shape
    H = GDN_HEADS
    pad16 = lambda x: jnp.concatenate([jnp.zeros((H,), F32), x.astype(F32), jnp.zeros((LANES - 2 * H,), F32)])
    blk = lambda *s: pl.BlockSpec((None,) + s, lambda b: (b,) + (0,) * len(s))
    return pl.pallas_call(
        functools.partial(_gdn_sample_kernel, t_real=t_real),
        grid=(B,),
        in_specs=[blk(R, 3 * GDN_W), blk(3, 3 * GDN_W), _resident(w_conv.shape), blk(R, LANES),
                  _resident((1, LANES)), _resident((1, LANES)), blk(R, GDN_W), _resident((1, LANES)),
                  blk(H, GDN_HEAD_DIM, GDN_HEAD_DIM)],
        out_specs=[blk(R, GDN_W), blk(H, GDN_HEAD_DIM, GDN_HEAD_DIM)],
        out_shape=[jax.ShapeDtypeStruct((B, R, GDN_W), F32),
                   jax.ShapeDtypeStruct((B, H, GDN_HEAD_DIM, GDN_HEAD_DIM), F32)],
        scratch_shapes=[pltpu.VMEM((CONV_PAD + R, 3 * GDN_W), F32)],
        compiler_params=_cparams(("parallel",)),
        name="gdn_sample",
    )(gq, conv_buf, w_conv, ba, pad16(a_log).reshape(1, LANES), pad16(dt_bias).reshape(1, LANES), z,
      norm_out.reshape(1, LANES), s0)


N_OG = 3 * DIL_SLABS


def _mix_kernel(*refs):
    h1, qm = _mix_body(*refs[:2 * N_OG + 9])
    h1_ref, qm_ref = refs[2 * N_OG + 9:]
    h1_ref[...] = h1
    qm_ref[...] = qm


def _mix_body(*refs):
    o_refs, l_refs = refs[:N_OG], refs[N_OG:2 * N_OG]
    ob_ref, ga_ref, gb_ref, h_ref, wa_ref, wb_ref, wo_ref, gq_ref, wq_ref = refs[2 * N_OG:]
    slabs = []
    for sl in range(DIL_SLABS):
        l0, l1, l2 = (l_refs[g * DIL_SLABS + sl][...] for g in range(3))
        o0, o1, o2 = (o_refs[g * DIL_SLABS + sl][...] for g in range(3))
        mx = jnp.maximum(jnp.maximum(l0, l1), l2)
        e0, e1, e2 = jnp.exp(l0 - mx), jnp.exp(l1 - mx), jnp.exp(l2 - mx)
        slabs.append((e0 * o0 + e1 * o1 + e2 * o2) / (e0 + e1 + e2))
    o_a = jnp.concatenate(slabs, axis=-1)
    a = _dot(o_a, wa_ref[...])
    b = jnp.dot(ob_ref[...], wb_ref[...], preferred_element_type=F32)
    merged = _sigmoid(ga_ref[...]) * a + _sigmoid(gb_ref[...]) * b
    h1 = h_ref[...] + _dot(merged, wo_ref[...])
    return h1, _dot(_rms(h1, gq_ref[...]), wq_ref[...]).astype(BF16)


def _mix(o_g, l_g, o_b, ga, gb, h, w_a, w_b, w_o, norm_mem_q, w_mem_q, tm):
    rows, d = h.shape
    rt = lambda n: pl.BlockSpec((tm, n), lambda i: (i, 0))
    assert len(o_g) == len(l_g) == 3 * DIL_SLABS
    return pl.pallas_call(
        _mix_kernel,
        grid=(rows // tm,),
        in_specs=[rt(LANES)] * (6 * DIL_SLABS) + [rt(GDN_W), rt(d), rt(d), rt(d),
                                     _resident(w_a.shape), _resident(w_b.shape), _resident(w_o.shape),
                                     _resident((1, d)), _resident(w_mem_q.shape)],
        out_specs=[rt(d), rt(w_mem_q.shape[1])],
        out_shape=[jax.ShapeDtypeStruct((rows, d), F32), jax.ShapeDtypeStruct((rows, w_mem_q.shape[1]), BF16)],
        compiler_params=_cparams(("parallel",)),
        name="mix",
    )(*o_g, *l_g, o_b, ga, gb, h, w_a, w_b, w_o, norm_mem_q.reshape(1, d), w_mem_q)


def _mem_kv_kernel(x_ref, g_ref, w_ref, k_ref, v_ref):
    u = _rms(x_ref[...], g_ref[...]).astype(BF16)
    tm = x_ref.shape[0]
    n = MEM_HEADS * MEM_HEAD_DIM
    for o_ref, off in ((k_ref, 0), (v_ref, n)):
        r = jnp.dot(u, w_ref[:, off:off + n], preferred_element_type=F32)
        for h in range(MEM_HEADS):
            o_ref[pl.ds(h, tm, stride=MEM_HEADS), :] = r[:, h * MEM_HEAD_DIM:(h + 1) * MEM_HEAD_DIM]


def _mem_kv(mem2d, gain, w, tm):
    rows, d = mem2d.shape
    return pl.pallas_call(
        _mem_kv_kernel,
        grid=(rows // tm,),
        in_specs=[pl.BlockSpec((tm, d), lambda i: (i, 0)), _resident((1, d)), _resident(w.shape)],
        out_specs=[pl.BlockSpec((tm * MEM_HEADS, MEM_HEAD_DIM), lambda i: (i, 0))] * 2,
        out_shape=[jax.ShapeDtypeStruct((rows * MEM_HEADS, MEM_HEAD_DIM), F32)] * 2,
        compiler_params=_cparams(("parallel",)),
        name="mem_kv",
    )(mem2d, gain.reshape(1, d), w)


def _mem_attn_body(q, k_refs, v_refs):
    nb = len(q)
    M = k_refs[0].shape[0] // MEM_HEADS
    units = [(b, h) for b in range(nb) for h in range(MEM_HEADS)]
    s = [_dot_nt(q[b][:, h * MEM_HEAD_DIM:(h + 1) * MEM_HEAD_DIM], k_refs[b][pl.ds(h, M, stride=MEM_HEADS), :])
         * (MEM_HEAD_DIM ** -0.5) for b, h in units]
    p = [jnp.exp(x - jnp.max(x, axis=-1, keepdims=True)) for x in s]
    o = [_dot(p[u], v_refs[b][pl.ds(h, M, stride=MEM_HEADS), :]) / jnp.sum(p[u], axis=-1, keepdims=True)
         for u, (b, h) in enumerate(units)]
    return [jnp.concatenate(o[b * MEM_HEADS:(b + 1) * MEM_HEADS], axis=-1) for b in range(nb)]


def _mem_attn_kernel(q_ref, k_ref, v_ref, o_ref):
    nb = q_ref.shape[0]
    o = _mem_attn_body([q_ref[b] for b in range(nb)], [k_ref.at[b] for b in range(nb)],
                       [v_ref.at[b] for b in range(nb)])
    for b in range(nb):
        o_ref[b] = o[b]


def _mem_attn(qm, mem_k, mem_v, tm, nb):
    B, T, w = qm.shape
    kv_spec = pl.BlockSpec((nb,) + mem_k.shape[1:], lambda b, j: (b, 0, 0))
    return pl.pallas_call(
        _mem_attn_kernel,
        grid=(B // nb, T // tm),
        in_specs=[pl.BlockSpec((nb, tm, w), lambda b, j: (b, j, 0)), kv_spec, kv_spec],
        out_specs=pl.BlockSpec((nb, tm, w), lambda b, j: (b, j, 0)),
        out_shape=jax.ShapeDtypeStruct((B, T, w), F32),
        compiler_params=_cparams(("parallel", "parallel")),
        name="mem_attn",
    )(qm, mem_k, mem_v)


def _ffn_kernel(*refs, inject, emit_gate):
    if inject:
        h1_ref, om_ref, init_ref, fill_ref = refs[:4]
        rest = refs[4:]
    else:
        h1_ref, om_ref, init_ref = refs[:3]
        fill_ref, rest = None, refs[3:]
    _ffn_body(h1_ref[...], om_ref[...], init_ref, fill_ref, *rest, emit_gate=emit_gate)


def _ffn_body(h1, om, init_ref, fill_ref, wmo_ref, gf_ref, wup_ref, wc_ref, bc_ref, wd_ref, gfin_ref,
              y_ref, fc_ref, gs, *, emit_gate):
    tm = h1.shape[0]
    F = wd_ref.shape[0]
    PAD = SUBLANES

    @pl.when(pl.program_id(1) == 0)
    def _():
        gs[PAD - 2:PAD, :] = init_ref[...]

    h2 = h1 + _dot(om, wmo_ref[...])
    n = _rms(h2, gf_ref[...]).astype(BF16)
    gate = jnp.dot(n, wup_ref[:, :F], preferred_element_type=F32)
    if fill_ref is not None:
        r = lax.broadcasted_iota(jnp.int32, (tm, 1), 0)
        gate = jnp.where((r % SAMPLE_ROWS) >= SAMPLE_ROWS - 2, fill_ref[...], gate)
    gs[PAD:PAD + tm, :] = gate
    conv = (gs[PAD - 2:PAD - 2 + tm, :] * wc_ref[0:1, :] + gs[PAD - 1:PAD - 1 + tm, :] * wc_ref[1:2, :]
            + gate * wc_ref[2:3, :])
    last2 = gs[PAD + tm - 2:PAD + tm, :]
    gs[PAD - 2:PAD, :] = last2
    if emit_gate:
        fc_ref[...] = gate
    else:
        fc_ref[...] = last2
    up = jnp.dot(n, wup_ref[:, F:], preferred_element_type=F32)
    act = _silu(conv + bc_ref[...]) * up
    y = h2 + _dot(act, wd_ref[...])
    y_ref[...] = _rms(y, gfin_ref[...])


def _ffn(h1, om, init, fill, w_mo, norm_ffn, w_up, w_conv, b_conv, w_down, norm_final, tm, emit_gate):
    B, T, d = h1.shape
    F = w_down.shape[0]
    inject = fill is not None
    rt = lambda n: pl.BlockSpec((None, tm, n), lambda b, j: (b, j, 0))
    in_specs = [rt(d), rt(om.shape[-1]), pl.BlockSpec((None, 2, F), lambda b, j: (b, 0, 0))]
    args = [h1, om, init]
    if inject:
        in_specs.append(rt(F))
        args.append(fill)
    in_specs += [_resident(w_mo.shape), _resident((1, d)), _resident(w_up.shape), _resident(w_conv.shape),
                 _resident((1, F)), _resident(w_down.shape), _resident((1, d))]
    args += [w_mo, norm_ffn.reshape(1, d), w_up, w_conv, b_conv.reshape(1, F), w_down, norm_final.reshape(1, d)]
    if emit_gate:
        fc_spec, fc_shape = rt(F), jax.ShapeDtypeStruct((B, T, F), F32)
    else:
        fc_spec = pl.BlockSpec((None, 2, F), lambda b, j: (b, 0, 0))
        fc_shape = jax.ShapeDtypeStruct((B, 2, F), F32)
    return pl.pallas_call(
        functools.partial(_ffn_kernel, inject=inject, emit_gate=emit_gate),
        grid=(B, T // tm),
        in_specs=in_specs,
        out_specs=[rt(d), fc_spec],
        out_shape=[jax.ShapeDtypeStruct((B, T, d), F32), fc_shape],
        scratch_shapes=[pltpu.VMEM((tm + SUBLANES, F), F32)],
        compiler_params=_cparams(("parallel", "arbitrary")),
        name="ffn",
    )(*args)


def _post_kernel(*refs):
    n_mix = 2 * N_OG + 4
    mix_in, (k_ref, v_ref, init_ref), rest = refs[:n_mix], refs[n_mix:n_mix + 3], refs[n_mix + 3:]
    mix_w, ffn_rest = rest[:5], rest[5:]
    h1, qm = _mix_body(*mix_in, *mix_w)
    om, = _mem_attn_body([qm], [k_ref], [v_ref])
    _ffn_body(h1, om, init_ref, None, *ffn_rest, emit_gate=False)


def _post(o_g, l_g, o_b, ga, gb, h, mem_k, mem_v, init, w_a, w_b, w_o, norm_mem_q, w_mem_q, w_mo, norm_ffn, w_up,
          w_conv, b_conv, w_down, norm_final, tm):
    B, T, d = h.shape
    F = w_down.shape[0]
    rt = lambda n: pl.BlockSpec((None, tm, n), lambda b, j: (b, j, 0))
    per_b = lambda a: pl.BlockSpec((None,) + a.shape[1:], lambda b, j: (b, 0, 0))
    weights = [w_a, w_b, w_o, norm_mem_q.reshape(1, d), w_mem_q, w_mo, norm_ffn.reshape(1, d), w_up, w_conv,
               b_conv.reshape(1, F), w_down, norm_final.reshape(1, d)]
    return pl.pallas_call(
        _post_kernel,
        grid=(B, T // tm),
        in_specs=([rt(LANES)] * (2 * N_OG) + [rt(GDN_W), rt(d), rt(d), rt(d), per_b(mem_k), per_b(mem_v), per_b(init)]
                  + [_resident(w.shape) for w in weights]),
        out_specs=[rt(d), pl.BlockSpec((None, 2, F), lambda b, j: (b, 0, 0))],
        out_shape=[jax.ShapeDtypeStruct((B, T, d), F32), jax.ShapeDtypeStruct((B, 2, F), F32)],
        scratch_shapes=[pltpu.VMEM((tm + SUBLANES, F), F32)],
        compiler_params=_cparams(("parallel", "arbitrary")),
        name="post",
    )(*o_g, *l_g, o_b, ga, gb, h, mem_k, mem_v, init, *weights)


def kernel(x_prompt, x_sample, cache_dil0_kv, cache_dil1_kv, cache_dil2_kv, state_delta, state_delta_conv, cache_mem_k, cache_mem_v, state_ffn_conv, mem_prompt, rel_bias, norm_mix, w_in, w_conv_delta, a_log, dt_bias, norm_delta_out, w_branch_a, w_branch_b, w_out, norm_mem_q, norm_mem_kv, w_mem_q, w_mem_kv, w_mem_o, norm_ffn, w_ffn_up, w_ffn_conv, b_ffn_conv, w_ffn_down, norm_final):
    B, S, D = x_prompt.shape
    Bs, Ts, _ = x_sample.shape
    depth = w_in.shape[0]
    assert depth == 1 and Ts <= SAMPLE_ROWS - 2 and Ts >= 3 and S % (16 * TILE_Q) == 0
    F = w_ffn_down.shape[1]
    M = mem_prompt.shape[1]
    l = 0
    w_arr = _arrange_w_in(w_in[l])
    w_a, w_b, w_o = (w.astype(BF16) for w in (w_branch_a[l], w_branch_b[l], w_out[l]))
    w_mq, w_mkv, w_mo = (w.astype(BF16) for w in (w_mem_q[l], w_mem_kv[l], w_mem_o[l]))
    w_up, w_dn = w_ffn_up[l].astype(BF16), w_ffn_down[l].astype(BF16)
    t_cat, t_cur = _prompt_bias_tables(rel_bias)

    xp = x_prompt.reshape(B * S, D)
    q, kv0, kv1, kv2, kt0, kt1, kt2, gq, gq_tail, z, ba, ga, gb = _in_proj(
        xp, norm_mix[l], w_arr, 256, seq=S, conv_buf=jnp.zeros((B, 3, 3 * GDN_W), F32), w_conv=w_conv_delta[l])
    kvs = [kv.reshape(B, S, 2 * DIL_GW) for kv in (kv0, kv1, kv2)]
    q3 = q.reshape(B, S, 3 * DIL_GW)
    o_g, l_g = [], []
    for g in range(3):
        o_sl, l_sl = _dil_prompt(q3, kvs[g], g, t_cat[g], t_cur[g])
        o_g += o_sl
        l_g += l_sl
    o_b, delta_p = _gdn(gq.reshape(B, S, -1), ba.reshape(B, S, LANES), z.reshape(B, S, GDN_W),
                        jnp.zeros((B, GDN_HEADS, GDN_HEAD_DIM, GDN_HEAD_DIM), F32), a_log[l], dt_bias[l],
                        norm_delta_out[l])
    mk_p, mv_p = _mem_kv(mem_prompt.reshape(B * M, D), norm_mem_kv[l], w_mkv, 256)
    mk_p, mv_p = (x.reshape(B, M * MEM_HEADS, MEM_HEAD_DIM) for x in (mk_p, mv_p))
    seq = lambda a: a.reshape(B, S, a.shape[-1])
    y_p, fconv_p = _post([seq(a) for a in o_g], [seq(a) for a in l_g], o_b, seq(ga), seq(gb), x_prompt, mk_p, mv_p,
                         jnp.zeros((B, 2, F), F32), w_a, w_b, w_o, norm_mem_q[l], w_mq, w_mo, norm_ffn[l], w_up,
                         w_ffn_conv[l], b_ffn_conv[l], w_dn, norm_final, 256)
    p_out = ([kt[:, :, S - min(w, S):].reshape(B, 2, DIL_HPG, DIL_HEAD_DIM, min(w, S)).transpose(0, 4, 1, 2, 3)[None]
              for kt, (w, _) in zip((kt0, kt1, kt2), DIL_GROUPS)]
             + [delta_p[None], gq_tail[:, SUBLANES - 3:][None], mk_p.reshape(1, B, M, MEM_HEADS, MEM_HEAD_DIM),
                mv_p.reshape(1, B, M, MEM_HEADS, MEM_HEAD_DIM), fconv_p[None]])

    R = SAMPLE_ROWS
    xs = jnp.pad(x_sample, ((0, 0), (0, R - Ts), (0, 0))).reshape(Bs * R, D)
    q, kv0, kv1, kv2, gq, z, ba, ga, gb = _in_proj(xs, norm_mix[l], w_arr, Bs * R)
    kvn = [kv.reshape(Bs, R, 2 * DIL_GW) for kv in (kv0, kv1, kv2)]
    caches_t = [jnp.transpose(c[l], (0, 2, 3, 4, 1)).reshape(Bs, 2 * DIL_GW, c.shape[2])
                for c in (cache_dil0_kv, cache_dil1_kv, cache_dil2_kv)]
    tabc, tabn = _sample_bias_tables(rel_bias, Ts)
    o_g, l_g, new_caches = _dil_sample(q.reshape(Bs, R, 3 * DIL_GW), kvn, caches_t, tabc, tabn, Ts)
    o_b, delta_s = _gdn_sample(gq.reshape(Bs, R, -1), ba.reshape(Bs, R, LANES), z.reshape(Bs, R, GDN_W),
                               state_delta_conv[l], state_delta[l], w_conv_delta[l], a_log[l], dt_bias[l],
                               norm_delta_out[l], Ts)
    o_b = o_b.reshape(Bs * R, GDN_W).astype(BF16)
    h1, qm = _mix(o_g, l_g, o_b, ga, gb, xs, w_a, w_b, w_o, norm_mem_q[l], w_mq, Bs * R)
    om = _mem_attn(qm.reshape(Bs, R, -1), cache_mem_k[l].reshape(Bs, M * MEM_HEADS, MEM_HEAD_DIM),
                   cache_mem_v[l].reshape(Bs, M * MEM_HEADS, MEM_HEAD_DIM), R, 4 if Bs % 4 == 0 else 1)
    fst = state_ffn_conv[l]
    fill = jnp.concatenate([jnp.zeros((Bs, R - 2, F), F32),
                            jnp.concatenate([fst[1:], jnp.zeros((1, 2, F), F32)], axis=0)], axis=1)
    y_s, gate_s = _ffn(h1.reshape(1, Bs * R, D), om.reshape(1, Bs * R, -1), fst[:1], fill.reshape(1, Bs * R, F),
                       w_mo, norm_ffn[l], w_up, w_ffn_conv[l], b_ffn_conv[l], w_dn, norm_final, Bs * R, True)
    y_s = y_s.reshape(Bs, R, D)[:, :Ts]
    gq3 = gq.reshape(Bs, R, -1)
    s_out = ([nc.reshape(Bs, 2, DIL_HPG, DIL_HEAD_DIM, nc.shape[2]).transpose(0, 4, 1, 2, 3)[None]
              for nc in new_caches]
             + [delta_s[None], gq3[:, Ts - 3:Ts][None], gate_s.reshape(Bs, R, F)[:, Ts - 2:Ts][None]])

    return (y_p.reshape(B, S, D), y_s, *p_out, *s_out)
```

```python
import functools
import math

import jax
import jax.numpy as jnp
import numpy as np
from jax import lax
from jax.experimental import pallas as pl
from jax.experimental.pallas import tpu as pltpu

F32 = jnp.float32
BF16 = jnp.bfloat16

PAST_LEN = 8192
DIL_GROUPS = ((128, 1), (512, 4), (2048, 16))
DIL_HPG = 4
DIL_HEAD_DIM = 64
DIL_GW = DIL_HPG * DIL_HEAD_DIM
DIL_NK = 129
REL_BUCKETS = 32
REL_MAX_DIST = 2048
GDN_HEADS = 8
GDN_HEAD_DIM = 128
GDN_W = GDN_HEADS * GDN_HEAD_DIM
GDN_CHUNK = 64
MEM_HEADS = 4
MEM_HEAD_DIM = 128
EPS = 1e-6
NEG = -1e30

LANES = 128
SUBLANES = 8
TILE_Q = 128
SAMPLE_ROWS = SUBLANES
CONV_PAD = SUBLANES
VMEM_LIMIT = 56 * 1024 * 1024


def _cparams(sem):
    return pltpu.CompilerParams(dimension_semantics=sem, vmem_limit_bytes=VMEM_LIMIT)


def _resident(shape):
    nd = len(shape)
    return pl.BlockSpec(shape, lambda *_: (0,) * nd, pipeline_mode=pl.Buffered(1))


def _rms(x, gain_row):
    return x * lax.rsqrt(jnp.mean(x * x, axis=-1, keepdims=True) + EPS) * gain_row


def _dot(a, b):
    return jnp.dot(a.astype(BF16), b.astype(BF16), preferred_element_type=F32)


def _dot_nt(a, b):
    return lax.dot_general(a.astype(BF16), b.astype(BF16), (((1,), (1,)), ((), ())), preferred_element_type=F32)


def _dot_tn(a, b):
    return lax.dot_general(a.astype(BF16), b.astype(BF16), (((0,), (0,)), ((), ())), preferred_element_type=F32)


def _split3(x):
    hi = x.astype(BF16)
    r1 = x - hi.astype(F32)
    mid = r1.astype(BF16)
    lo = (r1 - mid.astype(F32)).astype(BF16)
    return hi, mid, lo


def _dot_hp(a, b):
    ah, am, al = _split3(a)
    bh, bm, bl = _split3(b)
    d = functools.partial(jnp.dot, preferred_element_type=F32)
    return d(ah, bh) + (d(ah, bm) + d(am, bh)) + (d(am, bm) + d(ah, bl) + d(al, bh))


def _sigmoid(x):
    return 1.0 / (1.0 + jnp.exp(-x))


def _silu(x):
    return x * _sigmoid(x)


def _softplus(x):
    return jnp.maximum(x, 0.0) + jnp.log(1.0 + jnp.exp(-jnp.abs(x)))


IN_SEGS = (("q", 3 * DIL_GW), ("kv0", 2 * DIL_GW), ("kv1", 2 * DIL_GW), ("kv2", 2 * DIL_GW),
           ("gq", 3 * GDN_W), ("z", GDN_W), ("ba", LANES), ("ga", 1024), ("gb", 1024))


OFF_Q, OFF_K, OFF_V = 0, 3 * DIL_GW, 6 * DIL_GW
OFF_GQ = 9 * DIL_GW
OFF_Z = OFF_GQ + 3 * GDN_W
OFF_BA = OFF_Z + GDN_W
OFF_GATES = OFF_BA + 2 * GDN_HEADS


def _arrange_w_in(w_in):
    w = w_in.astype(BF16)
    return w, w[:, OFF_GATES:]


GQ_SLABS = 3 * GDN_W // LANES


def _l2n(x):
    return x * lax.rsqrt(jnp.sum(x * x, axis=-1, keepdims=True) + EPS)


def _in_proj_kernel(x_ref, g_ref, w_ref, wg_ref, *rest, seq_tiles):
    if seq_tiles:
        (cb_ref, wc_ref, q_ref, kv0_ref, kv1_ref, kv2_ref, kt0_ref, kt1_ref, kt2_ref, gq_ref, tail_ref, z_ref,
         ba_ref, ga_ref, gb_ref, cs) = rest
        kt_refs = (kt0_ref, kt1_ref, kt2_ref)
    else:
        q_ref, kv0_ref, kv1_ref, kv2_ref, gq_ref, z_ref, ba_ref, ga_ref, gb_ref = rest
    tm = x_ref.shape[0]
    u = _rms(x_ref[...], g_ref[...]).astype(BF16)

    def seg(ref, off, n):
        return jnp.dot(u, ref[:, off:off + n], preferred_element_type=F32)

    def q_seg():
        q_ref[...] = seg(w_ref, OFF_Q, 3 * DIL_GW) * (DIL_HEAD_DIM ** -0.5)

    def kv_seg(g, part):
        kv_ref = (kv0_ref, kv1_ref, kv2_ref)[g]
        r = seg(w_ref, (OFF_K, OFF_V)[part] + g * DIL_GW, DIL_GW)
        kv_ref[:, part * DIL_GW:(part + 1) * DIL_GW] = r
        if seq_tiles:
            kt_refs[g][part * DIL_GW:(part + 1) * DIL_GW, :] = r.T

    def ba_seg():
        ba_ref[...] = seg(w_ref, OFF_BA, LANES)

    def gate_seg(ref, off):
        ref[...] = seg(wg_ref, off, 1024)

    others = ([(q_seg, 3 * DIL_GW)] + [(functools.partial(kv_seg, g, p), DIL_GW) for g in range(3) for p in range(2)]
              + [(functools.partial(gate_seg, ga_ref, 0), 1024), (functools.partial(gate_seg, gb_ref, 1024), 1024),
                 (ba_seg, LANES)])
    if not seq_tiles:
        for f, _ in others:
            f()
        for c in range(3):
            gq_ref[:, c * GDN_W:(c + 1) * GDN_W] = seg(w_ref, OFF_GQ + c * GDN_W, GDN_W)
        z_ref[...] = seg(w_ref, OFF_Z, GDN_W)
        return
    i = pl.program_id(0)
    first = (i % seq_tiles) == 0

    @pl.when(first)
    def _():
        for s in range(GQ_SLABS):
            cs[s, CONV_PAD - 3:CONV_PAD, :] = cb_ref[:, s * LANES:(s + 1) * LANES]

    @pl.when(jnp.logical_not(first))
    def _():
        for s in range(GQ_SLABS):
            cs[s, CONV_PAD - 3:CONV_PAD, :] = cs[s, CONV_PAD + tm - 3:CONV_PAD + tm, :]

    for c in range(3):
        r = seg(w_ref, OFF_GQ + c * GDN_W, GDN_W)
        for hh in range(GDN_HEADS):
            cs[c * GDN_HEADS + hh, CONV_PAD:CONV_PAD + tm, :] = r[:, hh * LANES:(hh + 1) * LANES]
    base = CONV_PAD - 3 + jnp.minimum(i, 0)

    def conv_slab(s):
        ls = slice(s * LANES, (s + 1) * LANES)
        y = cs[s, pl.ds(base, tm), :] * wc_ref[0:1, ls]
        for j in range(1, 4):
            y = y + cs[s, pl.ds(base + j, tm), :] * wc_ref[j:j + 1, ls]
        y = _silu(y)
        if s < GDN_HEADS:
            y = _l2n(y) * (GDN_HEAD_DIM ** -0.5)
        elif s < 2 * GDN_HEADS:
            y = _l2n(y)
        gq_ref[:, ls] = y
        tail_ref[:, ls] = cs[s, tm:tm + SUBLANES, :]

    def z_seg():
        z_ref[...] = _silu(seg(w_ref, OFF_Z, GDN_W))

    others.insert(len(others) - 1, (z_seg, GDN_W))
    total = sum(n for _, n in others)
    done, cols = 0, 0
    for f, n in others:
        f()
        cols += n
        upto = min(GQ_SLABS, (cols * GQ_SLABS + total - 1) // total)
        for s in range(done, upto):
            conv_slab(s)
        done = upto
    assert done == GQ_SLABS


def _in_proj(x2d, gain, w_arr, tm, seq=None, conv_buf=None, w_conv=None):
    rows, d = x2d.shape
    fused = seq is not None
    nt = seq // tm if fused else None
    names = [n for n, _ in IN_SEGS]
    widths = dict(IN_SEGS)
    row_spec = lambda n: pl.BlockSpec((tm, n), lambda i: (i, 0))
    out_specs, out_shape = [], []
    for n in names:
        out_specs.append(row_spec(widths[n]))
        out_shape.append(jax.ShapeDtypeStruct((rows, widths[n]), F32))
        if n == "kv2" and fused:
            for _ in range(3):
                out_specs.append(pl.BlockSpec((None, 2 * DIL_GW, tm), lambda i: (i // nt, 0, i % nt)))
                out_shape.append(jax.ShapeDtypeStruct((rows // seq, 2 * DIL_GW, seq), F32))
        if n == "gq" and fused:
            out_specs.append(pl.BlockSpec((None, SUBLANES, 3 * GDN_W), lambda i: (i // nt, 0, 0)))
            out_shape.append(jax.ShapeDtypeStruct((rows // seq, SUBLANES, 3 * GDN_W), F32))
    in_specs = [pl.BlockSpec((tm, d), lambda i: (i, 0)), _resident((1, d)), _resident(w_arr[0].shape),
                _resident(w_arr[1].shape)]
    args = [x2d, gain.reshape(1, d), *w_arr]
    if fused:
        in_specs += [pl.BlockSpec((None, 3, 3 * GDN_W), lambda i: (i // nt, 0, 0)), _resident(w_conv.shape)]
        args += [conv_buf, w_conv]
    return pl.pallas_call(
        functools.partial(_in_proj_kernel, seq_tiles=nt),
        grid=(rows // tm,),
        in_specs=in_specs,
        out_specs=out_specs,
        out_shape=out_shape,
        scratch_shapes=[pltpu.VMEM((GQ_SLABS, CONV_PAD + tm, LANES), F32)] if fused else [],
        compiler_params=_cparams(("arbitrary",)),
        name="in_proj",
    )(*args)


def _rel_bucket(dist):
    exact = REL_BUCKETS // 2
    d = jnp.maximum(dist, 1).astype(F32)
    large = exact + (jnp.log(d / exact) / math.log(REL_MAX_DIST / exact) * (REL_BUCKETS - exact)).astype(jnp.int32)
    return jnp.where(dist < exact, dist, jnp.minimum(large, REL_BUCKETS - 1))


def _group_bias(rel_bias, g):
    dil = DIL_GROUPS[g][1]
    dist = dil * jnp.arange(DIL_NK, dtype=jnp.int32)
    tab = rel_bias[_rel_bucket(dist)]
    return tab[:, g * DIL_HPG:(g + 1) * DIL_HPG].T.astype(F32)


def _toeplitz(v, n, width):
    h, L = v.shape
    return jnp.tile(v, (1, n))[:, :n * (L - 1)].reshape(h, n, L - 1)[:, :, :width]


def _prompt_bias_tables(rel_bias):
    cat, cur = [], []
    for g in range(3):
        bg = _group_bias(rel_bias, g)
        v = jnp.concatenate([bg[:, ::-1], jnp.full((DIL_HPG, 3 * TILE_Q - DIL_NK), NEG, F32)], axis=1)
        t = _toeplitz(v, TILE_Q, 2 * TILE_Q)
        cat.append(t)
        cur.append(t[:, :, TILE_Q:])
    return cat, cur


DIL_TIF = 2
DIL_SLABS = DIL_GW // LANES


def _dil_prompt_kernel(q0_ref, q1_ref, k0_ref, k1_ref, v0_ref, v1_ref, tcat_ref, tcur_ref,
                       o0_ref, o1_ref, l0_ref, l1_ref, *, dil):
    S = q0_ref.shape[0]
    nb = S // dil // TILE_Q
    q_refs, k_refs, v_refs = (q0_ref, q1_ref), (k0_ref, k1_ref), (v0_ref, v1_ref)
    o_refs, l_refs = (o0_ref, o1_ref), (l0_ref, l1_ref)
    even = lax.broadcasted_iota(jnp.int32, (TILE_Q, LANES), 1) < DIL_HEAD_DIM

    def rows(r, t):
        start = r + dil * TILE_Q * t
        return pl.ds(start, TILE_Q, stride=dil) if dil > 1 else pl.ds(start, TILE_Q)

    tiles = [(r, t) for r in range(dil) for t in range(nb)]
    for i0 in range(0, len(tiles), DIL_TIF):
        grp = tiles[i0:i0 + DIL_TIF]
        qm, kc, vc = {}, {}, {}
        for ti, (r, t) in enumerate(grp):
            for sl in range(DIL_SLABS):
                qf = q_refs[sl][rows(r, t), :]
                qm[ti, 2 * sl] = jnp.where(even, qf, 0.0).astype(BF16)
                qm[ti, 2 * sl + 1] = jnp.where(even, 0.0, qf).astype(BF16)
                kc[ti, sl] = k_refs[sl][rows(r, t), :].astype(BF16)
                vc[ti, sl] = v_refs[sl][rows(r, t), :].astype(BF16)
                if t > 0:
                    kc[ti, sl] = jnp.concatenate([k_refs[sl][rows(r, t - 1), :].astype(BF16), kc[ti, sl]], axis=0)
                    vc[ti, sl] = jnp.concatenate([v_refs[sl][rows(r, t - 1), :].astype(BF16), vc[ti, sl]], axis=0)
        units = [(ti, h) for ti in range(len(grp)) for h in range(DIL_HPG)]
        s = [_dot_nt(qm[ti, h], kc[ti, h // 2]) + (tcat_ref[h] if grp[ti][1] > 0 else tcur_ref[h]) for ti, h in units]
        m = [jnp.max(x, axis=-1, keepdims=True) for x in s]
        p = [jnp.exp(x - mx) for x, mx in zip(s, m)]
        l = [jnp.sum(x, axis=-1, keepdims=True) for x in p]
        pv = [jnp.dot(p[u].astype(BF16), vc[ti, h // 2], preferred_element_type=F32) for u, (ti, h) in enumerate(units)]
        o = [pv[u] / l[u] for u in range(len(units))]
        lse = [m[u] + jnp.log(l[u]) for u in range(len(units))]
        for ti, (r, t) in enumerate(grp):
            for sl in range(DIL_SLABS):
                ue, uo = ti * DIL_HPG + 2 * sl, ti * DIL_HPG + 2 * sl + 1
                o_refs[sl][rows(r, t), :] = jnp.where(even, o[ue], o[uo])
                l_refs[sl][rows(r, t), :] = jnp.where(even, lse[ue], lse[uo])


def _dil_prompt(q, kv, g, t_cat, t_cur):
    B, S, _ = q.shape
    dil = DIL_GROUPS[g][1]
    slab = lambda c: pl.BlockSpec((None, S, LANES), lambda b: (b, 0, c))
    nq, nk = g * DIL_SLABS, 0
    outs = pl.pallas_call(
        functools.partial(_dil_prompt_kernel, dil=dil),
        grid=(B,),
        in_specs=[slab(nq), slab(nq + 1), slab(nk), slab(nk + 1), slab(nk + 2), slab(nk + 3),
                  _resident(t_cat.shape), _resident(t_cur.shape)],
        out_specs=[slab(0)] * 4,
        out_shape=[jax.ShapeDtypeStruct((B, S, LANES), F32)] * 4,
        compiler_params=_cparams(("parallel",)),
        name=f"dil_prompt_g{g}",
    )(q, q, kv, kv, kv, kv, t_cat, t_cur)
    o0, o1, l0, l1 = (x.reshape(B * S, LANES) for x in outs)
    return [o0, o1], [l0, l1]


def _sample_bias_tables(rel_bias, t_real):
    R = SAMPLE_ROWS
    tabc, tabn = [], []
    t_i = np.arange(R)[:, None]
    u_i = np.arange(R)[None, :]
    for g, (w, dil) in enumerate(DIL_GROUPS):
        bg = _group_bias(rel_bias, g)
        base = bg[:, ::-1][:, :TILE_Q]
        t0 = jnp.concatenate([base[:, :, None], jnp.full((DIL_HPG, TILE_Q, dil - 1), NEG, F32)], axis=2)
        t0 = t0.reshape(DIL_HPG, w)
        rows = [jnp.concatenate([jnp.full((DIL_HPG, t), NEG, F32), t0[:, :w - t]], axis=1) for t in range(t_real)]
        rows += [jnp.zeros((DIL_HPG, w), F32)] * (R - t_real)
        tabc.append(jnp.stack(rows, axis=1).reshape(DIL_HPG * R, w))
        tn = jnp.full((DIL_HPG, R, R), NEG, F32)
        for j in range(-(-t_real // dil)):
            hit = (t_i - u_i == j * dil) & (t_i < t_real)
            tn = jnp.where(hit[None], bg[:, j][:, None, None], tn)
        tn = jnp.where((t_i >= t_real)[None], 0.0, tn)
        tabn.append(tn.reshape(DIL_HPG * R, R))
    return tabc, jnp.stack(tabn)


def _dil_sample_kernel(q_ref, n0_ref, n1_ref, n2_ref, c0_ref, c1_ref, c2_ref, tc0_ref, tc1_ref, tc2_ref, tn_ref,
                       o_ref, l_ref, oc0_ref, oc1_ref, oc2_ref, *, t_real):
    R = SAMPLE_ROWS
    rows = lax.broadcasted_iota(jnp.int32, (DIL_HPG * R, DIL_GW), 0)
    lanes = lax.broadcasted_iota(jnp.int32, (DIL_HPG * R, DIL_GW), 1)
    head_mask = (lanes // DIL_HEAD_DIM) == (rows // R)
    lane_f = lax.broadcasted_iota(jnp.int32, (2 * DIL_GW, LANES), 1)
    keep = lane_f < LANES - t_real
    sel_l = lax.broadcasted_iota(jnp.int32, (LANES, R), 0)
    sel_u = lax.broadcasted_iota(jnp.int32, (LANES, R), 1)
    selT = ((sel_l == sel_u + LANES - t_real) & (sel_u < t_real)).astype(BF16)
    groups = ((n0_ref, c0_ref, tc0_ref, oc0_ref), (n1_ref, c1_ref, tc1_ref, oc1_ref), (n2_ref, c2_ref, tc2_ref, oc2_ref))

    def fold_heads(x):
        x = jnp.where(head_mask, x, 0.0)
        return x[0:R] + x[R:2 * R] + x[2 * R:3 * R] + x[3 * R:4 * R]

    for g, (n_ref, c_ref, tc_ref, oc_ref) in enumerate(groups):
        W = c_ref.shape[1]
        kvn = n_ref[...]
        q_g = q_ref[:, g * DIL_GW:(g + 1) * DIL_GW]
        q_bd = jnp.where(head_mask, jnp.concatenate([q_g] * DIL_HPG, axis=0), 0.0)
        s_c = _dot(q_bd, c_ref[:DIL_GW, :]) + tc_ref[...]
        s_n = _dot_nt(q_bd, kvn[:, :DIL_GW]) + tn_ref[g]
        m = jnp.maximum(jnp.max(s_c, axis=-1, keepdims=True), jnp.max(s_n, axis=-1, keepdims=True))
        p_c = jnp.exp(s_c - m)
        p_n = jnp.exp(s_n - m)
        l = jnp.sum(p_c, axis=-1, keepdims=True) + jnp.sum(p_n, axis=-1, keepdims=True)
        acc = (_dot_nt(p_c, c_ref[DIL_GW:, :]) + _dot(p_n, kvn[:, DIL_GW:])) / l
        o_ref[:, g * DIL_GW:(g + 1) * DIL_GW] = fold_heads(acc)
        l_ref[:, g * DIL_GW:(g + 1) * DIL_GW] = fold_heads(jnp.broadcast_to(m + jnp.log(l), acc.shape))
        hi, mid, lo = _split3(kvn)
        tail = (jnp.dot(selT, hi, preferred_element_type=F32) + jnp.dot(selT, mid, preferred_element_type=F32)
                + jnp.dot(selT, lo, preferred_element_type=F32)).T
        nxt = pltpu.roll(c_ref[:, 0:LANES], LANES - t_real, axis=1)
        for c in range(W // LANES):
            cur = nxt
            nxt = (pltpu.roll(c_ref[:, (c + 1) * LANES:(c + 2) * LANES], LANES - t_real, axis=1)
                   if (c + 1) * LANES < W else tail)
            oc_ref[:, c * LANES:(c + 1) * LANES] = jnp.where(keep, cur, nxt)


def _dil_sample(q, kvn, caches_t, tabc, tabn, t_real):
    B = q.shape[0]
    row = lambda n: pl.BlockSpec((None, SAMPLE_ROWS, n), lambda b: (b, 0, 0))
    cspecs = [pl.BlockSpec((None,) + c.shape[1:], lambda b: (b, 0, 0)) for c in caches_t]
    for g, (w, dil) in enumerate(DIL_GROUPS):
        assert caches_t[g].shape == (B, 2 * DIL_GW, w) and w // dil == TILE_Q
    out_spec = row(3 * DIL_GW)
    o, lse, *new_caches = pl.pallas_call(
        functools.partial(_dil_sample_kernel, t_real=t_real),
        grid=(B,),
        in_specs=([row(3 * DIL_GW)] + [row(2 * DIL_GW)] * 3 + cspecs + [_resident(t.shape) for t in tabc]
                  + [_resident(tabn.shape)]),
        out_specs=[out_spec, out_spec] + cspecs,
        out_shape=([jax.ShapeDtypeStruct((B, SAMPLE_ROWS, 3 * DIL_GW), F32)] * 2
                   + [jax.ShapeDtypeStruct(c.shape, F32) for c in caches_t]),
        compiler_params=_cparams(("parallel",)),
        name="dil_sample",
    )(q, *kvn, *caches_t, *tabc, tabn)
    o = o.reshape(B * SAMPLE_ROWS, 3 * DIL_GW)
    lse = lse.reshape(B * SAMPLE_ROWS, 3 * DIL_GW)
    n = 3 * DIL_SLABS
    return ([o[:, i * LANES:(i + 1) * LANES] for i in range(n)],
            [lse[:, i * LANES:(i + 1) * LANES] for i in range(n)], new_caches)


GDN_HPS = GDN_HEADS
GDN_SW = GDN_HPS * GDN_HEAD_DIM
GDN_TPI = 4
GDN_ROWS = 1024


def _gdn_kernel(q_ref, k_ref, v_ref, ba_ref, alog_ref, dtb_ref, z_ref, gn_ref, tri_ref, s0_ref, o_ref, s_ref, carry_s,
                *, tpi):
    T = q_ref.shape[0]
    C = GDN_CHUNK
    D = GDN_HEAD_DIM
    ri = lax.broadcasted_iota(jnp.int32, (TILE_Q, TILE_Q), 0)
    ci = lax.broadcasted_iota(jnp.int32, (TILE_Q, TILE_Q), 1)
    same = (ri // C) == (ci // C)
    incl = same & (ri >= ci)
    strict = same & (ri > ci)
    eye = (ri == ci).astype(F32)
    lane, row = ci, ri
    gain = gn_ref[...]
    zpad = jnp.zeros((C, D), F32)
    HH = range(GDN_HPS)
    hsl = [slice(hh * D, (hh + 1) * D) for hh in HH]
    RW = tpi * TILE_Q
    UU = [(tt, hh) for tt in range(tpi) for hh in HH]
    UI = range(len(UU))

    def tile(i, S):
        r0 = pl.multiple_of(i * RW, RW)
        rt = [r0 + tt * TILE_Q for tt in range(tpi)]
        q = [q_ref[pl.ds(rt[tt], TILE_Q), hsl[hh]] for tt, hh in UU]
        k = [k_ref[pl.ds(rt[tt], TILE_Q), hsl[hh]] for tt, hh in UU]
        v = [v_ref[pl.ds(rt[tt], TILE_Q), hsl[hh]] for tt, hh in UU]
        tri = tri_ref[...]
        beta_all, G_all = [], []
        for tt in range(tpi):
            ba = ba_ref[pl.ds(rt[tt], TILE_Q), :]
            beta_all.append(_sigmoid(ba))
            gh, gm, gl = _split3(-jnp.exp(alog_ref[...]) * _softplus(ba + dtb_ref[...]))
            G_all.append(jnp.dot(tri, gh, preferred_element_type=F32) + jnp.dot(tri, gm, preferred_element_type=F32)
                         + jnp.dot(tri, gl, preferred_element_type=F32))
        head = [hh for _, hh in UU]
        bc = [jnp.sum(jnp.where(lane == head[u], beta_all[UU[u][0]], 0.0), axis=-1, keepdims=True) for u in UI]
        Gc = [jnp.broadcast_to(jnp.sum(jnp.where(lane == head[u] + GDN_HEADS, G_all[UU[u][0]], 0.0), axis=-1,
                                       keepdims=True), (TILE_Q, TILE_Q)) for u in UI]
        gamma = [jnp.exp(jnp.where(incl, Gc[u] - Gc[u].T, NEG)) for u in UI]
        kk = [_dot_nt(k[u], k[u]) for u in UI]
        qk = [_dot_nt(q[u], k[u]) for u in UI]
        X = [jnp.where(strict, bc[u] * kk[u] * gamma[u], 0.0) for u in UI]
        P = [eye - X[u] for u in UI]
        for _ in range(int(math.log2(C)) - 1):
            X = [_dot(X[u], X[u]) for u in UI]
            P = [P[u] + _dot(P[u], X[u]) for u in UI]
        eG = [jnp.exp(Gc[u]) for u in UI]
        rhs = [jnp.concatenate([v[u] * bc[u], k[u] * (bc[u] * eG[u])], axis=-1) for u in UI]
        sol = [rhs[u] + _dot(P[u] - eye, rhs[u]) for u in UI]
        a_in = [qk[u] * gamma[u] for u in UI]
        q_dec = [q[u] * eG[u] for u in UI]
        kdT = [(k[u] * jnp.exp(jnp.where(row < C, Gc[u][C - 1:C, :], Gc[u][2 * C - 1:2 * C, :]) - Gc[u])).T
               for u in UI]
        S = list(S)
        for tt in range(tpi):
            us = [tt * GDN_HPS + hh for hh in HH]
            oq, vn = [[] for _ in HH], [[] for _ in HH]
            for c in range(TILE_Q // C):
                cs = slice(c * C, (c + 1) * C)
                r = [_dot(jnp.concatenate([sol[us[hh]][cs, D:], q_dec[us[hh]][cs]], axis=0), S[hh]) for hh in HH]
                for hh in HH:
                    oq[hh].append(r[hh][C:])
                    vn[hh].append(sol[us[hh]][cs, :D] - r[hh][:C])
                vpad = [jnp.concatenate([vn[hh][c], zpad] if c == 0 else [zpad, vn[hh][c]], axis=0) for hh in HH]
                S = [S[hh] * jnp.exp(Gc[us[hh]][(c + 1) * C - 1:(c + 1) * C, :]) + _dot(kdT[us[hh]], vpad[hh])
                     for hh in HH]
            o = [jnp.concatenate(oq[hh], axis=0) + _dot(a_in[us[hh]], jnp.concatenate(vn[hh], axis=0)) for hh in HH]
            outs = [_rms(o[hh], gain) * z_ref[pl.ds(rt[tt], TILE_Q), hsl[hh]] for hh in HH]
            o_ref[pl.ds(rt[tt], TILE_Q), :] = jnp.concatenate(outs, axis=-1).astype(o_ref.dtype)
        return tuple(S)

    @pl.when(pl.program_id(1) == 0)
    def _():
        carry_s[...] = s0_ref[...]

    S = lax.fori_loop(0, T // RW, tile, tuple(carry_s[hh] for hh in HH))
    for hh in HH:
        carry_s[hh] = S[hh]
        s_ref[hh] = S[hh]


def _gdn(gq, ba, z, s0, a_log, dt_bias, norm_out):
    B, T, _ = gq.shape
    H = GDN_HEADS
    pad16 = lambda x: jnp.concatenate([jnp.zeros((H,), F32), x.astype(F32), jnp.zeros((LANES - 2 * H,), F32)])
    assert GDN_HPS == H
    tb = min(GDN_ROWS, T)
    tpi = GDN_TPI if (tb // TILE_Q) % GDN_TPI == 0 else 1
    col = lambda off: pl.BlockSpec((None, tb, GDN_SW), lambda b, j: (b, j, off))
    sblk = pl.BlockSpec((None, H, GDN_HEAD_DIM, GDN_HEAD_DIM), lambda b, j: (b, 0, 0, 0))
    r = np.arange(TILE_Q)
    tri = jnp.asarray((r[:, None] >= r[None, :]) & (r[:, None] // GDN_CHUNK == r[None, :] // GDN_CHUNK), BF16)
    o, s_new = pl.pallas_call(
        functools.partial(_gdn_kernel, tpi=tpi),
        grid=(B, T // tb),
        in_specs=[col(0), col(1), col(2),
                  pl.BlockSpec((None, tb, LANES), lambda b, j: (b, j, 0)),
                  _resident((1, LANES)), _resident((1, LANES)),
                  col(0), _resident((1, LANES)), _resident((TILE_Q, TILE_Q)), sblk],
        out_specs=[col(0), sblk],
        out_shape=[jax.ShapeDtypeStruct((B, T, GDN_W), BF16),
                   jax.ShapeDtypeStruct((B, H, GDN_HEAD_DIM, GDN_HEAD_DIM), F32)],
        scratch_shapes=[pltpu.VMEM((H, GDN_HEAD_DIM, GDN_HEAD_DIM), F32)],
        compiler_params=_cparams(("parallel", "arbitrary")),
        name="gdn",
    )(gq, gq, gq, ba, pad16(a_log).reshape(1, LANES), pad16(dt_bias).reshape(1, LANES), z,
      norm_out.reshape(1, LANES), tri, s0)
    return o, s_new


def _gdn_sample_kernel(x_ref, b_ref, w_ref, ba_ref, alog_ref, dtb_ref, z_ref, gn_ref, s0_ref, o_ref, s_ref, xs,
                       *, t_real):
    R, D, H = SAMPLE_ROWS, GDN_HEAD_DIM, GDN_HEADS
    xs[CONV_PAD - 3:CONV_PAD, :] = b_ref[...]
    xs[CONV_PAD:CONV_PAD + R, :] = x_ref[...]
    y = xs[CONV_PAD - 3:CONV_PAD - 3 + R, :] * w_ref[0:1, :]
    for j in range(1, 4):
        y = y + xs[CONV_PAD - 3 + j:CONV_PAD - 3 + j + R, :] * w_ref[j:j + 1, :]
    y = _silu(y)
    ri = lax.broadcasted_iota(jnp.int32, (R, R), 0)
    ci = lax.broadcasted_iota(jnp.int32, (R, R), 1)
    incl, strict = ri >= ci, ri > ci
    eye = (ri == ci).astype(F32)
    tri = incl.astype(BF16)
    er = lax.broadcasted_iota(jnp.int32, (LANES, LANES), 0)
    ec = lax.broadcasted_iota(jnp.int32, (LANES, LANES), 1)
    eye_l = (er == ec).astype(BF16)
    live = lax.broadcasted_iota(jnp.int32, (R, LANES), 0) < t_real
    ba = ba_ref[...]
    beta_all = jnp.where(live, _sigmoid(ba), 0.0)
    g_all = jnp.where(live, -jnp.exp(alog_ref[...]) * _softplus(ba + dtb_ref[...]), 0.0)
    d32 = functools.partial(jnp.dot, preferred_element_type=F32)
    gh, gm, gl = _split3(g_all)
    G_all = d32(tri, gh) + d32(tri, gm) + d32(tri, gl)
    th, tm, tl = _split3(G_all)
    nt = lambda a, b: lax.dot_general(a, b, (((1,), (1,)), ((), ())), preferred_element_type=F32)
    G_allT = nt(eye_l, th) + nt(eye_l, tm) + nt(eye_l, tl)
    gain = gn_ref[...]
    HR = range(H)

    def l2n(x):
        return x * lax.rsqrt(jnp.sum(x * x, axis=-1, keepdims=True) + EPS)

    q = [l2n(y[:, h * D:(h + 1) * D]) * (D ** -0.5) for h in HR]
    k = [l2n(y[:, GDN_W + h * D:GDN_W + (h + 1) * D]) for h in HR]
    v = [y[:, 2 * GDN_W + h * D:2 * GDN_W + (h + 1) * D] for h in HR]
    bc = [beta_all[:, h:h + 1] for h in HR]
    Gc = [G_all[:, H + h:H + h + 1] for h in HR]
    gamma = [jnp.exp(jnp.where(incl, Gc[h] - G_allT[H + h:H + h + 1, :], NEG)) for h in HR]
    kk = [_dot_nt(k[h], k[h]) for h in HR]
    qk = [_dot_nt(q[h], k[h]) for h in HR]
    X = [jnp.where(strict, bc[h] * kk[h] * gamma[h], 0.0) for h in HR]
    P = [eye - X[h] for h in HR]
    for _ in range(int(math.log2(R)) - 1):
        X = [_dot(X[h], X[h]) for h in HR]
        P = [P[h] + _dot(P[h], X[h]) for h in HR]
    eG = [jnp.exp(Gc[h]) for h in HR]
    rhs = [jnp.concatenate([v[h] * bc[h], k[h] * (bc[h] * eG[h])], axis=-1) for h in HR]
    sol = [rhs[h] + _dot(P[h] - eye, rhs[h]) for h in HR]
    S = [s0_ref[h] for h in HR]
    r = [_dot(jnp.concatenate([sol[h][:, D:], q[h] * eG[h]], axis=0), S[h]) for h in HR]
    v_new = [sol[h][:, :D] - r[h][:R] for h in HR]
    o = [r[h][R:] + _dot(qk[h] * gamma[h], v_new[h]) for h in HR]
    kdT = [nt(eye_l, (k[h] * jnp.exp(Gc[h][R - 1:R, :] - Gc[h])).astype(BF16)) for h in HR]
    for h in HR:
        s_ref[h] = S[h] * jnp.exp(Gc[h][R - 1:R, :]) + _dot(kdT[h], v_new[h])
    o_ref[...] = jnp.concatenate([_rms(o[h], gain) * _silu(z_ref[:, h * D:(h + 1) * D]) for h in HR], axis=-1)


def _gdn_sample(gq, ba, z, conv_buf, s0, w_conv, a_log, dt_bias, norm_out, t_real):
    B, R, _ = gq.shape
    H = GDN_HEADS
    pad16 = lambda x: jnp.concatenate([jnp.zeros((H,), F32), x.astype(F32), jnp.zeros((LANES - 2 * H,), F32)])
    blk = lambda *s: pl.BlockSpec((None,) + s, lambda b: (b,) + (0,) * len(s))
    return pl.pallas_call(
        functools.partial(_gdn_sample_kernel, t_real=t_real),
        grid=(B,),
        in_specs=[blk(R, 3 * GDN_W), blk(3, 3 * GDN_W), _resident(w_conv.shape), blk(R, LANES),
                  _resident((1, LANES)), _resident((1, LANES)), blk(R, GDN_W), _resident((1, LANES)),
                  blk(H, GDN_HEAD_DIM, GDN_HEAD_DIM)],
        out_specs=[blk(R, GDN_W), blk(H, GDN_HEAD_DIM, GDN_HEAD_DIM)],
        out_shape=[jax.ShapeDtypeStruct((B, R, GDN_W), F32),
                   jax.ShapeDtypeStruct((B, H, GDN_HEAD_DIM, GDN_HEAD_DIM), F32)],
        scratch_shapes=[pltpu.VMEM((CONV_PAD + R, 3 * GDN_W), F32)],
        compiler_params=_cparams(("parallel",)),
        name="gdn_sample",
    )(gq, conv_buf, w_conv, ba, pad16(a_log).reshape(1, LANES), pad16(dt_bias).reshape(1, LANES), z,
      norm_out.reshape(1, LANES), s0)


N_OG = 3 * DIL_SLABS


def _mix_kernel(*refs):
    h1, qm = _mix_body(*refs[:2 * N_OG + 9])
    h1_ref, qm_ref = refs[2 * N_OG + 9:]
    h1_ref[...] = h1
    qm_ref[...] = qm


def _mix_body(*refs):
    o_refs, l_refs = refs[:N_OG], refs[N_OG:2 * N_OG]
    ob_ref, ga_ref, gb_ref, h_ref, wa_ref, wb_ref, wo_ref, gq_ref, wq_ref = refs[2 * N_OG:]
    slabs = []
    for sl in range(DIL_SLABS):
        l0, l1, l2 = (l_refs[g * DIL_SLABS + sl][...] for g in range(3))
        o0, o1, o2 = (o_refs[g * DIL_SLABS + sl][...] for g in range(3))
        mx = jnp.maximum(jnp.maximum(l0, l1), l2)
        e0, e1, e2 = jnp.exp(l0 - mx), jnp.exp(l1 - mx), jnp.exp(l2 - mx)
        slabs.append((e0 * o0 + e1 * o1 + e2 * o2) / (e0 + e1 + e2))
    o_a = jnp.concatenate(slabs, axis=-1)
    a = _dot(o_a, wa_ref[...])
    b = jnp.dot(ob_ref[...], wb_ref[...], preferred_element_type=F32)
    merged = _sigmoid(ga_ref[...]) * a + _sigmoid(gb_ref[...]) * b
    h1 = h_ref[...] + _dot(merged, wo_ref[...])
    return h1, _dot(_rms(h1, gq_ref[...]), wq_ref[...]).astype(BF16)


def _mix(o_g, l_g, o_b, ga, gb, h, w_a, w_b, w_o, norm_mem_q, w_mem_q, tm):
    rows, d = h.shape
    rt = lambda n: pl.BlockSpec((tm, n), lambda i: (i, 0))
    assert len(o_g) == len(l_g) == 3 * DIL_SLABS
    return pl.pallas_call(
        _mix_kernel,
        grid=(rows // tm,),
        in_specs=[rt(LANES)] * (6 * DIL_SLABS) + [rt(GDN_W), rt(d), rt(d), rt(d),
                                     _resident(w_a.shape), _resident(w_b.shape), _resident(w_o.shape),
                                     _resident((1, d)), _resident(w_mem_q.shape)],
        out_specs=[rt(d), rt(w_mem_q.shape[1])],
        out_shape=[jax.ShapeDtypeStruct((rows, d), F32), jax.ShapeDtypeStruct((rows, w_mem_q.shape[1]), BF16)],
        compiler_params=_cparams(("parallel",)),
        name="mix",
    )(*o_g, *l_g, o_b, ga, gb, h, w_a, w_b, w_o, norm_mem_q.reshape(1, d), w_mem_q)


def _mem_kv_kernel(x_ref, g_ref, w_ref, k_ref, v_ref):
    u = _rms(x_ref[...], g_ref[...]).astype(BF16)
    tm = x_ref.shape[0]
    n = MEM_HEADS * MEM_HEAD_DIM
    for o_ref, off in ((k_ref, 0), (v_ref, n)):
        r = jnp.dot(u, w_ref[:, off:off + n], preferred_element_type=F32)
        for h in range(MEM_HEADS):
            o_ref[pl.ds(h, tm, stride=MEM_HEADS), :] = r[:, h * MEM_HEAD_DIM:(h + 1) * MEM_HEAD_DIM]


def _mem_kv(mem2d, gain, w, tm):
    rows, d = mem2d.shape
    return pl.pallas_call(
        _mem_kv_kernel,
        grid=(rows // tm,),
        in_specs=[pl.BlockSpec((tm, d), lambda i: (i, 0)), _resident((1, d)), _resident(w.shape)],
        out_specs=[pl.BlockSpec((tm * MEM_HEADS, MEM_HEAD_DIM), lambda i: (i, 0))] * 2,
        out_shape=[jax.ShapeDtypeStruct((rows * MEM_HEADS, MEM_HEAD_DIM), F32)] * 2,
        compiler_params=_cparams(("parallel",)),
        name="mem_kv",
    )(mem2d, gain.reshape(1, d), w)


def _mem_attn_body(q, k_refs, v_refs):
    nb = len(q)
    M = k_refs[0].shape[0] // MEM_HEADS
    units = [(b, h) for b in range(nb) for h in range(MEM_HEADS)]
    s = [_dot_nt(q[b][:, h * MEM_HEAD_DIM:(h + 1) * MEM_HEAD_DIM], k_refs[b][pl.ds(h, M, stride=MEM_HEADS), :])
         * (MEM_HEAD_DIM ** -0.5) for b, h in units]
    p = [jnp.exp(x - jnp.max(x, axis=-1, keepdims=True)) for x in s]
    o = [_dot(p[u], v_refs[b][pl.ds(h, M, stride=MEM_HEADS), :]) / jnp.sum(p[u], axis=-1, keepdims=True)
         for u, (b, h) in enumerate(units)]
    return [jnp.concatenate(o[b * MEM_HEADS:(b + 1) * MEM_HEADS], axis=-1) for b in range(nb)]


def _mem_attn_kernel(q_ref, k_ref, v_ref, o_ref):
    nb = q_ref.shape[0]
    o = _mem_attn_body([q_ref[b] for b in range(nb)], [k_ref.at[b] for b in range(nb)],
                       [v_ref.at[b] for b in range(nb)])
    for b in range(nb):
        o_ref[b] = o[b]


def _mem_attn(qm, mem_k, mem_v, tm, nb):
    B, T, w = qm.shape
    kv_spec = pl.BlockSpec((nb,) + mem_k.shape[1:], lambda b, j: (b, 0, 0))
    return pl.pallas_call(
        _mem_attn_kernel,
        grid=(B // nb, T // tm),
        in_specs=[pl.BlockSpec((nb, tm, w), lambda b, j: (b, j, 0)), kv_spec, kv_spec],
        out_specs=pl.BlockSpec((nb, tm, w), lambda b, j: (b, j, 0)),
        out_shape=jax.ShapeDtypeStruct((B, T, w), F32),
        compiler_params=_cparams(("parallel", "parallel")),
        name="mem_attn",
    )(qm, mem_k, mem_v)


def _ffn_kernel(*refs, inject, emit_gate):
    if inject:
        h1_ref, om_ref, init_ref, fill_ref = refs[:4]
        rest = refs[4:]
    else:
        h1_ref, om_ref, init_ref = refs[:3]
        fill_ref, rest = None, refs[3:]
    _ffn_body(h1_ref[...], om_ref[...], init_ref, fill_ref, *rest, emit_gate=emit_gate)


def _ffn_body(h1, om, init_ref, fill_ref, wmo_ref, gf_ref, wup_ref, wc_ref, bc_ref, wd_ref, gfin_ref,
              y_ref, fc_ref, gs, *, emit_gate):
    tm = h1.shape[0]
    F = wd_ref.shape[0]
    PAD = SUBLANES

    @pl.when(pl.program_id(1) == 0)
    def _():
        gs[PAD - 2:PAD, :] = init_ref[...]

    h2 = h1 + _dot(om, wmo_ref[...])
    n = _rms(h2, gf_ref[...]).astype(BF16)
    gate = jnp.dot(n, wup_ref[:, :F], preferred_element_type=F32)
    if fill_ref is not None:
        r = lax.broadcasted_iota(jnp.int32, (tm, 1), 0)
        gate = jnp.where((r % SAMPLE_ROWS) >= SAMPLE_ROWS - 2, fill_ref[...], gate)
    gs[PAD:PAD + tm, :] = gate
    conv = (gs[PAD - 2:PAD - 2 + tm, :] * wc_ref[0:1, :] + gs[PAD - 1:PAD - 1 + tm, :] * wc_ref[1:2, :]
            + gate * wc_ref[2:3, :])
    last2 = gs[PAD + tm - 2:PAD + tm, :]
    gs[PAD - 2:PAD, :] = last2
    if emit_gate:
        fc_ref[...] = gate
    else:
        fc_ref[...] = last2
    up = jnp.dot(n, wup_ref[:, F:], preferred_element_type=F32)
    act = _silu(conv + bc_ref[...]) * up
    y = h2 + _dot(act, wd_ref[...])
    y_ref[...] = _rms(y, gfin_ref[...])


def _ffn(h1, om, init, fill, w_mo, norm_ffn, w_up, w_conv, b_conv, w_down, norm_final, tm, emit_gate):
    B, T, d = h1.shape
    F = w_down.shape[0]
    inject = fill is not None
    rt = lambda n: pl.BlockSpec((None, tm, n), lambda b, j: (b, j, 0))
    in_specs = [rt(d), rt(om.shape[-1]), pl.BlockSpec((None, 2, F), lambda b, j: (b, 0, 0))]
    args = [h1, om, init]
    if inject:
        in_specs.append(rt(F))
        args.append(fill)
    in_specs += [_resident(w_mo.shape), _resident((1, d)), _resident(w_up.shape), _resident(w_conv.shape),
                 _resident((1, F)), _resident(w_down.shape), _resident((1, d))]
    args += [w_mo, norm_ffn.reshape(1, d), w_up, w_conv, b_conv.reshape(1, F), w_down, norm_final.reshape(1, d)]
    if emit_gate:
        fc_spec, fc_shape = rt(F), jax.ShapeDtypeStruct((B, T, F), F32)
    else:
        fc_spec = pl.BlockSpec((None, 2, F), lambda b, j: (b, 0, 0))
        fc_shape = jax.ShapeDtypeStruct((B, 2, F), F32)
    return pl.pallas_call(
        functools.partial(_ffn_kernel, inject=inject, emit_gate=emit_gate),
        grid=(B, T // tm),
        in_specs=in_specs,
        out_specs=[rt(d), fc_spec],
        out_shape=[jax.ShapeDtypeStruct((B, T, d), F32), fc_shape],
        scratch_shapes=[pltpu.VMEM((tm + SUBLANES, F), F32)],
        compiler_params=_cparams(("parallel", "arbitrary")),
        name="ffn",
    )(*args)


def _post_kernel(*refs):
    n_mix = 2 * N_OG + 4
    mix_in, (k_ref, v_ref, init_ref), rest = refs[:n_mix], refs[n_mix:n_mix + 3], refs[n_mix + 3:]
    mix_w, ffn_rest = rest[:5], rest[5:]
    h1, qm = _mix_body(*mix_in, *mix_w)
    om, = _mem_attn_body([qm], [k_ref], [v_ref])
    _ffn_body(h1, om, init_ref, None, *ffn_rest, emit_gate=False)


def _post(o_g, l_g, o_b, ga, gb, h, mem_k, mem_v, init, w_a, w_b, w_o, norm_mem_q, w_mem_q, w_mo, norm_ffn, w_up,
          w_conv, b_conv, w_down, norm_final, tm):
    B, T, d = h.shape
    F = w_down.shape[0]
    rt = lambda n: pl.BlockSpec((None, tm, n), lambda b, j: (b, j, 0))
    per_b = lambda a: pl.BlockSpec((None,) + a.shape[1:], lambda b, j: (b, 0, 0))
    weights = [w_a, w_b, w_o, norm_mem_q.reshape(1, d), w_mem_q, w_mo, norm_ffn.reshape(1, d), w_up, w_conv,
               b_conv.reshape(1, F), w_down, norm_final.reshape(1, d)]
    return pl.pallas_call(
        _post_kernel,
        grid=(B, T // tm),
        in_specs=([rt(LANES)] * (2 * N_OG) + [rt(GDN_W), rt(d), rt(d), rt(d), per_b(mem_k), per_b(mem_v), per_b(init)]
                  + [_resident(w.shape) for w in weights]),
        out_specs=[rt(d), pl.BlockSpec((None, 2, F), lambda b, j: (b, 0, 0))],
        out_shape=[jax.ShapeDtypeStruct((B, T, d), F32), jax.ShapeDtypeStruct((B, 2, F), F32)],
        scratch_shapes=[pltpu.VMEM((tm + SUBLANES, F), F32)],
        compiler_params=_cparams(("parallel", "arbitrary")),
        name="post",
    )(*o_g, *l_g, o_b, ga, gb, h, mem_k, mem_v, init, *weights)


def kernel(x_prompt, x_sample, cache_dil0_kv, cache_dil1_kv, cache_dil2_kv, state_delta, state_delta_conv, cache_mem_k, cache_mem_v, state_ffn_conv, mem_prompt, rel_bias, norm_mix, w_in, w_conv_delta, a_log, dt_bias, norm_delta_out, w_branch_a, w_branch_b, w_out, norm_mem_q, norm_mem_kv, w_mem_q, w_mem_kv, w_mem_o, norm_ffn, w_ffn_up, w_ffn_conv, b_ffn_conv, w_ffn_down, norm_final):
    B, S, D = x_prompt.shape
    Bs, Ts, _ = x_sample.shape
    depth = w_in.shape[0]
    assert depth == 1 and Ts <= SAMPLE_ROWS - 2 and Ts >= 3 and S % (16 * TILE_Q) == 0
    F = w_ffn_down.shape[1]
    M = mem_prompt.shape[1]
    l = 0
    w_arr = _arrange_w_in(w_in[l])
    w_a, w_b, w_o = (w.astype(BF16) for w in (w_branch_a[l], w_branch_b[l], w_out[l]))
    w_mq, w_mkv, w_mo = (w.astype(BF16) for w in (w_mem_q[l], w_mem_kv[l], w_mem_o[l]))
    w_up, w_dn = w_ffn_up[l].astype(BF16), w_ffn_down[l].astype(BF16)
    t_cat, t_cur = _prompt_bias_tables(rel_bias)

    xp = x_prompt.reshape(B * S, D)
    q, kv0, kv1, kv2, kt0, kt1, kt2, gq, gq_tail, z, ba, ga, gb = _in_proj(
        xp, norm_mix[l], w_arr, 256, seq=S, conv_buf=jnp.zeros((B, 3, 3 * GDN_W), F32), w_conv=w_conv_delta[l])
    kvs = [kv.reshape(B, S, 2 * DIL_GW) for kv in (kv0, kv1, kv2)]
    q3 = q.reshape(B, S, 3 * DIL_GW)
    o_g, l_g = [], []
    for g in range(3):
        o_sl, l_sl = _dil_prompt(q3, kvs[g], g, t_cat[g], t_cur[g])
        o_g += o_sl
        l_g += l_sl
    o_b, delta_p = _gdn(gq.reshape(B, S, -1), ba.reshape(B, S, LANES), z.reshape(B, S, GDN_W),
                        jnp.zeros((B, GDN_HEADS, GDN_HEAD_DIM, GDN_HEAD_DIM), F32), a_log[l], dt_bias[l],
                        norm_delta_out[l])
    mk_p, mv_p = _mem_kv(mem_prompt.reshape(B * M, D), norm_mem_kv[l], w_mkv, 256)
    mk_p, mv_p = (x.reshape(B, M * MEM_HEADS, MEM_HEAD_DIM) for x in (mk_p, mv_p))
    seq = lambda a: a.reshape(B, S, a.shape[-1])
    y_p, fconv_p = _post([seq(a) for a in o_g], [seq(a) for a in l_g], o_b, seq(ga), seq(gb), x_prompt, mk_p, mv_p,
                         jnp.zeros((B, 2, F), F32), w_a, w_b, w_o, norm_mem_q[l], w_mq, w_mo, norm_ffn[l], w_up,
                         w_ffn_conv[l], b_ffn_conv[l], w_dn, norm_final, 256)
    p_out = ([kt[:, :, S - min(w, S):].reshape(B, 2, DIL_HPG, DIL_HEAD_DIM, min(w, S)).transpose(0, 4, 1, 2, 3)[None]
              for kt, (w, _) in zip((kt0, kt1, kt2), DIL_GROUPS)]
             + [delta_p[None], gq_tail[:, SUBLANES - 3:][None], mk_p.reshape(1, B, M, MEM_HEADS, MEM_HEAD_DIM),
                mv_p.reshape(1, B, M, MEM_HEADS, MEM_HEAD_DIM), fconv_p[None]])

    R = SAMPLE_ROWS
    xs = jnp.pad(x_sample, ((0, 0), (0, R - Ts), (0, 0))).reshape(Bs * R, D)
    q, kv0, kv1, kv2, gq, z, ba, ga, gb = _in_proj(xs, norm_mix[l], w_arr, Bs * R)
    kvn = [kv.reshape(Bs, R, 2 * DIL_GW) for kv in (kv0, kv1, kv2)]
    caches_t = [jnp.transpose(c[l], (0, 2, 3, 4, 1)).reshape(Bs, 2 * DIL_GW, c.shape[2])
                for c in (cache_dil0_kv, cache_dil1_kv, cache_dil2_kv)]
    tabc, tabn = _sample_bias_tables(rel_bias, Ts)
    o_g, l_g, new_caches = _dil_sample(q.reshape(Bs, R, 3 * DIL_GW), kvn, caches_t, tabc, tabn, Ts)
    o_b, delta_s = _gdn_sample(gq.reshape(Bs, R, -1), ba.reshape(Bs, R, LANES), z.reshape(Bs, R, GDN_W),
                               state_delta_conv[l], state_delta[l], w_conv_delta[l], a_log[l], dt_bias[l],
                               norm_delta_out[l], Ts)
    o_b = o_b.reshape(Bs * R, GDN_W).astype(BF16)
    h1, qm = _mix(o_g, l_g, o_b, ga, gb, xs, w_a, w_b, w_o, norm_mem_q[l], w_mq, Bs * R)
    om = _mem_attn(qm.reshape(Bs, R, -1), cache_mem_k[l].reshape(Bs, M * MEM_HEADS, MEM_HEAD_DIM),
                   cache_mem_v[l].reshape(Bs, M * MEM_HEADS, MEM_HEAD_DIM), R, 4 if Bs % 4 == 0 else 1)
    fst = state_ffn_conv[l]
    fill = jnp.concatenate([jnp.zeros((Bs, R - 2, F), F32),
                            jnp.concatenate([fst[1:], jnp.zeros((1, 2, F), F32)], axis=0)], axis=1)
    y_s, gate_s = _ffn(h1.reshape(1, Bs * R, D), om.reshape(1, Bs * R, -1), fst[:1], fill.reshape(1, Bs * R, F),
                       w_mo, norm_ffn[l], w_up, w_ffn_conv[l], b_ffn_conv[l], w_dn, norm_final, Bs * R, True)
    y_s = y_s.reshape(Bs, R, D)[:, :Ts]
    gq3 = gq.reshape(Bs, R, -1)
    s_out = ([nc.reshape(Bs, 2, DIL_HPG, DIL_HEAD_DIM, nc.shape[2]).transpose(0, 4, 1, 2, 3)[None]
              for nc in new_caches]
             + [delta_s[None], gq3[:, Ts - 3:Ts][None], gate_s.reshape(Bs, R, F)[:, Ts - 2:Ts][None]])

    return (y_p.reshape(B, S, D), y_s, *p_out, *s_out)
```

```python
import functools
import math

import jax
import jax.numpy as jnp
import numpy as np
from jax import lax
from jax.experimental import pallas as pl
from jax.experimental.pallas import tpu as pltpu

F32 = jnp.float32
BF16 = jnp.bfloat16

PAST_LEN = 8192
DIL_GROUPS = ((128, 1), (512, 4), (2048, 16))
DIL_HPG = 4
DIL_HEAD_DIM = 64
DIL_GW = DIL_HPG * DIL_HEAD_DIM
DIL_NK = 129
REL_BUCKETS = 32
REL_MAX_DIST = 2048
GDN_HEADS = 8
GDN_HEAD_DIM = 128
GDN_W = GDN_HEADS * GDN_HEAD_DIM
GDN_CHUNK = 64
MEM_HEADS = 4
MEM_HEAD_DIM = 128
EPS = 1e-6
NEG = -1e30

LANES = 128
SUBLANES = 8
TILE_Q = 128
SAMPLE_ROWS = SUBLANES
CONV_PAD = SUBLANES
VMEM_LIMIT = 56 * 1024 * 1024


def _cparams(sem):
    return pltpu.CompilerParams(dimension_semantics=sem, vmem_limit_bytes=VMEM_LIMIT)


def _resident(shape):
    nd = len(shape)
    return pl.BlockSpec(shape, lambda *_: (0,) * nd, pipeline_mode=pl.Buffered(1))


def _rms(x, gain_row):
    return x * lax.rsqrt(jnp.mean(x * x, axis=-1, keepdims=True) + EPS) * gain_row


def _dot(a, b):
    return jnp.dot(a.astype(BF16), b.astype(BF16), preferred_element_type=F32)


def _dot_nt(a, b):
    return lax.dot_general(a.astype(BF16), b.astype(BF16), (((1,), (1,)), ((), ())), preferred_element_type=F32)


def _dot_tn(a, b):
    return lax.dot_general(a.astype(BF16), b.astype(BF16), (((0,), (0,)), ((), ())), preferred_element_type=F32)


def _split3(x):
    hi = x.astype(BF16)
    r1 = x - hi.astype(F32)
    mid = r1.astype(BF16)
    lo = (r1 - mid.astype(F32)).astype(BF16)
    return hi, mid, lo


def _dot_hp(a, b):
    ah, am, al = _split3(a)
    bh, bm, bl = _split3(b)
    d = functools.partial(jnp.dot, preferred_element_type=F32)
    return d(ah, bh) + (d(ah, bm) + d(am, bh)) + (d(am, bm) + d(ah, bl) + d(al, bh))


def _sigmoid(x):
    return 1.0 / (1.0 + jnp.exp(-x))


def _silu(x):
    return x * _sigmoid(x)


def _softplus(x):
    return jnp.maximum(x, 0.0) + jnp.log(1.0 + jnp.exp(-jnp.abs(x)))


IN_SEGS = (("q", 3 * DIL_GW), ("kv0", 2 * DIL_GW), ("kv1", 2 * DIL_GW), ("kv2", 2 * DIL_GW),
           ("gq", 3 * GDN_W), ("z", GDN_W), ("ba", LANES), ("ga", 1024), ("gb", 1024))


OFF_Q, OFF_K, OFF_V = 0, 3 * DIL_GW, 6 * DIL_GW
OFF_GQ = 9 * DIL_GW
OFF_Z = OFF_GQ + 3 * GDN_W
OFF_BA = OFF_Z + GDN_W
OFF_GATES = OFF_BA + 2 * GDN_HEADS


def _arrange_w_in(w_in):
    w = w_in.astype(BF16)
    return w, w[:, OFF_GATES:]


GQ_SLABS = 3 * GDN_W // LANES


def _l2n(x):
    return x * lax.rsqrt(jnp.sum(x * x, axis=-1, keepdims=True) + EPS)


def _in_proj_kernel(x_ref, g_ref, w_ref, wg_ref, *rest, seq_tiles):
    if seq_tiles:
        (cb_ref, wc_ref, q_ref, kv0_ref, kv1_ref, kv2_ref, kt0_ref, kt1_ref, kt2_ref, gq_ref, tail_ref, z_ref,
         ba_ref, ga_ref, gb_ref, cs) = rest
        kt_refs = (kt0_ref, kt1_ref, kt2_ref)
    else:
        q_ref, kv0_ref, kv1_ref, kv2_ref, gq_ref, z_ref, ba_ref, ga_ref, gb_ref = rest
    tm = x_ref.shape[0]
    u = _rms(x_ref[...], g_ref[...]).astype(BF16)

    def seg(ref, off, n):
        return jnp.dot(u, ref[:, off:off + n], preferred_element_type=F32)

    def q_seg():
        q_ref[...] = seg(w_ref, OFF_Q, 3 * DIL_GW) * (DIL_HEAD_DIM ** -0.5)

    def kv_seg(g, part):
        kv_ref = (kv0_ref, kv1_ref, kv2_ref)[g]
        r = seg(w_ref, (OFF_K, OFF_V)[part] + g * DIL_GW, DIL_GW)
        kv_ref[:, part * DIL_GW:(part + 1) * DIL_GW] = r
        if seq_tiles:
            kt_refs[g][part * DIL_GW:(part + 1) * DIL_GW, :] = r.T

    def ba_seg():
        ba_ref[...] = seg(w_ref, OFF_BA, LANES)

    def gate_seg(ref, off):
        ref[...] = seg(wg_ref, off, 1024)

    others = ([(q_seg, 3 * DIL_GW)] + [(functools.partial(kv_seg, g, p), DIL_GW) for g in range(3) for p in range(2)]
              + [(functools.partial(gate_seg, ga_ref, 0), 1024), (functools.partial(gate_seg, gb_ref, 1024), 1024),
                 (ba_seg, LANES)])
    if not seq_tiles:
        for f, _ in others:
            f()
        for c in range(3):
            gq_ref[:, c * GDN_W:(c + 1) * GDN_W] = seg(w_ref, OFF_GQ + c * GDN_W, GDN_W)
        z_ref[...] = seg(w_ref, OFF_Z, GDN_W)
        return
    i = pl.program_id(0)
    first = (i % seq_tiles) == 0

    @pl.when(first)
    def _():
        for s in range(GQ_SLABS):
            cs[s, CONV_PAD - 3:CONV_PAD, :] = cb_ref[:, s * LANES:(s + 1) * LANES]

    @pl.when(jnp.logical_not(first))
    def _():
        for s in range(GQ_SLABS):
            cs[s, CONV_PAD - 3:CONV_PAD, :] = cs[s, CONV_PAD + tm - 3:CONV_PAD + tm, :]

    for c in range(3):
        r = seg(w_ref, OFF_GQ + c * GDN_W, GDN_W)
        for hh in range(GDN_HEADS):
            cs[c * GDN_HEADS + hh, CONV_PAD:CONV_PAD + tm, :] = r[:, hh * LANES:(hh + 1) * LANES]
    base = CONV_PAD - 3 + jnp.minimum(i, 0)

    def conv_slab(s):
        ls = slice(s * LANES, (s + 1) * LANES)
        y = cs[s, pl.ds(base, tm), :] * wc_ref[0:1, ls]
        for j in range(1, 4):
            y = y + cs[s, pl.ds(base + j, tm), :] * wc_ref[j:j + 1, ls]
        y = _silu(y)
        if s < GDN_HEADS:
            y = _l2n(y) * (GDN_HEAD_DIM ** -0.5)
        elif s < 2 * GDN_HEADS:
            y = _l2n(y)
        gq_ref[:, ls] = y
        tail_ref[:, ls] = cs[s, tm:tm + SUBLANES, :]

    def z_seg():
        z_ref[...] = _silu(seg(w_ref, OFF_Z, GDN_W))

    others.insert(len(others) - 1, (z_seg, GDN_W))
    total = sum(n for _, n in others)
    done, cols = 0, 0
    for f, n in others:
        f()
        cols += n
        upto = min(GQ_SLABS, (cols * GQ_SLABS + total - 1) // total)
        for s in range(done, upto):
            conv_slab(s)
        done = upto
    assert done == GQ_SLABS


def _in_proj(x2d, gain, w_arr, tm, seq=None, conv_buf=None, w_conv=None):
    rows, d = x2d.shape
    fused = seq is not None
    nt = seq // tm if fused else None
    names = [n for n, _ in IN_SEGS]
    widths = dict(IN_SEGS)
    row_spec = lambda n: pl.BlockSpec((tm, n), lambda i: (i, 0))
    out_specs, out_shape = [], []
    for n in names:
        out_specs.append(row_spec(widths[n]))
        out_shape.append(jax.ShapeDtypeStruct((rows, widths[n]), F32))
        if n == "kv2" and fused:
            for _ in range(3):
                out_specs.append(pl.BlockSpec((None, 2 * DIL_GW, tm), lambda i: (i // nt, 0, i % nt)))
                out_shape.append(jax.ShapeDtypeStruct((rows // seq, 2 * DIL_GW, seq), F32))
        if n == "gq" and fused:
            out_specs.append(pl.BlockSpec((None, SUBLANES, 3 * GDN_W), lambda i: (i // nt, 0, 0)))
            out_shape.append(jax.ShapeDtypeStruct((rows // seq, SUBLANES, 3 * GDN_W), F32))
    in_specs = [pl.BlockSpec((tm, d), lambda i: (i, 0)), _resident((1, d)), _resident(w_arr[0].shape),
                _resident(w_arr[1].shape)]
    args = [x2d, gain.reshape(1, d), *w_arr]
    if fused:
        in_specs += [pl.BlockSpec((None, 3, 3 * GDN_W), lambda i: (i // nt, 0, 0)), _resident(w_conv.shape)]
        args += [conv_buf, w_conv]
    return pl.pallas_call(
        functools.partial(_in_proj_kernel, seq_tiles=nt),
        grid=(rows // tm,),
        in_specs=in_specs,
        out_specs=out_specs,
        out_shape=out_shape,
        scratch_shapes=[pltpu.VMEM((GQ_SLABS, CONV_PAD + tm, LANES), F32)] if fused else [],
        compiler_params=_cparams(("arbitrary",)),
        name="in_proj",
    )(*args)


def _rel_bucket(dist):
    exact = REL_BUCKETS // 2
    d = jnp.maximum(dist, 1).astype(F32)
    large = exact + (jnp.log(d / exact) / math.log(REL_MAX_DIST / exact) * (REL_BUCKETS - exact)).astype(jnp.int32)
    return jnp.where(dist < exact, dist, jnp.minimum(large, REL_BUCKETS - 1))


def _group_bias(rel_bias, g):
    dil = DIL_GROUPS[g][1]
    dist = dil * jnp.arange(DIL_NK, dtype=jnp.int32)
    tab = rel_bias[_rel_bucket(dist)]
    return tab[:, g * DIL_HPG:(g + 1) * DIL_HPG].T.astype(F32)


def _toeplitz(v, n, width):
    h, L = v.shape
    return jnp.tile(v, (1, n))[:, :n * (L - 1)].reshape(h, n, L - 1)[:, :, :width]


def _prompt_bias_tables(rel_bias):
    cat, cur = [], []
    for g in range(3):
        bg = _group_bias(rel_bias, g)
        v = jnp.concatenate([bg[:, ::-1], jnp.full((DIL_HPG, 3 * TILE_Q - DIL_NK), NEG, F32)], axis=1)
        t = _toeplitz(v, TILE_Q, 2 * TILE_Q)
        cat.append(t)
        cur.append(t[:, :, TILE_Q:])
    return cat, cur


DIL_TIF = 2
DIL_SLABS = DIL_GW // LANES


def _dil_prompt_kernel(q0_ref, q1_ref, k0_ref, k1_ref, v0_ref, v1_ref, tcat_ref, tcur_ref,
                       o0_ref, o1_ref, l0_ref, l1_ref, *, dil):
    S = q0_ref.shape[0]
    nb = S // dil // TILE_Q
    q_refs, k_refs, v_refs = (q0_ref, q1_ref), (k0_ref, k1_ref), (v0_ref, v1_ref)
    o_refs, l_refs = (o0_ref, o1_ref), (l0_ref, l1_ref)
    even = lax.broadcasted_iota(jnp.int32, (TILE_Q, LANES), 1) < DIL_HEAD_DIM

    def rows(r, t):
        start = r + dil * TILE_Q * t
        return pl.ds(start, TILE_Q, stride=dil) if dil > 1 else pl.ds(start, TILE_Q)

    tiles = [(r, t) for r in range(dil) for t in range(nb)]
    for i0 in range(0, len(tiles), DIL_TIF):
        grp = tiles[i0:i0 + DIL_TIF]
        qm, kc, vc = {}, {}, {}
        for ti, (r, t) in enumerate(grp):
            for sl in range(DIL_SLABS):
                qf = q_refs[sl][rows(r, t), :]
                qm[ti, 2 * sl] = jnp.where(even, qf, 0.0).astype(BF16)
                qm[ti, 2 * sl + 1] = jnp.where(even, 0.0, qf).astype(BF16)
                kc[ti, sl] = k_refs[sl][rows(r, t), :].astype(BF16)
                vc[ti, sl] = v_refs[sl][rows(r, t), :].astype(BF16)
                if t > 0:
                    kc[ti, sl] = jnp.concatenate([k_refs[sl][rows(r, t - 1), :].astype(BF16), kc[ti, sl]], axis=0)
                    vc[ti, sl] = jnp.concatenate([v_refs[sl][rows(r, t - 1), :].astype(BF16), vc[ti, sl]], axis=0)
        units = [(ti, h) for ti in range(len(grp)) for h in range(DIL_HPG)]
        s = [_dot_nt(qm[ti, h], kc[ti, h // 2]) + (tcat_ref[h] if grp[ti][1] > 0 else tcur_ref[h]) for ti, h in units]
        m = [jnp.max(x, axis=-1, keepdims=True) for x in s]
        p = [jnp.exp(x - mx) for x, mx in zip(s, m)]
        l = [jnp.sum(x, axis=-1, keepdims=True) for x in p]
        pv = [jnp.dot(p[u].astype(BF16), vc[ti, h // 2], preferred_element_type=F32) for u, (ti, h) in enumerate(units)]
        o = [pv[u] / l[u] for u in range(len(units))]
        lse = [m[u] + jnp.log(l[u]) for u in range(len(units))]
        for ti, (r, t) in enumerate(grp):
            for sl in range(DIL_SLABS):
                ue, uo = ti * DIL_HPG + 2 * sl, ti * DIL_HPG + 2 * sl + 1
                o_refs[sl][rows(r, t), :] = jnp.where(even, o[ue], o[uo])
                l_refs[sl][rows(r, t), :] = jnp.where(even, lse[ue], lse[uo])


def _dil_prompt(q, kv, g, t_cat, t_cur):
    B, S, _ = q.shape
    dil = DIL_GROUPS[g][1]
    slab = lambda c: pl.BlockSpec((None, S, LANES), lambda b: (b, 0, c))
    nq, nk = g * DIL_SLABS, 0
    outs = pl.pallas_call(
        functools.partial(_dil_prompt_kernel, dil=dil),
        grid=(B,),
        in_specs=[slab(nq), slab(nq + 1), slab(nk), slab(nk + 1), slab(nk + 2), slab(nk + 3),
                  _resident(t_cat.shape), _resident(t_cur.shape)],
        out_specs=[slab(0)] * 4,
        out_shape=[jax.ShapeDtypeStruct((B, S, LANES), F32)] * 4,
        compiler_params=_cparams(("parallel",)),
        name=f"dil_prompt_g{g}",
    )(q, q, kv, kv, kv, kv, t_cat, t_cur)
    o0, o1, l0, l1 = (x.reshape(B * S, LANES) for x in outs)
    return [o0, o1], [l0, l1]


def _sample_bias_tables(rel_bias, t_real):
    R = SAMPLE_ROWS
    tabc, tabn = [], []
    t_i = np.arange(R)[:, None]
    u_i = np.arange(R)[None, :]
    for g, (w, dil) in enumerate(DIL_GROUPS):
        bg = _group_bias(rel_bias, g)
        base = bg[:, ::-1][:, :TILE_Q]
        t0 = jnp.concatenate([base[:, :, None], jnp.full((DIL_HPG, TILE_Q, dil - 1), NEG, F32)], axis=2)
        t0 = t0.reshape(DIL_HPG, w)
        rows = [jnp.concatenate([jnp.full((DIL_HPG, t), NEG, F32), t0[:, :w - t]], axis=1) for t in range(t_real)]
        rows += [jnp.zeros((DIL_HPG, w), F32)] * (R - t_real)
        tabc.append(jnp.stack(rows, axis=1).reshape(DIL_HPG * R, w))
        tn = jnp.full((DIL_HPG, R, R), NEG, F32)
        for j in range(-(-t_real // dil)):
            hit = (t_i - u_i == j * dil) & (t_i < t_real)
            tn = jnp.where(hit[None], bg[:, j][:, None, None], tn)
        tn = jnp.where((t_i >= t_real)[None], 0.0, tn)
        tabn.append(tn.reshape(DIL_HPG * R, R))
    return tabc, jnp.stack(tabn)


def _dil_sample_kernel(q_ref, n0_ref, n1_ref, n2_ref, c0_ref, c1_ref, c2_ref, tc0_ref, tc1_ref, tc2_ref, tn_ref,
                       o_ref, l_ref, oc0_ref, oc1_ref, oc2_ref, *, t_real):
    R = SAMPLE_ROWS
    rows = lax.broadcasted_iota(jnp.int32, (DIL_HPG * R, DIL_GW), 0)
    lanes = lax.broadcasted_iota(jnp.int32, (DIL_HPG * R, DIL_GW), 1)
    head_mask = (lanes // DIL_HEAD_DIM) == (rows // R)
    lane_f = lax.broadcasted_iota(jnp.int32, (2 * DIL_GW, LANES), 1)
    keep = lane_f < LANES - t_real
    sel_l = lax.broadcasted_iota(jnp.int32, (LANES, R), 0)
    sel_u = lax.broadcasted_iota(jnp.int32, (LANES, R), 1)
    selT = ((sel_l == sel_u + LANES - t_real) & (sel_u < t_real)).astype(BF16)
    groups = ((n0_ref, c0_ref, tc0_ref, oc0_ref), (n1_ref, c1_ref, tc1_ref, oc1_ref), (n2_ref, c2_ref, tc2_ref, oc2_ref))

    def fold_heads(x):
        x = jnp.where(head_mask, x, 0.0)
        return x[0:R] + x[R:2 * R] + x[2 * R:3 * R] + x[3 * R:4 * R]

    G3 = range(3)
    kvn = [groups[g][0][...] for g in G3]
    q_bd = [jnp.where(head_mask, jnp.concatenate([q_ref[:, g * DIL_GW:(g + 1) * DIL_GW]] * DIL_HPG, axis=0), 0.0)
            for g in G3]
    s_c = [_dot(q_bd[g], groups[g][1][:DIL_GW, :]) + groups[g][2][...] for g in G3]
    s_n = [_dot_nt(q_bd[g], kvn[g][:, :DIL_GW]) + tn_ref[g] for g in G3]
    m = [jnp.maximum(jnp.max(s_c[g], axis=-1, keepdims=True), jnp.max(s_n[g], axis=-1, keepdims=True)) for g in G3]
    p_c = [jnp.exp(s_c[g] - m[g]) for g in G3]
    p_n = [jnp.exp(s_n[g] - m[g]) for g in G3]
    l = [jnp.sum(p_c[g], axis=-1, keepdims=True) + jnp.sum(p_n[g], axis=-1, keepdims=True) for g in G3]
    acc = [(_dot_nt(p_c[g], groups[g][1][DIL_GW:, :]) + _dot(p_n[g], kvn[g][:, DIL_GW:])) / l[g] for g in G3]
    for g in G3:
        o_ref[:, g * DIL_GW:(g + 1) * DIL_GW] = fold_heads(acc[g])
        l_ref[:, g * DIL_GW:(g + 1) * DIL_GW] = fold_heads(jnp.broadcast_to(m[g] + jnp.log(l[g]), acc[g].shape))
    for g, (n_ref, c_ref, tc_ref, oc_ref) in enumerate(groups):
        W = c_ref.shape[1]
        hi, mid, lo = _split3(kvn[g])
        tail = (jnp.dot(selT, hi, preferred_element_type=F32) + jnp.dot(selT, mid, preferred_element_type=F32)
                + jnp.dot(selT, lo, preferred_element_type=F32)).T
        nxt = pltpu.roll(c_ref[:, 0:LANES], LANES - t_real, axis=1)
        for c in range(W // LANES):
            cur = nxt
            nxt = (pltpu.roll(c_ref[:, (c + 1) * LANES:(c + 2) * LANES], LANES - t_real, axis=1)
                   if (c + 1) * LANES < W else tail)
            oc_ref[:, c * LANES:(c + 1) * LANES] = jnp.where(keep, cur, nxt)


def _dil_sample(q, kvn, caches_t, tabc, tabn, t_real):
    B = q.shape[0]
    row = lambda n: pl.BlockSpec((None, SAMPLE_ROWS, n), lambda b: (b, 0, 0))
    cspecs = [pl.BlockSpec((None,) + c.shape[1:], lambda b: (b, 0, 0)) for c in caches_t]
    for g, (w, dil) in enumerate(DIL_GROUPS):
        assert caches_t[g].shape == (B, 2 * DIL_GW, w) and w // dil == TILE_Q
    out_spec = row(3 * DIL_GW)
    o, lse, *new_caches = pl.pallas_call(
        functools.partial(_dil_sample_kernel, t_real=t_real),
        grid=(B,),
        in_specs=([row(3 * DIL_GW)] + [row(2 * DIL_GW)] * 3 + cspecs + [_resident(t.shape) for t in tabc]
                  + [_resident(tabn.shape)]),
        out_specs=[out_spec, out_spec] + cspecs,
        out_shape=([jax.ShapeDtypeStruct((B, SAMPLE_ROWS, 3 * DIL_GW), F32)] * 2
                   + [jax.ShapeDtypeStruct(c.shape, F32) for c in caches_t]),
        compiler_params=_cparams(("parallel",)),
        name="dil_sample",
    )(q, *kvn, *caches_t, *tabc, tabn)
    o = o.reshape(B * SAMPLE_ROWS, 3 * DIL_GW)
    lse = lse.reshape(B * SAMPLE_ROWS, 3 * DIL_GW)
    n = 3 * DIL_SLABS
    return ([o[:, i * LANES:(i + 1) * LANES] for i in range(n)],
            [lse[:, i * LANES:(i + 1) * LANES] for i in range(n)], new_caches)


GDN_HPS = GDN_HEADS
GDN_SW = GDN_HPS * GDN_HEAD_DIM
GDN_TPI = 4
GDN_ROWS = 1024


def _gdn_kernel(q_ref, k_ref, v_ref, ba_ref, alog_ref, dtb_ref, z_ref, gn_ref, tri_ref, s0_ref, o_ref, s_ref, carry_s,
                *, tpi):
    T = q_ref.shape[0]
    C = GDN_CHUNK
    D = GDN_HEAD_DIM
    ri = lax.broadcasted_iota(jnp.int32, (TILE_Q, TILE_Q), 0)
    ci = lax.broadcasted_iota(jnp.int32, (TILE_Q, TILE_Q), 1)
    same = (ri // C) == (ci // C)
    incl = same & (ri >= ci)
    strict = same & (ri > ci)
    eye = (ri == ci).astype(F32)
    lane, row = ci, ri
    gain = gn_ref[...]
    zpad = jnp.zeros((C, D), F32)
    HH = range(GDN_HPS)
    hsl = [slice(hh * D, (hh + 1) * D) for hh in HH]
    RW = tpi * TILE_Q
    UU = [(tt, hh) for tt in range(tpi) for hh in HH]
    UI = range(len(UU))

    def tile(i, S):
        r0 = pl.multiple_of(i * RW, RW)
        rt = [r0 + tt * TILE_Q for tt in range(tpi)]
        q = [q_ref[pl.ds(rt[tt], TILE_Q), hsl[hh]] for tt, hh in UU]
        k = [k_ref[pl.ds(rt[tt], TILE_Q), hsl[hh]] for tt, hh in UU]
        v = [v_ref[pl.ds(rt[tt], TILE_Q), hsl[hh]] for tt, hh in UU]
        tri = tri_ref[...]
        beta_all, G_all = [], []
        for tt in range(tpi):
            ba = ba_ref[pl.ds(rt[tt], TILE_Q), :]
            beta_all.append(_sigmoid(ba))
            gh, gm, gl = _split3(-jnp.exp(alog_ref[...]) * _softplus(ba + dtb_ref[...]))
            G_all.append(jnp.dot(tri, gh, preferred_element_type=F32) + jnp.dot(tri, gm, preferred_element_type=F32)
                         + jnp.dot(tri, gl, preferred_element_type=F32))
        head = [hh for _, hh in UU]
        bc = [jnp.sum(jnp.where(lane == head[u], beta_all[UU[u][0]], 0.0), axis=-1, keepdims=True) for u in UI]
        Gc = [jnp.broadcast_to(jnp.sum(jnp.where(lane == head[u] + GDN_HEADS, G_all[UU[u][0]], 0.0), axis=-1,
                                       keepdims=True), (TILE_Q, TILE_Q)) for u in UI]
        gamma = [jnp.exp(jnp.where(incl, Gc[u] - Gc[u].T, NEG)) for u in UI]
        kk = [_dot_nt(k[u], k[u]) for u in UI]
        qk = [_dot_nt(q[u], k[u]) for u in UI]
        X = [jnp.where(strict, bc[u] * kk[u] * gamma[u], 0.0) for u in UI]
        P = [eye - X[u] for u in UI]
        for _ in range(int(math.log2(C)) - 1):
            X = [_dot(X[u], X[u]) for u in UI]
            P = [P[u] + _dot(P[u], X[u]) for u in UI]
        eG = [jnp.exp(Gc[u]) for u in UI]
        rhs = [jnp.concatenate([v[u] * bc[u], k[u] * (bc[u] * eG[u])], axis=-1) for u in UI]
        sol = [rhs[u] + _dot(P[u] - eye, rhs[u]) for u in UI]
        a_in = [qk[u] * gamma[u] for u in UI]
        q_dec = [q[u] * eG[u] for u in UI]
        kdT = [(k[u] * jnp.exp(jnp.where(row < C, Gc[u][C - 1:C, :], Gc[u][2 * C - 1:2 * C, :]) - Gc[u])).T
               for u in UI]
        S = list(S)
        for tt in range(tpi):
            us = [tt * GDN_HPS + hh for hh in HH]
            oq, vn = [[] for _ in HH], [[] for _ in HH]
            for c in range(TILE_Q // C):
                cs = slice(c * C, (c + 1) * C)
                r = [_dot(jnp.concatenate([sol[us[hh]][cs, D:], q_dec[us[hh]][cs]], axis=0), S[hh]) for hh in HH]
                for hh in HH:
                    oq[hh].append(r[hh][C:])
                    vn[hh].append(sol[us[hh]][cs, :D] - r[hh][:C])
                vpad = [jnp.concatenate([vn[hh][c], zpad] if c == 0 else [zpad, vn[hh][c]], axis=0) for hh in HH]
                S = [S[hh] * jnp.exp(Gc[us[hh]][(c + 1) * C - 1:(c + 1) * C, :]) + _dot(kdT[us[hh]], vpad[hh])
                     for hh in HH]
            o = [jnp.concatenate(oq[hh], axis=0) + _dot(a_in[us[hh]], jnp.concatenate(vn[hh], axis=0)) for hh in HH]
            outs = [_rms(o[hh], gain) * z_ref[pl.ds(rt[tt], TILE_Q), hsl[hh]] for hh in HH]
            o_ref[pl.ds(rt[tt], TILE_Q), :] = jnp.concatenate(outs, axis=-1).astype(o_ref.dtype)
        return tuple(S)

    @pl.when(pl.program_id(1) == 0)
    def _():
        carry_s[...] = s0_ref[...]

    S = lax.fori_loop(0, T // RW, tile, tuple(carry_s[hh] for hh in HH))
    for hh in HH:
        carry_s[hh] = S[hh]
        s_ref[hh] = S[hh]


def _gdn(gq, ba, z, s0, a_log, dt_bias, norm_out):
    B, T, _ = gq.shape
    H = GDN_HEADS
    pad16 = lambda x: jnp.concatenate([jnp.zeros((H,), F32), x.astype(F32), jnp.zeros((LANES - 2 * H,), F32)])
    assert GDN_HPS == H
    tb = min(GDN_ROWS, T)
    tpi = GDN_TPI if (tb // TILE_Q) % GDN_TPI == 0 else 1
    col = lambda off: pl.BlockSpec((None, tb, GDN_SW), lambda b, j: (b, j, off))
    sblk = pl.BlockSpec((None, H, GDN_HEAD_DIM, GDN_HEAD_DIM), lambda b, j: (b, 0, 0, 0))
    r = np.arange(TILE_Q)
    tri = jnp.asarray((r[:, None] >= r[None, :]) & (r[:, None] // GDN_CHUNK == r[None, :] // GDN_CHUNK), BF16)
    o, s_new = pl.pallas_call(
        functools.partial(_gdn_kernel, tpi=tpi),
        grid=(B, T // tb),
        in_specs=[col(0), col(1), col(2),
                  pl.BlockSpec((None, tb, LANES), lambda b, j: (b, j, 0)),
                  _resident((1, LANES)), _resident((1, LANES)),
                  col(0), _resident((1, LANES)), _resident((TILE_Q, TILE_Q)), sblk],
        out_specs=[col(0), sblk],
        out_shape=[jax.ShapeDtypeStruct((B, T, GDN_W), BF16),
                   jax.ShapeDtypeStruct((B, H, GDN_HEAD_DIM, GDN_HEAD_DIM), F32)],
        scratch_shapes=[pltpu.VMEM((H, GDN_HEAD_DIM, GDN_HEAD_DIM), F32)],
        compiler_params=_cparams(("parallel", "arbitrary")),
        name="gdn",
    )(gq, gq, gq, ba, pad16(a_log).reshape(1, LANES), pad16(dt_bias).reshape(1, LANES), z,
      norm_out.reshape(1, LANES), tri, s0)
    return o, s_new


def _gdn_sample_kernel(x_ref, b_ref, w_ref, ba_ref, alog_ref, dtb_ref, z_ref, gn_ref, s0_ref, o_ref, s_ref, xs,
                       *, t_real):
    R, D, H = SAMPLE_ROWS, GDN_HEAD_DIM, GDN_HEADS
    nb = x_ref.shape[0]
    ri = lax.broadcasted_iota(jnp.int32, (R, R), 0)
    ci = lax.broadcasted_iota(jnp.int32, (R, R), 1)
    incl, strict = ri >= ci, ri > ci
    eye = (ri == ci).astype(F32)
    tri = incl.astype(BF16)
    er = lax.broadcasted_iota(jnp.int32, (LANES, LANES), 0)
    ec = lax.broadcasted_iota(jnp.int32, (LANES, LANES), 1)
    eye_l = (er == ec).astype(BF16)
    live = lax.broadcasted_iota(jnp.int32, (R, LANES), 0) < t_real
    d32 = functools.partial(jnp.dot, preferred_element_type=F32)
    nt = lambda a, b: lax.dot_general(a, b, (((1,), (1,)), ((), ())), preferred_element_type=F32)
    gain = gn_ref[...]
    y, beta_all, G_all, G_allT = [], [], [], []
    for s in range(nb):
        xs[s, CONV_PAD - 3:CONV_PAD, :] = b_ref[s]
        xs[s, CONV_PAD:CONV_PAD + R, :] = x_ref[s]
        ys = xs[s, CONV_PAD - 3:CONV_PAD - 3 + R, :] * w_ref[0:1, :]
        for j in range(1, 4):
            ys = ys + xs[s, CONV_PAD - 3 + j:CONV_PAD - 3 + j + R, :] * w_ref[j:j + 1, :]
        y.append(_silu(ys))
        ba = ba_ref[s]
        beta_all.append(jnp.where(live, _sigmoid(ba), 0.0))
        gh, gm, gl = _split3(jnp.where(live, -jnp.exp(alog_ref[...]) * _softplus(ba + dtb_ref[...]), 0.0))
        G = d32(tri, gh) + d32(tri, gm) + d32(tri, gl)
        th, tm, tl = _split3(G)
        G_all.append(G)
        G_allT.append(nt(eye_l, th) + nt(eye_l, tm) + nt(eye_l, tl))
    US = [(s, h) for s in range(nb) for h in range(H)]
    UI = range(len(US))

    def l2n(x):
        return x * lax.rsqrt(jnp.sum(x * x, axis=-1, keepdims=True) + EPS)

    q = [l2n(y[s][:, h * D:(h + 1) * D]) * (D ** -0.5) for s, h in US]
    k = [l2n(y[s][:, GDN_W + h * D:GDN_W + (h + 1) * D]) for s, h in US]
    v = [y[s][:, 2 * GDN_W + h * D:2 * GDN_W + (h + 1) * D] for s, h in US]
    bc = [beta_all[s][:, h:h + 1] for s, h in US]
    Gc = [G_all[s][:, H + h:H + h + 1] for s, h in US]
    gamma = [jnp.exp(jnp.where(incl, Gc[u] - G_allT[s][H + h:H + h + 1, :], NEG)) for u, (s, h) in enumerate(US)]
    kk = [_dot_nt(k[u], k[u]) for u in UI]
    qk = [_dot_nt(q[u], k[u]) for u in UI]
    X = [jnp.where(strict, bc[u] * kk[u] * gamma[u], 0.0) for u in UI]
    P = [eye - X[u] for u in UI]
    for _ in range(int(math.log2(R)) - 1):
        X = [_dot(X[u], X[u]) for u in UI]
        P = [P[u] + _dot(P[u], X[u]) for u in UI]
    eG = [jnp.exp(Gc[u]) for u in UI]
    rhs = [jnp.concatenate([v[u] * bc[u], k[u] * (bc[u] * eG[u])], axis=-1) for u in UI]
    sol = [rhs[u] + _dot(P[u] - eye, rhs[u]) for u in UI]
    S = [s0_ref[s, h] for s, h in US]
    r = [_dot(jnp.concatenate([sol[u][:, D:], q[u] * eG[u]], axis=0), S[u]) for u in UI]
    v_new = [sol[u][:, :D] - r[u][:R] for u in UI]
    o = [r[u][R:] + _dot(qk[u] * gamma[u], v_new[u]) for u in UI]
    kdT = [nt(eye_l, (k[u] * jnp.exp(Gc[u][R - 1:R, :] - Gc[u])).astype(BF16)) for u in UI]
    for u, (s, h) in enumerate(US):
        s_ref[s, h] = S[u] * jnp.exp(Gc[u][R - 1:R, :]) + _dot(kdT[u], v_new[u])
    for s in range(nb):
        o_ref[s] = jnp.concatenate([_rms(o[s * H + h], gain) * _silu(z_ref[s, :, h * D:(h + 1) * D])
                                    for h in range(H)], axis=-1)


GDN_SAMPLE_SEQS = 4


def _gdn_sample(gq, ba, z, conv_buf, s0, w_conv, a_log, dt_bias, norm_out, t_real):
    B, R, _ = gq.shape
    H = GDN_HEADS
    pad16 = lambda x: jnp.concatenate([jnp.zeros((H,), F32), x.astype(F32), jnp.zeros((LANES - 2 * H,), F32)])
    nb = GDN_SAMPLE_SEQS if B % GDN_SAMPLE_SEQS == 0 else 1
    blk = lambda *s: pl.BlockSpec((nb,) + s, lambda b: (b,) + (0,) * len(s))
    return pl.pallas_call(
        functools.partial(_gdn_sample_kernel, t_real=t_real),
        grid=(B // nb,),
        in_specs=[blk(R, 3 * GDN_W), blk(3, 3 * GDN_W), _resident(w_conv.shape), blk(R, LANES),
                  _resident((1, LANES)), _resident((1, LANES)), blk(R, GDN_W), _resident((1, LANES)),
                  blk(H, GDN_HEAD_DIM, GDN_HEAD_DIM)],
        out_specs=[blk(R, GDN_W), blk(H, GDN_HEAD_DIM, GDN_HEAD_DIM)],
        out_shape=[jax.ShapeDtypeStruct((B, R, GDN_W), F32),
                   jax.ShapeDtypeStruct((B, H, GDN_HEAD_DIM, GDN_HEAD_DIM), F32)],
        scratch_shapes=[pltpu.VMEM((nb, CONV_PAD + R, 3 * GDN_W), F32)],
        compiler_params=_cparams(("parallel",)),
        name="gdn_sample",
    )(gq, conv_buf, w_conv, ba, pad16(a_log).reshape(1, LANES), pad16(dt_bias).reshape(1, LANES), z,
      norm_out.reshape(1, LANES), s0)


N_OG = 3 * DIL_SLABS


def _mix_kernel(*refs):
    h1, qm = _mix_body(*refs[:2 * N_OG + 9])
    h1_ref, qm_ref = refs[2 * N_OG + 9:]
    h1_ref[...] = h1
    qm_ref[...] = qm


def _mix_body(*refs):
    o_refs, l_refs = refs[:N_OG], refs[N_OG:2 * N_OG]
    ob_ref, ga_ref, gb_ref, h_ref, wa_ref, wb_ref, wo_ref, gq_ref, wq_ref = refs[2 * N_OG:]
    slabs = []
    for sl in range(DIL_SLABS):
        l0, l1, l2 = (l_refs[g * DIL_SLABS + sl][...] for g in range(3))
        o0, o1, o2 = (o_refs[g * DIL_SLABS + sl][...] for g in range(3))
        mx = jnp.maximum(jnp.maximum(l0, l1), l2)
        e0, e1, e2 = jnp.exp(l0 - mx), jnp.exp(l1 - mx), jnp.exp(l2 - mx)
        slabs.append((e0 * o0 + e1 * o1 + e2 * o2) / (e0 + e1 + e2))
    o_a = jnp.concatenate(slabs, axis=-1)
    a = _dot(o_a, wa_ref[...])
    b = jnp.dot(ob_ref[...], wb_ref[...], preferred_element_type=F32)
    merged = _sigmoid(ga_ref[...]) * a + _sigmoid(gb_ref[...]) * b
    h1 = h_ref[...] + _dot(merged, wo_ref[...])
    return h1, _dot(_rms(h1, gq_ref[...]), wq_ref[...]).astype(BF16)


def _mix(o_g, l_g, o_b, ga, gb, h, w_a, w_b, w_o, norm_mem_q, w_mem_q, tm):
    rows, d = h.shape
    rt = lambda n: pl.BlockSpec((tm, n), lambda i: (i, 0))
    assert len(o_g) == len(l_g) == 3 * DIL_SLABS
    return pl.pallas_call(
        _mix_kernel,
        grid=(rows // tm,),
        in_specs=[rt(LANES)] * (6 * DIL_SLABS) + [rt(GDN_W), rt(d), rt(d), rt(d),
                                     _resident(w_a.shape), _resident(w_b.shape), _resident(w_o.shape),
                                     _resident((1, d)), _resident(w_mem_q.shape)],
        out_specs=[rt(d), rt(w_mem_q.shape[1])],
        out_shape=[jax.ShapeDtypeStruct((rows, d), F32), jax.ShapeDtypeStruct((rows, w_mem_q.shape[1]), BF16)],
        compiler_params=_cparams(("parallel",)),
        name="mix",
    )(*o_g, *l_g, o_b, ga, gb, h, w_a, w_b, w_o, norm_mem_q.reshape(1, d), w_mem_q)


def _mem_kv_kernel(x_ref, g_ref, w_ref, k_ref, v_ref):
    u = _rms(x_ref[...], g_ref[...]).astype(BF16)
    tm = x_ref.shape[0]
    n = MEM_HEADS * MEM_HEAD_DIM
    for o_ref, off in ((k_ref, 0), (v_ref, n)):
        r = jnp.dot(u, w_ref[:, off:off + n], preferred_element_type=F32)
        for h in range(MEM_HEADS):
            o_ref[pl.ds(h, tm, stride=MEM_HEADS), :] = r[:, h * MEM_HEAD_DIM:(h + 1) * MEM_HEAD_DIM]


def _mem_kv(mem2d, gain, w, tm):
    rows, d = mem2d.shape
    return pl.pallas_call(
        _mem_kv_kernel,
        grid=(rows // tm,),
        in_specs=[pl.BlockSpec((tm, d), lambda i: (i, 0)), _resident((1, d)), _resident(w.shape)],
        out_specs=[pl.BlockSpec((tm * MEM_HEADS, MEM_HEAD_DIM), lambda i: (i, 0))] * 2,
        out_shape=[jax.ShapeDtypeStruct((rows * MEM_HEADS, MEM_HEAD_DIM), F32)] * 2,
        compiler_params=_cparams(("parallel",)),
        name="mem_kv",
    )(mem2d, gain.reshape(1, d), w)


def _mem_attn_body(q, k_refs, v_refs):
    nb = len(q)
    M = k_refs[0].shape[0] // MEM_HEADS
    units = [(b, h) for b in range(nb) for h in range(MEM_HEADS)]
    s = [_dot_nt(q[b][:, h * MEM_HEAD_DIM:(h + 1) * MEM_HEAD_DIM], k_refs[b][pl.ds(h, M, stride=MEM_HEADS), :])
         * (MEM_HEAD_DIM ** -0.5) for b, h in units]
    p = [jnp.exp(x - jnp.max(x, axis=-1, keepdims=True)) for x in s]
    o = [_dot(p[u], v_refs[b][pl.ds(h, M, stride=MEM_HEADS), :]) / jnp.sum(p[u], axis=-1, keepdims=True)
         for u, (b, h) in enumerate(units)]
    return [jnp.concatenate(o[b * MEM_HEADS:(b + 1) * MEM_HEADS], axis=-1) for b in range(nb)]


def _mem_attn_kernel(q_ref, k_ref, v_ref, o_ref):
    nb = q_ref.shape[0]
    o = _mem_attn_body([q_ref[b] for b in range(nb)], [k_ref.at[b] for b in range(nb)],
                       [v_ref.at[b] for b in range(nb)])
    for b in range(nb):
        o_ref[b] = o[b]


def _mem_attn(qm, mem_k, mem_v, tm, nb):
    B, T, w = qm.shape
    kv_spec = pl.BlockSpec((nb,) + mem_k.shape[1:], lambda b, j: (b, 0, 0))
    return pl.pallas_call(
        _mem_attn_kernel,
        grid=(B // nb, T // tm),
        in_specs=[pl.BlockSpec((nb, tm, w), lambda b, j: (b, j, 0)), kv_spec, kv_spec],
        out_specs=pl.BlockSpec((nb, tm, w), lambda b, j: (b, j, 0)),
        out_shape=jax.ShapeDtypeStruct((B, T, w), F32),
        compiler_params=_cparams(("parallel", "parallel")),
        name="mem_attn",
    )(qm, mem_k, mem_v)


def _ffn_kernel(*refs, inject, emit_gate):
    if inject:
        h1_ref, om_ref, init_ref, fill_ref = refs[:4]
        rest = refs[4:]
    else:
        h1_ref, om_ref, init_ref = refs[:3]
        fill_ref, rest = None, refs[3:]
    _ffn_body(h1_ref[...], om_ref[...], init_ref, fill_ref, *rest, emit_gate=emit_gate)


def _ffn_body(h1, om, init_ref, fill_ref, wmo_ref, gf_ref, wup_ref, wc_ref, bc_ref, wd_ref, gfin_ref,
              y_ref, fc_ref, gs, *, emit_gate):
    tm = h1.shape[0]
    F = wd_ref.shape[0]
    PAD = SUBLANES

    @pl.when(pl.program_id(1) == 0)
    def _():
        gs[PAD - 2:PAD, :] = init_ref[...]

    h2 = h1 + _dot(om, wmo_ref[...])
    n = _rms(h2, gf_ref[...]).astype(BF16)
    gate = jnp.dot(n, wup_ref[:, :F], preferred_element_type=F32)
    if fill_ref is not None:
        r = lax.broadcasted_iota(jnp.int32, (tm, 1), 0)
        gate = jnp.where((r % SAMPLE_ROWS) >= SAMPLE_ROWS - 2, fill_ref[...], gate)
    gs[PAD:PAD + tm, :] = gate
    conv = (gs[PAD - 2:PAD - 2 + tm, :] * wc_ref[0:1, :] + gs[PAD - 1:PAD - 1 + tm, :] * wc_ref[1:2, :]
            + gate * wc_ref[2:3, :])
    last2 = gs[PAD + tm - 2:PAD + tm, :]
    gs[PAD - 2:PAD, :] = last2
    if emit_gate:
        fc_ref[...] = gate
    else:
        fc_ref[...] = last2
    up = jnp.dot(n, wup_ref[:, F:], preferred_element_type=F32)
    act = _silu(conv + bc_ref[...]) * up
    y = h2 + _dot(act, wd_ref[...])
    y_ref[...] = _rms(y, gfin_ref[...])


def _ffn(h1, om, init, fill, w_mo, norm_ffn, w_up, w_conv, b_conv, w_down, norm_final, tm, emit_gate):
    B, T, d = h1.shape
    F = w_down.shape[0]
    inject = fill is not None
    rt = lambda n: pl.BlockSpec((None, tm, n), lambda b, j: (b, j, 0))
    in_specs = [rt(d), rt(om.shape[-1]), pl.BlockSpec((None, 2, F), lambda b, j: (b, 0, 0))]
    args = [h1, om, init]
    if inject:
        in_specs.append(rt(F))
        args.append(fill)
    in_specs += [_resident(w_mo.shape), _resident((1, d)), _resident(w_up.shape), _resident(w_conv.shape),
                 _resident((1, F)), _resident(w_down.shape), _resident((1, d))]
    args += [w_mo, norm_ffn.reshape(1, d), w_up, w_conv, b_conv.reshape(1, F), w_down, norm_final.reshape(1, d)]
    if emit_gate:
        fc_spec, fc_shape = rt(F), jax.ShapeDtypeStruct((B, T, F), F32)
    else:
        fc_spec = pl.BlockSpec((None, 2, F), lambda b, j: (b, 0, 0))
        fc_shape = jax.ShapeDtypeStruct((B, 2, F), F32)
    return pl.pallas_call(
        functools.partial(_ffn_kernel, inject=inject, emit_gate=emit_gate),
        grid=(B, T // tm),
        in_specs=in_specs,
        out_specs=[rt(d), fc_spec],
        out_shape=[jax.ShapeDtypeStruct((B, T, d), F32), fc_shape],
        scratch_shapes=[pltpu.VMEM((tm + SUBLANES, F), F32)],
        compiler_params=_cparams(("parallel", "arbitrary")),
        name="ffn",
    )(*args)


def _post_kernel(*refs):
    n_mix = 2 * N_OG + 4
    mix_in, (k_ref, v_ref, init_ref), rest = refs[:n_mix], refs[n_mix:n_mix + 3], refs[n_mix + 3:]
    mix_w, ffn_rest = rest[:5], rest[5:]
    h1, qm = _mix_body(*mix_in, *mix_w)
    om, = _mem_attn_body([qm], [k_ref], [v_ref])
    _ffn_body(h1, om, init_ref, None, *ffn_rest, emit_gate=False)


def _post(o_g, l_g, o_b, ga, gb, h, mem_k, mem_v, init, w_a, w_b, w_o, norm_mem_q, w_mem_q, w_mo, norm_ffn, w_up,
          w_conv, b_conv, w_down, norm_final, tm):
    B, T, d = h.shape
    F = w_down.shape[0]
    rt = lambda n: pl.BlockSpec((None, tm, n), lambda b, j: (b, j, 0))
    per_b = lambda a: pl.BlockSpec((None,) + a.shape[1:], lambda b, j: (b, 0, 0))
    weights = [w_a, w_b, w_o, norm_mem_q.reshape(1, d), w_mem_q, w_mo, norm_ffn.reshape(1, d), w_up, w_conv,
               b_conv.reshape(1, F), w_down, norm_final.reshape(1, d)]
    return pl.pallas_call(
        _post_kernel,
        grid=(B, T // tm),
        in_specs=([rt(LANES)] * (2 * N_OG) + [rt(GDN_W), rt(d), rt(d), rt(d), per_b(mem_k), per_b(mem_v), per_b(init)]
                  + [_resident(w.shape) for w in weights]),
        out_specs=[rt(d), pl.BlockSpec((None, 2, F), lambda b, j: (b, 0, 0))],
        out_shape=[jax.ShapeDtypeStruct((B, T, d), F32), jax.ShapeDtypeStruct((B, 2, F), F32)],
        scratch_shapes=[pltpu.VMEM((tm + SUBLANES, F), F32)],
        compiler_params=_cparams(("parallel", "arbitrary")),
        name="post",
    )(*o_g, *l_g, o_b, ga, gb, h, mem_k, mem_v, init, *weights)


def kernel(x_prompt, x_sample, cache_dil0_kv, cache_dil1_kv, cache_dil2_kv, state_delta, state_delta_conv, cache_mem_k, cache_mem_v, state_ffn_conv, mem_prompt, rel_bias, norm_mix, w_in, w_conv_delta, a_log, dt_bias, norm_delta_out, w_branch_a, w_branch_b, w_out, norm_mem_q, norm_mem_kv, w_mem_q, w_mem_kv, w_mem_o, norm_ffn, w_ffn_up, w_ffn_conv, b_ffn_conv, w_ffn_down, norm_final):
    B, S, D = x_prompt.shape
    Bs, Ts, _ = x_sample.shape
    depth = w_in.shape[0]
    assert depth == 1 and Ts <= SAMPLE_ROWS - 2 and Ts >= 3 and S % (16 * TILE_Q) == 0
    F = w_ffn_down.shape[1]
    M = mem_prompt.shape[1]
    l = 0
    w_arr = _arrange_w_in(w_in[l])
    w_a, w_b, w_o = (w.astype(BF16) for w in (w_branch_a[l], w_branch_b[l], w_out[l]))
    w_mq, w_mkv, w_mo = (w.astype(BF16) for w in (w_mem_q[l], w_mem_kv[l], w_mem_o[l]))
    w_up, w_dn = w_ffn_up[l].astype(BF16), w_ffn_down[l].astype(BF16)
    t_cat, t_cur = _prompt_bias_tables(rel_bias)

    xp = x_prompt.reshape(B * S, D)
    q, kv0, kv1, kv2, kt0, kt1, kt2, gq, gq_tail, z, ba, ga, gb = _in_proj(
        xp, norm_mix[l], w_arr, 256, seq=S, conv_buf=jnp.zeros((B, 3, 3 * GDN_W), F32), w_conv=w_conv_delta[l])
    kvs = [kv.reshape(B, S, 2 * DIL_GW) for kv in (kv0, kv1, kv2)]
    q3 = q.reshape(B, S, 3 * DIL_GW)
    o_g, l_g = [], []
    for g in range(3):
        o_sl, l_sl = _dil_prompt(q3, kvs[g], g, t_cat[g], t_cur[g])
        o_g += o_sl
        l_g += l_sl
    o_b, delta_p = _gdn(gq.reshape(B, S, -1), ba.reshape(B, S, LANES), z.reshape(B, S, GDN_W),
                        jnp.zeros((B, GDN_HEADS, GDN_HEAD_DIM, GDN_HEAD_DIM), F32), a_log[l], dt_bias[l],
                        norm_delta_out[l])
    mk_p, mv_p = _mem_kv(mem_prompt.reshape(B * M, D), norm_mem_kv[l], w_mkv, 256)
    mk_p, mv_p = (x.reshape(B, M * MEM_HEADS, MEM_HEAD_DIM) for x in (mk_p, mv_p))
    seq = lambda a: a.reshape(B, S, a.shape[-1])
    y_p, fconv_p = _post([seq(a) for a in o_g], [seq(a) for a in l_g], o_b, seq(ga), seq(gb), x_prompt, mk_p, mv_p,
                         jnp.zeros((B, 2, F), F32), w_a, w_b, w_o, norm_mem_q[l], w_mq, w_mo, norm_ffn[l], w_up,
                         w_ffn_conv[l], b_ffn_conv[l], w_dn, norm_final, 256)
    p_out = ([kt[:, :, S - min(w, S):].reshape(B, 2, DIL_HPG, DIL_HEAD_DIM, min(w, S)).transpose(0, 4, 1, 2, 3)[None]
              for kt, (w, _) in zip((kt0, kt1, kt2), DIL_GROUPS)]
             + [delta_p[None], gq_tail[:, SUBLANES - 3:][None], mk_p.reshape(1, B, M, MEM_HEADS, MEM_HEAD_DIM),
                mv_p.reshape(1, B, M, MEM_HEADS, MEM_HEAD_DIM), fconv_p[None]])

    R = SAMPLE_ROWS
    xs = jnp.pad(x_sample, ((0, 0), (0, R - Ts), (0, 0))).reshape(Bs * R, D)
    q, kv0, kv1, kv2, gq, z, ba, ga, gb = _in_proj(xs, norm_mix[l], w_arr, Bs * R)
    kvn = [kv.reshape(Bs, R, 2 * DIL_GW) for kv in (kv0, kv1, kv2)]
    caches_t = [jnp.transpose(c[l], (0, 2, 3, 4, 1)).reshape(Bs, 2 * DIL_GW, c.shape[2])
                for c in (cache_dil0_kv, cache_dil1_kv, cache_dil2_kv)]
    tabc, tabn = _sample_bias_tables(rel_bias, Ts)
    o_g, l_g, new_caches = _dil_sample(q.reshape(Bs, R, 3 * DIL_GW), kvn, caches_t, tabc, tabn, Ts)
    o_b, delta_s = _gdn_sample(gq.reshape(Bs, R, -1), ba.reshape(Bs, R, LANES), z.reshape(Bs, R, GDN_W),
                               state_delta_conv[l], state_delta[l], w_conv_delta[l], a_log[l], dt_bias[l],
                               norm_delta_out[l], Ts)
    o_b = o_b.reshape(Bs * R, GDN_W).astype(BF16)
    h1, qm = _mix(o_g, l_g, o_b, ga, gb, xs, w_a, w_b, w_o, norm_mem_q[l], w_mq, Bs * R)
    om = _mem_attn(qm.reshape(Bs, R, -1), cache_mem_k[l].reshape(Bs, M * MEM_HEADS, MEM_HEAD_DIM),
                   cache_mem_v[l].reshape(Bs, M * MEM_HEADS, MEM_HEAD_DIM), R, 4 if Bs % 4 == 0 else 1)
    fst = state_ffn_conv[l]
    fill = jnp.concatenate([jnp.zeros((Bs, R - 2, F), F32),
                            jnp.concatenate([fst[1:], jnp.zeros((1, 2, F), F32)], axis=0)], axis=1)
    y_s, gate_s = _ffn(h1.reshape(1, Bs * R, D), om.reshape(1, Bs * R, -1), fst[:1], fill.reshape(1, Bs * R, F),
                       w_mo, norm_ffn[l], w_up, w_ffn_conv[l], b_ffn_conv[l], w_dn, norm_final, Bs * R, True)
    y_s = y_s.reshape(Bs, R, D)[:, :Ts]
    gq3 = gq.reshape(Bs, R, -1)
    s_out = ([nc.reshape(Bs, 2, DIL_HPG, DIL_HEAD_DIM, nc.shape[2]).transpose(0, 4, 1, 2, 3)[None]
              for nc in new_caches]
             + [delta_s[None], gq3[:, Ts - 3:Ts][None], gate_s.reshape(Bs, R, F)[:, Ts - 2:Ts][None]])

    return (y_p.reshape(B, S, D), y_s, *p_out, *s_out)
```

```python
import functools
import math

import jax
import jax.numpy as jnp
import numpy as np
from jax import lax
from jax.experimental import pallas as pl
from jax.experimental.pallas import tpu as pltpu

F32 = jnp.float32
BF16 = jnp.bfloat16

PAST_LEN = 8192
DIL_GROUPS = ((128, 1), (512, 4), (2048, 16))
DIL_HPG = 4
DIL_HEAD_DIM = 64
DIL_GW = DIL_HPG * DIL_HEAD_DIM
DIL_NK = 129
REL_BUCKETS = 32
REL_MAX_DIST = 2048
GDN_HEADS = 8
GDN_HEAD_DIM = 128
GDN_W = GDN_HEADS * GDN_HEAD_DIM
GDN_CHUNK = 64
MEM_HEADS = 4
MEM_HEAD_DIM = 128
EPS = 1e-6
NEG = -1e30

LANES = 128
SUBLANES = 8
TILE_Q = 128
D_MODEL = 1024
ROW_TILE = 256
SAMPLE_ROWS = SUBLANES
CONV_PAD = SUBLANES
VMEM_LIMIT = 56 * 1024 * 1024


def _cparams(sem):
    return pltpu.CompilerParams(dimension_semantics=sem, vmem_limit_bytes=VMEM_LIMIT)


def _resident(shape):
    nd = len(shape)
    return pl.BlockSpec(shape, lambda *_: (0,) * nd, pipeline_mode=pl.Buffered(1))


def _rms(x, gain_row):
    return x * lax.rsqrt(jnp.mean(x * x, axis=-1, keepdims=True) + EPS) * gain_row


def _dot(a, b):
    return jnp.dot(a.astype(BF16), b.astype(BF16), preferred_element_type=F32)


def _dot_nt(a, b):
    return lax.dot_general(a.astype(BF16), b.astype(BF16), (((1,), (1,)), ((), ())), preferred_element_type=F32)


def _split3(x):
    hi = x.astype(BF16)
    r1 = x - hi.astype(F32)
    mid = r1.astype(BF16)
    lo = (r1 - mid.astype(F32)).astype(BF16)
    return hi, mid, lo


def _l2n(x):
    return x * lax.rsqrt(jnp.sum(x * x, axis=-1, keepdims=True) + EPS)


def _sigmoid(x):
    return 1.0 / (1.0 + jnp.exp(-x))


def _silu(x):
    return x * _sigmoid(x)


def _softplus(x):
    return jnp.maximum(x, 0.0) + jnp.log(1.0 + jnp.exp(-jnp.abs(x)))


IN_SEGS = (("q", 3 * DIL_GW), ("kv0", 2 * DIL_GW), ("kv1", 2 * DIL_GW), ("kv2", 2 * DIL_GW),
           ("gq", 3 * GDN_W), ("z", GDN_W), ("ba", LANES), ("ga", D_MODEL), ("gb", D_MODEL))


OFF_Q, OFF_K, OFF_V = 0, 3 * DIL_GW, 6 * DIL_GW
OFF_GQ = 9 * DIL_GW
OFF_Z = OFF_GQ + 3 * GDN_W
OFF_BA = OFF_Z + GDN_W
OFF_GATES = OFF_BA + 2 * GDN_HEADS


def _arrange_w_in(w_in):
    w = w_in.astype(BF16)
    return w, w[:, OFF_GATES:]


GQ_SLABS = 3 * GDN_W // LANES


def _in_proj_kernel(x_ref, g_ref, w_ref, wg_ref, *rest, seq_tiles):
    if seq_tiles:
        (cb_ref, wc_ref, q_ref, kv0_ref, kv1_ref, kv2_ref, kt0_ref, kt1_ref, kt2_ref, gq_ref, tail_ref, z_ref,
         ba_ref, ga_ref, gb_ref, cs) = rest
        kt_refs = (kt0_ref, kt1_ref, kt2_ref)
    else:
        q_ref, kv0_ref, kv1_ref, kv2_ref, gq_ref, z_ref, ba_ref, ga_ref, gb_ref = rest
    tm = x_ref.shape[0]
    u = _rms(x_ref[...], g_ref[...]).astype(BF16)

    def seg(ref, off, n):
        return jnp.dot(u, ref[:, off:off + n], preferred_element_type=F32)

    def q_seg():
        q_ref[...] = seg(w_ref, OFF_Q, 3 * DIL_GW) * (DIL_HEAD_DIM ** -0.5)

    def kv_seg(g, part):
        kv_ref = (kv0_ref, kv1_ref, kv2_ref)[g]
        r = seg(w_ref, (OFF_K, OFF_V)[part] + g * DIL_GW, DIL_GW)
        kv_ref[:, part * DIL_GW:(part + 1) * DIL_GW] = r
        if seq_tiles:
            kt_refs[g][part * DIL_GW:(part + 1) * DIL_GW, :] = r.T

    def ba_seg():
        ba_ref[...] = seg(w_ref, OFF_BA, LANES)

    def gate_seg(ref, off):
        ref[...] = seg(wg_ref, off, D_MODEL)

    others = ([(q_seg, 3 * DIL_GW)] + [(functools.partial(kv_seg, g, p), DIL_GW) for g in range(3) for p in range(2)]
              + [(functools.partial(gate_seg, ga_ref, 0), D_MODEL),
                 (functools.partial(gate_seg, gb_ref, D_MODEL), D_MODEL),
                 (ba_seg, LANES)])
    if not seq_tiles:
        for f, _ in others:
            f()
        for c in range(3):
            gq_ref[:, c * GDN_W:(c + 1) * GDN_W] = seg(w_ref, OFF_GQ + c * GDN_W, GDN_W)
        z_ref[...] = seg(w_ref, OFF_Z, GDN_W)
        return
    i = pl.program_id(0)
    first = (i % seq_tiles) == 0

    @pl.when(first)
    def _():
        for s in range(GQ_SLABS):
            cs[s, CONV_PAD - 3:CONV_PAD, :] = cb_ref[:, s * LANES:(s + 1) * LANES]

    @pl.when(jnp.logical_not(first))
    def _():
        for s in range(GQ_SLABS):
            cs[s, CONV_PAD - 3:CONV_PAD, :] = cs[s, CONV_PAD + tm - 3:CONV_PAD + tm, :]

    for c in range(3):
        r = seg(w_ref, OFF_GQ + c * GDN_W, GDN_W)
        for hh in range(GDN_HEADS):
            cs[c * GDN_HEADS + hh, CONV_PAD:CONV_PAD + tm, :] = r[:, hh * LANES:(hh + 1) * LANES]
    base = CONV_PAD - 3 + jnp.minimum(i, 0)

    def conv_slab(s):
        ls = slice(s * LANES, (s + 1) * LANES)
        y = cs[s, pl.ds(base, tm), :] * wc_ref[0:1, ls]
        for j in range(1, 4):
            y = y + cs[s, pl.ds(base + j, tm), :] * wc_ref[j:j + 1, ls]
        y = _silu(y)
        if s < GDN_HEADS:
            y = _l2n(y) * (GDN_HEAD_DIM ** -0.5)
        elif s < 2 * GDN_HEADS:
            y = _l2n(y)
        gq_ref[:, ls] = y
        tail_ref[:, ls] = cs[s, tm:tm + SUBLANES, :]

    def z_seg():
        z_ref[...] = _silu(seg(w_ref, OFF_Z, GDN_W))

    others.insert(len(others) - 1, (z_seg, GDN_W))
    total = sum(n for _, n in others)
    done, cols = 0, 0
    for f, n in others:
        f()
        cols += n
        upto = min(GQ_SLABS, (cols * GQ_SLABS + total - 1) // total)
        for s in range(done, upto):
            conv_slab(s)
        done = upto
    assert done == GQ_SLABS


def _in_proj(x2d, gain, w_arr, tm, seq=None, conv_buf=None, w_conv=None):
    rows, d = x2d.shape
    fused = seq is not None
    nt = seq // tm if fused else None
    names = [n for n, _ in IN_SEGS]
    widths = dict(IN_SEGS)
    row_spec = lambda n: pl.BlockSpec((tm, n), lambda i: (i, 0))
    out_specs, out_shape = [], []
    for n in names:
        out_specs.append(row_spec(widths[n]))
        out_shape.append(jax.ShapeDtypeStruct((rows, widths[n]), F32))
        if n == "kv2" and fused:
            for _ in range(3):
                out_specs.append(pl.BlockSpec((None, 2 * DIL_GW, tm), lambda i: (i // nt, 0, i % nt)))
                out_shape.append(jax.ShapeDtypeStruct((rows // seq, 2 * DIL_GW, seq), F32))
        if n == "gq" and fused:
            out_specs.append(pl.BlockSpec((None, SUBLANES, 3 * GDN_W), lambda i: (i // nt, 0, 0)))
            out_shape.append(jax.ShapeDtypeStruct((rows // seq, SUBLANES, 3 * GDN_W), F32))
    in_specs = [pl.BlockSpec((tm, d), lambda i: (i, 0)), _resident((1, d)), _resident(w_arr[0].shape),
                _resident(w_arr[1].shape)]
    args = [x2d, gain.reshape(1, d), *w_arr]
    if fused:
        in_specs += [pl.BlockSpec((None, 3, 3 * GDN_W), lambda i: (i // nt, 0, 0)), _resident(w_conv.shape)]
        args += [conv_buf, w_conv]
    return pl.pallas_call(
        functools.partial(_in_proj_kernel, seq_tiles=nt),
        grid=(rows // tm,),
        in_specs=in_specs,
        out_specs=out_specs,
        out_shape=out_shape,
        scratch_shapes=[pltpu.VMEM((GQ_SLABS, CONV_PAD + tm, LANES), F32)] if fused else [],
        compiler_params=_cparams(("arbitrary",)),
        name="in_proj",
    )(*args)


def _rel_bucket(dist):
    exact = REL_BUCKETS // 2
    d = jnp.maximum(dist, 1).astype(F32)
    large = exact + (jnp.log(d / exact) / math.log(REL_MAX_DIST / exact) * (REL_BUCKETS - exact)).astype(jnp.int32)
    return jnp.where(dist < exact, dist, jnp.minimum(large, REL_BUCKETS - 1))


def _group_bias(rel_bias, g):
    dil = DIL_GROUPS[g][1]
    dist = dil * jnp.arange(DIL_NK, dtype=jnp.int32)
    tab = rel_bias[_rel_bucket(dist)]
    return tab[:, g * DIL_HPG:(g + 1) * DIL_HPG].T.astype(F32)


def _toeplitz(v, n, width):
    h, L = v.shape
    return jnp.tile(v, (1, n))[:, :n * (L - 1)].reshape(h, n, L - 1)[:, :, :width]


def _prompt_bias_tables(rel_bias):
    cat, cur = [], []
    for g in range(3):
        bg = _group_bias(rel_bias, g)
        v = jnp.concatenate([bg[:, ::-1], jnp.full((DIL_HPG, 3 * TILE_Q - DIL_NK), NEG, F32)], axis=1)
        t = _toeplitz(v, TILE_Q, 2 * TILE_Q)
        cat.append(t)
        cur.append(t[:, :, TILE_Q:])
    return cat, cur


DIL_TIF = 2
DIL_SLABS = DIL_GW // LANES


def _dil_prompt_kernel(q0_ref, q1_ref, k0_ref, k1_ref, v0_ref, v1_ref, tcat_ref, tcur_ref,
                       o0_ref, o1_ref, l0_ref, l1_ref, *, dil):
    S = q0_ref.shape[0]
    nb = S // dil // TILE_Q
    q_refs, k_refs, v_refs = (q0_ref, q1_ref), (k0_ref, k1_ref), (v0_ref, v1_ref)
    o_refs, l_refs = (o0_ref, o1_ref), (l0_ref, l1_ref)
    even = lax.broadcasted_iota(jnp.int32, (TILE_Q, LANES), 1) < DIL_HEAD_DIM

    def rows(r, t):
        start = r + dil * TILE_Q * t
        return pl.ds(start, TILE_Q, stride=dil) if dil > 1 else pl.ds(start, TILE_Q)

    tiles = [(r, t) for r in range(dil) for t in range(nb)]
    for i0 in range(0, len(tiles), DIL_TIF):
        grp = tiles[i0:i0 + DIL_TIF]
        qm, kc, vc = {}, {}, {}
        for ti, (r, t) in enumerate(grp):
            for sl in range(DIL_SLABS):
                qf = q_refs[sl][rows(r, t), :]
                qm[ti, 2 * sl] = jnp.where(even, qf, 0.0).astype(BF16)
                qm[ti, 2 * sl + 1] = jnp.where(even, 0.0, qf).astype(BF16)
                kc[ti, sl] = k_refs[sl][rows(r, t), :].astype(BF16)
                vc[ti, sl] = v_refs[sl][rows(r, t), :].astype(BF16)
                if t > 0:
                    kc[ti, sl] = jnp.concatenate([k_refs[sl][rows(r, t - 1), :].astype(BF16), kc[ti, sl]], axis=0)
                    vc[ti, sl] = jnp.concatenate([v_refs[sl][rows(r, t - 1), :].astype(BF16), vc[ti, sl]], axis=0)
        units = [(ti, h) for ti in range(len(grp)) for h in range(DIL_HPG)]
        s = [_dot_nt(qm[ti, h], kc[ti, h // 2]) + (tcat_ref[h] if grp[ti][1] > 0 else tcur_ref[h]) for ti, h in units]
        m = [jnp.max(x, axis=-1, keepdims=True) for x in s]
        p = [jnp.exp(x - mx) for x, mx in zip(s, m)]
        l = [jnp.sum(x, axis=-1, keepdims=True) for x in p]
        pv = [jnp.dot(p[u].astype(BF16), vc[ti, h // 2], preferred_element_type=F32) for u, (ti, h) in enumerate(units)]
        o = [pv[u] / l[u] for u in range(len(units))]
        lse = [m[u] + jnp.log(l[u]) for u in range(len(units))]
        for ti, (r, t) in enumerate(grp):
            for sl in range(DIL_SLABS):
                ue, uo = ti * DIL_HPG + 2 * sl, ti * DIL_HPG + 2 * sl + 1
                o_refs[sl][rows(r, t), :] = jnp.where(even, o[ue], o[uo])
                l_refs[sl][rows(r, t), :] = jnp.where(even, lse[ue], lse[uo])


def _dil_prompt(q, kv, g, t_cat, t_cur):
    B, S, _ = q.shape
    dil = DIL_GROUPS[g][1]
    slab = lambda c: pl.BlockSpec((None, S, LANES), lambda b: (b, 0, c))
    nq, nk = g * DIL_SLABS, 0
    outs = pl.pallas_call(
        functools.partial(_dil_prompt_kernel, dil=dil),
        grid=(B,),
        in_specs=[slab(nq), slab(nq + 1), slab(nk), slab(nk + 1), slab(nk + 2), slab(nk + 3),
                  _resident(t_cat.shape), _resident(t_cur.shape)],
        out_specs=[slab(0)] * 4,
        out_shape=[jax.ShapeDtypeStruct((B, S, LANES), F32)] * 4,
        compiler_params=_cparams(("parallel",)),
        name=f"dil_prompt_g{g}",
    )(q, q, kv, kv, kv, kv, t_cat, t_cur)
    o0, o1, l0, l1 = (x.reshape(B * S, LANES) for x in outs)
    return [o0, o1], [l0, l1]


def _sample_bias_tables(rel_bias, t_real):
    R = SAMPLE_ROWS
    tabc, tabn = [], []
    t_i = np.arange(R)[:, None]
    u_i = np.arange(R)[None, :]
    for g, (w, dil) in enumerate(DIL_GROUPS):
        bg = _group_bias(rel_bias, g)
        base = bg[:, ::-1][:, :TILE_Q]
        t0 = jnp.concatenate([base[:, :, None], jnp.full((DIL_HPG, TILE_Q, dil - 1), NEG, F32)], axis=2)
        t0 = t0.reshape(DIL_HPG, w)
        rows = [jnp.concatenate([jnp.full((DIL_HPG, t), NEG, F32), t0[:, :w - t]], axis=1) for t in range(t_real)]
        rows += [jnp.zeros((DIL_HPG, w), F32)] * (R - t_real)
        tabc.append(jnp.stack(rows, axis=1).reshape(DIL_HPG * R, w))
        tn = jnp.full((DIL_HPG, R, R), NEG, F32)
        for j in range(-(-t_real // dil)):
            hit = (t_i - u_i == j * dil) & (t_i < t_real)
            tn = jnp.where(hit[None], bg[:, j][:, None, None], tn)
        tn = jnp.where((t_i >= t_real)[None], 0.0, tn)
        tabn.append(tn.reshape(DIL_HPG * R, R))
    return tabc, jnp.stack(tabn)


def _dil_sample_kernel(q_ref, n0_ref, n1_ref, n2_ref, c0_ref, c1_ref, c2_ref, tc0_ref, tc1_ref, tc2_ref, tn_ref,
                       o_ref, l_ref, oc0_ref, oc1_ref, oc2_ref, *, t_real):
    R = SAMPLE_ROWS
    rows = lax.broadcasted_iota(jnp.int32, (DIL_HPG * R, DIL_GW), 0)
    lanes = lax.broadcasted_iota(jnp.int32, (DIL_HPG * R, DIL_GW), 1)
    head_mask = (lanes // DIL_HEAD_DIM) == (rows // R)
    lane_f = lax.broadcasted_iota(jnp.int32, (2 * DIL_GW, LANES), 1)
    keep = lane_f < LANES - t_real
    sel_l = lax.broadcasted_iota(jnp.int32, (LANES, R), 0)
    sel_u = lax.broadcasted_iota(jnp.int32, (LANES, R), 1)
    selT = ((sel_l == sel_u + LANES - t_real) & (sel_u < t_real)).astype(BF16)
    groups = ((n0_ref, c0_ref, tc0_ref, oc0_ref), (n1_ref, c1_ref, tc1_ref, oc1_ref), (n2_ref, c2_ref, tc2_ref, oc2_ref))

    def fold_heads(x):
        x = jnp.where(head_mask, x, 0.0)
        return x[0:R] + x[R:2 * R] + x[2 * R:3 * R] + x[3 * R:4 * R]

    G3 = range(3)
    kvn = [groups[g][0][...] for g in G3]
    q_bd = [jnp.where(head_mask, jnp.concatenate([q_ref[:, g * DIL_GW:(g + 1) * DIL_GW]] * DIL_HPG, axis=0), 0.0)
            for g in G3]
    s_c = [_dot(q_bd[g], groups[g][1][:DIL_GW, :]) + groups[g][2][...] for g in G3]
    s_n = [_dot_nt(q_bd[g], kvn[g][:, :DIL_GW]) + tn_ref[g] for g in G3]
    m = [jnp.maximum(jnp.max(s_c[g], axis=-1, keepdims=True), jnp.max(s_n[g], axis=-1, keepdims=True)) for g in G3]
    p_c = [jnp.exp(s_c[g] - m[g]) for g in G3]
    p_n = [jnp.exp(s_n[g] - m[g]) for g in G3]
    l = [jnp.sum(p_c[g], axis=-1, keepdims=True) + jnp.sum(p_n[g], axis=-1, keepdims=True) for g in G3]
    acc = [(_dot_nt(p_c[g], groups[g][1][DIL_GW:, :]) + _dot(p_n[g], kvn[g][:, DIL_GW:])) / l[g] for g in G3]
    for g in G3:
        o_ref[:, g * DIL_GW:(g + 1) * DIL_GW] = fold_heads(acc[g])
        l_ref[:, g * DIL_GW:(g + 1) * DIL_GW] = fold_heads(jnp.broadcast_to(m[g] + jnp.log(l[g]), acc[g].shape))
    for g, (n_ref, c_ref, tc_ref, oc_ref) in enumerate(groups):
        W = c_ref.shape[1]
        hi, mid, lo = _split3(kvn[g])
        tail = (jnp.dot(selT, hi, preferred_element_type=F32) + jnp.dot(selT, mid, preferred_element_type=F32)
                + jnp.dot(selT, lo, preferred_element_type=F32)).T
        nxt = pltpu.roll(c_ref[:, 0:LANES], LANES - t_real, axis=1)
        for c in range(W // LANES):
            cur = nxt
            nxt = (pltpu.roll(c_ref[:, (c + 1) * LANES:(c + 2) * LANES], LANES - t_real, axis=1)
                   if (c + 1) * LANES < W else tail)
            oc_ref[:, c * LANES:(c + 1) * LANES] = jnp.where(keep, cur, nxt)


def _dil_sample(q, kvn, caches_t, tabc, tabn, t_real):
    B = q.shape[0]
    row = lambda n: pl.BlockSpec((None, SAMPLE_ROWS, n), lambda b: (b, 0, 0))
    cspecs = [pl.BlockSpec((None,) + c.shape[1:], lambda b: (b, 0, 0)) for c in caches_t]
    for g, (w, dil) in enumerate(DIL_GROUPS):
        assert caches_t[g].shape == (B, 2 * DIL_GW, w) and w // dil == TILE_Q
    out_spec = row(3 * DIL_GW)
    o, lse, *new_caches = pl.pallas_call(
        functools.partial(_dil_sample_kernel, t_real=t_real),
        grid=(B,),
        in_specs=([row(3 * DIL_GW)] + [row(2 * DIL_GW)] * 3 + cspecs + [_resident(t.shape) for t in tabc]
                  + [_resident(tabn.shape)]),
        out_specs=[out_spec, out_spec] + cspecs,
        out_shape=([jax.ShapeDtypeStruct((B, SAMPLE_ROWS, 3 * DIL_GW), F32)] * 2
                   + [jax.ShapeDtypeStruct(c.shape, F32) for c in caches_t]),
        compiler_params=_cparams(("parallel",)),
        name="dil_sample",
    )(q, *kvn, *caches_t, *tabc, tabn)
    o = o.reshape(B * SAMPLE_ROWS, 3 * DIL_GW)
    lse = lse.reshape(B * SAMPLE_ROWS, 3 * DIL_GW)
    n = 3 * DIL_SLABS
    return ([o[:, i * LANES:(i + 1) * LANES] for i in range(n)],
            [lse[:, i * LANES:(i + 1) * LANES] for i in range(n)], new_caches)


GDN_HPS = GDN_HEADS
GDN_SW = GDN_HPS * GDN_HEAD_DIM
GDN_TPI = 4
GDN_ROWS = 1024


def _gdn_kernel(q_ref, k_ref, v_ref, ba_ref, alog_ref, dtb_ref, z_ref, gn_ref, tri_ref, s0_ref, o_ref, s_ref, carry_s,
                *, tpi):
    T = q_ref.shape[0]
    C = GDN_CHUNK
    D = GDN_HEAD_DIM
    ri = lax.broadcasted_iota(jnp.int32, (TILE_Q, TILE_Q), 0)
    ci = lax.broadcasted_iota(jnp.int32, (TILE_Q, TILE_Q), 1)
    same = (ri // C) == (ci // C)
    incl = same & (ri >= ci)
    strict = same & (ri > ci)
    eye = (ri == ci).astype(F32)
    lane, row = ci, ri
    gain = gn_ref[...]
    zpad = jnp.zeros((C, D), F32)
    HH = range(GDN_HPS)
    hsl = [slice(hh * D, (hh + 1) * D) for hh in HH]
    RW = tpi * TILE_Q
    UU = [(tt, hh) for tt in range(tpi) for hh in HH]
    UI = range(len(UU))

    def tile(i, S):
        r0 = pl.multiple_of(i * RW, RW)
        rt = [r0 + tt * TILE_Q for tt in range(tpi)]
        q = [q_ref[pl.ds(rt[tt], TILE_Q), hsl[hh]] for tt, hh in UU]
        k = [k_ref[pl.ds(rt[tt], TILE_Q), hsl[hh]] for tt, hh in UU]
        v = [v_ref[pl.ds(rt[tt], TILE_Q), hsl[hh]] for tt, hh in UU]
        tri = tri_ref[...]
        beta_all, G_all = [], []
        for tt in range(tpi):
            ba = ba_ref[pl.ds(rt[tt], TILE_Q), :]
            beta_all.append(_sigmoid(ba))
            gh, gm, gl = _split3(-jnp.exp(alog_ref[...]) * _softplus(ba + dtb_ref[...]))
            G_all.append(jnp.dot(tri, gh, preferred_element_type=F32) + jnp.dot(tri, gm, preferred_element_type=F32)
                         + jnp.dot(tri, gl, preferred_element_type=F32))
        head = [hh for _, hh in UU]
        bc = [jnp.sum(jnp.where(lane == head[u], beta_all[UU[u][0]], 0.0), axis=-1, keepdims=True) for u in UI]
        Gc = [jnp.broadcast_to(jnp.sum(jnp.where(lane == head[u] + GDN_HEADS, G_all[UU[u][0]], 0.0), axis=-1,
                                       keepdims=True), (TILE_Q, TILE_Q)) for u in UI]
        gamma = [jnp.exp(jnp.where(incl, Gc[u] - Gc[u].T, NEG)) for u in UI]
        kk = [_dot_nt(k[u], k[u]) for u in UI]
        qk = [_dot_nt(q[u], k[u]) for u in UI]
        X = [jnp.where(strict, bc[u] * kk[u] * gamma[u], 0.0) for u in UI]
        P = [eye - X[u] for u in UI]
        for _ in range(int(math.log2(C)) - 1):
            X = [_dot(X[u], X[u]) for u in UI]
            P = [P[u] + _dot(P[u], X[u]) for u in UI]
        eG = [jnp.exp(Gc[u]) for u in UI]
        rhs = [jnp.concatenate([v[u] * bc[u], k[u] * (bc[u] * eG[u])], axis=-1) for u in UI]
        sol = [rhs[u] + _dot(P[u] - eye, rhs[u]) for u in UI]
        a_in = [qk[u] * gamma[u] for u in UI]
        q_dec = [q[u] * eG[u] for u in UI]
        kdT = [(k[u] * jnp.exp(jnp.where(row < C, Gc[u][C - 1:C, :], Gc[u][2 * C - 1:2 * C, :]) - Gc[u])).T
               for u in UI]
        S = list(S)
        for tt in range(tpi):
            us = [tt * GDN_HPS + hh for hh in HH]
            oq, vn = [[] for _ in HH], [[] for _ in HH]
            for c in range(TILE_Q // C):
                cs = slice(c * C, (c + 1) * C)
                r = [_dot(jnp.concatenate([sol[us[hh]][cs, D:], q_dec[us[hh]][cs]], axis=0), S[hh]) for hh in HH]
                for hh in HH:
                    oq[hh].append(r[hh][C:])
                    vn[hh].append(sol[us[hh]][cs, :D] - r[hh][:C])
                vpad = [jnp.concatenate([vn[hh][c], zpad] if c == 0 else [zpad, vn[hh][c]], axis=0) for hh in HH]
                S = [S[hh] * jnp.exp(Gc[us[hh]][(c + 1) * C - 1:(c + 1) * C, :]) + _dot(kdT[us[hh]], vpad[hh])
                     for hh in HH]
            o = [jnp.concatenate(oq[hh], axis=0) + _dot(a_in[us[hh]], jnp.concatenate(vn[hh], axis=0)) for hh in HH]
            outs = [_rms(o[hh], gain) * z_ref[pl.ds(rt[tt], TILE_Q), hsl[hh]] for hh in HH]
            o_ref[pl.ds(rt[tt], TILE_Q), :] = jnp.concatenate(outs, axis=-1).astype(o_ref.dtype)
        return tuple(S)

    @pl.when(pl.program_id(1) == 0)
    def _():
        carry_s[...] = s0_ref[...]

    S = lax.fori_loop(0, T // RW, tile, tuple(carry_s[hh] for hh in HH))
    for hh in HH:
        carry_s[hh] = S[hh]
        s_ref[hh] = S[hh]


def _gdn(gq, ba, z, s0, a_log, dt_bias, norm_out):
    B, T, _ = gq.shape
    H = GDN_HEADS
    pad16 = lambda x: jnp.concatenate([jnp.zeros((H,), F32), x.astype(F32), jnp.zeros((LANES - 2 * H,), F32)])
    assert GDN_HPS == H
    tb = min(GDN_ROWS, T)
    tpi = GDN_TPI if (tb // TILE_Q) % GDN_TPI == 0 else 1
    col = lambda off: pl.BlockSpec((None, tb, GDN_SW), lambda b, j: (b, j, off))
    sblk = pl.BlockSpec((None, H, GDN_HEAD_DIM, GDN_HEAD_DIM), lambda b, j: (b, 0, 0, 0))
    r = np.arange(TILE_Q)
    tri = jnp.asarray((r[:, None] >= r[None, :]) & (r[:, None] // GDN_CHUNK == r[None, :] // GDN_CHUNK), BF16)
    o, s_new = pl.pallas_call(
        functools.partial(_gdn_kernel, tpi=tpi),
        grid=(B, T // tb),
        in_specs=[col(0), col(1), col(2),
                  pl.BlockSpec((None, tb, LANES), lambda b, j: (b, j, 0)),
                  _resident((1, LANES)), _resident((1, LANES)),
                  col(0), _resident((1, LANES)), _resident((TILE_Q, TILE_Q)), sblk],
        out_specs=[col(0), sblk],
        out_shape=[jax.ShapeDtypeStruct((B, T, GDN_W), BF16),
                   jax.ShapeDtypeStruct((B, H, GDN_HEAD_DIM, GDN_HEAD_DIM), F32)],
        scratch_shapes=[pltpu.VMEM((H, GDN_HEAD_DIM, GDN_HEAD_DIM), F32)],
        compiler_params=_cparams(("parallel", "arbitrary")),
        name="gdn",
    )(gq, gq, gq, ba, pad16(a_log).reshape(1, LANES), pad16(dt_bias).reshape(1, LANES), z,
      norm_out.reshape(1, LANES), tri, s0)
    return o, s_new


def _gdn_sample_kernel(x_ref, b_ref, w_ref, ba_ref, alog_ref, dtb_ref, z_ref, gn_ref, s0_ref, o_ref, s_ref, xs,
                       *, t_real):
    R, D, H = SAMPLE_ROWS, GDN_HEAD_DIM, GDN_HEADS
    nb = x_ref.shape[0]
    ri = lax.broadcasted_iota(jnp.int32, (R, R), 0)
    ci = lax.broadcasted_iota(jnp.int32, (R, R), 1)
    incl, strict = ri >= ci, ri > ci
    eye = (ri == ci).astype(F32)
    tri = incl.astype(BF16)
    er = lax.broadcasted_iota(jnp.int32, (LANES, LANES), 0)
    ec = lax.broadcasted_iota(jnp.int32, (LANES, LANES), 1)
    eye_l = (er == ec).astype(BF16)
    live = lax.broadcasted_iota(jnp.int32, (R, LANES), 0) < t_real
    d32 = functools.partial(jnp.dot, preferred_element_type=F32)
    nt = lambda a, b: lax.dot_general(a, b, (((1,), (1,)), ((), ())), preferred_element_type=F32)
    gain = gn_ref[...]
    y, beta_all, G_all, G_allT = [], [], [], []
    for s in range(nb):
        xs[s, CONV_PAD - 3:CONV_PAD, :] = b_ref[s]
        xs[s, CONV_PAD:CONV_PAD + R, :] = x_ref[s]
        ys = xs[s, CONV_PAD - 3:CONV_PAD - 3 + R, :] * w_ref[0:1, :]
        for j in range(1, 4):
            ys = ys + xs[s, CONV_PAD - 3 + j:CONV_PAD - 3 + j + R, :] * w_ref[j:j + 1, :]
        y.append(_silu(ys))
        ba = ba_ref[s]
        beta_all.append(jnp.where(live, _sigmoid(ba), 0.0))
        gh, gm, gl = _split3(jnp.where(live, -jnp.exp(alog_ref[...]) * _softplus(ba + dtb_ref[...]), 0.0))
        G = d32(tri, gh) + d32(tri, gm) + d32(tri, gl)
        th, tm, tl = _split3(G)
        G_all.append(G)
        G_allT.append(nt(eye_l, th) + nt(eye_l, tm) + nt(eye_l, tl))
    US = [(s, h) for s in range(nb) for h in range(H)]
    UI = range(len(US))

    q = [_l2n(y[s][:, h * D:(h + 1) * D]) * (D ** -0.5) for s, h in US]
    k = [_l2n(y[s][:, GDN_W + h * D:GDN_W + (h + 1) * D]) for s, h in US]
    v = [y[s][:, 2 * GDN_W + h * D:2 * GDN_W + (h + 1) * D] for s, h in US]
    bc = [beta_all[s][:, h:h + 1] for s, h in US]
    Gc = [G_all[s][:, H + h:H + h + 1] for s, h in US]
    gamma = [jnp.exp(jnp.where(incl, Gc[u] - G_allT[s][H + h:H + h + 1, :], NEG)) for u, (s, h) in enumerate(US)]
    kk = [_dot_nt(k[u], k[u]) for u in UI]
    qk = [_dot_nt(q[u], k[u]) for u in UI]
    X = [jnp.where(strict, bc[u] * kk[u] * gamma[u], 0.0) for u in UI]
    P = [eye - X[u] for u in UI]
    for _ in range(int(math.log2(R)) - 1):
        X = [_dot(X[u], X[u]) for u in UI]
        P = [P[u] + _dot(P[u], X[u]) for u in UI]
    eG = [jnp.exp(Gc[u]) for u in UI]
    rhs = [jnp.concatenate([v[u] * bc[u], k[u] * (bc[u] * eG[u])], axis=-1) for u in UI]
    sol = [rhs[u] + _dot(P[u] - eye, rhs[u]) for u in UI]
    S = [s0_ref[s, h] for s, h in US]
    r = [_dot(jnp.concatenate([sol[u][:, D:], q[u] * eG[u]], axis=0), S[u]) for u in UI]
    v_new = [sol[u][:, :D] - r[u][:R] for u in UI]
    o = [r[u][R:] + _dot(qk[u] * gamma[u], v_new[u]) for u in UI]
    kdT = [nt(eye_l, (k[u] * jnp.exp(Gc[u][R - 1:R, :] - Gc[u])).astype(BF16)) for u in UI]
    for u, (s, h) in enumerate(US):
        s_ref[s, h] = S[u] * jnp.exp(Gc[u][R - 1:R, :]) + _dot(kdT[u], v_new[u])
    for s in range(nb):
        o_ref[s] = jnp.concatenate([_rms(o[s * H + h], gain) * _silu(z_ref[s, :, h * D:(h + 1) * D])
                                    for h in range(H)], axis=-1)


GDN_SAMPLE_SEQS = 4


def _gdn_sample(gq, ba, z, conv_buf, s0, w_conv, a_log, dt_bias, norm_out, t_real):
    B, R, _ = gq.shape
    H = GDN_HEADS
    pad16 = lambda x: jnp.concatenate([jnp.zeros((H,), F32), x.astype(F32), jnp.zeros((LANES - 2 * H,), F32)])
    nb = GDN_SAMPLE_SEQS if B % GDN_SAMPLE_SEQS == 0 else 1
    blk = lambda *s: pl.BlockSpec((nb,) + s, lambda b: (b,) + (0,) * len(s))
    return pl.pallas_call(
        functools.partial(_gdn_sample_kernel, t_real=t_real),
        grid=(B // nb,),
        in_specs=[blk(R, 3 * GDN_W), blk(3, 3 * GDN_W), _resident(w_conv.shape), blk(R, LANES),
                  _resident((1, LANES)), _resident((1, LANES)), blk(R, GDN_W), _resident((1, LANES)),
                  blk(H, GDN_HEAD_DIM, GDN_HEAD_DIM)],
        out_specs=[blk(R, GDN_W), blk(H, GDN_HEAD_DIM, GDN_HEAD_DIM)],
        out_shape=[jax.ShapeDtypeStruct((B, R, GDN_W), F32),
                   jax.ShapeDtypeStruct((B, H, GDN_HEAD_DIM, GDN_HEAD_DIM), F32)],
        scratch_shapes=[pltpu.VMEM((nb, CONV_PAD + R, 3 * GDN_W), F32)],
        compiler_params=_cparams(("parallel",)),
        name="gdn_sample",
    )(gq, conv_buf, w_conv, ba, pad16(a_log).reshape(1, LANES), pad16(dt_bias).reshape(1, LANES), z,
      norm_out.reshape(1, LANES), s0)


N_OG = 3 * DIL_SLABS


def _mix_kernel(*refs):
    h1, qm = _mix_body(*refs[:2 * N_OG + 9])
    h1_ref, qm_ref = refs[2 * N_OG + 9:]
    h1_ref[...] = h1
    qm_ref[...] = qm


def _mix_body(*refs):
    o_refs, l_refs = refs[:N_OG], refs[N_OG:2 * N_OG]
    ob_ref, ga_ref, gb_ref, h_ref, wa_ref, wb_ref, wo_ref, gq_ref, wq_ref = refs[2 * N_OG:]
    slabs = []
    for sl in range(DIL_SLABS):
        l0, l1, l2 = (l_refs[g * DIL_SLABS + sl][...] for g in range(3))
        o0, o1, o2 = (o_refs[g * DIL_SLABS + sl][...] for g in range(3))
        mx = jnp.maximum(jnp.maximum(l0, l1), l2)
        e0, e1, e2 = jnp.exp(l0 - mx), jnp.exp(l1 - mx), jnp.exp(l2 - mx)
        slabs.append((e0 * o0 + e1 * o1 + e2 * o2) / (e0 + e1 + e2))
    o_a = jnp.concatenate(slabs, axis=-1)
    a = _dot(o_a, wa_ref[...])
    b = jnp.dot(ob_ref[...], wb_ref[...], preferred_element_type=F32)
    merged = _sigmoid(ga_ref[...]) * a + _sigmoid(gb_ref[...]) * b
    h1 = h_ref[...] + _dot(merged, wo_ref[...])
    return h1, _dot(_rms(h1, gq_ref[...]), wq_ref[...]).astype(BF16)


def _mix(o_g, l_g, o_b, ga, gb, h, w_a, w_b, w_o, norm_mem_q, w_mem_q, tm):
    rows, d = h.shape
    rt = lambda n: pl.BlockSpec((tm, n), lambda i: (i, 0))
    assert len(o_g) == len(l_g) == 3 * DIL_SLABS
    return pl.pallas_call(
        _mix_kernel,
        grid=(rows // tm,),
        in_specs=[rt(LANES)] * (6 * DIL_SLABS) + [rt(GDN_W), rt(d), rt(d), rt(d),
                                     _resident(w_a.shape), _resident(w_b.shape), _resident(w_o.shape),
                                     _resident((1, d)), _resident(w_mem_q.shape)],
        out_specs=[rt(d), rt(w_mem_q.shape[1])],
        out_shape=[jax.ShapeDtypeStruct((rows, d), F32), jax.ShapeDtypeStruct((rows, w_mem_q.shape[1]), BF16)],
        compiler_params=_cparams(("parallel",)),
        name="mix",
    )(*o_g, *l_g, o_b, ga, gb, h, w_a, w_b, w_o, norm_mem_q.reshape(1, d), w_mem_q)


def _mem_kv_kernel(x_ref, g_ref, w_ref, k_ref, v_ref):
    u = _rms(x_ref[...], g_ref[...]).astype(BF16)
    tm = x_ref.shape[0]
    n = MEM_HEADS * MEM_HEAD_DIM
    for o_ref, off in ((k_ref, 0), (v_ref, n)):
        r = jnp.dot(u, w_ref[:, off:off + n], preferred_element_type=F32)
        for h in range(MEM_HEADS):
            o_ref[pl.ds(h, tm, stride=MEM_HEADS), :] = r[:, h * MEM_HEAD_DIM:(h + 1) * MEM_HEAD_DIM]


def _mem_kv(mem2d, gain, w, tm):
    rows, d = mem2d.shape
    return pl.pallas_call(
        _mem_kv_kernel,
        grid=(rows // tm,),
        in_specs=[pl.BlockSpec((tm, d), lambda i: (i, 0)), _resident((1, d)), _resident(w.shape)],
        out_specs=[pl.BlockSpec((tm * MEM_HEADS, MEM_HEAD_DIM), lambda i: (i, 0))] * 2,
        out_shape=[jax.ShapeDtypeStruct((rows * MEM_HEADS, MEM_HEAD_DIM), F32)] * 2,
        compiler_params=_cparams(("parallel",)),
        name="mem_kv",
    )(mem2d, gain.reshape(1, d), w)


def _mem_attn_body(q, k_refs, v_refs):
    nb = len(q)
    M = k_refs[0].shape[0] // MEM_HEADS
    units = [(b, h) for b in range(nb) for h in range(MEM_HEADS)]
    s = [_dot_nt(q[b][:, h * MEM_HEAD_DIM:(h + 1) * MEM_HEAD_DIM], k_refs[b][pl.ds(h, M, stride=MEM_HEADS), :])
         * (MEM_HEAD_DIM ** -0.5) for b, h in units]
    p = [jnp.exp(x - jnp.max(x, axis=-1, keepdims=True)) for x in s]
    o = [_dot(p[u], v_refs[b][pl.ds(h, M, stride=MEM_HEADS), :]) / jnp.sum(p[u], axis=-1, keepdims=True)
         for u, (b, h) in enumerate(units)]
    return [jnp.concatenate(o[b * MEM_HEADS:(b + 1) * MEM_HEADS], axis=-1) for b in range(nb)]


def _mem_attn_kernel(q_ref, k_ref, v_ref, o_ref):
    nb = q_ref.shape[0]
    o = _mem_attn_body([q_ref[b] for b in range(nb)], [k_ref.at[b] for b in range(nb)],
                       [v_ref.at[b] for b in range(nb)])
    for b in range(nb):
        o_ref[b] = o[b]


def _mem_attn(qm, mem_k, mem_v, tm, nb):
    B, T, w = qm.shape
    kv_spec = pl.BlockSpec((nb,) + mem_k.shape[1:], lambda b, j: (b, 0, 0))
    return pl.pallas_call(
        _mem_attn_kernel,
        grid=(B // nb, T // tm),
        in_specs=[pl.BlockSpec((nb, tm, w), lambda b, j: (b, j, 0)), kv_spec, kv_spec],
        out_specs=pl.BlockSpec((nb, tm, w), lambda b, j: (b, j, 0)),
        out_shape=jax.ShapeDtypeStruct((B, T, w), F32),
        compiler_params=_cparams(("parallel", "parallel")),
        name="mem_attn",
    )(qm, mem_k, mem_v)


def _ffn_kernel(*refs, inject, emit_gate):
    if inject:
        h1_ref, om_ref, init_ref, fill_ref = refs[:4]
        rest = refs[4:]
    else:
        h1_ref, om_ref, init_ref = refs[:3]
        fill_ref, rest = None, refs[3:]
    _ffn_body(h1_ref[...], om_ref[...], init_ref, fill_ref, *rest, emit_gate=emit_gate)


def _ffn_body(h1, om, init_ref, fill_ref, wmo_ref, gf_ref, wup_ref, wc_ref, bc_ref, wd_ref, gfin_ref,
              y_ref, fc_ref, gs, *, emit_gate):
    tm = h1.shape[0]
    F = wd_ref.shape[0]
    PAD = SUBLANES

    @pl.when(pl.program_id(1) == 0)
    def _():
        gs[PAD - 2:PAD, :] = init_ref[...]

    h2 = h1 + _dot(om, wmo_ref[...])
    n = _rms(h2, gf_ref[...]).astype(BF16)
    gate = jnp.dot(n, wup_ref[:, :F], preferred_element_type=F32)
    if fill_ref is not None:
        r = lax.broadcasted_iota(jnp.int32, (tm, 1), 0)
        gate = jnp.where((r % SAMPLE_ROWS) >= SAMPLE_ROWS - 2, fill_ref[...], gate)
    gs[PAD:PAD + tm, :] = gate
    conv = (gs[PAD - 2:PAD - 2 + tm, :] * wc_ref[0:1, :] + gs[PAD - 1:PAD - 1 + tm, :] * wc_ref[1:2, :]
            + gate * wc_ref[2:3, :])
    last2 = gs[PAD + tm - 2:PAD + tm, :]
    gs[PAD - 2:PAD, :] = last2
    if emit_gate:
        fc_ref[...] = gate
    else:
        fc_ref[...] = last2
    up = jnp.dot(n, wup_ref[:, F:], preferred_element_type=F32)
    act = _silu(conv + bc_ref[...]) * up
    y = h2 + _dot(act, wd_ref[...])
    y_ref[...] = _rms(y, gfin_ref[...])


def _ffn(h1, om, init, fill, w_mo, norm_ffn, w_up, w_conv, b_conv, w_down, norm_final, tm, emit_gate):
    B, T, d = h1.shape
    F = w_down.shape[0]
    inject = fill is not None
    rt = lambda n: pl.BlockSpec((None, tm, n), lambda b, j: (b, j, 0))
    in_specs = [rt(d), rt(om.shape[-1]), pl.BlockSpec((None, 2, F), lambda b, j: (b, 0, 0))]
    args = [h1, om, init]
    if inject:
        in_specs.append(rt(F))
        args.append(fill)
    in_specs += [_resident(w_mo.shape), _resident((1, d)), _resident(w_up.shape), _resident(w_conv.shape),
                 _resident((1, F)), _resident(w_down.shape), _resident((1, d))]
    args += [w_mo, norm_ffn.reshape(1, d), w_up, w_conv, b_conv.reshape(1, F), w_down, norm_final.reshape(1, d)]
    if emit_gate:
        fc_spec, fc_shape = rt(F), jax.ShapeDtypeStruct((B, T, F), F32)
    else:
        fc_spec = pl.BlockSpec((None, 2, F), lambda b, j: (b, 0, 0))
        fc_shape = jax.ShapeDtypeStruct((B, 2, F), F32)
    return pl.pallas_call(
        functools.partial(_ffn_kernel, inject=inject, emit_gate=emit_gate),
        grid=(B, T // tm),
        in_specs=in_specs,
        out_specs=[rt(d), fc_spec],
        out_shape=[jax.ShapeDtypeStruct((B, T, d), F32), fc_shape],
        scratch_shapes=[pltpu.VMEM((tm + SUBLANES, F), F32)],
        compiler_params=_cparams(("parallel", "arbitrary")),
        name="ffn",
    )(*args)


def _post_kernel(*refs):
    n_mix = 2 * N_OG + 4
    mix_in, (k_ref, v_ref, init_ref), rest = refs[:n_mix], refs[n_mix:n_mix + 3], refs[n_mix + 3:]
    mix_w, ffn_rest = rest[:5], rest[5:]
    h1, qm = _mix_body(*mix_in, *mix_w)
    om, = _mem_attn_body([qm], [k_ref], [v_ref])
    _ffn_body(h1, om, init_ref, None, *ffn_rest, emit_gate=False)


def _post(o_g, l_g, o_b, ga, gb, h, mem_k, mem_v, init, w_a, w_b, w_o, norm_mem_q, w_mem_q, w_mo, norm_ffn, w_up,
          w_conv, b_conv, w_down, norm_final, tm):
    B, T, d = h.shape
    F = w_down.shape[0]
    rt = lambda n: pl.BlockSpec((None, tm, n), lambda b, j: (b, j, 0))
    per_b = lambda a: pl.BlockSpec((None,) + a.shape[1:], lambda b, j: (b, 0, 0))
    weights = [w_a, w_b, w_o, norm_mem_q.reshape(1, d), w_mem_q, w_mo, norm_ffn.reshape(1, d), w_up, w_conv,
               b_conv.reshape(1, F), w_down, norm_final.reshape(1, d)]
    return pl.pallas_call(
        _post_kernel,
        grid=(B, T // tm),
        in_specs=([rt(LANES)] * (2 * N_OG) + [rt(GDN_W), rt(d), rt(d), rt(d), per_b(mem_k), per_b(mem_v), per_b(init)]
                  + [_resident(w.shape) for w in weights]),
        out_specs=[rt(d), pl.BlockSpec((None, 2, F), lambda b, j: (b, 0, 0))],
        out_shape=[jax.ShapeDtypeStruct((B, T, d), F32), jax.ShapeDtypeStruct((B, 2, F), F32)],
        scratch_shapes=[pltpu.VMEM((tm + SUBLANES, F), F32)],
        compiler_params=_cparams(("parallel", "arbitrary")),
        name="post",
    )(*o_g, *l_g, o_b, ga, gb, h, mem_k, mem_v, init, *weights)


def kernel(x_prompt, x_sample, cache_dil0_kv, cache_dil1_kv, cache_dil2_kv, state_delta, state_delta_conv, cache_mem_k, cache_mem_v, state_ffn_conv, mem_prompt, rel_bias, norm_mix, w_in, w_conv_delta, a_log, dt_bias, norm_delta_out, w_branch_a, w_branch_b, w_out, norm_mem_q, norm_mem_kv, w_mem_q, w_mem_kv, w_mem_o, norm_ffn, w_ffn_up, w_ffn_conv, b_ffn_conv, w_ffn_down, norm_final):
    B, S, D = x_prompt.shape
    Bs, Ts, _ = x_sample.shape
    depth = w_in.shape[0]
    assert depth == 1 and D == D_MODEL and 3 <= Ts <= SAMPLE_ROWS - 2 and S % (16 * TILE_Q) == 0
    assert PAST_LEN >= max(w for w, _ in DIL_GROUPS)
    F = w_ffn_down.shape[1]
    M = mem_prompt.shape[1]
    l = 0
    w_arr = _arrange_w_in(w_in[l])
    w_a, w_b, w_o = (w.astype(BF16) for w in (w_branch_a[l], w_branch_b[l], w_out[l]))
    w_mq, w_mkv, w_mo = (w.astype(BF16) for w in (w_mem_q[l], w_mem_kv[l], w_mem_o[l]))
    w_up, w_dn = w_ffn_up[l].astype(BF16), w_ffn_down[l].astype(BF16)
    t_cat, t_cur = _prompt_bias_tables(rel_bias)

    xp = x_prompt.reshape(B * S, D)
    q, kv0, kv1, kv2, kt0, kt1, kt2, gq, gq_tail, z, ba, ga, gb = _in_proj(
        xp, norm_mix[l], w_arr, ROW_TILE, seq=S, conv_buf=jnp.zeros((B, 3, 3 * GDN_W), F32), w_conv=w_conv_delta[l])
    kvs = [kv.reshape(B, S, 2 * DIL_GW) for kv in (kv0, kv1, kv2)]
    q3 = q.reshape(B, S, 3 * DIL_GW)
    o_g, l_g = [], []
    for g in range(3):
        o_sl, l_sl = _dil_prompt(q3, kvs[g], g, t_cat[g], t_cur[g])
        o_g += o_sl
        l_g += l_sl
    o_b, delta_p = _gdn(gq.reshape(B, S, -1), ba.reshape(B, S, LANES), z.reshape(B, S, GDN_W),
                        jnp.zeros((B, GDN_HEADS, GDN_HEAD_DIM, GDN_HEAD_DIM), F32), a_log[l], dt_bias[l],
                        norm_delta_out[l])
    mk_p, mv_p = _mem_kv(mem_prompt.reshape(B * M, D), norm_mem_kv[l], w_mkv, ROW_TILE)
    mk_p, mv_p = (x.reshape(B, M * MEM_HEADS, MEM_HEAD_DIM) for x in (mk_p, mv_p))
    seq = lambda a: a.reshape(B, S, a.shape[-1])
    y_p, fconv_p = _post([seq(a) for a in o_g], [seq(a) for a in l_g], o_b, seq(ga), seq(gb), x_prompt, mk_p, mv_p,
                         jnp.zeros((B, 2, F), F32), w_a, w_b, w_o, norm_mem_q[l], w_mq, w_mo, norm_ffn[l], w_up,
                         w_ffn_conv[l], b_ffn_conv[l], w_dn, norm_final, ROW_TILE)
    p_out = ([kt[:, :, S - min(w, S):].reshape(B, 2, DIL_HPG, DIL_HEAD_DIM, min(w, S)).transpose(0, 4, 1, 2, 3)[None]
              for kt, (w, _) in zip((kt0, kt1, kt2), DIL_GROUPS)]
             + [delta_p[None], gq_tail[:, SUBLANES - 3:][None], mk_p.reshape(1, B, M, MEM_HEADS, MEM_HEAD_DIM),
                mv_p.reshape(1, B, M, MEM_HEADS, MEM_HEAD_DIM), fconv_p[None]])

    R = SAMPLE_ROWS
    xs = jnp.pad(x_sample, ((0, 0), (0, R - Ts), (0, 0))).reshape(Bs * R, D)
    q, kv0, kv1, kv2, gq, z, ba, ga, gb = _in_proj(xs, norm_mix[l], w_arr, Bs * R)
    kvn = [kv.reshape(Bs, R, 2 * DIL_GW) for kv in (kv0, kv1, kv2)]
    caches_t = [jnp.transpose(c[l], (0, 2, 3, 4, 1)).reshape(Bs, 2 * DIL_GW, c.shape[2])
                for c in (cache_dil0_kv, cache_dil1_kv, cache_dil2_kv)]
    tabc, tabn = _sample_bias_tables(rel_bias, Ts)
    o_g, l_g, new_caches = _dil_sample(q.reshape(Bs, R, 3 * DIL_GW), kvn, caches_t, tabc, tabn, Ts)
    o_b, delta_s = _gdn_sample(gq.reshape(Bs, R, -1), ba.reshape(Bs, R, LANES), z.reshape(Bs, R, GDN_W),
                               state_delta_conv[l], state_delta[l], w_conv_delta[l], a_log[l], dt_bias[l],
                               norm_delta_out[l], Ts)
    o_b = o_b.reshape(Bs * R, GDN_W).astype(BF16)
    h1, qm = _mix(o_g, l_g, o_b, ga, gb, xs, w_a, w_b, w_o, norm_mem_q[l], w_mq, Bs * R)
    om = _mem_attn(qm.reshape(Bs, R, -1), cache_mem_k[l].reshape(Bs, M * MEM_HEADS, MEM_HEAD_DIM),
                   cache_mem_v[l].reshape(Bs, M * MEM_HEADS, MEM_HEAD_DIM), R, 4 if Bs % 4 == 0 else 1)
    fst = state_ffn_conv[l]
    fill = jnp.concatenate([jnp.zeros((Bs, R - 2, F), F32),
                            jnp.concatenate([fst[1:], jnp.zeros((1, 2, F), F32)], axis=0)], axis=1)
    y_s, gate_s = _ffn(h1.reshape(1, Bs * R, D), om.reshape(1, Bs * R, -1), fst[:1], fill.reshape(1, Bs * R, F),
                       w_mo, norm_ffn[l], w_up, w_ffn_conv[l], b_ffn_conv[l], w_dn, norm_final, Bs * R, True)
    y_s = y_s.reshape(Bs, R, D)[:, :Ts]
    gq3 = gq.reshape(Bs, R, -1)
    s_out = ([nc.reshape(Bs, 2, DIL_HPG, DIL_HEAD_DIM, nc.shape[2]).transpose(0, 4, 1, 2, 3)[None]
              for nc in new_caches]
             + [delta_s[None], gq3[:, Ts - 3:Ts][None], gate_s.reshape(Bs, R, F)[:, Ts - 2:Ts][None]])

    return (y_p.reshape(B, S, D), y_s, *p_out, *s_out)
```

```python
import functools
import math

import jax
import jax.numpy as jnp
import numpy as np
from jax import lax
from jax.experimental import pallas as pl
from jax.experimental.pallas import tpu as pltpu

F32 = jnp.float32
BF16 = jnp.bfloat16

PAST_LEN = 8192
DIL_GROUPS = ((128, 1), (512, 4), (2048, 16))
DIL_HPG = 4
DIL_HEAD_DIM = 64
DIL_GW = DIL_HPG * DIL_HEAD_DIM
DIL_NK = 129
REL_BUCKETS = 32
REL_MAX_DIST = 2048
GDN_HEADS = 8
GDN_HEAD_DIM = 128
GDN_W = GDN_HEADS * GDN_HEAD_DIM
GDN_CHUNK = 64
MEM_HEADS = 4
MEM_HEAD_DIM = 128
EPS = 1e-6
NEG = -1e30

LANES = 128
SUBLANES = 8
TILE_Q = 128
D_MODEL = 1024
ROW_TILE = 256
SAMPLE_ROWS = SUBLANES
CONV_PAD = SUBLANES
VMEM_LIMIT = 56 * 1024 * 1024


def _cparams(sem):
    return pltpu.CompilerParams(dimension_semantics=sem, vmem_limit_bytes=VMEM_LIMIT)


def _resident(shape):
    nd = len(shape)
    return pl.BlockSpec(shape, lambda *_: (0,) * nd, pipeline_mode=pl.Buffered(1))


def _rms(x, gain_row):
    return x * lax.rsqrt(jnp.mean(x * x, axis=-1, keepdims=True) + EPS) * gain_row


def _dot(a, b):
    return jnp.dot(a.astype(BF16), b.astype(BF16), preferred_element_type=F32)


def _dot_nt(a, b):
    return lax.dot_general(a.astype(BF16), b.astype(BF16), (((1,), (1,)), ((), ())), preferred_element_type=F32)


def _split3(x):
    hi = x.astype(BF16)
    r1 = x - hi.astype(F32)
    mid = r1.astype(BF16)
    lo = (r1 - mid.astype(F32)).astype(BF16)
    return hi, mid, lo


def _l2n(x):
    return x * lax.rsqrt(jnp.sum(x * x, axis=-1, keepdims=True) + EPS)


def _sigmoid(x):
    return 1.0 / (1.0 + jnp.exp(-x))


def _silu(x):
    return x * _sigmoid(x)


def _softplus(x):
    return jnp.maximum(x, 0.0) + jnp.log(1.0 + jnp.exp(-jnp.abs(x)))


IN_SEGS = (("q", 3 * DIL_GW), ("kv0", 2 * DIL_GW), ("kv1", 2 * DIL_GW), ("kv2", 2 * DIL_GW),
           ("gq", 3 * GDN_W), ("z", GDN_W), ("ba", LANES), ("ga", D_MODEL), ("gb", D_MODEL))


OFF_Q, OFF_K, OFF_V = 0, 3 * DIL_GW, 6 * DIL_GW
OFF_GQ = 9 * DIL_GW
OFF_Z = OFF_GQ + 3 * GDN_W
OFF_BA = OFF_Z + GDN_W
OFF_GATES = OFF_BA + 2 * GDN_HEADS


def _arrange_w_in(w_in):
    w = w_in.astype(BF16)
    return w, w[:, OFF_GATES:]


GQ_SLABS = 3 * GDN_W // LANES


def _in_proj_kernel(x_ref, g_ref, w_ref, wg_ref, *rest, seq_tiles):
    if seq_tiles:
        (cb_ref, wc_ref, q_ref, kv0_ref, kv1_ref, kv2_ref, kt0_ref, kt1_ref, kt2_ref, gq_ref, tail_ref, z_ref,
         ba_ref, ga_ref, gb_ref, cs) = rest
        kt_refs = (kt0_ref, kt1_ref, kt2_ref)
    else:
        q_ref, kv0_ref, kv1_ref, kv2_ref, gq_ref, z_ref, ba_ref, ga_ref, gb_ref = rest
    tm = x_ref.shape[0]
    u = _rms(x_ref[...], g_ref[...]).astype(BF16)

    def seg(ref, off, n):
        return jnp.dot(u, ref[:, off:off + n], preferred_element_type=F32)

    def q_seg():
        q_ref[...] = seg(w_ref, OFF_Q, 3 * DIL_GW) * (DIL_HEAD_DIM ** -0.5)

    def kv_seg(g, part):
        kv_ref = (kv0_ref, kv1_ref, kv2_ref)[g]
        r = seg(w_ref, (OFF_K, OFF_V)[part] + g * DIL_GW, DIL_GW)
        kv_ref[:, part * DIL_GW:(part + 1) * DIL_GW] = r
        if seq_tiles:
            kt_refs[g][part * DIL_GW:(part + 1) * DIL_GW, :] = r.T[:, tm - kt_refs[g].shape[1]:]

    def ba_seg():
        ba_ref[...] = seg(w_ref, OFF_BA, LANES)

    def gate_seg(ref, off):
        ref[...] = seg(wg_ref, off, D_MODEL)

    others = ([(q_seg, 3 * DIL_GW)] + [(functools.partial(kv_seg, g, p), DIL_GW) for g in range(3) for p in range(2)]
              + [(functools.partial(gate_seg, ga_ref, 0), D_MODEL),
                 (functools.partial(gate_seg, gb_ref, D_MODEL), D_MODEL),
                 (ba_seg, LANES)])
    if not seq_tiles:
        for f, _ in others:
            f()
        for c in range(3):
            gq_ref[:, c * GDN_W:(c + 1) * GDN_W] = seg(w_ref, OFF_GQ + c * GDN_W, GDN_W)
        z_ref[...] = seg(w_ref, OFF_Z, GDN_W)
        return
    i = pl.program_id(0)
    first = (i % seq_tiles) == 0

    @pl.when(first)
    def _():
        for s in range(GQ_SLABS):
            cs[s, CONV_PAD - 3:CONV_PAD, :] = cb_ref[:, s * LANES:(s + 1) * LANES]

    @pl.when(jnp.logical_not(first))
    def _():
        for s in range(GQ_SLABS):
            cs[s, CONV_PAD - 3:CONV_PAD, :] = cs[s, CONV_PAD + tm - 3:CONV_PAD + tm, :]

    for c in range(3):
        r = seg(w_ref, OFF_GQ + c * GDN_W, GDN_W)
        for hh in range(GDN_HEADS):
            cs[c * GDN_HEADS + hh, CONV_PAD:CONV_PAD + tm, :] = r[:, hh * LANES:(hh + 1) * LANES]
    base = CONV_PAD - 3 + jnp.minimum(i, 0)

    def conv_slab(s):
        ls = slice(s * LANES, (s + 1) * LANES)
        y = cs[s, pl.ds(base, tm), :] * wc_ref[0:1, ls]
        for j in range(1, 4):
            y = y + cs[s, pl.ds(base + j, tm), :] * wc_ref[j:j + 1, ls]
        y = _silu(y)
        if s < GDN_HEADS:
            y = _l2n(y) * (GDN_HEAD_DIM ** -0.5)
        elif s < 2 * GDN_HEADS:
            y = _l2n(y)
        gq_ref[:, ls] = y
        tail_ref[:, ls] = cs[s, tm:tm + SUBLANES, :]

    def z_seg():
        z_ref[...] = _silu(seg(w_ref, OFF_Z, GDN_W))

    others.insert(len(others) - 1, (z_seg, GDN_W))
    total = sum(n for _, n in others)
    done, cols = 0, 0
    for f, n in others:
        f()
        cols += n
        upto = min(GQ_SLABS, (cols * GQ_SLABS + total - 1) // total)
        for s in range(done, upto):
            conv_slab(s)
        done = upto
    assert done == GQ_SLABS


def _in_proj(x2d, gain, w_arr, tm, seq=None, conv_buf=None, w_conv=None):
    rows, d = x2d.shape
    fused = seq is not None
    nt = seq // tm if fused else None
    names = [n for n, _ in IN_SEGS]
    widths = dict(IN_SEGS)
    row_spec = lambda n: pl.BlockSpec((tm, n), lambda i: (i, 0))
    out_specs, out_shape = [], []
    for n in names:
        out_specs.append(row_spec(widths[n]))
        out_shape.append(jax.ShapeDtypeStruct((rows, widths[n]), F32))
        if n == "kv2" and fused:
            for w, _ in DIL_GROUPS:
                keep = min(w, seq)
                bw = min(tm, keep)
                skip = nt - keep // bw if bw == tm else nt - 1
                out_specs.append(pl.BlockSpec((None, 2 * DIL_GW, bw),
                                              lambda i, skip=skip: (i // nt, 0, jnp.maximum(i % nt - skip, 0))))
                out_shape.append(jax.ShapeDtypeStruct((rows // seq, 2 * DIL_GW, keep), F32))
        if n == "gq" and fused:
            out_specs.append(pl.BlockSpec((None, SUBLANES, 3 * GDN_W), lambda i: (i // nt, 0, 0)))
            out_shape.append(jax.ShapeDtypeStruct((rows // seq, SUBLANES, 3 * GDN_W), F32))
    in_specs = [pl.BlockSpec((tm, d), lambda i: (i, 0)), _resident((1, d)), _resident(w_arr[0].shape),
                _resident(w_arr[1].shape)]
    args = [x2d, gain.reshape(1, d), *w_arr]
    if fused:
        in_specs += [pl.BlockSpec((None, 3, 3 * GDN_W), lambda i: (i // nt, 0, 0)), _resident(w_conv.shape)]
        args += [conv_buf, w_conv]
    return pl.pallas_call(
        functools.partial(_in_proj_kernel, seq_tiles=nt),
        grid=(rows // tm,),
        in_specs=in_specs,
        out_specs=out_specs,
        out_shape=out_shape,
        scratch_shapes=[pltpu.VMEM((GQ_SLABS, CONV_PAD + tm, LANES), F32)] if fused else [],
        compiler_params=_cparams(("arbitrary",)),
        name="in_proj",
    )(*args)


def _rel_bucket(dist):
    exact = REL_BUCKETS // 2
    d = jnp.maximum(dist, 1).astype(F32)
    large = exact + (jnp.log(d / exact) / math.log(REL_MAX_DIST / exact) * (REL_BUCKETS - exact)).astype(jnp.int32)
    return jnp.where(dist < exact, dist, jnp.minimum(large, REL_BUCKETS - 1))


def _group_bias(rel_bias, g):
    dil = DIL_GROUPS[g][1]
    dist = dil * jnp.arange(DIL_NK, dtype=jnp.int32)
    tab = rel_bias[_rel_bucket(dist)]
    return tab[:, g * DIL_HPG:(g + 1) * DIL_HPG].T.astype(F32)


def _toeplitz(v, n, width):
    h, L = v.shape
    return jnp.tile(v, (1, n))[:, :n * (L - 1)].reshape(h, n, L - 1)[:, :, :width]


def _prompt_bias_tables(rel_bias):
    cat, cur = [], []
    for g in range(3):
        bg = _group_bias(rel_bias, g)
        v = jnp.concatenate([bg[:, ::-1], jnp.full((DIL_HPG, 3 * TILE_Q - DIL_NK), NEG, F32)], axis=1)
        t = _toeplitz(v, TILE_Q, 2 * TILE_Q)
        cat.append(t)
        cur.append(t[:, :, TILE_Q:])
    return cat, cur


DIL_TIF = 2
DIL_SLABS = DIL_GW // LANES


def _dil_prompt_kernel(q0_ref, q1_ref, k0_ref, k1_ref, v0_ref, v1_ref, tcat_ref, tcur_ref,
                       o0_ref, o1_ref, l0_ref, l1_ref, *, dil):
    S = q0_ref.shape[0]
    nb = S // dil // TILE_Q
    q_refs, k_refs, v_refs = (q0_ref, q1_ref), (k0_ref, k1_ref), (v0_ref, v1_ref)
    o_refs, l_refs = (o0_ref, o1_ref), (l0_ref, l1_ref)
    even = lax.broadcasted_iota(jnp.int32, (TILE_Q, LANES), 1) < DIL_HEAD_DIM

    def rows(r, t):
        start = r + dil * TILE_Q * t
        return pl.ds(start, TILE_Q, stride=dil) if dil > 1 else pl.ds(start, TILE_Q)

    tiles = [(r, t) for r in range(dil) for t in range(nb)]
    for i0 in range(0, len(tiles), DIL_TIF):
        grp = tiles[i0:i0 + DIL_TIF]
        qm, kc, vc = {}, {}, {}
        for ti, (r, t) in enumerate(grp):
            for sl in range(DIL_SLABS):
                qf = q_refs[sl][rows(r, t), :]
                qm[ti, 2 * sl] = jnp.where(even, qf, 0.0).astype(BF16)
                qm[ti, 2 * sl + 1] = jnp.where(even, 0.0, qf).astype(BF16)
                kc[ti, sl] = k_refs[sl][rows(r, t), :].astype(BF16)
                vc[ti, sl] = v_refs[sl][rows(r, t), :].astype(BF16)
                if t > 0:
                    kc[ti, sl] = jnp.concatenate([k_refs[sl][rows(r, t - 1), :].astype(BF16), kc[ti, sl]], axis=0)
                    vc[ti, sl] = jnp.concatenate([v_refs[sl][rows(r, t - 1), :].astype(BF16), vc[ti, sl]], axis=0)
        units = [(ti, h) for ti in range(len(grp)) for h in range(DIL_HPG)]
        s = [_dot_nt(qm[ti, h], kc[ti, h // 2]) + (tcat_ref[h] if grp[ti][1] > 0 else tcur_ref[h]) for ti, h in units]
        m = [jnp.max(x, axis=-1, keepdims=True) for x in s]
        p = [jnp.exp(x - mx) for x, mx in zip(s, m)]
        l = [jnp.sum(x, axis=-1, keepdims=True) for x in p]
        pv = [jnp.dot(p[u].astype(BF16), vc[ti, h // 2], preferred_element_type=F32) for u, (ti, h) in enumerate(units)]
        o = [pv[u] / l[u] for u in range(len(units))]
        lse = [m[u] + jnp.log(l[u]) for u in range(len(units))]
        for ti, (r, t) in enumerate(grp):
            for sl in range(DIL_SLABS):
                ue, uo = ti * DIL_HPG + 2 * sl, ti * DIL_HPG + 2 * sl + 1
                o_refs[sl][rows(r, t), :] = jnp.where(even, o[ue], o[uo])
                l_refs[sl][rows(r, t), :] = jnp.where(even, lse[ue], lse[uo])


def _dil_prompt(q, kv, g, t_cat, t_cur):
    B, S, _ = q.shape
    dil = DIL_GROUPS[g][1]
    slab = lambda c: pl.BlockSpec((None, S, LANES), lambda b: (b, 0, c))
    nq, nk = g * DIL_SLABS, 0
    outs = pl.pallas_call(
        functools.partial(_dil_prompt_kernel, dil=dil),
        grid=(B,),
        in_specs=[slab(nq), slab(nq + 1), slab(nk), slab(nk + 1), slab(nk + 2), slab(nk + 3),
                  _resident(t_cat.shape), _resident(t_cur.shape)],
        out_specs=[slab(0)] * 4,
        out_shape=[jax.ShapeDtypeStruct((B, S, LANES), F32)] * 4,
        compiler_params=_cparams(("parallel",)),
        name=f"dil_prompt_g{g}",
    )(q, q, kv, kv, kv, kv, t_cat, t_cur)
    o0, o1, l0, l1 = (x.reshape(B * S, LANES) for x in outs)
    return [o0, o1], [l0, l1]


def _sample_bias_tables(rel_bias, t_real):
    R = SAMPLE_ROWS
    tabc, tabn = [], []
    t_i = np.arange(R)[:, None]
    u_i = np.arange(R)[None, :]
    for g, (w, dil) in enumerate(DIL_GROUPS):
        bg = _group_bias(rel_bias, g)
        base = bg[:, ::-1][:, :TILE_Q]
        t0 = jnp.concatenate([base[:, :, None], jnp.full((DIL_HPG, TILE_Q, dil - 1), NEG, F32)], axis=2)
        t0 = t0.reshape(DIL_HPG, w)
        tc = _toeplitz(jnp.concatenate([t0, jnp.full((DIL_HPG, R), NEG, F32)], axis=1), R, w)
        tabc.append(jnp.where((t_i < t_real)[None], tc, 0.0).reshape(DIL_HPG * R, w))
        tn = jnp.full((DIL_HPG, R, R), NEG, F32)
        for j in range(-(-t_real // dil)):
            hit = (t_i - u_i == j * dil) & (t_i < t_real)
            tn = jnp.where(hit[None], bg[:, j][:, None, None], tn)
        tn = jnp.where((t_i >= t_real)[None], 0.0, tn)
        tabn.append(tn.reshape(DIL_HPG * R, R))
    return tabc, jnp.stack(tabn)


def _dil_sample_kernel(q_ref, n0_ref, n1_ref, n2_ref, c0_ref, c1_ref, c2_ref, tc0_ref, tc1_ref, tc2_ref, tn_ref,
                       o_ref, l_ref, oc0_ref, oc1_ref, oc2_ref, *, t_real):
    R = SAMPLE_ROWS
    rows = lax.broadcasted_iota(jnp.int32, (DIL_HPG * R, DIL_GW), 0)
    lanes = lax.broadcasted_iota(jnp.int32, (DIL_HPG * R, DIL_GW), 1)
    head_mask = (lanes // DIL_HEAD_DIM) == (rows // R)
    lane_f = lax.broadcasted_iota(jnp.int32, (2 * DIL_GW, LANES), 1)
    keep = lane_f < LANES - t_real
    sel_l = lax.broadcasted_iota(jnp.int32, (LANES, R), 0)
    sel_u = lax.broadcasted_iota(jnp.int32, (LANES, R), 1)
    selT = ((sel_l == sel_u + LANES - t_real) & (sel_u < t_real)).astype(BF16)
    groups = ((n0_ref, c0_ref, tc0_ref, oc0_ref), (n1_ref, c1_ref, tc1_ref, oc1_ref), (n2_ref, c2_ref, tc2_ref, oc2_ref))

    def fold_heads(x):
        x = jnp.where(head_mask, x, 0.0)
        return x[0:R] + x[R:2 * R] + x[2 * R:3 * R] + x[3 * R:4 * R]

    G3 = range(3)
    kvn = [groups[g][0][...] for g in G3]
    q_bd = [jnp.where(head_mask, jnp.concatenate([q_ref[:, g * DIL_GW:(g + 1) * DIL_GW]] * DIL_HPG, axis=0), 0.0)
            for g in G3]
    s_c = [_dot(q_bd[g], groups[g][1][:DIL_GW, :]) + groups[g][2][...] for g in G3]
    s_n = [_dot_nt(q_bd[g], kvn[g][:, :DIL_GW]) + tn_ref[g] for g in G3]
    m = [jnp.maximum(jnp.max(s_c[g], axis=-1, keepdims=True), jnp.max(s_n[g], axis=-1, keepdims=True)) for g in G3]
    p_c = [jnp.exp(s_c[g] - m[g]) for g in G3]
    p_n = [jnp.exp(s_n[g] - m[g]) for g in G3]
    l = [jnp.sum(p_c[g], axis=-1, keepdims=True) + jnp.sum(p_n[g], axis=-1, keepdims=True) for g in G3]
    acc = [(_dot_nt(p_c[g], groups[g][1][DIL_GW:, :]) + _dot(p_n[g], kvn[g][:, DIL_GW:])) / l[g] for g in G3]
    for g in G3:
        o_ref[:, g * DIL_GW:(g + 1) * DIL_GW] = fold_heads(acc[g])
        l_ref[:, g * DIL_GW:(g + 1) * DIL_GW] = fold_heads(jnp.broadcast_to(m[g] + jnp.log(l[g]), acc[g].shape))
    for g, (n_ref, c_ref, tc_ref, oc_ref) in enumerate(groups):
        W = c_ref.shape[1]
        hi, mid, lo = _split3(kvn[g])
        tail = (jnp.dot(selT, hi, preferred_element_type=F32) + jnp.dot(selT, mid, preferred_element_type=F32)
                + jnp.dot(selT, lo, preferred_element_type=F32)).T
        nxt = pltpu.roll(c_ref[:, 0:LANES], LANES - t_real, axis=1)
        for c in range(W // LANES):
            cur = nxt
            nxt = (pltpu.roll(c_ref[:, (c + 1) * LANES:(c + 2) * LANES], LANES - t_real, axis=1)
                   if (c + 1) * LANES < W else tail)
            oc_ref[:, c * LANES:(c + 1) * LANES] = jnp.where(keep, cur, nxt)


def _dil_sample(q, kvn, caches_t, tabc, tabn, t_real):
    B = q.shape[0]
    row = lambda n: pl.BlockSpec((None, SAMPLE_ROWS, n), lambda b: (b, 0, 0))
    cspecs = [pl.BlockSpec((None,) + c.shape[1:], lambda b: (b, 0, 0)) for c in caches_t]
    for g, (w, dil) in enumerate(DIL_GROUPS):
        assert caches_t[g].shape == (B, 2 * DIL_GW, w) and w // dil == TILE_Q
    out_spec = row(3 * DIL_GW)
    o, lse, *new_caches = pl.pallas_call(
        functools.partial(_dil_sample_kernel, t_real=t_real),
        grid=(B,),
        in_specs=([row(3 * DIL_GW)] + [row(2 * DIL_GW)] * 3 + cspecs + [_resident(t.shape) for t in tabc]
                  + [_resident(tabn.shape)]),
        out_specs=[out_spec, out_spec] + cspecs,
        out_shape=([jax.ShapeDtypeStruct((B, SAMPLE_ROWS, 3 * DIL_GW), F32)] * 2
                   + [jax.ShapeDtypeStruct(c.shape, F32) for c in caches_t]),
        compiler_params=_cparams(("parallel",)),
        name="dil_sample",
    )(q, *kvn, *caches_t, *tabc, tabn)
    o = o.reshape(B * SAMPLE_ROWS, 3 * DIL_GW)
    lse = lse.reshape(B * SAMPLE_ROWS, 3 * DIL_GW)
    n = 3 * DIL_SLABS
    return ([o[:, i * LANES:(i + 1) * LANES] for i in range(n)],
            [lse[:, i * LANES:(i + 1) * LANES] for i in range(n)], new_caches)


GDN_HPS = GDN_HEADS
GDN_SW = GDN_HPS * GDN_HEAD_DIM
GDN_TPI = 4
GDN_ROWS = 1024


def _gdn_kernel(q_ref, k_ref, v_ref, ba_ref, alog_ref, dtb_ref, z_ref, gn_ref, tri_ref, s0_ref, o_ref, s_ref, carry_s,
                *, tpi):
    T = q_ref.shape[0]
    C = GDN_CHUNK
    D = GDN_HEAD_DIM
    ri = lax.broadcasted_iota(jnp.int32, (TILE_Q, TILE_Q), 0)
    ci = lax.broadcasted_iota(jnp.int32, (TILE_Q, TILE_Q), 1)
    same = (ri // C) == (ci // C)
    incl = same & (ri >= ci)
    strict = same & (ri > ci)
    eye = (ri == ci).astype(F32)
    lane, row = ci, ri
    gain = gn_ref[...]
    zpad = jnp.zeros((C, D), F32)
    HH = range(GDN_HPS)
    hsl = [slice(hh * D, (hh + 1) * D) for hh in HH]
    RW = tpi * TILE_Q
    UU = [(tt, hh) for tt in range(tpi) for hh in HH]
    UI = range(len(UU))

    def tile(i, S):
        r0 = pl.multiple_of(i * RW, RW)
        rt = [r0 + tt * TILE_Q for tt in range(tpi)]
        q = [q_ref[pl.ds(rt[tt], TILE_Q), hsl[hh]] for tt, hh in UU]
        k = [k_ref[pl.ds(rt[tt], TILE_Q), hsl[hh]] for tt, hh in UU]
        v = [v_ref[pl.ds(rt[tt], TILE_Q), hsl[hh]] for tt, hh in UU]
        tri = tri_ref[...]
        beta_all, G_all = [], []
        for tt in range(tpi):
            ba = ba_ref[pl.ds(rt[tt], TILE_Q), :]
            beta_all.append(_sigmoid(ba))
            gh, gm, gl = _split3(-jnp.exp(alog_ref[...]) * _softplus(ba + dtb_ref[...]))
            G_all.append(jnp.dot(tri, gh, preferred_element_type=F32) + jnp.dot(tri, gm, preferred_element_type=F32)
                         + jnp.dot(tri, gl, preferred_element_type=F32))
        head = [hh for _, hh in UU]
        bc = [jnp.sum(jnp.where(lane == head[u], beta_all[UU[u][0]], 0.0), axis=-1, keepdims=True) for u in UI]
        Gc = [jnp.broadcast_to(jnp.sum(jnp.where(lane == head[u] + GDN_HEADS, G_all[UU[u][0]], 0.0), axis=-1,
                                       keepdims=True), (TILE_Q, TILE_Q)) for u in UI]
        gamma = [jnp.exp(jnp.where(incl, Gc[u] - Gc[u].T, NEG)) for u in UI]
        kk = [_dot_nt(k[u], k[u]) for u in UI]
        qk = [_dot_nt(q[u], k[u]) for u in UI]
        X = [jnp.where(strict, bc[u] * kk[u] * gamma[u], 0.0) for u in UI]
        P = [eye - X[u] for u in UI]
        for _ in range(int(math.log2(C)) - 1):
            X = [_dot(X[u], X[u]) for u in UI]
            P = [P[u] + _dot(P[u], X[u]) for u in UI]
        eG = [jnp.exp(Gc[u]) for u in UI]
        rhs = [jnp.concatenate([v[u] * bc[u], k[u] * (bc[u] * eG[u])], axis=-1) for u in UI]
        sol = [rhs[u] + _dot(P[u] - eye, rhs[u]) for u in UI]
        a_in = [qk[u] * gamma[u] for u in UI]
        q_dec = [q[u] * eG[u] for u in UI]
        kdT = [(k[u] * jnp.exp(jnp.where(row < C, Gc[u][C - 1:C, :], Gc[u][2 * C - 1:2 * C, :]) - Gc[u])).T
               for u in UI]
        S = list(S)
        for tt in range(tpi):
            us = [tt * GDN_HPS + hh for hh in HH]
            oq, vn = [[] for _ in HH], [[] for _ in HH]
            for c in range(TILE_Q // C):
                cs = slice(c * C, (c + 1) * C)
                r = [_dot(jnp.concatenate([sol[us[hh]][cs, D:], q_dec[us[hh]][cs]], axis=0), S[hh]) for hh in HH]
                for hh in HH:
                    oq[hh].append(r[hh][C:])
                    vn[hh].append(sol[us[hh]][cs, :D] - r[hh][:C])
                vpad = [jnp.concatenate([vn[hh][c], zpad] if c == 0 else [zpad, vn[hh][c]], axis=0) for hh in HH]
                S = [S[hh] * jnp.exp(Gc[us[hh]][(c + 1) * C - 1:(c + 1) * C, :]) + _dot(kdT[us[hh]], vpad[hh])
                     for hh in HH]
            o = [jnp.concatenate(oq[hh], axis=0) + _dot(a_in[us[hh]], jnp.concatenate(vn[hh], axis=0)) for hh in HH]
            outs = [_rms(o[hh], gain) * z_ref[pl.ds(rt[tt], TILE_Q), hsl[hh]] for hh in HH]
            o_ref[pl.ds(rt[tt], TILE_Q), :] = jnp.concatenate(outs, axis=-1).astype(o_ref.dtype)
        return tuple(S)

    @pl.when(pl.program_id(1) == 0)
    def _():
        carry_s[...] = s0_ref[...]

    S = lax.fori_loop(0, T // RW, tile, tuple(carry_s[hh] for hh in HH))
    for hh in HH:
        carry_s[hh] = S[hh]
        s_ref[hh] = S[hh]


def _gdn(gq, ba, z, s0, a_log, dt_bias, norm_out):
    B, T, _ = gq.shape
    H = GDN_HEADS
    pad16 = lambda x: jnp.concatenate([jnp.zeros((H,), F32), x.astype(F32), jnp.zeros((LANES - 2 * H,), F32)])
    assert GDN_HPS == H
    tb = min(GDN_ROWS, T)
    tpi = GDN_TPI if (tb // TILE_Q) % GDN_TPI == 0 else 1
    col = lambda off: pl.BlockSpec((None, tb, GDN_SW), lambda b, j: (b, j, off))
    sblk = pl.BlockSpec((None, H, GDN_HEAD_DIM, GDN_HEAD_DIM), lambda b, j: (b, 0, 0, 0))
    r = np.arange(TILE_Q)
    tri = jnp.asarray((r[:, None] >= r[None, :]) & (r[:, None] // GDN_CHUNK == r[None, :] // GDN_CHUNK), BF16)
    o, s_new = pl.pallas_call(
        functools.partial(_gdn_kernel, tpi=tpi),
        grid=(B, T // tb),
        in_specs=[col(0), col(1), col(2),
                  pl.BlockSpec((None, tb, LANES), lambda b, j: (b, j, 0)),
                  _resident((1, LANES)), _resident((1, LANES)),
                  col(0), _resident((1, LANES)), _resident((TILE_Q, TILE_Q)), sblk],
        out_specs=[col(0), sblk],
        out_shape=[jax.ShapeDtypeStruct((B, T, GDN_W), BF16),
                   jax.ShapeDtypeStruct((B, H, GDN_HEAD_DIM, GDN_HEAD_DIM), F32)],
        scratch_shapes=[pltpu.VMEM((H, GDN_HEAD_DIM, GDN_HEAD_DIM), F32)],
        compiler_params=_cparams(("parallel", "arbitrary")),
        name="gdn",
    )(gq, gq, gq, ba, pad16(a_log).reshape(1, LANES), pad16(dt_bias).reshape(1, LANES), z,
      norm_out.reshape(1, LANES), tri, s0)
    return o, s_new


def _gdn_sample_kernel(x_ref, b_ref, w_ref, ba_ref, alog_ref, dtb_ref, z_ref, gn_ref, s0_ref, o_ref, s_ref, xs,
                       *, t_real):
    R, D, H = SAMPLE_ROWS, GDN_HEAD_DIM, GDN_HEADS
    nb = x_ref.shape[0]
    ri = lax.broadcasted_iota(jnp.int32, (R, R), 0)
    ci = lax.broadcasted_iota(jnp.int32, (R, R), 1)
    incl, strict = ri >= ci, ri > ci
    eye = (ri == ci).astype(F32)
    tri = incl.astype(BF16)
    er = lax.broadcasted_iota(jnp.int32, (LANES, LANES), 0)
    ec = lax.broadcasted_iota(jnp.int32, (LANES, LANES), 1)
    eye_l = (er == ec).astype(BF16)
    live = lax.broadcasted_iota(jnp.int32, (R, LANES), 0) < t_real
    d32 = functools.partial(jnp.dot, preferred_element_type=F32)
    nt = lambda a, b: lax.dot_general(a, b, (((1,), (1,)), ((), ())), preferred_element_type=F32)
    gain = gn_ref[...]
    y, beta_all, G_all, G_allT = [], [], [], []
    for s in range(nb):
        xs[s, CONV_PAD - 3:CONV_PAD, :] = b_ref[s]
        xs[s, CONV_PAD:CONV_PAD + R, :] = x_ref[s]
        ys = xs[s, CONV_PAD - 3:CONV_PAD - 3 + R, :] * w_ref[0:1, :]
        for j in range(1, 4):
            ys = ys + xs[s, CONV_PAD - 3 + j:CONV_PAD - 3 + j + R, :] * w_ref[j:j + 1, :]
        y.append(_silu(ys))
        ba = ba_ref[s]
        beta_all.append(jnp.where(live, _sigmoid(ba), 0.0))
        gh, gm, gl = _split3(jnp.where(live, -jnp.exp(alog_ref[...]) * _softplus(ba + dtb_ref[...]), 0.0))
        G = d32(tri, gh) + d32(tri, gm) + d32(tri, gl)
        th, tm, tl = _split3(G)
        G_all.append(G)
        G_allT.append(nt(eye_l, th) + nt(eye_l, tm) + nt(eye_l, tl))
    US = [(s, h) for s in range(nb) for h in range(H)]
    UI = range(len(US))

    q = [_l2n(y[s][:, h * D:(h + 1) * D]) * (D ** -0.5) for s, h in US]
    k = [_l2n(y[s][:, GDN_W + h * D:GDN_W + (h + 1) * D]) for s, h in US]
    v = [y[s][:, 2 * GDN_W + h * D:2 * GDN_W + (h + 1) * D] for s, h in US]
    bc = [beta_all[s][:, h:h + 1] for s, h in US]
    Gc = [G_all[s][:, H + h:H + h + 1] for s, h in US]
    gamma = [jnp.exp(jnp.where(incl, Gc[u] - G_allT[s][H + h:H + h + 1, :], NEG)) for u, (s, h) in enumerate(US)]
    kk = [_dot_nt(k[u], k[u]) for u in UI]
    qk = [_dot_nt(q[u], k[u]) for u in UI]
    X = [jnp.where(strict, bc[u] * kk[u] * gamma[u], 0.0) for u in UI]
    P = [eye - X[u] for u in UI]
    for _ in range(int(math.log2(R)) - 1):
        X = [_dot(X[u], X[u]) for u in UI]
        P = [P[u] + _dot(P[u], X[u]) for u in UI]
    eG = [jnp.exp(Gc[u]) for u in UI]
    rhs = [jnp.concatenate([v[u] * bc[u], k[u] * (bc[u] * eG[u])], axis=-1) for u in UI]
    sol = [rhs[u] + _dot(P[u] - eye, rhs[u]) for u in UI]
    S = [s0_ref[s, h] for s, h in US]
    r = [_dot(jnp.concatenate([sol[u][:, D:], q[u] * eG[u]], axis=0), S[u]) for u in UI]
    v_new = [sol[u][:, :D] - r[u][:R] for u in UI]
    o = [r[u][R:] + _dot(qk[u] * gamma[u], v_new[u]) for u in UI]
    kdT = [nt(eye_l, (k[u] * jnp.exp(Gc[u][R - 1:R, :] - Gc[u])).astype(BF16)) for u in UI]
    for u, (s, h) in enumerate(US):
        s_ref[s, h] = S[u] * jnp.exp(Gc[u][R - 1:R, :]) + _dot(kdT[u], v_new[u])
    for s in range(nb):
        o_ref[s] = jnp.concatenate([_rms(o[s * H + h], gain) * _silu(z_ref[s, :, h * D:(h + 1) * D])
                                    for h in range(H)], axis=-1)


GDN_SAMPLE_SEQS = 4


def _gdn_sample(gq, ba, z, conv_buf, s0, w_conv, a_log, dt_bias, norm_out, t_real):
    B, R, _ = gq.shape
    H = GDN_HEADS
    pad16 = lambda x: jnp.concatenate([jnp.zeros((H,), F32), x.astype(F32), jnp.zeros((LANES - 2 * H,), F32)])
    nb = GDN_SAMPLE_SEQS if B % GDN_SAMPLE_SEQS == 0 else 1
    blk = lambda *s: pl.BlockSpec((nb,) + s, lambda b: (b,) + (0,) * len(s))
    return pl.pallas_call(
        functools.partial(_gdn_sample_kernel, t_real=t_real),
        grid=(B // nb,),
        in_specs=[blk(R, 3 * GDN_W), blk(3, 3 * GDN_W), _resident(w_conv.shape), blk(R, LANES),
                  _resident((1, LANES)), _resident((1, LANES)), blk(R, GDN_W), _resident((1, LANES)),
                  blk(H, GDN_HEAD_DIM, GDN_HEAD_DIM)],
        out_specs=[blk(R, GDN_W), blk(H, GDN_HEAD_DIM, GDN_HEAD_DIM)],
        out_shape=[jax.ShapeDtypeStruct((B, R, GDN_W), F32),
                   jax.ShapeDtypeStruct((B, H, GDN_HEAD_DIM, GDN_HEAD_DIM), F32)],
        scratch_shapes=[pltpu.VMEM((nb, CONV_PAD + R, 3 * GDN_W), F32)],
        compiler_params=_cparams(("parallel",)),
        name="gdn_sample",
    )(gq, conv_buf, w_conv, ba, pad16(a_log).reshape(1, LANES), pad16(dt_bias).reshape(1, LANES), z,
      norm_out.reshape(1, LANES), s0)


N_OG = 3 * DIL_SLABS


def _mix_kernel(*refs):
    h1, qm = _mix_body(*refs[:2 * N_OG + 9])
    h1_ref, qm_ref = refs[2 * N_OG + 9:]
    h1_ref[...] = h1
    qm_ref[...] = qm


def _mix_body(*refs):
    o_refs, l_refs = refs[:N_OG], refs[N_OG:2 * N_OG]
    ob_ref, ga_ref, gb_ref, h_ref, wa_ref, wb_ref, wo_ref, gq_ref, wq_ref = refs[2 * N_OG:]
    slabs = []
    for sl in range(DIL_SLABS):
        l0, l1, l2 = (l_refs[g * DIL_SLABS + sl][...] for g in range(3))
        o0, o1, o2 = (o_refs[g * DIL_SLABS + sl][...] for g in range(3))
        mx = jnp.maximum(jnp.maximum(l0, l1), l2)
        e0, e1, e2 = jnp.exp(l0 - mx), jnp.exp(l1 - mx), jnp.exp(l2 - mx)
        slabs.append((e0 * o0 + e1 * o1 + e2 * o2) / (e0 + e1 + e2))
    o_a = jnp.concatenate(slabs, axis=-1)
    a = _dot(o_a, wa_ref[...])
    b = jnp.dot(ob_ref[...], wb_ref[...], preferred_element_type=F32)
    merged = _sigmoid(ga_ref[...]) * a + _sigmoid(gb_ref[...]) * b
    h1 = h_ref[...] + _dot(merged, wo_ref[...])
    return h1, _dot(_rms(h1, gq_ref[...]), wq_ref[...]).astype(BF16)


def _mix(o_g, l_g, o_b, ga, gb, h, w_a, w_b, w_o, norm_mem_q, w_mem_q, tm):
    rows, d = h.shape
    rt = lambda n: pl.BlockSpec((tm, n), lambda i: (i, 0))
    assert len(o_g) == len(l_g) == 3 * DIL_SLABS
    return pl.pallas_call(
        _mix_kernel,
        grid=(rows // tm,),
        in_specs=[rt(LANES)] * (6 * DIL_SLABS) + [rt(GDN_W), rt(d), rt(d), rt(d),
                                     _resident(w_a.shape), _resident(w_b.shape), _resident(w_o.shape),
                                     _resident((1, d)), _resident(w_mem_q.shape)],
        out_specs=[rt(d), rt(w_mem_q.shape[1])],
        out_shape=[jax.ShapeDtypeStruct((rows, d), F32), jax.ShapeDtypeStruct((rows, w_mem_q.shape[1]), BF16)],
        compiler_params=_cparams(("parallel",)),
        name="mix",
    )(*o_g, *l_g, o_b, ga, gb, h, w_a, w_b, w_o, norm_mem_q.reshape(1, d), w_mem_q)


def _mem_kv_kernel(x_ref, g_ref, w_ref, k_ref, v_ref):
    u = _rms(x_ref[...], g_ref[...]).astype(BF16)
    tm = x_ref.shape[0]
    n = MEM_HEADS * MEM_HEAD_DIM
    for o_ref, off in ((k_ref, 0), (v_ref, n)):
        r = jnp.dot(u, w_ref[:, off:off + n], preferred_element_type=F32)
        for h in range(MEM_HEADS):
            o_ref[pl.ds(h, tm, stride=MEM_HEADS), :] = r[:, h * MEM_HEAD_DIM:(h + 1) * MEM_HEAD_DIM]


def _mem_kv(mem2d, gain, w, tm):
    rows, d = mem2d.shape
    return pl.pallas_call(
        _mem_kv_kernel,
        grid=(rows // tm,),
        in_specs=[pl.BlockSpec((tm, d), lambda i: (i, 0)), _resident((1, d)), _resident(w.shape)],
        out_specs=[pl.BlockSpec((tm * MEM_HEADS, MEM_HEAD_DIM), lambda i: (i, 0))] * 2,
        out_shape=[jax.ShapeDtypeStruct((rows * MEM_HEADS, MEM_HEAD_DIM), F32)] * 2,
        compiler_params=_cparams(("parallel",)),
        name="mem_kv",
    )(mem2d, gain.reshape(1, d), w)


def _mem_attn_body(q, k_refs, v_refs):
    nb = len(q)
    M = k_refs[0].shape[0] // MEM_HEADS
    units = [(b, h) for b in range(nb) for h in range(MEM_HEADS)]
    s = [_dot_nt(q[b][:, h * MEM_HEAD_DIM:(h + 1) * MEM_HEAD_DIM], k_refs[b][pl.ds(h, M, stride=MEM_HEADS), :])
         * (MEM_HEAD_DIM ** -0.5) for b, h in units]
    p = [jnp.exp(x - jnp.max(x, axis=-1, keepdims=True)) for x in s]
    o = [_dot(p[u], v_refs[b][pl.ds(h, M, stride=MEM_HEADS), :]) / jnp.sum(p[u], axis=-1, keepdims=True)
         for u, (b, h) in enumerate(units)]
    return [jnp.concatenate(o[b * MEM_HEADS:(b + 1) * MEM_HEADS], axis=-1) for b in range(nb)]


def _mem_attn_kernel(q_ref, k_ref, v_ref, o_ref):
    nb = q_ref.shape[0]
    o = _mem_attn_body([q_ref[b] for b in range(nb)], [k_ref.at[b] for b in range(nb)],
                       [v_ref.at[b] for b in range(nb)])
    for b in range(nb):
        o_ref[b] = o[b]


def _mem_attn(qm, mem_k, mem_v, tm, nb):
    B, T, w = qm.shape
    kv_spec = pl.BlockSpec((nb,) + mem_k.shape[1:], lambda b, j: (b, 0, 0))
    return pl.pallas_call(
        _mem_attn_kernel,
        grid=(B // nb, T // tm),
        in_specs=[pl.BlockSpec((nb, tm, w), lambda b, j: (b, j, 0)), kv_spec, kv_spec],
        out_specs=pl.BlockSpec((nb, tm, w), lambda b, j: (b, j, 0)),
        out_shape=jax.ShapeDtypeStruct((B, T, w), F32),
        compiler_params=_cparams(("parallel", "parallel")),
        name="mem_attn",
    )(qm, mem_k, mem_v)


def _ffn_kernel(*refs, inject, emit_gate):
    if inject:
        h1_ref, om_ref, init_ref, fill_ref = refs[:4]
        rest = refs[4:]
    else:
        h1_ref, om_ref, init_ref = refs[:3]
        fill_ref, rest = None, refs[3:]
    _ffn_body(h1_ref[...], om_ref[...], init_ref, fill_ref, *rest, emit_gate=emit_gate)


def _ffn_body(h1, om, init_ref, fill_ref, wmo_ref, gf_ref, wup_ref, wc_ref, bc_ref, wd_ref, gfin_ref,
              y_ref, fc_ref, gs, *, emit_gate):
    tm = h1.shape[0]
    F = wd_ref.shape[0]
    PAD = SUBLANES

    @pl.when(pl.program_id(1) == 0)
    def _():
        gs[PAD - 2:PAD, :] = init_ref[...]

    h2 = h1 + _dot(om, wmo_ref[...])
    n = _rms(h2, gf_ref[...]).astype(BF16)
    gate = jnp.dot(n, wup_ref[:, :F], preferred_element_type=F32)
    if fill_ref is not None:
        r = lax.broadcasted_iota(jnp.int32, (tm, 1), 0)
        gate = jnp.where((r % SAMPLE_ROWS) >= SAMPLE_ROWS - 2, fill_ref[...], gate)
    gs[PAD:PAD + tm, :] = gate
    conv = (gs[PAD - 2:PAD - 2 + tm, :] * wc_ref[0:1, :] + gs[PAD - 1:PAD - 1 + tm, :] * wc_ref[1:2, :]
            + gate * wc_ref[2:3, :])
    last2 = gs[PAD + tm - 2:PAD + tm, :]
    gs[PAD - 2:PAD, :] = last2
    if emit_gate:
        fc_ref[...] = gate
    else:
        fc_ref[...] = last2
    up = jnp.dot(n, wup_ref[:, F:], preferred_element_type=F32)
    act = _silu(conv + bc_ref[...]) * up
    y = h2 + _dot(act, wd_ref[...])
    y_ref[...] = _rms(y, gfin_ref[...])


def _ffn(h1, om, init, fill, w_mo, norm_ffn, w_up, w_conv, b_conv, w_down, norm_final, tm, emit_gate):
    B, T, d = h1.shape
    F = w_down.shape[0]
    inject = fill is not None
    rt = lambda n: pl.BlockSpec((None, tm, n), lambda b, j: (b, j, 0))
    in_specs = [rt(d), rt(om.shape[-1]), pl.BlockSpec((None, 2, F), lambda b, j: (b, 0, 0))]
    args = [h1, om, init]
    if inject:
        in_specs.append(rt(F))
        args.append(fill)
    in_specs += [_resident(w_mo.shape), _resident((1, d)), _resident(w_up.shape), _resident(w_conv.shape),
                 _resident((1, F)), _resident(w_down.shape), _resident((1, d))]
    args += [w_mo, norm_ffn.reshape(1, d), w_up, w_conv, b_conv.reshape(1, F), w_down, norm_final.reshape(1, d)]
    if emit_gate:
        fc_spec, fc_shape = rt(F), jax.ShapeDtypeStruct((B, T, F), F32)
    else:
        fc_spec = pl.BlockSpec((None, 2, F), lambda b, j: (b, 0, 0))
        fc_shape = jax.ShapeDtypeStruct((B, 2, F), F32)
    return pl.pallas_call(
        functools.partial(_ffn_kernel, inject=inject, emit_gate=emit_gate),
        grid=(B, T // tm),
        in_specs=in_specs,
        out_specs=[rt(d), fc_spec],
        out_shape=[jax.ShapeDtypeStruct((B, T, d), F32), fc_shape],
        scratch_shapes=[pltpu.VMEM((tm + SUBLANES, F), F32)],
        compiler_params=_cparams(("parallel", "arbitrary")),
        name="ffn",
    )(*args)


def _post_kernel(*refs):
    n_mix = 2 * N_OG + 4
    mix_in, (k_ref, v_ref, init_ref), rest = refs[:n_mix], refs[n_mix:n_mix + 3], refs[n_mix + 3:]
    mix_w, ffn_rest = rest[:5], rest[5:]
    h1, qm = _mix_body(*mix_in, *mix_w)
    om, = _mem_attn_body([qm], [k_ref], [v_ref])
    _ffn_body(h1, om, init_ref, None, *ffn_rest, emit_gate=False)


def _post(o_g, l_g, o_b, ga, gb, h, mem_k, mem_v, init, w_a, w_b, w_o, norm_mem_q, w_mem_q, w_mo, norm_ffn, w_up,
          w_conv, b_conv, w_down, norm_final, tm):
    B, T, d = h.shape
    F = w_down.shape[0]
    rt = lambda n: pl.BlockSpec((None, tm, n), lambda b, j: (b, j, 0))
    per_b = lambda a: pl.BlockSpec((None,) + a.shape[1:], lambda b, j: (b, 0, 0))
    weights = [w_a, w_b, w_o, norm_mem_q.reshape(1, d), w_mem_q, w_mo, norm_ffn.reshape(1, d), w_up, w_conv,
               b_conv.reshape(1, F), w_down, norm_final.reshape(1, d)]
    return pl.pallas_call(
        _post_kernel,
        grid=(B, T // tm),
        in_specs=([rt(LANES)] * (2 * N_OG) + [rt(GDN_W), rt(d), rt(d), rt(d), per_b(mem_k), per_b(mem_v), per_b(init)]
                  + [_resident(w.shape) for w in weights]),
        out_specs=[rt(d), pl.BlockSpec((None, 2, F), lambda b, j: (b, 0, 0))],
        out_shape=[jax.ShapeDtypeStruct((B, T, d), F32), jax.ShapeDtypeStruct((B, 2, F), F32)],
        scratch_shapes=[pltpu.VMEM((tm + SUBLANES, F), F32)],
        compiler_params=_cparams(("parallel", "arbitrary")),
        name="post",
    )(*o_g, *l_g, o_b, ga, gb, h, mem_k, mem_v, init, *weights)


def kernel(x_prompt, x_sample, cache_dil0_kv, cache_dil1_kv, cache_dil2_kv, state_delta, state_delta_conv, cache_mem_k, cache_mem_v, state_ffn_conv, mem_prompt, rel_bias, norm_mix, w_in, w_conv_delta, a_log, dt_bias, norm_delta_out, w_branch_a, w_branch_b, w_out, norm_mem_q, norm_mem_kv, w_mem_q, w_mem_kv, w_mem_o, norm_ffn, w_ffn_up, w_ffn_conv, b_ffn_conv, w_ffn_down, norm_final):
    B, S, D = x_prompt.shape
    Bs, Ts, _ = x_sample.shape
    depth = w_in.shape[0]
    assert depth == 1 and D == D_MODEL and 3 <= Ts <= SAMPLE_ROWS - 2 and S % (16 * TILE_Q) == 0
    assert PAST_LEN >= max(w for w, _ in DIL_GROUPS)
    F = w_ffn_down.shape[1]
    M = mem_prompt.shape[1]
    l = 0
    w_arr = _arrange_w_in(w_in[l])
    w_a, w_b, w_o = (w.astype(BF16) for w in (w_branch_a[l], w_branch_b[l], w_out[l]))
    w_mq, w_mkv, w_mo = (w.astype(BF16) for w in (w_mem_q[l], w_mem_kv[l], w_mem_o[l]))
    w_up, w_dn = w_ffn_up[l].astype(BF16), w_ffn_down[l].astype(BF16)
    t_cat, t_cur = _prompt_bias_tables(rel_bias)

    xp = x_prompt.reshape(B * S, D)
    q, kv0, kv1, kv2, kt0, kt1, kt2, gq, gq_tail, z, ba, ga, gb = _in_proj(
        xp, norm_mix[l], w_arr, ROW_TILE, seq=S, conv_buf=jnp.zeros((B, 3, 3 * GDN_W), F32), w_conv=w_conv_delta[l])
    kvs = [kv.reshape(B, S, 2 * DIL_GW) for kv in (kv0, kv1, kv2)]
    q3 = q.reshape(B, S, 3 * DIL_GW)
    o_g, l_g = [], []
    for g in range(3):
        o_sl, l_sl = _dil_prompt(q3, kvs[g], g, t_cat[g], t_cur[g])
        o_g += o_sl
        l_g += l_sl
    o_b, delta_p = _gdn(gq.reshape(B, S, -1), ba.reshape(B, S, LANES), z.reshape(B, S, GDN_W),
                        jnp.zeros((B, GDN_HEADS, GDN_HEAD_DIM, GDN_HEAD_DIM), F32), a_log[l], dt_bias[l],
                        norm_delta_out[l])
    mk_p, mv_p = _mem_kv(mem_prompt.reshape(B * M, D), norm_mem_kv[l], w_mkv, ROW_TILE)
    mk_p, mv_p = (x.reshape(B, M * MEM_HEADS, MEM_HEAD_DIM) for x in (mk_p, mv_p))
    seq = lambda a: a.reshape(B, S, a.shape[-1])
    y_p, fconv_p = _post([seq(a) for a in o_g], [seq(a) for a in l_g], o_b, seq(ga), seq(gb), x_prompt, mk_p, mv_p,
                         jnp.zeros((B, 2, F), F32), w_a, w_b, w_o, norm_mem_q[l], w_mq, w_mo, norm_ffn[l], w_up,
                         w_ffn_conv[l], b_ffn_conv[l], w_dn, norm_final, ROW_TILE)
    p_out = ([kt.reshape(B, 2, DIL_HPG, DIL_HEAD_DIM, kt.shape[2]).transpose(0, 4, 1, 2, 3)[None]
              for kt in (kt0, kt1, kt2)]
             + [delta_p[None], gq_tail[:, SUBLANES - 3:][None], mk_p.reshape(1, B, M, MEM_HEADS, MEM_HEAD_DIM),
                mv_p.reshape(1, B, M, MEM_HEADS, MEM_HEAD_DIM), fconv_p[None]])

    R = SAMPLE_ROWS
    xs = jnp.pad(x_sample, ((0, 0), (0, R - Ts), (0, 0))).reshape(Bs * R, D)
    q, kv0, kv1, kv2, gq, z, ba, ga, gb = _in_proj(xs, norm_mix[l], w_arr, Bs * R)
    kvn = [kv.reshape(Bs, R, 2 * DIL_GW) for kv in (kv0, kv1, kv2)]
    caches_t = [jnp.transpose(c[l], (0, 2, 3, 4, 1)).reshape(Bs, 2 * DIL_GW, c.shape[2])
                for c in (cache_dil0_kv, cache_dil1_kv, cache_dil2_kv)]
    tabc, tabn = _sample_bias_tables(rel_bias, Ts)
    o_g, l_g, new_caches = _dil_sample(q.reshape(Bs, R, 3 * DIL_GW), kvn, caches_t, tabc, tabn, Ts)
    o_b, delta_s = _gdn_sample(gq.reshape(Bs, R, -1), ba.reshape(Bs, R, LANES), z.reshape(Bs, R, GDN_W),
                               state_delta_conv[l], state_delta[l], w_conv_delta[l], a_log[l], dt_bias[l],
                               norm_delta_out[l], Ts)
    o_b = o_b.reshape(Bs * R, GDN_W).astype(BF16)
    h1, qm = _mix(o_g, l_g, o_b, ga, gb, xs, w_a, w_b, w_o, norm_mem_q[l], w_mq, Bs * R)
    om = _mem_attn(qm.reshape(Bs, R, -1), cache_mem_k[l].reshape(Bs, M * MEM_HEADS, MEM_HEAD_DIM),
                   cache_mem_v[l].reshape(Bs, M * MEM_HEADS, MEM_HEAD_DIM), R, 8 if Bs % 8 == 0 else 1)
    fst = state_ffn_conv[l]
    fill = jnp.concatenate([jnp.zeros((Bs, R - 2, F), F32),
                            jnp.concatenate([fst[1:], jnp.zeros((1, 2, F), F32)], axis=0)], axis=1)
    y_s, gate_s = _ffn(h1.reshape(1, Bs * R, D), om.reshape(1, Bs * R, -1), fst[:1], fill.reshape(1, Bs * R, F),
                       w_mo, norm_ffn[l], w_up, w_ffn_conv[l], b_ffn_conv[l], w_dn, norm_final, Bs * R, True)
    y_s = y_s.reshape(Bs, R, D)[:, :Ts]
    gq3 = gq.reshape(Bs, R, -1)
    s_out = ([nc.reshape(Bs, 2, DIL_HPG, DIL_HEAD_DIM, nc.shape[2]).transpose(0, 4, 1, 2, 3)[None]
              for nc in new_caches]
             + [delta_s[None], gq3[:, Ts - 3:Ts][None], gate_s.reshape(Bs, R, F)[:, Ts - 2:Ts][None]])

    return (y_p.reshape(B, S, D), y_s, *p_out, *s_out)
```

```python
import functools
import math

import jax
import jax.numpy as jnp
import numpy as np
from jax import lax
from jax.experimental import pallas as pl
from jax.experimental.pallas import tpu as pltpu

F32 = jnp.float32
BF16 = jnp.bfloat16

PAST_LEN = 8192
DIL_GROUPS = ((128, 1), (512, 4), (2048, 16))
DIL_HPG = 4
DIL_HEAD_DIM = 64
DIL_GW = DIL_HPG * DIL_HEAD_DIM
DIL_NK = 129
REL_BUCKETS = 32
REL_MAX_DIST = 2048
GDN_HEADS = 8
GDN_HEAD_DIM = 128
GDN_W = GDN_HEADS * GDN_HEAD_DIM
GDN_CHUNK = 64
MEM_HEADS = 4
MEM_HEAD_DIM = 128
EPS = 1e-6
NEG = -1e30

LANES = 128
SUBLANES = 8
TILE_Q = 128
D_MODEL = 1024
ROW_TILE = 256
SAMPLE_ROWS = SUBLANES
CONV_PAD = SUBLANES
VMEM_LIMIT = 56 * 1024 * 1024


def _cparams(sem):
    return pltpu.CompilerParams(dimension_semantics=sem, vmem_limit_bytes=VMEM_LIMIT)


def _resident(shape):
    nd = len(shape)
    return pl.BlockSpec(shape, lambda *_: (0,) * nd, pipeline_mode=pl.Buffered(1))


def _rms(x, gain_row):
    return x * lax.rsqrt(jnp.mean(x * x, axis=-1, keepdims=True) + EPS) * gain_row


def _dot(a, b):
    return jnp.dot(a.astype(BF16), b.astype(BF16), preferred_element_type=F32)


def _dot_nt(a, b):
    return lax.dot_general(a.astype(BF16), b.astype(BF16), (((1,), (1,)), ((), ())), preferred_element_type=F32)


def _split3(x):
    hi = x.astype(BF16)
    r1 = x - hi.astype(F32)
    mid = r1.astype(BF16)
    lo = (r1 - mid.astype(F32)).astype(BF16)
    return hi, mid, lo


def _l2n(x):
    return x * lax.rsqrt(jnp.sum(x * x, axis=-1, keepdims=True) + EPS)


def _sigmoid(x):
    return 1.0 / (1.0 + jnp.exp(-x))


def _silu(x):
    return x * _sigmoid(x)


def _softplus(x):
    return jnp.maximum(x, 0.0) + jnp.log(1.0 + jnp.exp(-jnp.abs(x)))


IN_SEGS = (("q", 3 * DIL_GW), ("kv0", 2 * DIL_GW), ("kv1", 2 * DIL_GW), ("kv2", 2 * DIL_GW),
           ("gq", 3 * GDN_W), ("z", GDN_W), ("ba", LANES), ("ga", D_MODEL), ("gb", D_MODEL))


OFF_Q, OFF_K, OFF_V = 0, 3 * DIL_GW, 6 * DIL_GW
OFF_GQ = 9 * DIL_GW
OFF_Z = OFF_GQ + 3 * GDN_W
OFF_BA = OFF_Z + GDN_W
OFF_GATES = OFF_BA + 2 * GDN_HEADS


def _arrange_w_in(w_in):
    w = w_in.astype(BF16)
    return w, w[:, OFF_GATES:]


GQ_SLABS = 3 * GDN_W // LANES


def _in_proj_kernel(x_ref, g_ref, w_ref, wg_ref, *rest, seq_tiles):
    if seq_tiles:
        (cb_ref, wc_ref, q_ref, kv0_ref, kv1_ref, kv2_ref, kt0_ref, kt1_ref, kt2_ref, gq_ref, tail_ref, z_ref,
         ba_ref, ga_ref, gb_ref, cs) = rest
        kt_refs = (kt0_ref, kt1_ref, kt2_ref)
    else:
        q_ref, kv0_ref, kv1_ref, kv2_ref, gq_ref, z_ref, ba_ref, ga_ref, gb_ref = rest
    tm = x_ref.shape[0]
    u = _rms(x_ref[...], g_ref[...]).astype(BF16)

    def seg(ref, off, n):
        return jnp.dot(u, ref[:, off:off + n], preferred_element_type=F32)

    def q_seg():
        q_ref[...] = seg(w_ref, OFF_Q, 3 * DIL_GW) * (DIL_HEAD_DIM ** -0.5)

    def kv_seg(g, part):
        kv_ref = (kv0_ref, kv1_ref, kv2_ref)[g]
        r = seg(w_ref, (OFF_K, OFF_V)[part] + g * DIL_GW, DIL_GW)
        kv_ref[:, part * DIL_GW:(part + 1) * DIL_GW] = r
        if seq_tiles:
            kt_refs[g][part * DIL_GW:(part + 1) * DIL_GW, :] = r.T[:, tm - kt_refs[g].shape[1]:]

    def ba_seg():
        ba_ref[...] = seg(w_ref, OFF_BA, LANES)

    def gate_seg(ref, off):
        ref[...] = seg(wg_ref, off, D_MODEL)

    others = ([(q_seg, 3 * DIL_GW)] + [(functools.partial(kv_seg, g, p), DIL_GW) for g in range(3) for p in range(2)]
              + [(functools.partial(gate_seg, ga_ref, 0), D_MODEL),
                 (functools.partial(gate_seg, gb_ref, D_MODEL), D_MODEL),
                 (ba_seg, LANES)])
    if not seq_tiles:
        for f, _ in others:
            f()
        for c in range(3):
            gq_ref[:, c * GDN_W:(c + 1) * GDN_W] = seg(w_ref, OFF_GQ + c * GDN_W, GDN_W)
        z_ref[...] = seg(w_ref, OFF_Z, GDN_W)
        return
    i = pl.program_id(0)
    first = (i % seq_tiles) == 0

    @pl.when(first)
    def _():
        for s in range(GQ_SLABS):
            cs[s, CONV_PAD - 3:CONV_PAD, :] = cb_ref[:, s * LANES:(s + 1) * LANES]

    @pl.when(jnp.logical_not(first))
    def _():
        for s in range(GQ_SLABS):
            cs[s, CONV_PAD - 3:CONV_PAD, :] = cs[s, CONV_PAD + tm - 3:CONV_PAD + tm, :]

    for c in range(3):
        r = seg(w_ref, OFF_GQ + c * GDN_W, GDN_W)
        for hh in range(GDN_HEADS):
            cs[c * GDN_HEADS + hh, CONV_PAD:CONV_PAD + tm, :] = r[:, hh * LANES:(hh + 1) * LANES]
    base = CONV_PAD - 3 + jnp.minimum(i, 0)

    def conv_slab(s):
        ls = slice(s * LANES, (s + 1) * LANES)
        y = cs[s, pl.ds(base, tm), :] * wc_ref[0:1, ls]
        for j in range(1, 4):
            y = y + cs[s, pl.ds(base + j, tm), :] * wc_ref[j:j + 1, ls]
        y = _silu(y)
        if s < GDN_HEADS:
            y = _l2n(y) * (GDN_HEAD_DIM ** -0.5)
        elif s < 2 * GDN_HEADS:
            y = _l2n(y)
        gq_ref[:, ls] = y
        tail_ref[:, ls] = cs[s, tm:tm + SUBLANES, :]

    def z_seg():
        z_ref[...] = _silu(seg(w_ref, OFF_Z, GDN_W))

    others.insert(len(others) - 1, (z_seg, GDN_W))
    total = sum(n for _, n in others)
    done, cols = 0, 0
    for f, n in others:
        f()
        cols += n
        upto = min(GQ_SLABS, (cols * GQ_SLABS + total - 1) // total)
        for s in range(done, upto):
            conv_slab(s)
        done = upto
    assert done == GQ_SLABS


def _in_proj(x2d, gain, w_arr, tm, seq=None, conv_buf=None, w_conv=None):
    rows, d = x2d.shape
    fused = seq is not None
    nt = seq // tm if fused else None
    names = [n for n, _ in IN_SEGS]
    widths = dict(IN_SEGS)
    row_spec = lambda n: pl.BlockSpec((tm, n), lambda i: (i, 0))
    out_specs, out_shape = [], []
    for n in names:
        out_specs.append(row_spec(widths[n]))
        out_shape.append(jax.ShapeDtypeStruct((rows, widths[n]), F32))
        if n == "kv2" and fused:
            for w, _ in DIL_GROUPS:
                keep = min(w, seq)
                bw = min(tm, keep)
                skip = nt - keep // bw if bw == tm else nt - 1
                out_specs.append(pl.BlockSpec((None, 2 * DIL_GW, bw),
                                              lambda i, skip=skip: (i // nt, 0, jnp.maximum(i % nt - skip, 0))))
                out_shape.append(jax.ShapeDtypeStruct((rows // seq, 2 * DIL_GW, keep), F32))
        if n == "gq" and fused:
            out_specs.append(pl.BlockSpec((None, SUBLANES, 3 * GDN_W), lambda i: (i // nt, 0, 0)))
            out_shape.append(jax.ShapeDtypeStruct((rows // seq, SUBLANES, 3 * GDN_W), F32))
    in_specs = [pl.BlockSpec((tm, d), lambda i: (i, 0)), _resident((1, d)), _resident(w_arr[0].shape),
                _resident(w_arr[1].shape)]
    args = [x2d, gain.reshape(1, d), *w_arr]
    if fused:
        in_specs += [pl.BlockSpec((None, 3, 3 * GDN_W), lambda i: (i // nt, 0, 0)), _resident(w_conv.shape)]
        args += [conv_buf, w_conv]
    return pl.pallas_call(
        functools.partial(_in_proj_kernel, seq_tiles=nt),
        grid=(rows // tm,),
        in_specs=in_specs,
        out_specs=out_specs,
        out_shape=out_shape,
        scratch_shapes=[pltpu.VMEM((GQ_SLABS, CONV_PAD + tm, LANES), F32)] if fused else [],
        compiler_params=_cparams(("arbitrary",)),
        name="in_proj",
    )(*args)


def _rel_bucket(dist):
    exact = REL_BUCKETS // 2
    d = jnp.maximum(dist, 1).astype(F32)
    large = exact + (jnp.log(d / exact) / math.log(REL_MAX_DIST / exact) * (REL_BUCKETS - exact)).astype(jnp.int32)
    return jnp.where(dist < exact, dist, jnp.minimum(large, REL_BUCKETS - 1))


def _group_bias(rel_bias, g):
    dil = DIL_GROUPS[g][1]
    dist = dil * jnp.arange(DIL_NK, dtype=jnp.int32)
    tab = rel_bias[_rel_bucket(dist)]
    return tab[:, g * DIL_HPG:(g + 1) * DIL_HPG].T.astype(F32)


def _toeplitz(v, n, width):
    h, L = v.shape
    return jnp.tile(v, (1, n))[:, :n * (L - 1)].reshape(h, n, L - 1)[:, :, :width]


def _prompt_bias_tables(rel_bias):
    cat, cur = [], []
    for g in range(3):
        bg = _group_bias(rel_bias, g)
        v = jnp.concatenate([bg[:, ::-1], jnp.full((DIL_HPG, 3 * TILE_Q - DIL_NK), NEG, F32)], axis=1)
        t = _toeplitz(v, TILE_Q, 2 * TILE_Q)
        cat.append(t)
        cur.append(t[:, :, TILE_Q:])
    return cat, cur


DIL_TIF = 2
DIL_SLABS = DIL_GW // LANES


def _dil_prompt_kernel(q0_ref, q1_ref, k0_ref, k1_ref, v0_ref, v1_ref, tcat_ref, tcur_ref,
                       o0_ref, o1_ref, l0_ref, l1_ref, *, dil):
    S = q0_ref.shape[0]
    nb = S // dil // TILE_Q
    q_refs, k_refs, v_refs = (q0_ref, q1_ref), (k0_ref, k1_ref), (v0_ref, v1_ref)
    o_refs, l_refs = (o0_ref, o1_ref), (l0_ref, l1_ref)
    even = lax.broadcasted_iota(jnp.int32, (TILE_Q, LANES), 1) < DIL_HEAD_DIM

    def rows(r, t):
        start = r + dil * TILE_Q * t
        return pl.ds(start, TILE_Q, stride=dil) if dil > 1 else pl.ds(start, TILE_Q)

    tiles = [(r, t) for r in range(dil) for t in range(nb)]
    for i0 in range(0, len(tiles), DIL_TIF):
        grp = tiles[i0:i0 + DIL_TIF]
        qm, kc, vc = {}, {}, {}
        for ti, (r, t) in enumerate(grp):
            for sl in range(DIL_SLABS):
                qf = q_refs[sl][rows(r, t), :]
                qm[ti, 2 * sl] = jnp.where(even, qf, 0.0).astype(BF16)
                qm[ti, 2 * sl + 1] = jnp.where(even, 0.0, qf).astype(BF16)
                kc[ti, sl] = k_refs[sl][rows(r, t), :].astype(BF16)
                vc[ti, sl] = v_refs[sl][rows(r, t), :].astype(BF16)
                if t > 0:
                    kc[ti, sl] = jnp.concatenate([k_refs[sl][rows(r, t - 1), :].astype(BF16), kc[ti, sl]], axis=0)
                    vc[ti, sl] = jnp.concatenate([v_refs[sl][rows(r, t - 1), :].astype(BF16), vc[ti, sl]], axis=0)
        units = [(ti, h) for ti in range(len(grp)) for h in range(DIL_HPG)]
        s = [_dot_nt(qm[ti, h], kc[ti, h // 2]) + (tcat_ref[h] if grp[ti][1] > 0 else tcur_ref[h]) for ti, h in units]
        m = [jnp.max(x, axis=-1, keepdims=True) for x in s]
        p = [jnp.exp(x - mx) for x, mx in zip(s, m)]
        l = [jnp.sum(x, axis=-1, keepdims=True) for x in p]
        pv = [jnp.dot(p[u].astype(BF16), vc[ti, h // 2], preferred_element_type=F32) for u, (ti, h) in enumerate(units)]
        o = [pv[u] / l[u] for u in range(len(units))]
        lse = [m[u] + jnp.log(l[u]) for u in range(len(units))]
        for ti, (r, t) in enumerate(grp):
            for sl in range(DIL_SLABS):
                ue, uo = ti * DIL_HPG + 2 * sl, ti * DIL_HPG + 2 * sl + 1
                o_refs[sl][rows(r, t), :] = jnp.where(even, o[ue], o[uo])
                l_refs[sl][rows(r, t), :] = jnp.where(even, lse[ue], lse[uo])


def _dil_prompt(q, kv, g, t_cat, t_cur):
    B, S, _ = q.shape
    dil = DIL_GROUPS[g][1]
    slab = lambda c: pl.BlockSpec((None, S, LANES), lambda b: (b, 0, c))
    nq, nk = g * DIL_SLABS, 0
    outs = pl.pallas_call(
        functools.partial(_dil_prompt_kernel, dil=dil),
        grid=(B,),
        in_specs=[slab(nq), slab(nq + 1), slab(nk), slab(nk + 1), slab(nk + 2), slab(nk + 3),
                  _resident(t_cat.shape), _resident(t_cur.shape)],
        out_specs=[slab(0)] * 4,
        out_shape=[jax.ShapeDtypeStruct((B, S, LANES), F32)] * 4,
        compiler_params=_cparams(("parallel",)),
        name=f"dil_prompt_g{g}",
    )(q, q, kv, kv, kv, kv, t_cat, t_cur)
    o0, o1, l0, l1 = (x.reshape(B * S, LANES) for x in outs)
    return [o0, o1], [l0, l1]


def _sample_bias_tables(rel_bias, t_real):
    R = SAMPLE_ROWS
    tabc, tabn = [], []
    t_i = np.arange(R)[:, None]
    u_i = np.arange(R)[None, :]
    for g, (w, dil) in enumerate(DIL_GROUPS):
        bg = _group_bias(rel_bias, g)
        base = bg[:, ::-1][:, :TILE_Q]
        t0 = jnp.concatenate([base[:, :, None], jnp.full((DIL_HPG, TILE_Q, dil - 1), NEG, F32)], axis=2)
        t0 = t0.reshape(DIL_HPG, w)
        tc = _toeplitz(jnp.concatenate([t0, jnp.full((DIL_HPG, R), NEG, F32)], axis=1), R, w)
        tabc.append(jnp.where((t_i < t_real)[None], tc, 0.0).reshape(DIL_HPG * R, w))
        tn = jnp.full((DIL_HPG, R, R), NEG, F32)
        for j in range(-(-t_real // dil)):
            hit = (t_i - u_i == j * dil) & (t_i < t_real)
            tn = jnp.where(hit[None], bg[:, j][:, None, None], tn)
        tn = jnp.where((t_i >= t_real)[None], 0.0, tn)
        tabn.append(tn.reshape(DIL_HPG * R, R))
    return tabc, jnp.stack(tabn)


def _dil_sample_kernel(q_ref, n0_ref, n1_ref, n2_ref, c0_ref, c1_ref, c2_ref, tc0_ref, tc1_ref, tc2_ref, tn_ref,
                       o_ref, l_ref, oc0_ref, oc1_ref, oc2_ref, *, t_real):
    R = SAMPLE_ROWS
    rows = lax.broadcasted_iota(jnp.int32, (DIL_HPG * R, DIL_GW), 0)
    lanes = lax.broadcasted_iota(jnp.int32, (DIL_HPG * R, DIL_GW), 1)
    head_mask = (lanes // DIL_HEAD_DIM) == (rows // R)
    lane_f = lax.broadcasted_iota(jnp.int32, (2 * DIL_GW, LANES), 1)
    keep = lane_f < LANES - t_real
    sel_l = lax.broadcasted_iota(jnp.int32, (LANES, R), 0)
    sel_u = lax.broadcasted_iota(jnp.int32, (LANES, R), 1)
    selT = ((sel_l == sel_u + LANES - t_real) & (sel_u < t_real)).astype(BF16)
    groups = ((n0_ref, c0_ref, tc0_ref, oc0_ref), (n1_ref, c1_ref, tc1_ref, oc1_ref), (n2_ref, c2_ref, tc2_ref, oc2_ref))

    def fold_heads(x):
        x = jnp.where(head_mask, x, 0.0)
        return x[0:R] + x[R:2 * R] + x[2 * R:3 * R] + x[3 * R:4 * R]

    G3 = range(3)
    kvn = [groups[g][0][...] for g in G3]
    q_bd = [jnp.where(head_mask, jnp.concatenate([q_ref[:, g * DIL_GW:(g + 1) * DIL_GW]] * DIL_HPG, axis=0), 0.0)
            for g in G3]
    s_c = [_dot(q_bd[g], groups[g][1][:DIL_GW, :]) + groups[g][2][...] for g in G3]
    s_n = [_dot_nt(q_bd[g], kvn[g][:, :DIL_GW]) + tn_ref[g] for g in G3]
    m = [jnp.maximum(jnp.max(s_c[g], axis=-1, keepdims=True), jnp.max(s_n[g], axis=-1, keepdims=True)) for g in G3]
    p_c = [jnp.exp(s_c[g] - m[g]) for g in G3]
    p_n = [jnp.exp(s_n[g] - m[g]) for g in G3]
    l = [jnp.sum(p_c[g], axis=-1, keepdims=True) + jnp.sum(p_n[g], axis=-1, keepdims=True) for g in G3]
    acc = [(_dot_nt(p_c[g], groups[g][1][DIL_GW:, :]) + _dot(p_n[g], kvn[g][:, DIL_GW:])) / l[g] for g in G3]
    for g in G3:
        o_ref[:, g * DIL_GW:(g + 1) * DIL_GW] = fold_heads(acc[g])
        l_ref[:, g * DIL_GW:(g + 1) * DIL_GW] = fold_heads(jnp.broadcast_to(m[g] + jnp.log(l[g]), acc[g].shape))
    for g, (n_ref, c_ref, tc_ref, oc_ref) in enumerate(groups):
        W = c_ref.shape[1]
        hi, mid, lo = _split3(kvn[g])
        tail = (jnp.dot(selT, hi, preferred_element_type=F32) + jnp.dot(selT, mid, preferred_element_type=F32)
                + jnp.dot(selT, lo, preferred_element_type=F32)).T
        nxt = pltpu.roll(c_ref[:, 0:LANES], LANES - t_real, axis=1)
        for c in range(W // LANES):
            cur = nxt
            nxt = (pltpu.roll(c_ref[:, (c + 1) * LANES:(c + 2) * LANES], LANES - t_real, axis=1)
                   if (c + 1) * LANES < W else tail)
            oc_ref[:, c * LANES:(c + 1) * LANES] = jnp.where(keep, cur, nxt)


def _dil_sample(q, kvn, caches_t, tabc, tabn, t_real):
    B = q.shape[0]
    row = lambda n: pl.BlockSpec((None, SAMPLE_ROWS, n), lambda b: (b, 0, 0))
    cspecs = [pl.BlockSpec((None,) + c.shape[1:], lambda b: (b, 0, 0)) for c in caches_t]
    for g, (w, dil) in enumerate(DIL_GROUPS):
        assert caches_t[g].shape == (B, 2 * DIL_GW, w) and w // dil == TILE_Q
    out_spec = row(3 * DIL_GW)
    o, lse, *new_caches = pl.pallas_call(
        functools.partial(_dil_sample_kernel, t_real=t_real),
        grid=(B,),
        in_specs=([row(3 * DIL_GW)] + [row(2 * DIL_GW)] * 3 + cspecs + [_resident(t.shape) for t in tabc]
                  + [_resident(tabn.shape)]),
        out_specs=[out_spec, out_spec] + cspecs,
        out_shape=([jax.ShapeDtypeStruct((B, SAMPLE_ROWS, 3 * DIL_GW), F32)] * 2
                   + [jax.ShapeDtypeStruct(c.shape, F32) for c in caches_t]),
        compiler_params=_cparams(("parallel",)),
        name="dil_sample",
    )(q, *kvn, *caches_t, *tabc, tabn)
    o = o.reshape(B * SAMPLE_ROWS, 3 * DIL_GW)
    lse = lse.reshape(B * SAMPLE_ROWS, 3 * DIL_GW)
    n = 3 * DIL_SLABS
    return ([o[:, i * LANES:(i + 1) * LANES] for i in range(n)],
            [lse[:, i * LANES:(i + 1) * LANES] for i in range(n)], new_caches)


GDN_HPS = GDN_HEADS
GDN_SW = GDN_HPS * GDN_HEAD_DIM
GDN_TPI = 4
GDN_ROWS = 1024


def _gdn_kernel(q_ref, k_ref, v_ref, ba_ref, alog_ref, dtb_ref, z_ref, gn_ref, tri_ref, s0_ref, o_ref, s_ref, carry_s,
                *, tpi):
    T = q_ref.shape[0]
    C = GDN_CHUNK
    D = GDN_HEAD_DIM
    ri = lax.broadcasted_iota(jnp.int32, (TILE_Q, TILE_Q), 0)
    ci = lax.broadcasted_iota(jnp.int32, (TILE_Q, TILE_Q), 1)
    same = (ri // C) == (ci // C)
    incl = same & (ri >= ci)
    strict = same & (ri > ci)
    eye = (ri == ci).astype(F32)
    lane, row = ci, ri
    gain = gn_ref[...]
    zpad = jnp.zeros((C, D), F32)
    HH = range(GDN_HPS)
    hsl = [slice(hh * D, (hh + 1) * D) for hh in HH]
    RW = tpi * TILE_Q
    UU = [(tt, hh) for tt in range(tpi) for hh in HH]
    UI = range(len(UU))

    def tile(i, S):
        r0 = pl.multiple_of(i * RW, RW)
        rt = [r0 + tt * TILE_Q for tt in range(tpi)]
        q = [q_ref[pl.ds(rt[tt], TILE_Q), hsl[hh]] for tt, hh in UU]
        k = [k_ref[pl.ds(rt[tt], TILE_Q), hsl[hh]] for tt, hh in UU]
        v = [v_ref[pl.ds(rt[tt], TILE_Q), hsl[hh]] for tt, hh in UU]
        tri = tri_ref[...]
        beta_all, G_all = [], []
        for tt in range(tpi):
            ba = ba_ref[pl.ds(rt[tt], TILE_Q), :]
            beta_all.append(_sigmoid(ba))
            gh, gm, gl = _split3(-jnp.exp(alog_ref[...]) * _softplus(ba + dtb_ref[...]))
            G_all.append(jnp.dot(tri, gh, preferred_element_type=F32) + jnp.dot(tri, gm, preferred_element_type=F32)
                         + jnp.dot(tri, gl, preferred_element_type=F32))
        head = [hh for _, hh in UU]
        bc = [jnp.sum(jnp.where(lane == head[u], beta_all[UU[u][0]], 0.0), axis=-1, keepdims=True) for u in UI]
        Gc = [jnp.broadcast_to(jnp.sum(jnp.where(lane == head[u] + GDN_HEADS, G_all[UU[u][0]], 0.0), axis=-1,
                                       keepdims=True), (TILE_Q, TILE_Q)) for u in UI]
        gamma = [jnp.exp(jnp.where(incl, Gc[u] - Gc[u].T, NEG)) for u in UI]
        kk = [_dot_nt(k[u], k[u]) for u in UI]
        qk = [_dot_nt(q[u], k[u]) for u in UI]
        lo = lax.broadcasted_iota(jnp.int32, (C, TILE_Q), 1) < C

        def pack(m):
            return jnp.where(lo, m[:C], m[C:])

        def blockdiag(p):
            return jnp.concatenate([jnp.where(lo, p, 0.0), jnp.where(lo, 0.0, p)], axis=0)

        eye_p = pack(eye)
        Xp = [pack(jnp.where(strict, bc[u] * kk[u] * gamma[u], 0.0)) for u in UI]
        Pp = [eye_p - Xp[u] for u in UI]
        Xb = [blockdiag(Xp[u]).astype(BF16) for u in UI]
        for _ in range(int(math.log2(C)) - 1):
            Xp = [jnp.dot(Xp[u].astype(BF16), Xb[u], preferred_element_type=F32) for u in UI]
            Xb = [blockdiag(Xp[u]).astype(BF16) for u in UI]
            Pp = [Pp[u] + jnp.dot(Pp[u].astype(BF16), Xb[u], preferred_element_type=F32) for u in UI]
        eG = [jnp.exp(Gc[u]) for u in UI]
        rhs = [jnp.concatenate([v[u] * bc[u], k[u] * (bc[u] * eG[u])], axis=-1) for u in UI]
        sol = [rhs[u] + _dot(blockdiag(Pp[u] - eye_p), rhs[u]) for u in UI]
        a_in = [qk[u] * gamma[u] for u in UI]
        q_dec = [q[u] * eG[u] for u in UI]
        kdT = [(k[u] * jnp.exp(jnp.where(row < C, Gc[u][C - 1:C, :], Gc[u][2 * C - 1:2 * C, :]) - Gc[u])).T
               for u in UI]
        S = list(S)
        for tt in range(tpi):
            us = [tt * GDN_HPS + hh for hh in HH]
            oq, vn = [[] for _ in HH], [[] for _ in HH]
            for c in range(TILE_Q // C):
                cs = slice(c * C, (c + 1) * C)
                r = [_dot(jnp.concatenate([sol[us[hh]][cs, D:], q_dec[us[hh]][cs]], axis=0), S[hh]) for hh in HH]
                for hh in HH:
                    oq[hh].append(r[hh][C:])
                    vn[hh].append(sol[us[hh]][cs, :D] - r[hh][:C])
                vpad = [jnp.concatenate([vn[hh][c], zpad] if c == 0 else [zpad, vn[hh][c]], axis=0) for hh in HH]
                S = [S[hh] * jnp.exp(Gc[us[hh]][(c + 1) * C - 1:(c + 1) * C, :]) + _dot(kdT[us[hh]], vpad[hh])
                     for hh in HH]
            o = [jnp.concatenate(oq[hh], axis=0) + _dot(a_in[us[hh]], jnp.concatenate(vn[hh], axis=0)) for hh in HH]
            outs = [_rms(o[hh], gain) * z_ref[pl.ds(rt[tt], TILE_Q), hsl[hh]] for hh in HH]
            o_ref[pl.ds(rt[tt], TILE_Q), :] = jnp.concatenate(outs, axis=-1).astype(o_ref.dtype)
        return tuple(S)

    @pl.when(pl.program_id(1) == 0)
    def _():
        carry_s[...] = s0_ref[...]

    S = lax.fori_loop(0, T // RW, tile, tuple(carry_s[hh] for hh in HH))
    for hh in HH:
        carry_s[hh] = S[hh]
        s_ref[hh] = S[hh]


def _gdn(gq, ba, z, s0, a_log, dt_bias, norm_out):
    B, T, _ = gq.shape
    H = GDN_HEADS
    pad16 = lambda x: jnp.concatenate([jnp.zeros((H,), F32), x.astype(F32), jnp.zeros((LANES - 2 * H,), F32)])
    assert GDN_HPS == H
    tb = min(GDN_ROWS, T)
    tpi = GDN_TPI if (tb // TILE_Q) % GDN_TPI == 0 else 1
    col = lambda off: pl.BlockSpec((None, tb, GDN_SW), lambda b, j: (b, j, off))
    sblk = pl.BlockSpec((None, H, GDN_HEAD_DIM, GDN_HEAD_DIM), lambda b, j: (b, 0, 0, 0))
    r = np.arange(TILE_Q)
    tri = jnp.asarray((r[:, None] >= r[None, :]) & (r[:, None] // GDN_CHUNK == r[None, :] // GDN_CHUNK), BF16)
    o, s_new = pl.pallas_call(
        functools.partial(_gdn_kernel, tpi=tpi),
        grid=(B, T // tb),
        in_specs=[col(0), col(1), col(2),
                  pl.BlockSpec((None, tb, LANES), lambda b, j: (b, j, 0)),
                  _resident((1, LANES)), _resident((1, LANES)),
                  col(0), _resident((1, LANES)), _resident((TILE_Q, TILE_Q)), sblk],
        out_specs=[col(0), sblk],
        out_shape=[jax.ShapeDtypeStruct((B, T, GDN_W), BF16),
                   jax.ShapeDtypeStruct((B, H, GDN_HEAD_DIM, GDN_HEAD_DIM), F32)],
        scratch_shapes=[pltpu.VMEM((H, GDN_HEAD_DIM, GDN_HEAD_DIM), F32)],
        compiler_params=_cparams(("parallel", "arbitrary")),
        name="gdn",
    )(gq, gq, gq, ba, pad16(a_log).reshape(1, LANES), pad16(dt_bias).reshape(1, LANES), z,
      norm_out.reshape(1, LANES), tri, s0)
    return o, s_new


def _gdn_sample_kernel(x_ref, b_ref, w_ref, ba_ref, alog_ref, dtb_ref, z_ref, gn_ref, s0_ref, o_ref, s_ref, xs,
                       *, t_real):
    R, D, H = SAMPLE_ROWS, GDN_HEAD_DIM, GDN_HEADS
    nb = x_ref.shape[0]
    ri = lax.broadcasted_iota(jnp.int32, (R, R), 0)
    ci = lax.broadcasted_iota(jnp.int32, (R, R), 1)
    incl, strict = ri >= ci, ri > ci
    eye = (ri == ci).astype(F32)
    tri = incl.astype(BF16)
    er = lax.broadcasted_iota(jnp.int32, (LANES, LANES), 0)
    ec = lax.broadcasted_iota(jnp.int32, (LANES, LANES), 1)
    eye_l = (er == ec).astype(BF16)
    live = lax.broadcasted_iota(jnp.int32, (R, LANES), 0) < t_real
    d32 = functools.partial(jnp.dot, preferred_element_type=F32)
    nt = lambda a, b: lax.dot_general(a, b, (((1,), (1,)), ((), ())), preferred_element_type=F32)
    gain = gn_ref[...]
    y, beta_all, G_all, G_allT = [], [], [], []
    for s in range(nb):
        xs[s, CONV_PAD - 3:CONV_PAD, :] = b_ref[s]
        xs[s, CONV_PAD:CONV_PAD + R, :] = x_ref[s]
        ys = xs[s, CONV_PAD - 3:CONV_PAD - 3 + R, :] * w_ref[0:1, :]
        for j in range(1, 4):
            ys = ys + xs[s, CONV_PAD - 3 + j:CONV_PAD - 3 + j + R, :] * w_ref[j:j + 1, :]
        y.append(_silu(ys))
        ba = ba_ref[s]
        beta_all.append(jnp.where(live, _sigmoid(ba), 0.0))
        gh, gm, gl = _split3(jnp.where(live, -jnp.exp(alog_ref[...]) * _softplus(ba + dtb_ref[...]), 0.0))
        G = d32(tri, gh) + d32(tri, gm) + d32(tri, gl)
        th, tm, tl = _split3(G)
        G_all.append(G)
        G_allT.append(nt(eye_l, th) + nt(eye_l, tm) + nt(eye_l, tl))
    US = [(s, h) for s in range(nb) for h in range(H)]
    UI = range(len(US))

    q = [_l2n(y[s][:, h * D:(h + 1) * D]) * (D ** -0.5) for s, h in US]
    k = [_l2n(y[s][:, GDN_W + h * D:GDN_W + (h + 1) * D]) for s, h in US]
    v = [y[s][:, 2 * GDN_W + h * D:2 * GDN_W + (h + 1) * D] for s, h in US]
    bc = [beta_all[s][:, h:h + 1] for s, h in US]
    Gc = [G_all[s][:, H + h:H + h + 1] for s, h in US]
    gamma = [jnp.exp(jnp.where(incl, Gc[u] - G_allT[s][H + h:H + h + 1, :], NEG)) for u, (s, h) in enumerate(US)]
    kk = [_dot_nt(k[u], k[u]) for u in UI]
    qk = [_dot_nt(q[u], k[u]) for u in UI]
    X = [jnp.where(strict, bc[u] * kk[u] * gamma[u], 0.0) for u in UI]
    P = [eye - X[u] for u in UI]
    for _ in range(int(math.log2(R)) - 1):
        X = [_dot(X[u], X[u]) for u in UI]
        P = [P[u] + _dot(P[u], X[u]) for u in UI]
    eG = [jnp.exp(Gc[u]) for u in UI]
    rhs = [jnp.concatenate([v[u] * bc[u], k[u] * (bc[u] * eG[u])], axis=-1) for u in UI]
    sol = [rhs[u] + _dot(P[u] - eye, rhs[u]) for u in UI]
    S = [s0_ref[s, h] for s, h in US]
    r = [_dot(jnp.concatenate([sol[u][:, D:], q[u] * eG[u]], axis=0), S[u]) for u in UI]
    v_new = [sol[u][:, :D] - r[u][:R] for u in UI]
    o = [r[u][R:] + _dot(qk[u] * gamma[u], v_new[u]) for u in UI]
    kdT = [nt(eye_l, (k[u] * jnp.exp(Gc[u][R - 1:R, :] - Gc[u])).astype(BF16)) for u in UI]
    for u, (s, h) in enumerate(US):
        s_ref[s, h] = S[u] * jnp.exp(Gc[u][R - 1:R, :]) + _dot(kdT[u], v_new[u])
    for s in range(nb):
        o_ref[s] = jnp.concatenate([_rms(o[s * H + h], gain) * _silu(z_ref[s, :, h * D:(h + 1) * D])
                                    for h in range(H)], axis=-1)


GDN_SAMPLE_SEQS = 4


def _gdn_sample(gq, ba, z, conv_buf, s0, w_conv, a_log, dt_bias, norm_out, t_real):
    B, R, _ = gq.shape
    H = GDN_HEADS
    pad16 = lambda x: jnp.concatenate([jnp.zeros((H,), F32), x.astype(F32), jnp.zeros((LANES - 2 * H,), F32)])
    nb = GDN_SAMPLE_SEQS if B % GDN_SAMPLE_SEQS == 0 else 1
    blk = lambda *s: pl.BlockSpec((nb,) + s, lambda b: (b,) + (0,) * len(s))
    return pl.pallas_call(
        functools.partial(_gdn_sample_kernel, t_real=t_real),
        grid=(B // nb,),
        in_specs=[blk(R, 3 * GDN_W), blk(3, 3 * GDN_W), _resident(w_conv.shape), blk(R, LANES),
                  _resident((1, LANES)), _resident((1, LANES)), blk(R, GDN_W), _resident((1, LANES)),
                  blk(H, GDN_HEAD_DIM, GDN_HEAD_DIM)],
        out_specs=[blk(R, GDN_W), blk(H, GDN_HEAD_DIM, GDN_HEAD_DIM)],
        out_shape=[jax.ShapeDtypeStruct((B, R, GDN_W), F32),
                   jax.ShapeDtypeStruct((B, H, GDN_HEAD_DIM, GDN_HEAD_DIM), F32)],
        scratch_shapes=[pltpu.VMEM((nb, CONV_PAD + R, 3 * GDN_W), F32)],
        compiler_params=_cparams(("parallel",)),
        name="gdn_sample",
    )(gq, conv_buf, w_conv, ba, pad16(a_log).reshape(1, LANES), pad16(dt_bias).reshape(1, LANES), z,
      norm_out.reshape(1, LANES), s0)


N_OG = 3 * DIL_SLABS


def _mix_kernel(*refs):
    h1, qm = _mix_body(*refs[:2 * N_OG + 9])
    h1_ref, qm_ref = refs[2 * N_OG + 9:]
    h1_ref[...] = h1
    qm_ref[...] = qm


def _mix_body(*refs):
    o_refs, l_refs = refs[:N_OG], refs[N_OG:2 * N_OG]
    ob_ref, ga_ref, gb_ref, h_ref, wa_ref, wb_ref, wo_ref, gq_ref, wq_ref = refs[2 * N_OG:]
    slabs = []
    for sl in range(DIL_SLABS):
        l0, l1, l2 = (l_refs[g * DIL_SLABS + sl][...] for g in range(3))
        o0, o1, o2 = (o_refs[g * DIL_SLABS + sl][...] for g in range(3))
        mx = jnp.maximum(jnp.maximum(l0, l1), l2)
        e0, e1, e2 = jnp.exp(l0 - mx), jnp.exp(l1 - mx), jnp.exp(l2 - mx)
        slabs.append((e0 * o0 + e1 * o1 + e2 * o2) / (e0 + e1 + e2))
    o_a = jnp.concatenate(slabs, axis=-1)
    a = _dot(o_a, wa_ref[...])
    b = jnp.dot(ob_ref[...], wb_ref[...], preferred_element_type=F32)
    merged = _sigmoid(ga_ref[...]) * a + _sigmoid(gb_ref[...]) * b
    h1 = h_ref[...] + _dot(merged, wo_ref[...])
    return h1, _dot(_rms(h1, gq_ref[...]), wq_ref[...]).astype(BF16)


def _mix(o_g, l_g, o_b, ga, gb, h, w_a, w_b, w_o, norm_mem_q, w_mem_q, tm):
    rows, d = h.shape
    rt = lambda n: pl.BlockSpec((tm, n), lambda i: (i, 0))
    assert len(o_g) == len(l_g) == 3 * DIL_SLABS
    return pl.pallas_call(
        _mix_kernel,
        grid=(rows // tm,),
        in_specs=[rt(LANES)] * (6 * DIL_SLABS) + [rt(GDN_W), rt(d), rt(d), rt(d),
                                     _resident(w_a.shape), _resident(w_b.shape), _resident(w_o.shape),
                                     _resident((1, d)), _resident(w_mem_q.shape)],
        out_specs=[rt(d), rt(w_mem_q.shape[1])],
        out_shape=[jax.ShapeDtypeStruct((rows, d), F32), jax.ShapeDtypeStruct((rows, w_mem_q.shape[1]), BF16)],
        compiler_params=_cparams(("parallel",)),
        name="mix",
    )(*o_g, *l_g, o_b, ga, gb, h, w_a, w_b, w_o, norm_mem_q.reshape(1, d), w_mem_q)


def _mem_kv_kernel(x_ref, g_ref, w_ref, k_ref, v_ref):
    u = _rms(x_ref[...], g_ref[...]).astype(BF16)
    tm = x_ref.shape[0]
    n = MEM_HEADS * MEM_HEAD_DIM
    for o_ref, off in ((k_ref, 0), (v_ref, n)):
        r = jnp.dot(u, w_ref[:, off:off + n], preferred_element_type=F32)
        for h in range(MEM_HEADS):
            o_ref[pl.ds(h, tm, stride=MEM_HEADS), :] = r[:, h * MEM_HEAD_DIM:(h + 1) * MEM_HEAD_DIM]


def _mem_kv(mem2d, gain, w, tm):
    rows, d = mem2d.shape
    return pl.pallas_call(
        _mem_kv_kernel,
        grid=(rows // tm,),
        in_specs=[pl.BlockSpec((tm, d), lambda i: (i, 0)), _resident((1, d)), _resident(w.shape)],
        out_specs=[pl.BlockSpec((tm * MEM_HEADS, MEM_HEAD_DIM), lambda i: (i, 0))] * 2,
        out_shape=[jax.ShapeDtypeStruct((rows * MEM_HEADS, MEM_HEAD_DIM), F32)] * 2,
        compiler_params=_cparams(("parallel",)),
        name="mem_kv",
    )(mem2d, gain.reshape(1, d), w)


def _mem_attn_body(q, k_refs, v_refs):
    nb = len(q)
    M = k_refs[0].shape[0] // MEM_HEADS
    units = [(b, h) for b in range(nb) for h in range(MEM_HEADS)]
    s = [_dot_nt(q[b][:, h * MEM_HEAD_DIM:(h + 1) * MEM_HEAD_DIM], k_refs[b][pl.ds(h, M, stride=MEM_HEADS), :])
         * (MEM_HEAD_DIM ** -0.5) for b, h in units]
    p = [jnp.exp(x - jnp.max(x, axis=-1, keepdims=True)) for x in s]
    o = [_dot(p[u], v_refs[b][pl.ds(h, M, stride=MEM_HEADS), :]) / jnp.sum(p[u], axis=-1, keepdims=True)
         for u, (b, h) in enumerate(units)]
    return [jnp.concatenate(o[b * MEM_HEADS:(b + 1) * MEM_HEADS], axis=-1) for b in range(nb)]


def _mem_attn_kernel(q_ref, k_ref, v_ref, o_ref):
    nb = q_ref.shape[0]
    o = _mem_attn_body([q_ref[b] for b in range(nb)], [k_ref.at[b] for b in range(nb)],
                       [v_ref.at[b] for b in range(nb)])
    for b in range(nb):
        o_ref[b] = o[b]


def _mem_attn(qm, mem_k, mem_v, tm, nb):
    B, T, w = qm.shape
    kv_spec = pl.BlockSpec((nb,) + mem_k.shape[1:], lambda b, j: (b, 0, 0))
    return pl.pallas_call(
        _mem_attn_kernel,
        grid=(B // nb, T // tm),
        in_specs=[pl.BlockSpec((nb, tm, w), lambda b, j: (b, j, 0)), kv_spec, kv_spec],
        out_specs=pl.BlockSpec((nb, tm, w), lambda b, j: (b, j, 0)),
        out_shape=jax.ShapeDtypeStruct((B, T, w), F32),
        compiler_params=_cparams(("parallel", "parallel")),
        name="mem_attn",
    )(qm, mem_k, mem_v)


def _ffn_kernel(*refs, inject, emit_gate):
    if inject:
        h1_ref, om_ref, init_ref, fill_ref = refs[:4]
        rest = refs[4:]
    else:
        h1_ref, om_ref, init_ref = refs[:3]
        fill_ref, rest = None, refs[3:]
    _ffn_body(h1_ref[...], om_ref[...], init_ref, fill_ref, *rest, emit_gate=emit_gate)


def _ffn_body(h1, om, init_ref, fill_ref, wmo_ref, gf_ref, wup_ref, wc_ref, bc_ref, wd_ref, gfin_ref,
              y_ref, fc_ref, gs, *, emit_gate):
    tm = h1.shape[0]
    F = wd_ref.shape[0]
    PAD = SUBLANES

    @pl.when(pl.program_id(1) == 0)
    def _():
        gs[PAD - 2:PAD, :] = init_ref[...]

    h2 = h1 + _dot(om, wmo_ref[...])
    n = _rms(h2, gf_ref[...]).astype(BF16)
    gate = jnp.dot(n, wup_ref[:, :F], preferred_element_type=F32)
    if fill_ref is not None:
        r = lax.broadcasted_iota(jnp.int32, (tm, 1), 0)
        gate = jnp.where((r % SAMPLE_ROWS) >= SAMPLE_ROWS - 2, fill_ref[...], gate)
    gs[PAD:PAD + tm, :] = gate
    conv = (gs[PAD - 2:PAD - 2 + tm, :] * wc_ref[0:1, :] + gs[PAD - 1:PAD - 1 + tm, :] * wc_ref[1:2, :]
            + gate * wc_ref[2:3, :])
    last2 = gs[PAD + tm - 2:PAD + tm, :]
    gs[PAD - 2:PAD, :] = last2
    if emit_gate:
        fc_ref[...] = gate
    else:
        fc_ref[...] = last2
    up = jnp.dot(n, wup_ref[:, F:], preferred_element_type=F32)
    act = _silu(conv + bc_ref[...]) * up
    y = h2 + _dot(act, wd_ref[...])
    y_ref[...] = _rms(y, gfin_ref[...])


def _ffn(h1, om, init, fill, w_mo, norm_ffn, w_up, w_conv, b_conv, w_down, norm_final, tm, emit_gate):
    B, T, d = h1.shape
    F = w_down.shape[0]
    inject = fill is not None
    rt = lambda n: pl.BlockSpec((None, tm, n), lambda b, j: (b, j, 0))
    in_specs = [rt(d), rt(om.shape[-1]), pl.BlockSpec((None, 2, F), lambda b, j: (b, 0, 0))]
    args = [h1, om, init]
    if inject:
        in_specs.append(rt(F))
        args.append(fill)
    in_specs += [_resident(w_mo.shape), _resident((1, d)), _resident(w_up.shape), _resident(w_conv.shape),
                 _resident((1, F)), _resident(w_down.shape), _resident((1, d))]
    args += [w_mo, norm_ffn.reshape(1, d), w_up, w_conv, b_conv.reshape(1, F), w_down, norm_final.reshape(1, d)]
    if emit_gate:
        fc_spec, fc_shape = rt(F), jax.ShapeDtypeStruct((B, T, F), F32)
    else:
        fc_spec = pl.BlockSpec((None, 2, F), lambda b, j: (b, 0, 0))
        fc_shape = jax.ShapeDtypeStruct((B, 2, F), F32)
    return pl.pallas_call(
        functools.partial(_ffn_kernel, inject=inject, emit_gate=emit_gate),
        grid=(B, T // tm),
        in_specs=in_specs,
        out_specs=[rt(d), fc_spec],
        out_shape=[jax.ShapeDtypeStruct((B, T, d), F32), fc_shape],
        scratch_shapes=[pltpu.VMEM((tm + SUBLANES, F), F32)],
        compiler_params=_cparams(("parallel", "arbitrary")),
        name="ffn",
    )(*args)


def _post_kernel(*refs):
    n_mix = 2 * N_OG + 4
    mix_in, (k_ref, v_ref, init_ref), rest = refs[:n_mix], refs[n_mix:n_mix + 3], refs[n_mix + 3:]
    mix_w, ffn_rest = rest[:5], rest[5:]
    h1, qm = _mix_body(*mix_in, *mix_w)
    om, = _mem_attn_body([qm], [k_ref], [v_ref])
    _ffn_body(h1, om, init_ref, None, *ffn_rest, emit_gate=False)


def _post(o_g, l_g, o_b, ga, gb, h, mem_k, mem_v, init, w_a, w_b, w_o, norm_mem_q, w_mem_q, w_mo, norm_ffn, w_up,
          w_conv, b_conv, w_down, norm_final, tm):
    B, T, d = h.shape
    F = w_down.shape[0]
    rt = lambda n: pl.BlockSpec((None, tm, n), lambda b, j: (b, j, 0))
    per_b = lambda a: pl.BlockSpec((None,) + a.shape[1:], lambda b, j: (b, 0, 0))
    weights = [w_a, w_b, w_o, norm_mem_q.reshape(1, d), w_mem_q, w_mo, norm_ffn.reshape(1, d), w_up, w_conv,
               b_conv.reshape(1, F), w_down, norm_final.reshape(1, d)]
    return pl.pallas_call(
        _post_kernel,
        grid=(B, T // tm),
        in_specs=([rt(LANES)] * (2 * N_OG) + [rt(GDN_W), rt(d), rt(d), rt(d), per_b(mem_k), per_b(mem_v), per_b(init)]
                  + [_resident(w.shape) for w in weights]),
        out_specs=[rt(d), pl.BlockSpec((None, 2, F), lambda b, j: (b, 0, 0))],
        out_shape=[jax.ShapeDtypeStruct((B, T, d), F32), jax.ShapeDtypeStruct((B, 2, F), F32)],
        scratch_shapes=[pltpu.VMEM((tm + SUBLANES, F), F32)],
        compiler_params=_cparams(("parallel", "arbitrary")),
        name="post",
    )(*o_g, *l_g, o_b, ga, gb, h, mem_k, mem_v, init, *weights)


def kernel(x_prompt, x_sample, cache_dil0_kv, cache_dil1_kv, cache_dil2_kv, state_delta, state_delta_conv, cache_mem_k, cache_mem_v, state_ffn_conv, mem_prompt, rel_bias, norm_mix, w_in, w_conv_delta, a_log, dt_bias, norm_delta_out, w_branch_a, w_branch_b, w_out, norm_mem_q, norm_mem_kv, w_mem_q, w_mem_kv, w_mem_o, norm_ffn, w_ffn_up, w_ffn_conv, b_ffn_conv, w_ffn_down, norm_final):
    B, S, D = x_prompt.shape
    Bs, Ts, _ = x_sample.shape
    depth = w_in.shape[0]
    assert depth == 1 and D == D_MODEL and 3 <= Ts <= SAMPLE_ROWS - 2 and S % (16 * TILE_Q) == 0
    assert PAST_LEN >= max(w for w, _ in DIL_GROUPS)
    F = w_ffn_down.shape[1]
    M = mem_prompt.shape[1]
    l = 0
    w_arr = _arrange_w_in(w_in[l])
    w_a, w_b, w_o = (w.astype(BF16) for w in (w_branch_a[l], w_branch_b[l], w_out[l]))
    w_mq, w_mkv, w_mo = (w.astype(BF16) for w in (w_mem_q[l], w_mem_kv[l], w_mem_o[l]))
    w_up, w_dn = w_ffn_up[l].astype(BF16), w_ffn_down[l].astype(BF16)
    t_cat, t_cur = _prompt_bias_tables(rel_bias)

    xp = x_prompt.reshape(B * S, D)
    q, kv0, kv1, kv2, kt0, kt1, kt2, gq, gq_tail, z, ba, ga, gb = _in_proj(
        xp, norm_mix[l], w_arr, ROW_TILE, seq=S, conv_buf=jnp.zeros((B, 3, 3 * GDN_W), F32), w_conv=w_conv_delta[l])
    kvs = [kv.reshape(B, S, 2 * DIL_GW) for kv in (kv0, kv1, kv2)]
    q3 = q.reshape(B, S, 3 * DIL_GW)
    o_g, l_g = [], []
    for g in range(3):
        o_sl, l_sl = _dil_prompt(q3, kvs[g], g, t_cat[g], t_cur[g])
        o_g += o_sl
        l_g += l_sl
    o_b, delta_p = _gdn(gq.reshape(B, S, -1), ba.reshape(B, S, LANES), z.reshape(B, S, GDN_W),
                        jnp.zeros((B, GDN_HEADS, GDN_HEAD_DIM, GDN_HEAD_DIM), F32), a_log[l], dt_bias[l],
                        norm_delta_out[l])
    mk_p, mv_p = _mem_kv(mem_prompt.reshape(B * M, D), norm_mem_kv[l], w_mkv, ROW_TILE)
    mk_p, mv_p = (x.reshape(B, M * MEM_HEADS, MEM_HEAD_DIM) for x in (mk_p, mv_p))
    seq = lambda a: a.reshape(B, S, a.shape[-1])
    y_p, fconv_p = _post([seq(a) for a in o_g], [seq(a) for a in l_g], o_b, seq(ga), seq(gb), x_prompt, mk_p, mv_p,
                         jnp.zeros((B, 2, F), F32), w_a, w_b, w_o, norm_mem_q[l], w_mq, w_mo, norm_ffn[l], w_up,
                         w_ffn_conv[l], b_ffn_conv[l], w_dn, norm_final, ROW_TILE)
    p_out = ([kt.reshape(B, 2, DIL_HPG, DIL_HEAD_DIM, kt.shape[2]).transpose(0, 4, 1, 2, 3)[None]
              for kt in (kt0, kt1, kt2)]
             + [delta_p[None], gq_tail[:, SUBLANES - 3:][None], mk_p.reshape(1, B, M, MEM_HEADS, MEM_HEAD_DIM),
                mv_p.reshape(1, B, M, MEM_HEADS, MEM_HEAD_DIM), fconv_p[None]])

    R = SAMPLE_ROWS
    xs = jnp.pad(x_sample, ((0, 0), (0, R - Ts), (0, 0))).reshape(Bs * R, D)
    q, kv0, kv1, kv2, gq, z, ba, ga, gb = _in_proj(xs, norm_mix[l], w_arr, Bs * R)
    kvn = [kv.reshape(Bs, R, 2 * DIL_GW) for kv in (kv0, kv1, kv2)]
    caches_t = [jnp.transpose(c[l], (0, 2, 3, 4, 1)).reshape(Bs, 2 * DIL_GW, c.shape[2])
                for c in (cache_dil0_kv, cache_dil1_kv, cache_dil2_kv)]
    tabc, tabn = _sample_bias_tables(rel_bias, Ts)
    o_g, l_g, new_caches = _dil_sample(q.reshape(Bs, R, 3 * DIL_GW), kvn, caches_t, tabc, tabn, Ts)
    o_b, delta_s = _gdn_sample(gq.reshape(Bs, R, -1), ba.reshape(Bs, R, LANES), z.reshape(Bs, R, GDN_W),
                               state_delta_conv[l], state_delta[l], w_conv_delta[l], a_log[l], dt_bias[l],
                               norm_delta_out[l], Ts)
    o_b = o_b.reshape(Bs * R, GDN_W).astype(BF16)
    h1, qm = _mix(o_g, l_g, o_b, ga, gb, xs, w_a, w_b, w_o, norm_mem_q[l], w_mq, Bs * R)
    om = _mem_attn(qm.reshape(Bs, R, -1), cache_mem_k[l].reshape(Bs, M * MEM_HEADS, MEM_HEAD_DIM),
                   cache_mem_v[l].reshape(Bs, M * MEM_HEADS, MEM_HEAD_DIM), R, 8 if Bs % 8 == 0 else 1)
    fst = state_ffn_conv[l]
    fill = jnp.concatenate([jnp.zeros((Bs, R - 2, F), F32),
                            jnp.concatenate([fst[1:], jnp.zeros((1, 2, F), F32)], axis=0)], axis=1)
    y_s, gate_s = _ffn(h1.reshape(1, Bs * R, D), om.reshape(1, Bs * R, -1), fst[:1], fill.reshape(1, Bs * R, F),
                       w_mo, norm_ffn[l], w_up, w_ffn_conv[l], b_ffn_conv[l], w_dn, norm_final, Bs * R, True)
    y_s = y_s.reshape(Bs, R, D)[:, :Ts]
    gq3 = gq.reshape(Bs, R, -1)
    s_out = ([nc.reshape(Bs, 2, DIL_HPG, DIL_HEAD_DIM, nc.shape[2]).transpose(0, 4, 1, 2, 3)[None]
              for nc in new_caches]
             + [delta_s[None], gq3[:, Ts - 3:Ts][None], gate_s.reshape(Bs, R, F)[:, Ts - 2:Ts][None]])

    return (y_p.reshape(B, S, D), y_s, *p_out, *s_out)
```

```python
import functools
import math

import jax
import jax.numpy as jnp
import numpy as np
from jax import lax
from jax.experimental import pallas as pl
from jax.experimental.pallas import tpu as pltpu

F32 = jnp.float32
BF16 = jnp.bfloat16

PAST_LEN = 8192
DIL_GROUPS = ((128, 1), (512, 4), (2048, 16))
DIL_HPG = 4
DIL_HEAD_DIM = 64
DIL_GW = DIL_HPG * DIL_HEAD_DIM
DIL_NK = 129
REL_BUCKETS = 32
REL_MAX_DIST = 2048
GDN_HEADS = 8
GDN_HEAD_DIM = 128
GDN_W = GDN_HEADS * GDN_HEAD_DIM
GDN_CHUNK = 64
MEM_HEADS = 4
MEM_HEAD_DIM = 128
EPS = 1e-6
NEG = -1e30

LANES = 128
SUBLANES = 8
TILE_Q = 128
D_MODEL = 1024
ROW_TILE = 256
SAMPLE_ROWS = SUBLANES
CONV_PAD = SUBLANES
VMEM_LIMIT = 56 * 1024 * 1024


def _cparams(sem):
    return pltpu.CompilerParams(dimension_semantics=sem, vmem_limit_bytes=VMEM_LIMIT)


def _resident(shape):
    nd = len(shape)
    return pl.BlockSpec(shape, lambda *_: (0,) * nd, pipeline_mode=pl.Buffered(1))


def _rms(x, gain_row):
    return x * lax.rsqrt(jnp.mean(x * x, axis=-1, keepdims=True) + EPS) * gain_row


def _dot(a, b):
    return jnp.dot(a.astype(BF16), b.astype(BF16), preferred_element_type=F32)


def _dot_nt(a, b):
    return lax.dot_general(a.astype(BF16), b.astype(BF16), (((1,), (1,)), ((), ())), preferred_element_type=F32)


def _split3(x):
    hi = x.astype(BF16)
    r1 = x - hi.astype(F32)
    mid = r1.astype(BF16)
    lo = (r1 - mid.astype(F32)).astype(BF16)
    return hi, mid, lo


def _l2n(x):
    return x * lax.rsqrt(jnp.sum(x * x, axis=-1, keepdims=True) + EPS)


def _sigmoid(x):
    return 1.0 / (1.0 + jnp.exp(-x))


def _silu(x):
    return x * _sigmoid(x)


def _softplus(x):
    return jnp.maximum(x, 0.0) + jnp.log(1.0 + jnp.exp(-jnp.abs(x)))


IN_SEGS = (("q", 3 * DIL_GW), ("kv0", 2 * DIL_GW), ("kv1", 2 * DIL_GW), ("kv2", 2 * DIL_GW),
           ("gq", 3 * GDN_W), ("z", GDN_W), ("ba", LANES), ("ga", D_MODEL), ("gb", D_MODEL))


OFF_Q, OFF_K, OFF_V = 0, 3 * DIL_GW, 6 * DIL_GW
OFF_GQ = 9 * DIL_GW
OFF_Z = OFF_GQ + 3 * GDN_W
OFF_BA = OFF_Z + GDN_W
OFF_GATES = OFF_BA + 2 * GDN_HEADS


def _arrange_w_in(w_in):
    w = w_in.astype(BF16)
    return w, w[:, OFF_GATES:]


GQ_SLABS = 3 * GDN_W // LANES
NORM_ROWS = 4 * SUBLANES


def _in_proj_kernel(x_ref, g_ref, w_ref, wg_ref, *rest, seq_tiles):
    tm = x_ref.shape[0]
    if seq_tiles:
        (xn_ref, cb_ref, wc_ref, q_ref, kv0_ref, kv1_ref, kv2_ref, kt0_ref, kt1_ref, kt2_ref, gq_ref, tail_ref, z_ref,
         ba_ref, ga_ref, gb_ref, cs, us) = rest
        kt_refs = (kt0_ref, kt1_ref, kt2_ref)
        i = pl.program_id(0)

        @pl.when(i == 0)
        def _():
            us[0] = _rms(x_ref[...], g_ref[...]).astype(BF16)

        u_ref, un_ref = us.at[i % 2], us.at[(i + 1) % 2]
    else:
        q_ref, kv0_ref, kv1_ref, kv2_ref, gq_ref, z_ref, ba_ref, ga_ref, gb_ref = rest
        u = _rms(x_ref[...], g_ref[...]).astype(BF16)

    def seg(ref, off, n):
        return jnp.dot(u_ref[...] if seq_tiles else u, ref[:, off:off + n], preferred_element_type=F32)

    def q_seg():
        q_ref[...] = seg(w_ref, OFF_Q, 3 * DIL_GW) * (DIL_HEAD_DIM ** -0.5)

    def kv_seg(g, part):
        kv_ref = (kv0_ref, kv1_ref, kv2_ref)[g]
        r = seg(w_ref, (OFF_K, OFF_V)[part] + g * DIL_GW, DIL_GW)
        kv_ref[:, part * DIL_GW:(part + 1) * DIL_GW] = r
        if seq_tiles:
            kt_refs[g][part * DIL_GW:(part + 1) * DIL_GW, :] = r.T[:, tm - kt_refs[g].shape[1]:]

    def ba_seg():
        ba_ref[...] = seg(w_ref, OFF_BA, LANES)

    def gate_seg(ref, off):
        ref[...] = seg(wg_ref, off, D_MODEL)

    others = ([(q_seg, 3 * DIL_GW)] + [(functools.partial(kv_seg, g, p), DIL_GW) for g in range(3) for p in range(2)]
              + [(functools.partial(gate_seg, ga_ref, 0), D_MODEL),
                 (functools.partial(gate_seg, gb_ref, D_MODEL), D_MODEL),
                 (ba_seg, LANES)])
    if not seq_tiles:
        for f, _ in others:
            f()
        for c in range(3):
            gq_ref[:, c * GDN_W:(c + 1) * GDN_W] = seg(w_ref, OFF_GQ + c * GDN_W, GDN_W)
        z_ref[...] = seg(w_ref, OFF_Z, GDN_W)
        return
    first = (i % seq_tiles) == 0

    @pl.when(first)
    def _():
        for s in range(GQ_SLABS):
            cs[s, CONV_PAD - 3:CONV_PAD, :] = cb_ref[:, s * LANES:(s + 1) * LANES]

    @pl.when(jnp.logical_not(first))
    def _():
        for s in range(GQ_SLABS):
            cs[s, CONV_PAD - 3:CONV_PAD, :] = cs[s, CONV_PAD + tm - 3:CONV_PAD + tm, :]

    for c in range(3):
        r = seg(w_ref, OFF_GQ + c * GDN_W, GDN_W)
        for hh in range(GDN_HEADS):
            cs[c * GDN_HEADS + hh, CONV_PAD:CONV_PAD + tm, :] = r[:, hh * LANES:(hh + 1) * LANES]
        for rs in range(c * tm // 3 // NORM_ROWS, (c + 1) * tm // 3 // NORM_ROWS):
            rows = slice(rs * NORM_ROWS, (rs + 1) * NORM_ROWS)
            un_ref[rows, :] = _rms(xn_ref[rows, :], g_ref[...]).astype(BF16)
    base =CONV_PAD - 3 + jnp.minimum(i, 0)

    def conv_slab(s):
        ls = slice(s * LANES, (s + 1) * LANES)
        y = cs[s, pl.ds(base, tm), :] * wc_ref[0:1, ls]
        for j in range(1, 4):
            y = y + cs[s, pl.ds(base + j, tm), :] * wc_ref[j:j + 1, ls]
        y = _silu(y)
        if s < GDN_HEADS:
            y = _l2n(y) * (GDN_HEAD_DIM ** -0.5)
        elif s < 2 * GDN_HEADS:
            y = _l2n(y)
        gq_ref[:, ls] = y
        tail_ref[:, ls] = cs[s, tm:tm + SUBLANES, :]

    def z_seg():
        z_ref[...] = _silu(seg(w_ref, OFF_Z, GDN_W))

    others.insert(len(others) - 1, (z_seg, GDN_W))
    total = sum(n for _, n in others)
    done, cols = 0, 0
    for f, n in others:
        f()
        cols += n
        upto = min(GQ_SLABS, (cols * GQ_SLABS + total - 1) // total)
        for s in range(done, upto):
            conv_slab(s)
        done = upto
    assert done == GQ_SLABS


def _in_proj(x2d, gain, w_arr, tm, seq=None, conv_buf=None, w_conv=None):
    rows, d = x2d.shape
    fused = seq is not None
    nt = seq // tm if fused else None
    names = [n for n, _ in IN_SEGS]
    widths = dict(IN_SEGS)
    row_spec = lambda n: pl.BlockSpec((tm, n), lambda i: (i, 0))
    out_specs, out_shape = [], []
    for n in names:
        out_specs.append(row_spec(widths[n]))
        out_shape.append(jax.ShapeDtypeStruct((rows, widths[n]), F32))
        if n == "kv2" and fused:
            for w, _ in DIL_GROUPS:
                keep = min(w, seq)
                bw = min(tm, keep)
                skip = nt - keep // bw if bw == tm else nt - 1
                out_specs.append(pl.BlockSpec((None, 2 * DIL_GW, bw),
                                              lambda i, skip=skip: (i // nt, 0, jnp.maximum(i % nt - skip, 0))))
                out_shape.append(jax.ShapeDtypeStruct((rows // seq, 2 * DIL_GW, keep), F32))
        if n == "gq" and fused:
            out_specs.append(pl.BlockSpec((None, SUBLANES, 3 * GDN_W), lambda i: (i // nt, 0, 0)))
            out_shape.append(jax.ShapeDtypeStruct((rows // seq, SUBLANES, 3 * GDN_W), F32))
    in_specs = [pl.BlockSpec((tm, d), lambda i: (0 if fused else i, 0)), _resident((1, d)),
                _resident(w_arr[0].shape), _resident(w_arr[1].shape)]
    args = [x2d, gain.reshape(1, d), *w_arr]
    scratch = []
    if fused:
        in_specs += [pl.BlockSpec((tm, d), lambda i: (jnp.minimum(i + 1, rows // tm - 1), 0)),
                     pl.BlockSpec((None, 3, 3 * GDN_W), lambda i: (i // nt, 0, 0)), _resident(w_conv.shape)]
        args += [x2d, conv_buf, w_conv]
        scratch = [pltpu.VMEM((GQ_SLABS, CONV_PAD + tm, LANES), F32), pltpu.VMEM((2, tm, d), BF16)]
    return pl.pallas_call(
        functools.partial(_in_proj_kernel, seq_tiles=nt),
        grid=(rows // tm,),
        in_specs=in_specs,
        out_specs=out_specs,
        out_shape=out_shape,
        scratch_shapes=scratch,
        compiler_params=_cparams(("arbitrary",)),
        name="in_proj",
    )(*args)


def _rel_bucket(dist):
    exact = REL_BUCKETS // 2
    d = jnp.maximum(dist, 1).astype(F32)
    large = exact + (jnp.log(d / exact) / math.log(REL_MAX_DIST / exact) * (REL_BUCKETS - exact)).astype(jnp.int32)
    return jnp.where(dist < exact, dist, jnp.minimum(large, REL_BUCKETS - 1))


def _group_bias(rel_bias, g):
    dil = DIL_GROUPS[g][1]
    dist = dil * jnp.arange(DIL_NK, dtype=jnp.int32)
    tab = rel_bias[_rel_bucket(dist)]
    return tab[:, g * DIL_HPG:(g + 1) * DIL_HPG].T.astype(F32)


def _toeplitz(v, n, width):
    h, L = v.shape
    return jnp.tile(v, (1, n))[:, :n * (L - 1)].reshape(h, n, L - 1)[:, :, :width]


def _prompt_bias_tables(rel_bias):
    cat, cur = [], []
    for g in range(3):
        bg = _group_bias(rel_bias, g)
        v = jnp.concatenate([bg[:, ::-1], jnp.full((DIL_HPG, 3 * TILE_Q - DIL_NK), NEG, F32)], axis=1)
        t = _toeplitz(v, TILE_Q, 2 * TILE_Q)
        cat.append(t)
        cur.append(t[:, :, TILE_Q:])
    return cat, cur


DIL_TIF = 2
DIL_SLABS = DIL_GW // LANES


def _dil_prompt_kernel(q0_ref, q1_ref, k0_ref, k1_ref, v0_ref, v1_ref, tcat_ref, tcur_ref,
                       o0_ref, o1_ref, l0_ref, l1_ref, *, dil):
    S = q0_ref.shape[0]
    nb = S // dil // TILE_Q
    q_refs, k_refs, v_refs = (q0_ref, q1_ref), (k0_ref, k1_ref), (v0_ref, v1_ref)
    o_refs, l_refs = (o0_ref, o1_ref), (l0_ref, l1_ref)
    even = lax.broadcasted_iota(jnp.int32, (TILE_Q, LANES), 1) < DIL_HEAD_DIM

    def rows(r, t):
        start = r + dil * TILE_Q * t
        return pl.ds(start, TILE_Q, stride=dil) if dil > 1 else pl.ds(start, TILE_Q)

    tiles = [(r, t) for r in range(dil) for t in range(nb)]
    for i0 in range(0, len(tiles), DIL_TIF):
        grp = tiles[i0:i0 + DIL_TIF]
        qm, kc, vc = {}, {}, {}
        for ti, (r, t) in enumerate(grp):
            for sl in range(DIL_SLABS):
                qf = q_refs[sl][rows(r, t), :]
                qm[ti, 2 * sl] = jnp.where(even, qf, 0.0).astype(BF16)
                qm[ti, 2 * sl + 1] = jnp.where(even, 0.0, qf).astype(BF16)
                kc[ti, sl] = k_refs[sl][rows(r, t), :].astype(BF16)
                vc[ti, sl] = v_refs[sl][rows(r, t), :].astype(BF16)
                if t > 0:
                    kc[ti, sl] = jnp.concatenate([k_refs[sl][rows(r, t - 1), :].astype(BF16), kc[ti, sl]], axis=0)
                    vc[ti, sl] = jnp.concatenate([v_refs[sl][rows(r, t - 1), :].astype(BF16), vc[ti, sl]], axis=0)
        units = [(ti, h) for ti in range(len(grp)) for h in range(DIL_HPG)]
        s = [_dot_nt(qm[ti, h], kc[ti, h // 2]) + (tcat_ref[h] if grp[ti][1] > 0 else tcur_ref[h]) for ti, h in units]
        m = [jnp.max(x, axis=-1, keepdims=True) for x in s]
        p = [jnp.exp(x - mx) for x, mx in zip(s, m)]
        l = [jnp.sum(x, axis=-1, keepdims=True) for x in p]
        pv = [jnp.dot(p[u].astype(BF16), vc[ti, h // 2], preferred_element_type=F32) for u, (ti, h) in enumerate(units)]
        o = [pv[u] / l[u] for u in range(len(units))]
        lse = [m[u] + jnp.log(l[u]) for u in range(len(units))]
        for ti, (r, t) in enumerate(grp):
            for sl in range(DIL_SLABS):
                ue, uo = ti * DIL_HPG + 2 * sl, ti * DIL_HPG + 2 * sl + 1
                o_refs[sl][rows(r, t), :] = jnp.where(even, o[ue], o[uo])
                l_refs[sl][rows(r, t), :] = jnp.where(even, lse[ue], lse[uo])


def _dil_prompt(q, kv, g, t_cat, t_cur):
    B, S, _ = q.shape
    dil = DIL_GROUPS[g][1]
    slab = lambda c: pl.BlockSpec((None, S, LANES), lambda b: (b, 0, c))
    nq, nk = g * DIL_SLABS, 0
    outs = pl.pallas_call(
        functools.partial(_dil_prompt_kernel, dil=dil),
        grid=(B,),
        in_specs=[slab(nq), slab(nq + 1), slab(nk), slab(nk + 1), slab(nk + 2), slab(nk + 3),
                  _resident(t_cat.shape), _resident(t_cur.shape)],
        out_specs=[slab(0)] * 4,
        out_shape=[jax.ShapeDtypeStruct((B, S, LANES), F32)] * 4,
        compiler_params=_cparams(("parallel",)),
        name=f"dil_prompt_g{g}",
    )(q, q, kv, kv, kv, kv, t_cat, t_cur)
    o0, o1, l0, l1 = (x.reshape(B * S, LANES) for x in outs)
    return [o0, o1], [l0, l1]


def _sample_bias_tables(rel_bias, t_real):
    R = SAMPLE_ROWS
    tabc, tabn = [], []
    t_i = np.arange(R)[:, None]
    u_i = np.arange(R)[None, :]
    for g, (w, dil) in enumerate(DIL_GROUPS):
        bg = _group_bias(rel_bias, g)
        base = bg[:, ::-1][:, :TILE_Q]
        t0 = jnp.concatenate([base[:, :, None], jnp.full((DIL_HPG, TILE_Q, dil - 1), NEG, F32)], axis=2)
        t0 = t0.reshape(DIL_HPG, w)
        tc = _toeplitz(jnp.concatenate([t0, jnp.full((DIL_HPG, R), NEG, F32)], axis=1), R, w)
        tabc.append(jnp.where((t_i < t_real)[None], tc, 0.0).reshape(DIL_HPG * R, w))
        tn = jnp.full((DIL_HPG, R, R), NEG, F32)
        for j in range(-(-t_real // dil)):
            hit = (t_i - u_i == j * dil) & (t_i < t_real)
            tn = jnp.where(hit[None], bg[:, j][:, None, None], tn)
        tn = jnp.where((t_i >= t_real)[None], 0.0, tn)
        tabn.append(tn.reshape(DIL_HPG * R, R))
    return tabc, jnp.stack(tabn)


def _dil_sample_kernel(q_ref, n0_ref, n1_ref, n2_ref, c0_ref, c1_ref, c2_ref, tc0_ref, tc1_ref, tc2_ref, tn_ref,
                       o_ref, l_ref, oc0_ref, oc1_ref, oc2_ref, *, t_real):
    R = SAMPLE_ROWS
    rows = lax.broadcasted_iota(jnp.int32, (DIL_HPG * R, DIL_GW), 0)
    lanes = lax.broadcasted_iota(jnp.int32, (DIL_HPG * R, DIL_GW), 1)
    head_mask = (lanes // DIL_HEAD_DIM) == (rows // R)
    lane_f = lax.broadcasted_iota(jnp.int32, (2 * DIL_GW, LANES), 1)
    keep = lane_f < LANES - t_real
    sel_l = lax.broadcasted_iota(jnp.int32, (LANES, R), 0)
    sel_u = lax.broadcasted_iota(jnp.int32, (LANES, R), 1)
    selT = ((sel_l == sel_u + LANES - t_real) & (sel_u < t_real)).astype(BF16)
    groups = ((n0_ref, c0_ref, tc0_ref, oc0_ref), (n1_ref, c1_ref, tc1_ref, oc1_ref), (n2_ref, c2_ref, tc2_ref, oc2_ref))

    def fold_heads(x):
        x = jnp.where(head_mask, x, 0.0)
        return x[0:R] + x[R:2 * R] + x[2 * R:3 * R] + x[3 * R:4 * R]

    G3 = range(3)
    kvn = [groups[g][0][...] for g in G3]
    q_bd = [jnp.where(head_mask, jnp.concatenate([q_ref[:, g * DIL_GW:(g + 1) * DIL_GW]] * DIL_HPG, axis=0), 0.0)
            for g in G3]
    s_c = [_dot(q_bd[g], groups[g][1][:DIL_GW, :]) + groups[g][2][...] for g in G3]
    s_n = [_dot_nt(q_bd[g], kvn[g][:, :DIL_GW]) + tn_ref[g] for g in G3]
    m = [jnp.maximum(jnp.max(s_c[g], axis=-1, keepdims=True), jnp.max(s_n[g], axis=-1, keepdims=True)) for g in G3]
    p_c = [jnp.exp(s_c[g] - m[g]) for g in G3]
    p_n = [jnp.exp(s_n[g] - m[g]) for g in G3]
    l = [jnp.sum(p_c[g], axis=-1, keepdims=True) + jnp.sum(p_n[g], axis=-1, keepdims=True) for g in G3]
    acc = [(_dot_nt(p_c[g], groups[g][1][DIL_GW:, :]) + _dot(p_n[g], kvn[g][:, DIL_GW:])) / l[g] for g in G3]
    for g in G3:
        o_ref[:, g * DIL_GW:(g + 1) * DIL_GW] = fold_heads(acc[g])
        l_ref[:, g * DIL_GW:(g + 1) * DIL_GW] = fold_heads(jnp.broadcast_to(m[g] + jnp.log(l[g]), acc[g].shape))
    for g, (n_ref, c_ref, tc_ref, oc_ref) in enumerate(groups):
        W = c_ref.shape[1]
        hi, mid, lo = _split3(kvn[g])
        tail = (jnp.dot(selT, hi, preferred_element_type=F32) + jnp.dot(selT, mid, preferred_element_type=F32)
                + jnp.dot(selT, lo, preferred_element_type=F32)).T
        nxt = pltpu.roll(c_ref[:, 0:LANES], LANES - t_real, axis=1)
        for c in range(W // LANES):
            cur = nxt
            nxt = (pltpu.roll(c_ref[:, (c + 1) * LANES:(c + 2) * LANES], LANES - t_real, axis=1)
                   if (c + 1) * LANES < W else tail)
            oc_ref[:, c * LANES:(c + 1) * LANES] = jnp.where(keep, cur, nxt)


def _dil_sample(q, kvn, caches_t, tabc, tabn, t_real):
    B = q.shape[0]
    row = lambda n: pl.BlockSpec((None, SAMPLE_ROWS, n), lambda b: (b, 0, 0))
    cspecs = [pl.BlockSpec((None,) + c.shape[1:], lambda b: (b, 0, 0)) for c in caches_t]
    for g, (w, dil) in enumerate(DIL_GROUPS):
        assert caches_t[g].shape == (B, 2 * DIL_GW, w) and w // dil == TILE_Q
    out_spec = row(3 * DIL_GW)
    o, lse, *new_caches = pl.pallas_call(
        functools.partial(_dil_sample_kernel, t_real=t_real),
        grid=(B,),
        in_specs=([row(3 * DIL_GW)] + [row(2 * DIL_GW)] * 3 + cspecs + [_resident(t.shape) for t in tabc]
                  + [_resident(tabn.shape)]),
        out_specs=[out_spec, out_spec] + cspecs,
        out_shape=([jax.ShapeDtypeStruct((B, SAMPLE_ROWS, 3 * DIL_GW), F32)] * 2
                   + [jax.ShapeDtypeStruct(c.shape, F32) for c in caches_t]),
        compiler_params=_cparams(("parallel",)),
        name="dil_sample",
    )(q, *kvn, *caches_t, *tabc, tabn)
    o = o.reshape(B * SAMPLE_ROWS, 3 * DIL_GW)
    lse = lse.reshape(B * SAMPLE_ROWS, 3 * DIL_GW)
    n = 3 * DIL_SLABS
    return ([o[:, i * LANES:(i + 1) * LANES] for i in range(n)],
            [lse[:, i * LANES:(i + 1) * LANES] for i in range(n)], new_caches)


GDN_HPS = GDN_HEADS
GDN_SW = GDN_HPS * GDN_HEAD_DIM
GDN_TPI = 4
GDN_ROWS = 1024


def _gdn_kernel(q_ref, k_ref, v_ref, ba_ref, alog_ref, dtb_ref, z_ref, gn_ref, tri_ref, s0_ref, o_ref, s_ref, carry_s,
                *, tpi):
    T = q_ref.shape[0]
    C = GDN_CHUNK
    D = GDN_HEAD_DIM
    ri = lax.broadcasted_iota(jnp.int32, (TILE_Q, TILE_Q), 0)
    ci = lax.broadcasted_iota(jnp.int32, (TILE_Q, TILE_Q), 1)
    same = (ri // C) == (ci // C)
    incl = same & (ri >= ci)
    strict = same & (ri > ci)
    eye = (ri == ci).astype(F32)
    lane, row = ci, ri
    gain = gn_ref[...]
    zpad = jnp.zeros((C, D), F32)
    HH = range(GDN_HPS)
    hsl = [slice(hh * D, (hh + 1) * D) for hh in HH]
    RW = tpi * TILE_Q
    UU = [(tt, hh) for tt in range(tpi) for hh in HH]
    UI = range(len(UU))

    def tile(i, S):
        r0 = pl.multiple_of(i * RW, RW)
        rt = [r0 + tt * TILE_Q for tt in range(tpi)]
        q = [q_ref[pl.ds(rt[tt], TILE_Q), hsl[hh]] for tt, hh in UU]
        k = [k_ref[pl.ds(rt[tt], TILE_Q), hsl[hh]] for tt, hh in UU]
        v = [v_ref[pl.ds(rt[tt], TILE_Q), hsl[hh]] for tt, hh in UU]
        tri = tri_ref[...]
        beta_all, G_all = [], []
        for tt in range(tpi):
            ba = ba_ref[pl.ds(rt[tt], TILE_Q), :]
            beta_all.append(_sigmoid(ba))
            gh, gm, gl = _split3(-jnp.exp(alog_ref[...]) * _softplus(ba + dtb_ref[...]))
            G_all.append(jnp.dot(tri, gh, preferred_element_type=F32) + jnp.dot(tri, gm, preferred_element_type=F32)
                         + jnp.dot(tri, gl, preferred_element_type=F32))
        head = [hh for _, hh in UU]
        bc = [jnp.sum(jnp.where(lane == head[u], beta_all[UU[u][0]], 0.0), axis=-1, keepdims=True) for u in UI]
        Gc = [jnp.broadcast_to(jnp.sum(jnp.where(lane == head[u] + GDN_HEADS, G_all[UU[u][0]], 0.0), axis=-1,
                                       keepdims=True), (TILE_Q, TILE_Q)) for u in UI]
        gamma = [jnp.exp(jnp.where(incl, Gc[u] - Gc[u].T, NEG)) for u in UI]
        kk = [_dot_nt(k[u], k[u]) for u in UI]
        qk = [_dot_nt(q[u], k[u]) for u in UI]
        lo = lax.broadcasted_iota(jnp.int32, (C, TILE_Q), 1) < C

        def pack(m):
            return jnp.where(lo, m[:C], m[C:])

        def blockdiag(p):
            return jnp.concatenate([jnp.where(lo, p, 0.0), jnp.where(lo, 0.0, p)], axis=0)

        eye_p = pack(eye)
        Xp = [pack(jnp.where(strict, bc[u] * kk[u] * gamma[u], 0.0)) for u in UI]
        Pp = [eye_p - Xp[u] for u in UI]
        Xb = [blockdiag(Xp[u]).astype(BF16) for u in UI]
        for _ in range(int(math.log2(C)) - 1):
            Xp = [jnp.dot(Xp[u].astype(BF16), Xb[u], preferred_element_type=F32) for u in UI]
            Xb = [blockdiag(Xp[u]).astype(BF16) for u in UI]
            Pp = [Pp[u] + jnp.dot(Pp[u].astype(BF16), Xb[u], preferred_element_type=F32) for u in UI]
        eG = [jnp.exp(Gc[u]) for u in UI]
        rhs = [jnp.concatenate([v[u] * bc[u], k[u] * (bc[u] * eG[u])], axis=-1) for u in UI]
        sol = [rhs[u] + _dot(blockdiag(Pp[u] - eye_p), rhs[u]) for u in UI]
        a_in = [qk[u] * gamma[u] for u in UI]
        q_dec = [q[u] * eG[u] for u in UI]
        kdT = [(k[u] * jnp.exp(jnp.where(row < C, Gc[u][C - 1:C, :], Gc[u][2 * C - 1:2 * C, :]) - Gc[u])).T
               for u in UI]
        S = list(S)
        for tt in range(tpi):
            us = [tt * GDN_HPS + hh for hh in HH]
            oq, vn = [[] for _ in HH], [[] for _ in HH]
            for c in range(TILE_Q // C):
                cs = slice(c * C, (c + 1) * C)
                r = [_dot(jnp.concatenate([sol[us[hh]][cs, D:], q_dec[us[hh]][cs]], axis=0), S[hh]) for hh in HH]
                for hh in HH:
                    oq[hh].append(r[hh][C:])
                    vn[hh].append(sol[us[hh]][cs, :D] - r[hh][:C])
                vpad = [jnp.concatenate([vn[hh][c], zpad] if c == 0 else [zpad, vn[hh][c]], axis=0) for hh in HH]
                S = [S[hh] * jnp.exp(Gc[us[hh]][(c + 1) * C - 1:(c + 1) * C, :]) + _dot(kdT[us[hh]], vpad[hh])
                     for hh in HH]
            o = [jnp.concatenate(oq[hh], axis=0) + _dot(a_in[us[hh]], jnp.concatenate(vn[hh], axis=0)) for hh in HH]
            outs = [_rms(o[hh], gain) * z_ref[pl.ds(rt[tt], TILE_Q), hsl[hh]] for hh in HH]
            o_ref[pl.ds(rt[tt], TILE_Q), :] = jnp.concatenate(outs, axis=-1).astype(o_ref.dtype)
        return tuple(S)

    @pl.when(pl.program_id(1) == 0)
    def _():
        carry_s[...] = s0_ref[...]

    S = lax.fori_loop(0, T // RW, tile, tuple(carry_s[hh] for hh in HH))
    for hh in HH:
        carry_s[hh] = S[hh]
        s_ref[hh] = S[hh]


def _gdn(gq, ba, z, s0, a_log, dt_bias, norm_out):
    B, T, _ = gq.shape
    H = GDN_HEADS
    pad16 = lambda x: jnp.concatenate([jnp.zeros((H,), F32), x.astype(F32), jnp.zeros((LANES - 2 * H,), F32)])
    assert GDN_HPS == H
    tb = min(GDN_ROWS, T)
    tpi = GDN_TPI if (tb // TILE_Q) % GDN_TPI == 0 else 1
    col = lambda off: pl.BlockSpec((None, tb, GDN_SW), lambda b, j: (b, j, off))
    sblk = pl.BlockSpec((None, H, GDN_HEAD_DIM, GDN_HEAD_DIM), lambda b, j: (b, 0, 0, 0))
    r = np.arange(TILE_Q)
    tri = jnp.asarray((r[:, None] >= r[None, :]) & (r[:, None] // GDN_CHUNK == r[None, :] // GDN_CHUNK), BF16)
    o, s_new = pl.pallas_call(
        functools.partial(_gdn_kernel, tpi=tpi),
        grid=(B, T // tb),
        in_specs=[col(0), col(1), col(2),
                  pl.BlockSpec((None, tb, LANES), lambda b, j: (b, j, 0)),
                  _resident((1, LANES)), _resident((1, LANES)),
                  col(0), _resident((1, LANES)), _resident((TILE_Q, TILE_Q)), sblk],
        out_specs=[col(0), sblk],
        out_shape=[jax.ShapeDtypeStruct((B, T, GDN_W), BF16),
                   jax.ShapeDtypeStruct((B, H, GDN_HEAD_DIM, GDN_HEAD_DIM), F32)],
        scratch_shapes=[pltpu.VMEM((H, GDN_HEAD_DIM, GDN_HEAD_DIM), F32)],
        compiler_params=_cparams(("parallel", "arbitrary")),
        name="gdn",
    )(gq, gq, gq, ba, pad16(a_log).reshape(1, LANES), pad16(dt_bias).reshape(1, LANES), z,
      norm_out.reshape(1, LANES), tri, s0)
    return o, s_new


def _gdn_sample_kernel(x_ref, b_ref, w_ref, ba_ref, alog_ref, dtb_ref, z_ref, gn_ref, s0_ref, o_ref, s_ref, xs,
                       *, t_real):
    R, D, H = SAMPLE_ROWS, GDN_HEAD_DIM, GDN_HEADS
    nb = x_ref.shape[0]
    ri = lax.broadcasted_iota(jnp.int32, (R, R), 0)
    ci = lax.broadcasted_iota(jnp.int32, (R, R), 1)
    incl, strict = ri >= ci, ri > ci
    eye = (ri == ci).astype(F32)
    tri = incl.astype(BF16)
    er = lax.broadcasted_iota(jnp.int32, (LANES, LANES), 0)
    ec = lax.broadcasted_iota(jnp.int32, (LANES, LANES), 1)
    eye_l = (er == ec).astype(BF16)
    live = lax.broadcasted_iota(jnp.int32, (R, LANES), 0) < t_real
    d32 = functools.partial(jnp.dot, preferred_element_type=F32)
    nt = lambda a, b: lax.dot_general(a, b, (((1,), (1,)), ((), ())), preferred_element_type=F32)
    gain = gn_ref[...]
    y, beta_all, G_all, G_allT = [], [], [], []
    for s in range(nb):
        xs[s, CONV_PAD - 3:CONV_PAD, :] = b_ref[s]
        xs[s, CONV_PAD:CONV_PAD + R, :] = x_ref[s]
        ys = xs[s, CONV_PAD - 3:CONV_PAD - 3 + R, :] * w_ref[0:1, :]
        for j in range(1, 4):
            ys = ys + xs[s, CONV_PAD - 3 + j:CONV_PAD - 3 + j + R, :] * w_ref[j:j + 1, :]
        y.append(_silu(ys))
        ba = ba_ref[s]
        beta_all.append(jnp.where(live, _sigmoid(ba), 0.0))
        gh, gm, gl = _split3(jnp.where(live, -jnp.exp(alog_ref[...]) * _softplus(ba + dtb_ref[...]), 0.0))
        G = d32(tri, gh) + d32(tri, gm) + d32(tri, gl)
        th, tm, tl = _split3(G)
        G_all.append(G)
        G_allT.append(nt(eye_l, th) + nt(eye_l, tm) + nt(eye_l, tl))
    US = [(s, h) for s in range(nb) for h in range(H)]
    UI = range(len(US))

    q = [_l2n(y[s][:, h * D:(h + 1) * D]) * (D ** -0.5) for s, h in US]
    k = [_l2n(y[s][:, GDN_W + h * D:GDN_W + (h + 1) * D]) for s, h in US]
    v = [y[s][:, 2 * GDN_W + h * D:2 * GDN_W + (h + 1) * D] for s, h in US]
    bc = [beta_all[s][:, h:h + 1] for s, h in US]
    Gc = [G_all[s][:, H + h:H + h + 1] for s, h in US]
    gamma = [jnp.exp(jnp.where(incl, Gc[u] - G_allT[s][H + h:H + h + 1, :], NEG)) for u, (s, h) in enumerate(US)]
    kk = [_dot_nt(k[u], k[u]) for u in UI]
    qk = [_dot_nt(q[u], k[u]) for u in UI]
    X = [jnp.where(strict, bc[u] * kk[u] * gamma[u], 0.0) for u in UI]
    P = [eye - X[u] for u in UI]
    for _ in range(int(math.log2(R)) - 1):
        X = [_dot(X[u], X[u]) for u in UI]
        P = [P[u] + _dot(P[u], X[u]) for u in UI]
    eG = [jnp.exp(Gc[u]) for u in UI]
    rhs = [jnp.concatenate([v[u] * bc[u], k[u] * (bc[u] * eG[u])], axis=-1) for u in UI]
    sol = [rhs[u] + _dot(P[u] - eye, rhs[u]) for u in UI]
    S = [s0_ref[s, h] for s, h in US]
    r = [_dot(jnp.concatenate([sol[u][:, D:], q[u] * eG[u]], axis=0), S[u]) for u in UI]
    v_new = [sol[u][:, :D] - r[u][:R] for u in UI]
    o = [r[u][R:] + _dot(qk[u] * gamma[u], v_new[u]) for u in UI]
    kdT = [nt(eye_l, (k[u] * jnp.exp(Gc[u][R - 1:R, :] - Gc[u])).astype(BF16)) for u in UI]
    for u, (s, h) in enumerate(US):
        s_ref[s, h] = S[u] * jnp.exp(Gc[u][R - 1:R, :]) + _dot(kdT[u], v_new[u])
    for s in range(nb):
        o_ref[s] = jnp.concatenate([_rms(o[s * H + h], gain) * _silu(z_ref[s, :, h * D:(h + 1) * D])
                                    for h in range(H)], axis=-1)


GDN_SAMPLE_SEQS = 4


def _gdn_sample(gq, ba, z, conv_buf, s0, w_conv, a_log, dt_bias, norm_out, t_real):
    B, R, _ = gq.shape
    H = GDN_HEADS
    pad16 = lambda x: jnp.concatenate([jnp.zeros((H,), F32), x.astype(F32), jnp.zeros((LANES - 2 * H,), F32)])
    nb = GDN_SAMPLE_SEQS if B % GDN_SAMPLE_SEQS == 0 else 1
    blk = lambda *s: pl.BlockSpec((nb,) + s, lambda b: (b,) + (0,) * len(s))
    return pl.pallas_call(
        functools.partial(_gdn_sample_kernel, t_real=t_real),
        grid=(B // nb,),
        in_specs=[blk(R, 3 * GDN_W), blk(3, 3 * GDN_W), _resident(w_conv.shape), blk(R, LANES),
                  _resident((1, LANES)), _resident((1, LANES)), blk(R, GDN_W), _resident((1, LANES)),
                  blk(H, GDN_HEAD_DIM, GDN_HEAD_DIM)],
        out_specs=[blk(R, GDN_W), blk(H, GDN_HEAD_DIM, GDN_HEAD_DIM)],
        out_shape=[jax.ShapeDtypeStruct((B, R, GDN_W), F32),
                   jax.ShapeDtypeStruct((B, H, GDN_HEAD_DIM, GDN_HEAD_DIM), F32)],
        scratch_shapes=[pltpu.VMEM((nb, CONV_PAD + R, 3 * GDN_W), F32)],
        compiler_params=_cparams(("parallel",)),
        name="gdn_sample",
    )(gq, conv_buf, w_conv, ba, pad16(a_log).reshape(1, LANES), pad16(dt_bias).reshape(1, LANES), z,
      norm_out.reshape(1, LANES), s0)


N_OG = 3 * DIL_SLABS


def _mix_kernel(*refs):
    h1, qm = _mix_body(*refs[:2 * N_OG + 9])
    h1_ref, qm_ref = refs[2 * N_OG + 9:]
    h1_ref[...] = h1
    qm_ref[...] = qm


def _mix_body(*refs):
    o_refs, l_refs = refs[:N_OG], refs[N_OG:2 * N_OG]
    ob_ref, ga_ref, gb_ref, h_ref, wa_ref, wb_ref, wo_ref, gq_ref, wq_ref = refs[2 * N_OG:]
    slabs = []
    for sl in range(DIL_SLABS):
        l0, l1, l2 = (l_refs[g * DIL_SLABS + sl][...] for g in range(3))
        o0, o1, o2 = (o_refs[g * DIL_SLABS + sl][...] for g in range(3))
        mx = jnp.maximum(jnp.maximum(l0, l1), l2)
        e0, e1, e2 = jnp.exp(l0 - mx), jnp.exp(l1 - mx), jnp.exp(l2 - mx)
        slabs.append((e0 * o0 + e1 * o1 + e2 * o2) / (e0 + e1 + e2))
    o_a = jnp.concatenate(slabs, axis=-1)
    a = _dot(o_a, wa_ref[...])
    b = jnp.dot(ob_ref[...], wb_ref[...], preferred_element_type=F32)
    merged = _sigmoid(ga_ref[...]) * a + _sigmoid(gb_ref[...]) * b
    h1 = h_ref[...] + _dot(merged, wo_ref[...])
    return h1, _dot(_rms(h1, gq_ref[...]), wq_ref[...]).astype(BF16)


def _mix(o_g, l_g, o_b, ga, gb, h, w_a, w_b, w_o, norm_mem_q, w_mem_q, tm):
    rows, d = h.shape
    rt = lambda n: pl.BlockSpec((tm, n), lambda i: (i, 0))
    assert len(o_g) == len(l_g) == 3 * DIL_SLABS
    return pl.pallas_call(
        _mix_kernel,
        grid=(rows // tm,),
        in_specs=[rt(LANES)] * (6 * DIL_SLABS) + [rt(GDN_W), rt(d), rt(d), rt(d),
                                     _resident(w_a.shape), _resident(w_b.shape), _resident(w_o.shape),
                                     _resident((1, d)), _resident(w_mem_q.shape)],
        out_specs=[rt(d), rt(w_mem_q.shape[1])],
        out_shape=[jax.ShapeDtypeStruct((rows, d), F32), jax.ShapeDtypeStruct((rows, w_mem_q.shape[1]), BF16)],
        compiler_params=_cparams(("parallel",)),
        name="mix",
    )(*o_g, *l_g, o_b, ga, gb, h, w_a, w_b, w_o, norm_mem_q.reshape(1, d), w_mem_q)


def _mem_kv_kernel(x_ref, g_ref, w_ref, k_ref, v_ref):
    u = _rms(x_ref[...], g_ref[...]).astype(BF16)
    tm = x_ref.shape[0]
    n = MEM_HEADS * MEM_HEAD_DIM
    for o_ref, off in ((k_ref, 0), (v_ref, n)):
        r = jnp.dot(u, w_ref[:, off:off + n], preferred_element_type=F32)
        for h in range(MEM_HEADS):
            o_ref[pl.ds(h, tm, stride=MEM_HEADS), :] = r[:, h * MEM_HEAD_DIM:(h + 1) * MEM_HEAD_DIM]


def _mem_kv(mem2d, gain, w, tm):
    rows, d = mem2d.shape
    return pl.pallas_call(
        _mem_kv_kernel,
        grid=(rows // tm,),
        in_specs=[pl.BlockSpec((tm, d), lambda i: (i, 0)), _resident((1, d)), _resident(w.shape)],
        out_specs=[pl.BlockSpec((tm * MEM_HEADS, MEM_HEAD_DIM), lambda i: (i, 0))] * 2,
        out_shape=[jax.ShapeDtypeStruct((rows * MEM_HEADS, MEM_HEAD_DIM), F32)] * 2,
        compiler_params=_cparams(("parallel",)),
        name="mem_kv",
    )(mem2d, gain.reshape(1, d), w)


def _mem_attn_body(q, k_refs, v_refs):
    nb = len(q)
    M = k_refs[0].shape[0] // MEM_HEADS
    units = [(b, h) for b in range(nb) for h in range(MEM_HEADS)]
    s = [_dot_nt(q[b][:, h * MEM_HEAD_DIM:(h + 1) * MEM_HEAD_DIM], k_refs[b][pl.ds(h, M, stride=MEM_HEADS), :])
         * (MEM_HEAD_DIM ** -0.5) for b, h in units]
    p = [jnp.exp(x - jnp.max(x, axis=-1, keepdims=True)) for x in s]
    o = [_dot(p[u], v_refs[b][pl.ds(h, M, stride=MEM_HEADS), :]) / jnp.sum(p[u], axis=-1, keepdims=True)
         for u, (b, h) in enumerate(units)]
    return [jnp.concatenate(o[b * MEM_HEADS:(b + 1) * MEM_HEADS], axis=-1) for b in range(nb)]


def _mem_attn_kernel(q_ref, k_ref, v_ref, o_ref):
    nb = q_ref.shape[0]
    o = _mem_attn_body([q_ref[b] for b in range(nb)], [k_ref.at[b] for b in range(nb)],
                       [v_ref.at[b] for b in range(nb)])
    for b in range(nb):
        o_ref[b] = o[b]


def _mem_attn(qm, mem_k, mem_v, tm, nb):
    B, T, w = qm.shape
    kv_spec = pl.BlockSpec((nb,) + mem_k.shape[1:], lambda b, j: (b, 0, 0))
    return pl.pallas_call(
        _mem_attn_kernel,
        grid=(B // nb, T // tm),
        in_specs=[pl.BlockSpec((nb, tm, w), lambda b, j: (b, j, 0)), kv_spec, kv_spec],
        out_specs=pl.BlockSpec((nb, tm, w), lambda b, j: (b, j, 0)),
        out_shape=jax.ShapeDtypeStruct((B, T, w), F32),
        compiler_params=_cparams(("parallel", "parallel")),
        name="mem_attn",
    )(qm, mem_k, mem_v)


def _ffn_kernel(*refs, inject, emit_gate):
    if inject:
        h1_ref, om_ref, init_ref, fill_ref = refs[:4]
        rest = refs[4:]
    else:
        h1_ref, om_ref, init_ref = refs[:3]
        fill_ref, rest = None, refs[3:]
    _ffn_body(h1_ref[...], om_ref[...], init_ref, fill_ref, *rest, emit_gate=emit_gate)


def _ffn_body(h1, om, init_ref, fill_ref, wmo_ref, gf_ref, wup_ref, wc_ref, bc_ref, wd_ref, gfin_ref,
              y_ref, fc_ref, gs, *, emit_gate):
    tm = h1.shape[0]
    F = wd_ref.shape[0]
    PAD = SUBLANES

    @pl.when(pl.program_id(1) == 0)
    def _():
        gs[PAD - 2:PAD, :] = init_ref[...]

    h2 = h1 + _dot(om, wmo_ref[...])
    n = _rms(h2, gf_ref[...]).astype(BF16)
    gate = jnp.dot(n, wup_ref[:, :F], preferred_element_type=F32)
    if fill_ref is not None:
        r = lax.broadcasted_iota(jnp.int32, (tm, 1), 0)
        gate = jnp.where((r % SAMPLE_ROWS) >= SAMPLE_ROWS - 2, fill_ref[...], gate)
    gs[PAD:PAD + tm, :] = gate
    conv = (gs[PAD - 2:PAD - 2 + tm, :] * wc_ref[0:1, :] + gs[PAD - 1:PAD - 1 + tm, :] * wc_ref[1:2, :]
            + gate * wc_ref[2:3, :])
    last2 = gs[PAD + tm - 2:PAD + tm, :]
    gs[PAD - 2:PAD, :] = last2
    if emit_gate:
        fc_ref[...] = gate
    else:
        fc_ref[...] = last2
    up = jnp.dot(n, wup_ref[:, F:], preferred_element_type=F32)
    act = _silu(conv + bc_ref[...]) * up
    y = h2 + _dot(act, wd_ref[...])
    y_ref[...] = _rms(y, gfin_ref[...])


def _ffn(h1, om, init, fill, w_mo, norm_ffn, w_up, w_conv, b_conv, w_down, norm_final, tm, emit_gate):
    B, T, d = h1.shape
    F = w_down.shape[0]
    inject = fill is not None
    rt = lambda n: pl.BlockSpec((None, tm, n), lambda b, j: (b, j, 0))
    in_specs = [rt(d), rt(om.shape[-1]), pl.BlockSpec((None, 2, F), lambda b, j: (b, 0, 0))]
    args = [h1, om, init]
    if inject:
        in_specs.append(rt(F))
        args.append(fill)
    in_specs += [_resident(w_mo.shape), _resident((1, d)), _resident(w_up.shape), _resident(w_conv.shape),
                 _resident((1, F)), _resident(w_down.shape), _resident((1, d))]
    args += [w_mo, norm_ffn.reshape(1, d), w_up, w_conv, b_conv.reshape(1, F), w_down, norm_final.reshape(1, d)]
    if emit_gate:
        fc_spec, fc_shape = rt(F), jax.ShapeDtypeStruct((B, T, F), F32)
    else:
        fc_spec = pl.BlockSpec((None, 2, F), lambda b, j: (b, 0, 0))
        fc_shape = jax.ShapeDtypeStruct((B, 2, F), F32)
    return pl.pallas_call(
        functools.partial(_ffn_kernel, inject=inject, emit_gate=emit_gate),
        grid=(B, T // tm),
        in_specs=in_specs,
        out_specs=[rt(d), fc_spec],
        out_shape=[jax.ShapeDtypeStruct((B, T, d), F32), fc_shape],
        scratch_shapes=[pltpu.VMEM((tm + SUBLANES, F), F32)],
        compiler_params=_cparams(("parallel", "arbitrary")),
        name="ffn",
    )(*args)


def _post_kernel(*refs):
    n_mix = 2 * N_OG + 4
    mix_in, (k_ref, v_ref, init_ref), rest = refs[:n_mix], refs[n_mix:n_mix + 3], refs[n_mix + 3:]
    mix_w, ffn_rest = rest[:5], rest[5:]
    h1, qm = _mix_body(*mix_in, *mix_w)
    om, = _mem_attn_body([qm], [k_ref], [v_ref])
    _ffn_body(h1, om, init_ref, None, *ffn_rest, emit_gate=False)


def _post(o_g, l_g, o_b, ga, gb, h, mem_k, mem_v, init, w_a, w_b, w_o, norm_mem_q, w_mem_q, w_mo, norm_ffn, w_up,
          w_conv, b_conv, w_down, norm_final, tm):
    B, T, d = h.shape
    F = w_down.shape[0]
    rt = lambda n: pl.BlockSpec((None, tm, n), lambda b, j: (b, j, 0))
    per_b = lambda a: pl.BlockSpec((None,) + a.shape[1:], lambda b, j: (b, 0, 0))
    weights = [w_a, w_b, w_o, norm_mem_q.reshape(1, d), w_mem_q, w_mo, norm_ffn.reshape(1, d), w_up, w_conv,
               b_conv.reshape(1, F), w_down, norm_final.reshape(1, d)]
    return pl.pallas_call(
        _post_kernel,
        grid=(B, T // tm),
        in_specs=([rt(LANES)] * (2 * N_OG) + [rt(GDN_W), rt(d), rt(d), rt(d), per_b(mem_k), per_b(mem_v), per_b(init)]
                  + [_resident(w.shape) for w in weights]),
        out_specs=[rt(d), pl.BlockSpec((None, 2, F), lambda b, j: (b, 0, 0))],
        out_shape=[jax.ShapeDtypeStruct((B, T, d), F32), jax.ShapeDtypeStruct((B, 2, F), F32)],
        scratch_shapes=[pltpu.VMEM((tm + SUBLANES, F), F32)],
        compiler_params=_cparams(("parallel", "arbitrary")),
        name="post",
    )(*o_g, *l_g, o_b, ga, gb, h, mem_k, mem_v, init, *weights)


def kernel(x_prompt, x_sample, cache_dil0_kv, cache_dil1_kv, cache_dil2_kv, state_delta, state_delta_conv, cache_mem_k, cache_mem_v, state_ffn_conv, mem_prompt, rel_bias, norm_mix, w_in, w_conv_delta, a_log, dt_bias, norm_delta_out, w_branch_a, w_branch_b, w_out, norm_mem_q, norm_mem_kv, w_mem_q, w_mem_kv, w_mem_o, norm_ffn, w_ffn_up, w_ffn_conv, b_ffn_conv, w_ffn_down, norm_final):
    B, S, D = x_prompt.shape
    Bs, Ts, _ = x_sample.shape
    depth = w_in.shape[0]
    assert depth == 1 and D == D_MODEL and 3 <= Ts <= SAMPLE_ROWS - 2 and S % (16 * TILE_Q) == 0
    assert PAST_LEN >= max(w for w, _ in DIL_GROUPS)
    F = w_ffn_down.shape[1]
    M = mem_prompt.shape[1]
    l = 0
    w_arr = _arrange_w_in(w_in[l])
    w_a, w_b, w_o = (w.astype(BF16) for w in (w_branch_a[l], w_branch_b[l], w_out[l]))
    w_mq, w_mkv, w_mo = (w.astype(BF16) for w in (w_mem_q[l], w_mem_kv[l], w_mem_o[l]))
    w_up, w_dn = w_ffn_up[l].astype(BF16), w_ffn_down[l].astype(BF16)
    t_cat, t_cur = _prompt_bias_tables(rel_bias)

    xp = x_prompt.reshape(B * S, D)
    q, kv0, kv1, kv2, kt0, kt1, kt2, gq, gq_tail, z, ba, ga, gb = _in_proj(
        xp, norm_mix[l], w_arr, ROW_TILE, seq=S, conv_buf=jnp.zeros((B, 3, 3 * GDN_W), F32), w_conv=w_conv_delta[l])
    kvs = [kv.reshape(B, S, 2 * DIL_GW) for kv in (kv0, kv1, kv2)]
    q3 = q.reshape(B, S, 3 * DIL_GW)
    o_g, l_g = [], []
    for g in range(3):
        o_sl, l_sl = _dil_prompt(q3, kvs[g], g, t_cat[g], t_cur[g])
        o_g += o_sl
        l_g += l_sl
    o_b, delta_p = _gdn(gq.reshape(B, S, -1), ba.reshape(B, S, LANES), z.reshape(B, S, GDN_W),
                        jnp.zeros((B, GDN_HEADS, GDN_HEAD_DIM, GDN_HEAD_DIM), F32), a_log[l], dt_bias[l],
                        norm_delta_out[l])
    mk_p, mv_p = _mem_kv(mem_prompt.reshape(B * M, D), norm_mem_kv[l], w_mkv, ROW_TILE)
    mk_p, mv_p = (x.reshape(B, M * MEM_HEADS, MEM_HEAD_DIM) for x in (mk_p, mv_p))
    seq = lambda a: a.reshape(B, S, a.shape[-1])
    y_p, fconv_p = _post([seq(a) for a in o_g], [seq(a) for a in l_g], o_b, seq(ga), seq(gb), x_prompt, mk_p, mv_p,
                         jnp.zeros((B, 2, F), F32), w_a, w_b, w_o, norm_mem_q[l], w_mq, w_mo, norm_ffn[l], w_up,
                         w_ffn_conv[l], b_ffn_conv[l], w_dn, norm_final, ROW_TILE)
    p_out = ([kt.reshape(B, 2, DIL_HPG, DIL_HEAD_DIM, kt.shape[2]).transpose(0, 4, 1, 2, 3)[None]
              for kt in (kt0, kt1, kt2)]
             + [delta_p[None], gq_tail[:, SUBLANES - 3:][None], mk_p.reshape(1, B, M, MEM_HEADS, MEM_HEAD_DIM),
                mv_p.reshape(1, B, M, MEM_HEADS, MEM_HEAD_DIM), fconv_p[None]])

    R = SAMPLE_ROWS
    xs = jnp.pad(x_sample, ((0, 0), (0, R - Ts), (0, 0))).reshape(Bs * R, D)
    q, kv0, kv1, kv2, gq, z, ba, ga, gb = _in_proj(xs, norm_mix[l], w_arr, Bs * R)
    kvn = [kv.reshape(Bs, R, 2 * DIL_GW) for kv in (kv0, kv1, kv2)]
    caches_t = [jnp.transpose(c[l], (0, 2, 3, 4, 1)).reshape(Bs, 2 * DIL_GW, c.shape[2])
                for c in (cache_dil0_kv, cache_dil1_kv, cache_dil2_kv)]
    tabc, tabn = _sample_bias_tables(rel_bias, Ts)
    o_g, l_g, new_caches = _dil_sample(q.reshape(Bs, R, 3 * DIL_GW), kvn, caches_t, tabc, tabn, Ts)
    o_b, delta_s = _gdn_sample(gq.reshape(Bs, R, -1), ba.reshape(Bs, R, LANES), z.reshape(Bs, R, GDN_W),
                               state_delta_conv[l], state_delta[l], w_conv_delta[l], a_log[l], dt_bias[l],
                               norm_delta_out[l], Ts)
    o_b = o_b.reshape(Bs * R, GDN_W).astype(BF16)
    h1, qm = _mix(o_g, l_g, o_b, ga, gb, xs, w_a, w_b, w_o, norm_mem_q[l], w_mq, Bs * R)
    om = _mem_attn(qm.reshape(Bs, R, -1), cache_mem_k[l].reshape(Bs, M * MEM_HEADS, MEM_HEAD_DIM),
                   cache_mem_v[l].reshape(Bs, M * MEM_HEADS, MEM_HEAD_DIM), R, 8 if Bs % 8 == 0 else 1)
    fst = state_ffn_conv[l]
    fill = jnp.concatenate([jnp.zeros((Bs, R - 2, F), F32),
                            jnp.concatenate([fst[1:], jnp.zeros((1, 2, F), F32)], axis=0)], axis=1)
    y_s, gate_s = _ffn(h1.reshape(1, Bs * R, D), om.reshape(1, Bs * R, -1), fst[:1], fill.reshape(1, Bs * R, F),
                       w_mo, norm_ffn[l], w_up, w_ffn_conv[l], b_ffn_conv[l], w_dn, norm_final, Bs * R, True)
    y_s = y_s.reshape(Bs, R, D)[:, :Ts]
    gq3 = gq.reshape(Bs, R, -1)
    s_out = ([nc.reshape(Bs, 2, DIL_HPG, DIL_HEAD_DIM, nc.shape[2]).transpose(0, 4, 1, 2, 3)[None]
              for nc in new_caches]
             + [delta_s[None], gq3[:, Ts - 3:Ts][None], gate_s.reshape(Bs, R, F)[:, Ts - 2:Ts][None]])

    return (y_p.reshape(B, S, D), y_s, *p_out, *s_out)
```

```python
import functools
import math

import jax
import jax.numpy as jnp
import numpy as np
from jax import lax
from jax.experimental import pallas as pl
from jax.experimental.pallas import tpu as pltpu

F32 = jnp.float32
BF16 = jnp.bfloat16

PAST_LEN = 8192
DIL_GROUPS = ((128, 1), (512, 4), (2048, 16))
DIL_HPG = 4
DIL_HEAD_DIM = 64
DIL_GW = DIL_HPG * DIL_HEAD_DIM
DIL_NK = 129
REL_BUCKETS = 32
REL_MAX_DIST = 2048
GDN_HEADS = 8
GDN_HEAD_DIM = 128
GDN_W = GDN_HEADS * GDN_HEAD_DIM
GDN_CHUNK = 64
MEM_HEADS = 4
MEM_HEAD_DIM = 128
EPS = 1e-6
NEG = -1e30

LANES = 128
SUBLANES = 8
TILE_Q = 128
D_MODEL = 1024
ROW_TILE = 256
SAMPLE_ROWS = SUBLANES
CONV_PAD = SUBLANES
VMEM_LIMIT = 56 * 1024 * 1024


def _cparams(sem):
    return pltpu.CompilerParams(dimension_semantics=sem, vmem_limit_bytes=VMEM_LIMIT)


def _resident(shape):
    nd = len(shape)
    return pl.BlockSpec(shape, lambda *_: (0,) * nd, pipeline_mode=pl.Buffered(1))


def _rms(x, gain_row):
    return x * lax.rsqrt(jnp.mean(x * x, axis=-1, keepdims=True) + EPS) * gain_row


def _dot(a, b):
    return jnp.dot(a.astype(BF16), b.astype(BF16), preferred_element_type=F32)


def _dot_nt(a, b):
    return lax.dot_general(a.astype(BF16), b.astype(BF16), (((1,), (1,)), ((), ())), preferred_element_type=F32)


def _split3(x):
    hi = x.astype(BF16)
    r1 = x - hi.astype(F32)
    mid = r1.astype(BF16)
    lo = (r1 - mid.astype(F32)).astype(BF16)
    return hi, mid, lo


def _l2n(x):
    return x * lax.rsqrt(jnp.sum(x * x, axis=-1, keepdims=True) + EPS)


def _sigmoid(x):
    return 1.0 / (1.0 + jnp.exp(-x))


def _silu(x):
    return x * _sigmoid(x)


def _softplus(x):
    return jnp.maximum(x, 0.0) + jnp.log(1.0 + jnp.exp(-jnp.abs(x)))


IN_SEGS = (("q", 3 * DIL_GW), ("kv0", 2 * DIL_GW), ("kv1", 2 * DIL_GW), ("kv2", 2 * DIL_GW),
           ("gq", 3 * GDN_W), ("z", GDN_W), ("ba", LANES), ("ga", D_MODEL), ("gb", D_MODEL))


OFF_Q, OFF_K, OFF_V = 0, 3 * DIL_GW, 6 * DIL_GW
OFF_GQ = 9 * DIL_GW
OFF_Z = OFF_GQ + 3 * GDN_W
OFF_BA = OFF_Z + GDN_W
OFF_GATES = OFF_BA + 2 * GDN_HEADS


def _arrange_w_in(w_in):
    w = w_in.astype(BF16)
    return w, w[:, OFF_GATES:]


GQ_SLABS = 3 * GDN_W // LANES
NORM_ROWS = 4 * SUBLANES


def _in_proj_kernel(x_ref, g_ref, w_ref, wg_ref, *rest, seq_tiles):
    tm = x_ref.shape[0]
    if seq_tiles:
        (xn_ref, cb_ref, wc_ref, q_ref, kv0_ref, kv1_ref, kv2_ref, kt0_ref, kt1_ref, kt2_ref, gq_ref, tail_ref, z_ref,
         ba_ref, ga_ref, gb_ref, cs, us) = rest
        kt_refs = (kt0_ref, kt1_ref, kt2_ref)
        i = pl.program_id(0)

        @pl.when(i == 0)
        def _():
            us[0] = _rms(x_ref[...], g_ref[...]).astype(BF16)

        u_ref, un_ref = us.at[i % 2], us.at[(i + 1) % 2]
    else:
        q_ref, kv0_ref, kv1_ref, kv2_ref, gq_ref, z_ref, ba_ref, ga_ref, gb_ref = rest
        u = _rms(x_ref[...], g_ref[...]).astype(BF16)

    def seg(ref, off, n):
        return jnp.dot(u_ref[...] if seq_tiles else u, ref[:, off:off + n], preferred_element_type=F32)

    def q_seg():
        q_ref[...] = seg(w_ref, OFF_Q, 3 * DIL_GW) * (DIL_HEAD_DIM ** -0.5)

    def kv_seg(g, part):
        kv_ref = (kv0_ref, kv1_ref, kv2_ref)[g]
        r = seg(w_ref, (OFF_K, OFF_V)[part] + g * DIL_GW, DIL_GW)
        kv_ref[:, part * DIL_GW:(part + 1) * DIL_GW] = r
        if seq_tiles:
            kt_refs[g][part * DIL_GW:(part + 1) * DIL_GW, :] = r.T[:, tm - kt_refs[g].shape[1]:]

    def ba_seg():
        ba_ref[...] = seg(w_ref, OFF_BA, LANES)

    def gate_seg(ref, off):
        ref[...] = seg(wg_ref, off, D_MODEL)

    others = ([(q_seg, 3 * DIL_GW)] + [(functools.partial(kv_seg, g, p), DIL_GW) for g in range(3) for p in range(2)]
              + [(functools.partial(gate_seg, ga_ref, 0), D_MODEL),
                 (functools.partial(gate_seg, gb_ref, D_MODEL), D_MODEL),
                 (ba_seg, LANES)])
    if not seq_tiles:
        for f, _ in others:
            f()
        for c in range(3):
            gq_ref[:, c * GDN_W:(c + 1) * GDN_W] = seg(w_ref, OFF_GQ + c * GDN_W, GDN_W)
        z_ref[...] = seg(w_ref, OFF_Z, GDN_W)
        return
    first = (i % seq_tiles) == 0

    @pl.when(first)
    def _():
        for s in range(GQ_SLABS):
            cs[s, CONV_PAD - 3:CONV_PAD, :] = cb_ref[:, s * LANES:(s + 1) * LANES]

    @pl.when(jnp.logical_not(first))
    def _():
        for s in range(GQ_SLABS):
            cs[s, CONV_PAD - 3:CONV_PAD, :] = cs[s, CONV_PAD + tm - 3:CONV_PAD + tm, :]

    for c in range(3):
        r = seg(w_ref, OFF_GQ + c * GDN_W, GDN_W)
        for hh in range(GDN_HEADS):
            cs[c * GDN_HEADS + hh, CONV_PAD:CONV_PAD + tm, :] = r[:, hh * LANES:(hh + 1) * LANES]
        for rs in range(c * tm // 3 // NORM_ROWS, (c + 1) * tm // 3 // NORM_ROWS):
            rows = slice(rs * NORM_ROWS, (rs + 1) * NORM_ROWS)
            un_ref[rows, :] = _rms(xn_ref[rows, :], g_ref[...]).astype(BF16)
    base = CONV_PAD - 3 + jnp.minimum(i, 0)

    def conv_slab(s):
        ls = slice(s * LANES, (s + 1) * LANES)
        y = cs[s, pl.ds(base, tm), :] * wc_ref[0:1, ls]
        for j in range(1, 4):
            y = y + cs[s, pl.ds(base + j, tm), :] * wc_ref[j:j + 1, ls]
        y = _silu(y)
        if s < GDN_HEADS:
            y = _l2n(y) * (GDN_HEAD_DIM ** -0.5)
        elif s < 2 * GDN_HEADS:
            y = _l2n(y)
        gq_ref[:, ls] = y
        tail_ref[:, ls] = cs[s, tm:tm + SUBLANES, :]

    def z_seg():
        z_ref[...] = _silu(seg(w_ref, OFF_Z, GDN_W))

    others.insert(len(others) - 1, (z_seg, GDN_W))
    total = sum(n for _, n in others)
    done, cols = 0, 0
    for f, n in others:
        f()
        cols += n
        upto = min(GQ_SLABS, (cols * GQ_SLABS + total - 1) // total)
        for s in range(done, upto):
            conv_slab(s)
        done = upto
    assert done == GQ_SLABS


def _in_proj(x2d, gain, w_arr, tm, seq=None, conv_buf=None, w_conv=None):
    rows, d = x2d.shape
    fused = seq is not None
    nt = seq // tm if fused else None
    names = [n for n, _ in IN_SEGS]
    widths = dict(IN_SEGS)
    row_spec = lambda n: pl.BlockSpec((tm, n), lambda i: (i, 0))
    out_specs, out_shape = [], []
    for n in names:
        out_specs.append(row_spec(widths[n]))
        out_shape.append(jax.ShapeDtypeStruct((rows, widths[n]), F32))
        if n == "kv2" and fused:
            for w, _ in DIL_GROUPS:
                keep = min(w, seq)
                bw = min(tm, keep)
                skip = nt - keep // bw if bw == tm else nt - 1
                out_specs.append(pl.BlockSpec((None, 2 * DIL_GW, bw),
                                              lambda i, skip=skip: (i // nt, 0, jnp.maximum(i % nt - skip, 0))))
                out_shape.append(jax.ShapeDtypeStruct((rows // seq, 2 * DIL_GW, keep), F32))
        if n == "gq" and fused:
            out_specs.append(pl.BlockSpec((None, SUBLANES, 3 * GDN_W), lambda i: (i // nt, 0, 0)))
            out_shape.append(jax.ShapeDtypeStruct((rows // seq, SUBLANES, 3 * GDN_W), F32))
    in_specs = [pl.BlockSpec((tm, d), lambda i: (0 if fused else i, 0)), _resident((1, d)),
                _resident(w_arr[0].shape), _resident(w_arr[1].shape)]
    args = [x2d, gain.reshape(1, d), *w_arr]
    scratch = []
    if fused:
        in_specs += [pl.BlockSpec((tm, d), lambda i: (jnp.minimum(i + 1, rows // tm - 1), 0)),
                     pl.BlockSpec((None, 3, 3 * GDN_W), lambda i: (i // nt, 0, 0)), _resident(w_conv.shape)]
        args += [x2d, conv_buf, w_conv]
        scratch = [pltpu.VMEM((GQ_SLABS, CONV_PAD + tm, LANES), F32), pltpu.VMEM((2, tm, d), BF16)]
    return pl.pallas_call(
        functools.partial(_in_proj_kernel, seq_tiles=nt),
        grid=(rows // tm,),
        in_specs=in_specs,
        out_specs=out_specs,
        out_shape=out_shape,
        scratch_shapes=scratch,
        compiler_params=_cparams(("arbitrary",)),
        name="in_proj",
    )(*args)


def _rel_bucket(dist):
    exact = REL_BUCKETS // 2
    d = jnp.maximum(dist, 1).astype(F32)
    large = exact + (jnp.log(d / exact) / math.log(REL_MAX_DIST / exact) * (REL_BUCKETS - exact)).astype(jnp.int32)
    return jnp.where(dist < exact, dist, jnp.minimum(large, REL_BUCKETS - 1))


def _group_bias(rel_bias, g):
    dil = DIL_GROUPS[g][1]
    dist = dil * jnp.arange(DIL_NK, dtype=jnp.int32)
    tab = rel_bias[_rel_bucket(dist)]
    return tab[:, g * DIL_HPG:(g + 1) * DIL_HPG].T.astype(F32)


def _toeplitz(v, n, width):
    h, L = v.shape
    return jnp.tile(v, (1, n))[:, :n * (L - 1)].reshape(h, n, L - 1)[:, :, :width]


def _prompt_bias_tables(rel_bias):
    cat, cur = [], []
    for g in range(3):
        bg = _group_bias(rel_bias, g)
        v = jnp.concatenate([bg[:, ::-1], jnp.full((DIL_HPG, 3 * TILE_Q - DIL_NK), NEG, F32)], axis=1)
        t = _toeplitz(v, TILE_Q, 2 * TILE_Q)
        cat.append(t)
        cur.append(t[:, :, TILE_Q:])
    return cat, cur


DIL_TIF = 2
DIL_SLABS = DIL_GW // LANES


def _dil_prompt_kernel(q0_ref, q1_ref, k0_ref, k1_ref, v0_ref, v1_ref, tcat_ref, tcur_ref,
                       o0_ref, o1_ref, l0_ref, l1_ref, *, dil):
    S = q0_ref.shape[0]
    nb = S // dil // TILE_Q
    q_refs, k_refs, v_refs = (q0_ref, q1_ref), (k0_ref, k1_ref), (v0_ref, v1_ref)
    o_refs, l_refs = (o0_ref, o1_ref), (l0_ref, l1_ref)
    even = lax.broadcasted_iota(jnp.int32, (TILE_Q, LANES), 1) < DIL_HEAD_DIM

    def rows(r, t):
        start = r + dil * TILE_Q * t
        return pl.ds(start, TILE_Q, stride=dil) if dil > 1 else pl.ds(start, TILE_Q)

    tiles = [(r, t) for r in range(dil) for t in range(nb)]
    for i0 in range(0, len(tiles), DIL_TIF):
        grp = tiles[i0:i0 + DIL_TIF]
        qm, kc, vc = {}, {}, {}
        for ti, (r, t) in enumerate(grp):
            for sl in range(DIL_SLABS):
                qf = q_refs[sl][rows(r, t), :]
                qm[ti, 2 * sl] = jnp.where(even, qf, 0.0).astype(BF16)
                qm[ti, 2 * sl + 1] = jnp.where(even, 0.0, qf).astype(BF16)
                kc[ti, sl] = k_refs[sl][rows(r, t), :].astype(BF16)
                vc[ti, sl] = v_refs[sl][rows(r, t), :].astype(BF16)
                if t > 0:
                    kc[ti, sl] = jnp.concatenate([k_refs[sl][rows(r, t - 1), :].astype(BF16), kc[ti, sl]], axis=0)
                    vc[ti, sl] = jnp.concatenate([v_refs[sl][rows(r, t - 1), :].astype(BF16), vc[ti, sl]], axis=0)
        units = [(ti, h) for ti in range(len(grp)) for h in range(DIL_HPG)]
        s = [_dot_nt(qm[ti, h], kc[ti, h // 2]) + (tcat_ref[h] if grp[ti][1] > 0 else tcur_ref[h]) for ti, h in units]
        m = [jnp.max(x, axis=-1, keepdims=True) for x in s]
        p = [jnp.exp(x - mx) for x, mx in zip(s, m)]
        l = [jnp.sum(x, axis=-1, keepdims=True) for x in p]
        pv = [jnp.dot(p[u].astype(BF16), vc[ti, h // 2], preferred_element_type=F32) for u, (ti, h) in enumerate(units)]
        o = [pv[u] / l[u] for u in range(len(units))]
        lse = [m[u] + jnp.log(l[u]) for u in range(len(units))]
        for ti, (r, t) in enumerate(grp):
            for sl in range(DIL_SLABS):
                ue, uo = ti * DIL_HPG + 2 * sl, ti * DIL_HPG + 2 * sl + 1
                o_refs[sl][rows(r, t), :] = jnp.where(even, o[ue], o[uo])
                l_refs[sl][rows(r, t), :] = jnp.where(even, lse[ue], lse[uo])


def _dil_prompt(q, kv, g, t_cat, t_cur):
    B, S, _ = q.shape
    dil = DIL_GROUPS[g][1]
    slab = lambda c: pl.BlockSpec((None, S, LANES), lambda b: (b, 0, c))
    nq, nk = g * DIL_SLABS, 0
    outs = pl.pallas_call(
        functools.partial(_dil_prompt_kernel, dil=dil),
        grid=(B,),
        in_specs=[slab(nq), slab(nq + 1), slab(nk), slab(nk + 1), slab(nk + 2), slab(nk + 3),
                  _resident(t_cat.shape), _resident(t_cur.shape)],
        out_specs=[slab(0)] * 4,
        out_shape=[jax.ShapeDtypeStruct((B, S, LANES), F32)] * 4,
        compiler_params=_cparams(("parallel",)),
        name=f"dil_prompt_g{g}",
    )(q, q, kv, kv, kv, kv, t_cat, t_cur)
    o0, o1, l0, l1 = (x.reshape(B * S, LANES) for x in outs)
    return [o0, o1], [l0, l1]


def _sample_bias_tables(rel_bias, t_real):
    R = SAMPLE_ROWS
    tabc, tabn = [], []
    t_i = np.arange(R)[:, None]
    u_i = np.arange(R)[None, :]
    for g, (w, dil) in enumerate(DIL_GROUPS):
        bg = _group_bias(rel_bias, g)
        base = bg[:, ::-1][:, :TILE_Q]
        t0 = jnp.concatenate([base[:, :, None], jnp.full((DIL_HPG, TILE_Q, dil - 1), NEG, F32)], axis=2)
        t0 = t0.reshape(DIL_HPG, w)
        tc = _toeplitz(jnp.concatenate([t0, jnp.full((DIL_HPG, R), NEG, F32)], axis=1), R, w)
        tabc.append(jnp.where((t_i < t_real)[None], tc, 0.0).reshape(DIL_HPG * R, w))
        tn = jnp.full((DIL_HPG, R, R), NEG, F32)
        for j in range(-(-t_real // dil)):
            hit = (t_i - u_i == j * dil) & (t_i < t_real)
            tn = jnp.where(hit[None], bg[:, j][:, None, None], tn)
        tn = jnp.where((t_i >= t_real)[None], 0.0, tn)
        tabn.append(tn.reshape(DIL_HPG * R, R))
    return tabc, jnp.stack(tabn)


def _dil_sample_kernel(q_ref, n0_ref, n1_ref, n2_ref, c0_ref, c1_ref, c2_ref, tc0_ref, tc1_ref, tc2_ref, tn_ref,
                       o_ref, l_ref, oc0_ref, oc1_ref, oc2_ref, *, t_real):
    R = SAMPLE_ROWS
    rows = lax.broadcasted_iota(jnp.int32, (DIL_HPG * R, DIL_GW), 0)
    lanes = lax.broadcasted_iota(jnp.int32, (DIL_HPG * R, DIL_GW), 1)
    head_mask = (lanes // DIL_HEAD_DIM) == (rows // R)
    lane_f = lax.broadcasted_iota(jnp.int32, (2 * DIL_GW, LANES), 1)
    keep = lane_f < LANES - t_real
    sel_l = lax.broadcasted_iota(jnp.int32, (LANES, R), 0)
    sel_u = lax.broadcasted_iota(jnp.int32, (LANES, R), 1)
    selT = ((sel_l == sel_u + LANES - t_real) & (sel_u < t_real)).astype(BF16)
    groups = ((n0_ref, c0_ref, tc0_ref, oc0_ref), (n1_ref, c1_ref, tc1_ref, oc1_ref), (n2_ref, c2_ref, tc2_ref, oc2_ref))

    def fold_heads(x):
        x = jnp.where(head_mask, x, 0.0)
        return x[0:R] + x[R:2 * R] + x[2 * R:3 * R] + x[3 * R:4 * R]

    G3 = range(3)
    kvn = [groups[g][0][...] for g in G3]
    q_bd = [jnp.where(head_mask, jnp.concatenate([q_ref[:, g * DIL_GW:(g + 1) * DIL_GW]] * DIL_HPG, axis=0), 0.0)
            for g in G3]
    s_c = [_dot(q_bd[g], groups[g][1][:DIL_GW, :]) + groups[g][2][...] for g in G3]
    s_n = [_dot_nt(q_bd[g], kvn[g][:, :DIL_GW]) + tn_ref[g] for g in G3]
    m = [jnp.maximum(jnp.max(s_c[g], axis=-1, keepdims=True), jnp.max(s_n[g], axis=-1, keepdims=True)) for g in G3]
    p_c = [jnp.exp(s_c[g] - m[g]) for g in G3]
    p_n = [jnp.exp(s_n[g] - m[g]) for g in G3]
    l = [jnp.sum(p_c[g], axis=-1, keepdims=True) + jnp.sum(p_n[g], axis=-1, keepdims=True) for g in G3]
    acc = [(_dot_nt(p_c[g], groups[g][1][DIL_GW:, :]) + _dot(p_n[g], kvn[g][:, DIL_GW:])) / l[g] for g in G3]
    for g in G3:
        o_ref[:, g * DIL_GW:(g + 1) * DIL_GW] = fold_heads(acc[g])
        l_ref[:, g * DIL_GW:(g + 1) * DIL_GW] = fold_heads(jnp.broadcast_to(m[g] + jnp.log(l[g]), acc[g].shape))
    for g, (n_ref, c_ref, tc_ref, oc_ref) in enumerate(groups):
        W = c_ref.shape[1]
        hi, mid, lo = _split3(kvn[g])
        tail = (jnp.dot(selT, hi, preferred_element_type=F32) + jnp.dot(selT, mid, preferred_element_type=F32)
                + jnp.dot(selT, lo, preferred_element_type=F32)).T
        nxt = pltpu.roll(c_ref[:, 0:LANES], LANES - t_real, axis=1)
        for c in range(W // LANES):
            cur = nxt
            nxt = (pltpu.roll(c_ref[:, (c + 1) * LANES:(c + 2) * LANES], LANES - t_real, axis=1)
                   if (c + 1) * LANES < W else tail)
            oc_ref[:, c * LANES:(c + 1) * LANES] = jnp.where(keep, cur, nxt)


def _dil_sample(q, kvn, caches_t, tabc, tabn, t_real):
    B = q.shape[0]
    row = lambda n: pl.BlockSpec((None, SAMPLE_ROWS, n), lambda b: (b, 0, 0))
    cspecs = [pl.BlockSpec((None,) + c.shape[1:], lambda b: (b, 0, 0)) for c in caches_t]
    for g, (w, dil) in enumerate(DIL_GROUPS):
        assert caches_t[g].shape == (B, 2 * DIL_GW, w) and w // dil == TILE_Q
    out_spec = row(3 * DIL_GW)
    o, lse, *new_caches = pl.pallas_call(
        functools.partial(_dil_sample_kernel, t_real=t_real),
        grid=(B,),
        in_specs=([row(3 * DIL_GW)] + [row(2 * DIL_GW)] * 3 + cspecs + [_resident(t.shape) for t in tabc]
                  + [_resident(tabn.shape)]),
        out_specs=[out_spec, out_spec] + cspecs,
        out_shape=([jax.ShapeDtypeStruct((B, SAMPLE_ROWS, 3 * DIL_GW), F32)] * 2
                   + [jax.ShapeDtypeStruct(c.shape, F32) for c in caches_t]),
        compiler_params=_cparams(("parallel",)),
        name="dil_sample",
    )(q, *kvn, *caches_t, *tabc, tabn)
    o = o.reshape(B * SAMPLE_ROWS, 3 * DIL_GW)
    lse = lse.reshape(B * SAMPLE_ROWS, 3 * DIL_GW)
    n = 3 * DIL_SLABS
    return ([o[:, i * LANES:(i + 1) * LANES] for i in range(n)],
            [lse[:, i * LANES:(i + 1) * LANES] for i in range(n)], new_caches)


GDN_HPS = GDN_HEADS
GDN_SW = GDN_HPS * GDN_HEAD_DIM
GDN_TPI = 4
GDN_ROWS = 1024


def _gdn_kernel(q_ref, k_ref, v_ref, ba_ref, alog_ref, dtb_ref, z_ref, gn_ref, tri_ref, s0_ref, o_ref, s_ref, carry_s,
                *, tpi):
    T = q_ref.shape[0]
    C = GDN_CHUNK
    D = GDN_HEAD_DIM
    ri = lax.broadcasted_iota(jnp.int32, (TILE_Q, TILE_Q), 0)
    ci = lax.broadcasted_iota(jnp.int32, (TILE_Q, TILE_Q), 1)
    same = (ri // C) == (ci // C)
    incl = same & (ri >= ci)
    strict = same & (ri > ci)
    eye = (ri == ci).astype(F32)
    lane, row = ci, ri
    gain = gn_ref[...]
    zpad = jnp.zeros((C, D), F32)
    HH = range(GDN_HPS)
    hsl = [slice(hh * D, (hh + 1) * D) for hh in HH]
    RW = tpi * TILE_Q
    UU = [(tt, hh) for tt in range(tpi) for hh in HH]
    UI = range(len(UU))

    def tile(i, S):
        r0 = pl.multiple_of(i * RW, RW)
        rt = [r0 + tt * TILE_Q for tt in range(tpi)]
        q = [q_ref[pl.ds(rt[tt], TILE_Q), hsl[hh]] for tt, hh in UU]
        k = [k_ref[pl.ds(rt[tt], TILE_Q), hsl[hh]] for tt, hh in UU]
        v = [v_ref[pl.ds(rt[tt], TILE_Q), hsl[hh]] for tt, hh in UU]
        tri = tri_ref[...]
        beta_all, G_all = [], []
        for tt in range(tpi):
            ba = ba_ref[pl.ds(rt[tt], TILE_Q), :]
            beta_all.append(_sigmoid(ba))
            gh, gm, gl = _split3(-jnp.exp(alog_ref[...]) * _softplus(ba + dtb_ref[...]))
            G_all.append(jnp.dot(tri, gh, preferred_element_type=F32) + jnp.dot(tri, gm, preferred_element_type=F32)
                         + jnp.dot(tri, gl, preferred_element_type=F32))
        head = [hh for _, hh in UU]
        bc = [jnp.sum(jnp.where(lane == head[u], beta_all[UU[u][0]], 0.0), axis=-1, keepdims=True) for u in UI]
        Gc = [jnp.broadcast_to(jnp.sum(jnp.where(lane == head[u] + GDN_HEADS, G_all[UU[u][0]], 0.0), axis=-1,
                                       keepdims=True), (TILE_Q, TILE_Q)) for u in UI]
        gamma = [jnp.exp(jnp.where(incl, Gc[u] - Gc[u].T, NEG)) for u in UI]
        kk = [_dot_nt(k[u], k[u]) for u in UI]
        qk = [_dot_nt(q[u], k[u]) for u in UI]
        lo = lax.broadcasted_iota(jnp.int32, (C, TILE_Q), 1) < C

        def pack(m):
            return jnp.where(lo, m[:C], m[C:])

        def blockdiag(p):
            return jnp.concatenate([jnp.where(lo, p, 0.0), jnp.where(lo, 0.0, p)], axis=0)

        eye_p = pack(eye)
        Xp = [pack(jnp.where(strict, bc[u] * kk[u] * gamma[u], 0.0)) for u in UI]
        Pp = [eye_p - Xp[u] for u in UI]
        Xb = [blockdiag(Xp[u]).astype(BF16) for u in UI]
        for _ in range(int(math.log2(C)) - 1):
            Xp = [jnp.dot(Xp[u].astype(BF16), Xb[u], preferred_element_type=F32) for u in UI]
            Xb = [blockdiag(Xp[u]).astype(BF16) for u in UI]
            Pp = [Pp[u] + jnp.dot(Pp[u].astype(BF16), Xb[u], preferred_element_type=F32) for u in UI]
        eG = [jnp.exp(Gc[u]) for u in UI]
        rhs = [jnp.concatenate([v[u] * bc[u], k[u] * (bc[u] * eG[u])], axis=-1) for u in UI]
        sol = [rhs[u] + _dot(blockdiag(Pp[u] - eye_p), rhs[u]) for u in UI]
        a_in = [qk[u] * gamma[u] for u in UI]
        q_dec = [q[u] * eG[u] for u in UI]
        kdT = [(k[u] * jnp.exp(jnp.where(row < C, Gc[u][C - 1:C, :], Gc[u][2 * C - 1:2 * C, :]) - Gc[u])).T
               for u in UI]
        S = list(S)
        for tt in range(tpi):
            us = [tt * GDN_HPS + hh for hh in HH]
            oq, vn = [[] for _ in HH], [[] for _ in HH]
            for c in range(TILE_Q // C):
                cs = slice(c * C, (c + 1) * C)
                r = [_dot(jnp.concatenate([sol[us[hh]][cs, D:], q_dec[us[hh]][cs]], axis=0), S[hh]) for hh in HH]
                for hh in HH:
                    oq[hh].append(r[hh][C:])
                    vn[hh].append(sol[us[hh]][cs, :D] - r[hh][:C])
                vpad = [jnp.concatenate([vn[hh][c], zpad] if c == 0 else [zpad, vn[hh][c]], axis=0) for hh in HH]
                S = [S[hh] * jnp.exp(Gc[us[hh]][(c + 1) * C - 1:(c + 1) * C, :]) + _dot(kdT[us[hh]], vpad[hh])
                     for hh in HH]
            o = [jnp.concatenate(oq[hh], axis=0) + _dot(a_in[us[hh]], jnp.concatenate(vn[hh], axis=0)) for hh in HH]
            outs = [_rms(o[hh], gain) * z_ref[pl.ds(rt[tt], TILE_Q), hsl[hh]] for hh in HH]
            o_ref[pl.ds(rt[tt], TILE_Q), :] = jnp.concatenate(outs, axis=-1).astype(o_ref.dtype)
        return tuple(S)

    @pl.when(pl.program_id(1) == 0)
    def _():
        carry_s[...] = s0_ref[...]

    S = lax.fori_loop(0, T // RW, tile, tuple(carry_s[hh] for hh in HH))
    for hh in HH:
        carry_s[hh] = S[hh]
        s_ref[hh] = S[hh]


def _gdn(gq, ba, z, s0, a_log, dt_bias, norm_out):
    B, T, _ = gq.shape
    H = GDN_HEADS
    pad16 = lambda x: jnp.concatenate([jnp.zeros((H,), F32), x.astype(F32), jnp.zeros((LANES - 2 * H,), F32)])
    assert GDN_HPS == H
    tb = min(GDN_ROWS, T)
    tpi = GDN_TPI if (tb // TILE_Q) % GDN_TPI == 0 else 1
    col = lambda off: pl.BlockSpec((None, tb, GDN_SW), lambda b, j: (b, j, off))
    sblk = pl.BlockSpec((None, H, GDN_HEAD_DIM, GDN_HEAD_DIM), lambda b, j: (b, 0, 0, 0))
    r = np.arange(TILE_Q)
    tri = jnp.asarray((r[:, None] >= r[None, :]) & (r[:, None] // GDN_CHUNK == r[None, :] // GDN_CHUNK), BF16)
    o, s_new = pl.pallas_call(
        functools.partial(_gdn_kernel, tpi=tpi),
        grid=(B, T // tb),
        in_specs=[col(0), col(1), col(2),
                  pl.BlockSpec((None, tb, LANES), lambda b, j: (b, j, 0)),
                  _resident((1, LANES)), _resident((1, LANES)),
                  col(0), _resident((1, LANES)), _resident((TILE_Q, TILE_Q)), sblk],
        out_specs=[col(0), sblk],
        out_shape=[jax.ShapeDtypeStruct((B, T, GDN_W), BF16),
                   jax.ShapeDtypeStruct((B, H, GDN_HEAD_DIM, GDN_HEAD_DIM), F32)],
        scratch_shapes=[pltpu.VMEM((H, GDN_HEAD_DIM, GDN_HEAD_DIM), F32)],
        compiler_params=_cparams(("parallel", "arbitrary")),
        name="gdn",
    )(gq, gq, gq, ba, pad16(a_log).reshape(1, LANES), pad16(dt_bias).reshape(1, LANES), z,
      norm_out.reshape(1, LANES), tri, s0)
    return o, s_new


def _gdn_sample_kernel(x_ref, b_ref, w_ref, ba_ref, alog_ref, dtb_ref, z_ref, gn_ref, s0_ref, o_ref, s_ref, xs,
                       *, t_real):
    R, D, H = SAMPLE_ROWS, GDN_HEAD_DIM, GDN_HEADS
    nb = x_ref.shape[0]
    ri = lax.broadcasted_iota(jnp.int32, (R, R), 0)
    ci = lax.broadcasted_iota(jnp.int32, (R, R), 1)
    incl, strict = ri >= ci, ri > ci
    eye = (ri == ci).astype(F32)
    tri = incl.astype(BF16)
    er = lax.broadcasted_iota(jnp.int32, (LANES, LANES), 0)
    ec = lax.broadcasted_iota(jnp.int32, (LANES, LANES), 1)
    eye_l = (er == ec).astype(BF16)
    live = lax.broadcasted_iota(jnp.int32, (R, LANES), 0) < t_real
    d32 = functools.partial(jnp.dot, preferred_element_type=F32)
    nt = lambda a, b: lax.dot_general(a, b, (((1,), (1,)), ((), ())), preferred_element_type=F32)
    gain = gn_ref[...]
    y, beta_all, G_all, G_allT = [], [], [], []
    for s in range(nb):
        xs[s, CONV_PAD - 3:CONV_PAD, :] = b_ref[s]
        xs[s, CONV_PAD:CONV_PAD + R, :] = x_ref[s]
        ys = xs[s, CONV_PAD - 3:CONV_PAD - 3 + R, :] * w_ref[0:1, :]
        for j in range(1, 4):
            ys = ys + xs[s, CONV_PAD - 3 + j:CONV_PAD - 3 + j + R, :] * w_ref[j:j + 1, :]
        y.append(_silu(ys))
        ba = ba_ref[s]
        beta_all.append(jnp.where(live, _sigmoid(ba), 0.0))
        gh, gm, gl = _split3(jnp.where(live, -jnp.exp(alog_ref[...]) * _softplus(ba + dtb_ref[...]), 0.0))
        G = d32(tri, gh) + d32(tri, gm) + d32(tri, gl)
        th, tm, tl = _split3(G)
        G_all.append(G)
        G_allT.append(nt(eye_l, th) + nt(eye_l, tm) + nt(eye_l, tl))
    US = [(s, h) for s in range(nb) for h in range(H)]
    UI = range(len(US))

    q = [_l2n(y[s][:, h * D:(h + 1) * D]) * (D ** -0.5) for s, h in US]
    k = [_l2n(y[s][:, GDN_W + h * D:GDN_W + (h + 1) * D]) for s, h in US]
    v = [y[s][:, 2 * GDN_W + h * D:2 * GDN_W + (h + 1) * D] for s, h in US]
    bc = [beta_all[s][:, h:h + 1] for s, h in US]
    Gc = [G_all[s][:, H + h:H + h + 1] for s, h in US]
    gamma = [jnp.exp(jnp.where(incl, Gc[u] - G_allT[s][H + h:H + h + 1, :], NEG)) for u, (s, h) in enumerate(US)]
    kk = [_dot_nt(k[u], k[u]) for u in UI]
    qk = [_dot_nt(q[u], k[u]) for u in UI]
    X = [jnp.where(strict, bc[u] * kk[u] * gamma[u], 0.0) for u in UI]
    P = [eye - X[u] for u in UI]
    for _ in range(int(math.log2(R)) - 1):
        X = [_dot(X[u], X[u]) for u in UI]
        P = [P[u] + _dot(P[u], X[u]) for u in UI]
    eG = [jnp.exp(Gc[u]) for u in UI]
    rhs = [jnp.concatenate([v[u] * bc[u], k[u] * (bc[u] * eG[u])], axis=-1) for u in UI]
    sol = [rhs[u] + _dot(P[u] - eye, rhs[u]) for u in UI]
    S = [s0_ref[s, h] for s, h in US]
    r = [_dot(jnp.concatenate([sol[u][:, D:], q[u] * eG[u]], axis=0), S[u]) for u in UI]
    v_new = [sol[u][:, :D] - r[u][:R] for u in UI]
    o = [r[u][R:] + _dot(qk[u] * gamma[u], v_new[u]) for u in UI]
    kdT = [nt(eye_l, (k[u] * jnp.exp(Gc[u][R - 1:R, :] - Gc[u])).astype(BF16)) for u in UI]
    for u, (s, h) in enumerate(US):
        s_ref[s, h] = S[u] * jnp.exp(Gc[u][R - 1:R, :]) + _dot(kdT[u], v_new[u])
    for s in range(nb):
        o_ref[s] = jnp.concatenate([_rms(o[s * H + h], gain) * _silu(z_ref[s, :, h * D:(h + 1) * D])
                                    for h in range(H)], axis=-1)


GDN_SAMPLE_SEQS = 4


def _gdn_sample(gq, ba, z, conv_buf, s0, w_conv, a_log, dt_bias, norm_out, t_real):
    B, R, _ = gq.shape
    H = GDN_HEADS
    pad16 = lambda x: jnp.concatenate([jnp.zeros((H,), F32), x.astype(F32), jnp.zeros((LANES - 2 * H,), F32)])
    nb = GDN_SAMPLE_SEQS if B % GDN_SAMPLE_SEQS == 0 else 1
    blk = lambda *s: pl.BlockSpec((nb,) + s, lambda b: (b,) + (0,) * len(s))
    return pl.pallas_call(
        functools.partial(_gdn_sample_kernel, t_real=t_real),
        grid=(B // nb,),
        in_specs=[blk(R, 3 * GDN_W), blk(3, 3 * GDN_W), _resident(w_conv.shape), blk(R, LANES),
                  _resident((1, LANES)), _resident((1, LANES)), blk(R, GDN_W), _resident((1, LANES)),
                  blk(H, GDN_HEAD_DIM, GDN_HEAD_DIM)],
        out_specs=[blk(R, GDN_W), blk(H, GDN_HEAD_DIM, GDN_HEAD_DIM)],
        out_shape=[jax.ShapeDtypeStruct((B, R, GDN_W), F32),
                   jax.ShapeDtypeStruct((B, H, GDN_HEAD_DIM, GDN_HEAD_DIM), F32)],
        scratch_shapes=[pltpu.VMEM((nb, CONV_PAD + R, 3 * GDN_W), F32)],
        compiler_params=_cparams(("parallel",)),
        name="gdn_sample",
    )(gq, conv_buf, w_conv, ba, pad16(a_log).reshape(1, LANES), pad16(dt_bias).reshape(1, LANES), z,
      norm_out.reshape(1, LANES), s0)


N_OG = 3 * DIL_SLABS


def _mix_kernel(*refs):
    h1, qm = _mix_body(*refs[:2 * N_OG + 9])
    h1_ref, qm_ref = refs[2 * N_OG + 9:]
    h1_ref[...] = h1
    qm_ref[...] = qm


def _mix_body(*refs):
    o_refs, l_refs = refs[:N_OG], refs[N_OG:2 * N_OG]
    ob_ref, ga_ref, gb_ref, h_ref, wa_ref, wb_ref, wo_ref, gq_ref, wq_ref = refs[2 * N_OG:]
    slabs = []
    for sl in range(DIL_SLABS):
        l0, l1, l2 = (l_refs[g * DIL_SLABS + sl][...] for g in range(3))
        o0, o1, o2 = (o_refs[g * DIL_SLABS + sl][...] for g in range(3))
        mx = jnp.maximum(jnp.maximum(l0, l1), l2)
        e0, e1, e2 = jnp.exp(l0 - mx), jnp.exp(l1 - mx), jnp.exp(l2 - mx)
        slabs.append((e0 * o0 + e1 * o1 + e2 * o2) / (e0 + e1 + e2))
    o_a = jnp.concatenate(slabs, axis=-1)
    a = _dot(o_a, wa_ref[...])
    b = jnp.dot(ob_ref[...], wb_ref[...], preferred_element_type=F32)
    merged = _sigmoid(ga_ref[...]) * a + _sigmoid(gb_ref[...]) * b
    h1 = h_ref[...] + _dot(merged, wo_ref[...])
    return h1, _dot(_rms(h1, gq_ref[...]), wq_ref[...]).astype(BF16)


def _mix(o_g, l_g, o_b, ga, gb, h, w_a, w_b, w_o, norm_mem_q, w_mem_q, tm):
    rows, d = h.shape
    rt = lambda n: pl.BlockSpec((tm, n), lambda i: (i, 0))
    assert len(o_g) == len(l_g) == 3 * DIL_SLABS
    return pl.pallas_call(
        _mix_kernel,
        grid=(rows // tm,),
        in_specs=[rt(LANES)] * (6 * DIL_SLABS) + [rt(GDN_W), rt(d), rt(d), rt(d),
                                     _resident(w_a.shape), _resident(w_b.shape), _resident(w_o.shape),
                                     _resident((1, d)), _resident(w_mem_q.shape)],
        out_specs=[rt(d), rt(w_mem_q.shape[1])],
        out_shape=[jax.ShapeDtypeStruct((rows, d), F32), jax.ShapeDtypeStruct((rows, w_mem_q.shape[1]), BF16)],
        compiler_params=_cparams(("parallel",)),
        name="mix",
    )(*o_g, *l_g, o_b, ga, gb, h, w_a, w_b, w_o, norm_mem_q.reshape(1, d), w_mem_q)


def _mem_kv_kernel(x_ref, g_ref, w_ref, k_ref, v_ref):
    u = _rms(x_ref[...], g_ref[...]).astype(BF16)
    tm = x_ref.shape[0]
    n = MEM_HEADS * MEM_HEAD_DIM
    for o_ref, off in ((k_ref, 0), (v_ref, n)):
        r = jnp.dot(u, w_ref[:, off:off + n], preferred_element_type=F32)
        for h in range(MEM_HEADS):
            o_ref[pl.ds(h, tm, stride=MEM_HEADS), :] = r[:, h * MEM_HEAD_DIM:(h + 1) * MEM_HEAD_DIM]


def _mem_kv(mem2d, gain, w, tm):
    rows, d = mem2d.shape
    return pl.pallas_call(
        _mem_kv_kernel,
        grid=(rows // tm,),
        in_specs=[pl.BlockSpec((tm, d), lambda i: (i, 0)), _resident((1, d)), _resident(w.shape)],
        out_specs=[pl.BlockSpec((tm * MEM_HEADS, MEM_HEAD_DIM), lambda i: (i, 0))] * 2,
        out_shape=[jax.ShapeDtypeStruct((rows * MEM_HEADS, MEM_HEAD_DIM), F32)] * 2,
        compiler_params=_cparams(("parallel",)),
        name="mem_kv",
    )(mem2d, gain.reshape(1, d), w)


def _mem_attn_body(q, k_refs, v_refs):
    nb = len(q)
    M = k_refs[0].shape[0] // MEM_HEADS
    units = [(b, h) for b in range(nb) for h in range(MEM_HEADS)]
    s = [_dot_nt(q[b][:, h * MEM_HEAD_DIM:(h + 1) * MEM_HEAD_DIM], k_refs[b][pl.ds(h, M, stride=MEM_HEADS), :])
         * (MEM_HEAD_DIM ** -0.5) for b, h in units]
    p = [jnp.exp(x - jnp.max(x, axis=-1, keepdims=True)) for x in s]
    o = [_dot(p[u], v_refs[b][pl.ds(h, M, stride=MEM_HEADS), :]) / jnp.sum(p[u], axis=-1, keepdims=True)
         for u, (b, h) in enumerate(units)]
    return [jnp.concatenate(o[b * MEM_HEADS:(b + 1) * MEM_HEADS], axis=-1) for b in range(nb)]


def _mem_attn_kernel(q_ref, k_ref, v_ref, o_ref):
    nb = q_ref.shape[0]
    o = _mem_attn_body([q_ref[b] for b in range(nb)], [k_ref.at[b] for b in range(nb)],
                       [v_ref.at[b] for b in range(nb)])
    for b in range(nb):
        o_ref[b] = o[b]


def _mem_attn(qm, mem_k, mem_v, tm, nb):
    B, T, w = qm.shape
    kv_spec = pl.BlockSpec((nb,) + mem_k.shape[1:], lambda b, j: (b, 0, 0))
    return pl.pallas_call(
        _mem_attn_kernel,
        grid=(B // nb, T // tm),
        in_specs=[pl.BlockSpec((nb, tm, w), lambda b, j: (b, j, 0)), kv_spec, kv_spec],
        out_specs=pl.BlockSpec((nb, tm, w), lambda b, j: (b, j, 0)),
        out_shape=jax.ShapeDtypeStruct((B, T, w), F32),
        compiler_params=_cparams(("parallel", "parallel")),
        name="mem_attn",
    )(qm, mem_k, mem_v)


def _ffn_kernel(*refs, inject, emit_gate):
    if inject:
        h1_ref, om_ref, init_ref, fill_ref = refs[:4]
        rest = refs[4:]
    else:
        h1_ref, om_ref, init_ref = refs[:3]
        fill_ref, rest = None, refs[3:]
    _ffn_body(h1_ref[...], om_ref[...], init_ref, fill_ref, *rest, emit_gate=emit_gate)


def _ffn_body(h1, om, init_ref, fill_ref, wmo_ref, gf_ref, wup_ref, wc_ref, bc_ref, wd_ref, gfin_ref,
              y_ref, fc_ref, gs, *, emit_gate):
    tm = h1.shape[0]
    F = wd_ref.shape[0]
    PAD = SUBLANES

    @pl.when(pl.program_id(1) == 0)
    def _():
        gs[PAD - 2:PAD, :] = init_ref[...]

    h2 = h1 + _dot(om, wmo_ref[...])
    n = _rms(h2, gf_ref[...]).astype(BF16)
    gate = jnp.dot(n, wup_ref[:, :F], preferred_element_type=F32)
    if fill_ref is not None:
        r = lax.broadcasted_iota(jnp.int32, (tm, 1), 0)
        gate = jnp.where((r % SAMPLE_ROWS) >= SAMPLE_ROWS - 2, fill_ref[...], gate)
    gs[PAD:PAD + tm, :] = gate
    conv = (gs[PAD - 2:PAD - 2 + tm, :] * wc_ref[0:1, :] + gs[PAD - 1:PAD - 1 + tm, :] * wc_ref[1:2, :]
            + gate * wc_ref[2:3, :])
    last2 = gs[PAD + tm - 2:PAD + tm, :]
    gs[PAD - 2:PAD, :] = last2
    if emit_gate:
        fc_ref[...] = gate
    else:
        fc_ref[...] = last2
    up = jnp.dot(n, wup_ref[:, F:], preferred_element_type=F32)
    act = _silu(conv + bc_ref[...]) * up
    y = h2 + _dot(act, wd_ref[...])
    y_ref[...] = _rms(y, gfin_ref[...])


def _ffn(h1, om, init, fill, w_mo, norm_ffn, w_up, w_conv, b_conv, w_down, norm_final, tm, emit_gate):
    B, T, d = h1.shape
    F = w_down.shape[0]
    inject = fill is not None
    rt = lambda n: pl.BlockSpec((None, tm, n), lambda b, j: (b, j, 0))
    in_specs = [rt(d), rt(om.shape[-1]), pl.BlockSpec((None, 2, F), lambda b, j: (b, 0, 0))]
    args = [h1, om, init]
    if inject:
        in_specs.append(rt(F))
        args.append(fill)
    in_specs += [_resident(w_mo.shape), _resident((1, d)), _resident(w_up.shape), _resident(w_conv.shape),
                 _resident((1, F)), _resident(w_down.shape), _resident((1, d))]
    args += [w_mo, norm_ffn.reshape(1, d), w_up, w_conv, b_conv.reshape(1, F), w_down, norm_final.reshape(1, d)]
    if emit_gate:
        fc_spec, fc_shape = rt(F), jax.ShapeDtypeStruct((B, T, F), F32)
    else:
        fc_spec = pl.BlockSpec((None, 2, F), lambda b, j: (b, 0, 0))
        fc_shape = jax.ShapeDtypeStruct((B, 2, F), F32)
    return pl.pallas_call(
        functools.partial(_ffn_kernel, inject=inject, emit_gate=emit_gate),
        grid=(B, T // tm),
        in_specs=in_specs,
        out_specs=[rt(d), fc_spec],
        out_shape=[jax.ShapeDtypeStruct((B, T, d), F32), fc_shape],
        scratch_shapes=[pltpu.VMEM((tm + SUBLANES, F), F32)],
        compiler_params=_cparams(("parallel", "arbitrary")),
        name="ffn",
    )(*args)


def _post_kernel(*refs):
    n_mix = 2 * N_OG + 4
    mix_in, (k_ref, v_ref, init_ref), rest = refs[:n_mix], refs[n_mix:n_mix + 3], refs[n_mix + 3:]
    mix_w, ffn_rest = rest[:5], rest[5:]
    h1, qm = _mix_body(*mix_in, *mix_w)
    om, = _mem_attn_body([qm], [k_ref], [v_ref])
    _ffn_body(h1, om, init_ref, None, *ffn_rest, emit_gate=False)


def _post(o_g, l_g, o_b, ga, gb, h, mem_k, mem_v, init, w_a, w_b, w_o, norm_mem_q, w_mem_q, w_mo, norm_ffn, w_up,
          w_conv, b_conv, w_down, norm_final, tm):
    B, T, d = h.shape
    F = w_down.shape[0]
    rt = lambda n: pl.BlockSpec((None, tm, n), lambda b, j: (b, j, 0))
    per_b = lambda a: pl.BlockSpec((None,) + a.shape[1:], lambda b, j: (b, 0, 0))
    weights = [w_a, w_b, w_o, norm_mem_q.reshape(1, d), w_mem_q, w_mo, norm_ffn.reshape(1, d), w_up, w_conv,
               b_conv.reshape(1, F), w_down, norm_final.reshape(1, d)]
    return pl.pallas_call(
        _post_kernel,
        grid=(B, T // tm),
        in_specs=([rt(LANES)] * (2 * N_OG) + [rt(GDN_W), rt(d), rt(d), rt(d), per_b(mem_k), per_b(mem_v), per_b(init)]
                  + [_resident(w.shape) for w in weights]),
        out_specs=[rt(d), pl.BlockSpec((None, 2, F), lambda b, j: (b, 0, 0))],
        out_shape=[jax.ShapeDtypeStruct((B, T, d), F32), jax.ShapeDtypeStruct((B, 2, F), F32)],
        scratch_shapes=[pltpu.VMEM((tm + SUBLANES, F), F32)],
        compiler_params=_cparams(("parallel", "arbitrary")),
        name="post",
    )(*o_g, *l_g, o_b, ga, gb, h, mem_k, mem_v, init, *weights)


def kernel(x_prompt, x_sample, cache_dil0_kv, cache_dil1_kv, cache_dil2_kv, state_delta, state_delta_conv, cache_mem_k, cache_mem_v, state_ffn_conv, mem_prompt, rel_bias, norm_mix, w_in, w_conv_delta, a_log, dt_bias, norm_delta_out, w_branch_a, w_branch_b, w_out, norm_mem_q, norm_mem_kv, w_mem_q, w_mem_kv, w_mem_o, norm_ffn, w_ffn_up, w_ffn_conv, b_ffn_conv, w_ffn_down, norm_final):
    B, S, D = x_prompt.shape
    Bs, Ts, _ = x_sample.shape
    depth = w_in.shape[0]
    assert depth == 1 and D == D_MODEL and 3 <= Ts <= SAMPLE_ROWS - 2 and S % (16 * TILE_Q) == 0
    assert PAST_LEN >= max(w for w, _ in DIL_GROUPS)
    F = w_ffn_down.shape[1]
    M = mem_prompt.shape[1]
    l = 0
    w_arr = _arrange_w_in(w_in[l])
    w_a, w_b, w_o = (w.astype(BF16) for w in (w_branch_a[l], w_branch_b[l], w_out[l]))
    w_mq, w_mkv, w_mo = (w.astype(BF16) for w in (w_mem_q[l], w_mem_kv[l], w_mem_o[l]))
    w_up, w_dn = w_ffn_up[l].astype(BF16), w_ffn_down[l].astype(BF16)
    t_cat, t_cur = _prompt_bias_tables(rel_bias)

    xp = x_prompt.reshape(B * S, D)
    q, kv0, kv1, kv2, kt0, kt1, kt2, gq, gq_tail, z, ba, ga, gb = _in_proj(
        xp, norm_mix[l], w_arr, ROW_TILE, seq=S, conv_buf=jnp.zeros((B, 3, 3 * GDN_W), F32), w_conv=w_conv_delta[l])
    kvs = [kv.reshape(B, S, 2 * DIL_GW) for kv in (kv0, kv1, kv2)]
    q3 = q.reshape(B, S, 3 * DIL_GW)
    o_g, l_g = [], []
    for g in range(3):
        o_sl, l_sl = _dil_prompt(q3, kvs[g], g, t_cat[g], t_cur[g])
        o_g += o_sl
        l_g += l_sl
    o_b, delta_p = _gdn(gq.reshape(B, S, -1), ba.reshape(B, S, LANES), z.reshape(B, S, GDN_W),
                        jnp.zeros((B, GDN_HEADS, GDN_HEAD_DIM, GDN_HEAD_DIM), F32), a_log[l], dt_bias[l],
                        norm_delta_out[l])
    mk_p, mv_p = _mem_kv(mem_prompt.reshape(B * M, D), norm_mem_kv[l], w_mkv, ROW_TILE)
    mk_p, mv_p = (x.reshape(B, M * MEM_HEADS, MEM_HEAD_DIM) for x in (mk_p, mv_p))
    seq = lambda a: a.reshape(B, S, a.shape[-1])
    y_p, fconv_p = _post([seq(a) for a in o_g], [seq(a) for a in l_g], o_b, seq(ga), seq(gb), x_prompt, mk_p, mv_p,
                         jnp.zeros((B, 2, F), F32), w_a, w_b, w_o, norm_mem_q[l], w_mq, w_mo, norm_ffn[l], w_up,
                         w_ffn_conv[l], b_ffn_conv[l], w_dn, norm_final, ROW_TILE)
    p_out = ([kt.reshape(B, 2, DIL_HPG, DIL_HEAD_DIM, kt.shape[2]).transpose(0, 4, 1, 2, 3)[None]
              for kt in (kt0, kt1, kt2)]
             + [delta_p[None], gq_tail[:, SUBLANES - 3:][None], mk_p.reshape(1, B, M, MEM_HEADS, MEM_HEAD_DIM),
                mv_p.reshape(1, B, M, MEM_HEADS, MEM_HEAD_DIM), fconv_p[None]])

    R = SAMPLE_ROWS
    xs = jnp.pad(x_sample, ((0, 0), (0, R - Ts), (0, 0))).reshape(Bs * R, D)
    q, kv0, kv1, kv2, gq, z, ba, ga, gb = _in_proj(xs, norm_mix[l], w_arr, Bs * R)
    kvn = [kv.reshape(Bs, R, 2 * DIL_GW) for kv in (kv0, kv1, kv2)]
    caches_t = [jnp.transpose(c[l], (0, 2, 3, 4, 1)).reshape(Bs, 2 * DIL_GW, c.shape[2])
                for c in (cache_dil0_kv, cache_dil1_kv, cache_dil2_kv)]
    tabc, tabn = _sample_bias_tables(rel_bias, Ts)
    o_g, l_g, new_caches = _dil_sample(q.reshape(Bs, R, 3 * DIL_GW), kvn, caches_t, tabc, tabn, Ts)
    o_b, delta_s = _gdn_sample(gq.reshape(Bs, R, -1), ba.reshape(Bs, R, LANES), z.reshape(Bs, R, GDN_W),
                               state_delta_conv[l], state_delta[l], w_conv_delta[l], a_log[l], dt_bias[l],
                               norm_delta_out[l], Ts)
    o_b = o_b.reshape(Bs * R, GDN_W).astype(BF16)
    h1, qm = _mix(o_g, l_g, o_b, ga, gb, xs, w_a, w_b, w_o, norm_mem_q[l], w_mq, Bs * R)
    om = _mem_attn(qm.reshape(Bs, R, -1), cache_mem_k[l].reshape(Bs, M * MEM_HEADS, MEM_HEAD_DIM),
                   cache_mem_v[l].reshape(Bs, M * MEM_HEADS, MEM_HEAD_DIM), R, 8 if Bs % 8 == 0 else 1)
    fst = state_ffn_conv[l]
    fill = jnp.concatenate([jnp.zeros((Bs, R - 2, F), F32),
                            jnp.concatenate([fst[1:], jnp.zeros((1, 2, F), F32)], axis=0)], axis=1)
    y_s, gate_s = _ffn(h1.reshape(1, Bs * R, D), om.reshape(1, Bs * R, -1), fst[:1], fill.reshape(1, Bs * R, F),
                       w_mo, norm_ffn[l], w_up, w_ffn_conv[l], b_ffn_conv[l], w_dn, norm_final, Bs * R, True)
    y_s = y_s.reshape(Bs, R, D)[:, :Ts]
    gq3 = gq.reshape(Bs, R, -1)
    s_out = ([nc.reshape(Bs, 2, DIL_HPG, DIL_HEAD_DIM, nc.shape[2]).transpose(0, 4, 1, 2, 3)[None]
              for nc in new_caches]
             + [delta_s[None], gq3[:, Ts - 3:Ts][None], gate_s.reshape(Bs, R, F)[:, Ts - 2:Ts][None]])

    return (y_p.reshape(B, S, D), y_s, *p_out, *s_out)
```

```python
import functools
import math

import jax
import jax.numpy as jnp
import numpy as np
from jax import lax
from jax.experimental import pallas as pl
from jax.experimental.pallas import tpu as pltpu

F32 = jnp.float32
BF16 = jnp.bfloat16

PAST_LEN = 8192
DIL_GROUPS = ((128, 1), (512, 4), (2048, 16))
DIL_HPG = 4
DIL_HEAD_DIM = 64
DIL_GW = DIL_HPG * DIL_HEAD_DIM
DIL_NK = 129
REL_BUCKETS = 32
REL_MAX_DIST = 2048
GDN_HEADS = 8
GDN_HEAD_DIM = 128
GDN_W = GDN_HEADS * GDN_HEAD_DIM
GDN_CHUNK = 64
MEM_HEADS = 4
MEM_HEAD_DIM = 128
EPS = 1e-6
NEG = -1e30

LANES = 128
SUBLANES = 8
TILE_Q = 128
D_MODEL = 1024
ROW_TILE = 256
SAMPLE_ROWS = SUBLANES
CONV_PAD = SUBLANES
VMEM_LIMIT = 56 * 1024 * 1024


def _cparams(sem):
    return pltpu.CompilerParams(dimension_semantics=sem, vmem_limit_bytes=VMEM_LIMIT)


def _resident(shape):
    nd = len(shape)
    return pl.BlockSpec(shape, lambda *_: (0,) * nd, pipeline_mode=pl.Buffered(1))


def _rms(x, gain_row):
    return x * lax.rsqrt(jnp.mean(x * x, axis=-1, keepdims=True) + EPS) * gain_row


def _dot(a, b):
    return jnp.dot(a.astype(BF16), b.astype(BF16), preferred_element_type=F32)


def _dot_nt(a, b):
    return lax.dot_general(a.astype(BF16), b.astype(BF16), (((1,), (1,)), ((), ())), preferred_element_type=F32)


def _split3(x):
    hi = x.astype(BF16)
    r1 = x - hi.astype(F32)
    mid = r1.astype(BF16)
    lo = (r1 - mid.astype(F32)).astype(BF16)
    return hi, mid, lo


def _l2n(x):
    return x * lax.rsqrt(jnp.sum(x * x, axis=-1, keepdims=True) + EPS)


def _sigmoid(x):
    return 1.0 / (1.0 + jnp.exp(-x))


def _silu(x):
    return x * _sigmoid(x)


def _softplus(x):
    return jnp.maximum(x, 0.0) + jnp.log(1.0 + jnp.exp(-jnp.abs(x)))


IN_SEGS = (("q", 3 * DIL_GW), ("kv0", 2 * DIL_GW), ("kv1", 2 * DIL_GW), ("kv2", 2 * DIL_GW),
           ("gq", 3 * GDN_W), ("z", GDN_W), ("ba", LANES), ("ga", D_MODEL), ("gb", D_MODEL))


OFF_Q, OFF_K, OFF_V = 0, 3 * DIL_GW, 6 * DIL_GW
OFF_GQ = 9 * DIL_GW
OFF_Z = OFF_GQ + 3 * GDN_W
OFF_BA = OFF_Z + GDN_W
OFF_GATES = OFF_BA + 2 * GDN_HEADS


def _arrange_w_in(w_in):
    w = w_in.astype(BF16)
    return w, w[:, OFF_GATES:]


GQ_SLABS = 3 * GDN_W // LANES
NORM_ROWS = 4 * SUBLANES


def _in_proj_kernel(x_ref, g_ref, w_ref, wg_ref, *rest, seq_tiles):
    tm = x_ref.shape[0]
    if seq_tiles:
        (xn_ref, cb_ref, wc_ref, q_ref, kv0_ref, kv1_ref, kv2_ref, kt0_ref, kt1_ref, kt2_ref, gq_ref, tail_ref, z_ref,
         ba_ref, ga_ref, gb_ref, cs, us) = rest
        kt_refs = (kt0_ref, kt1_ref, kt2_ref)
        i = pl.program_id(0)

        @pl.when(i == 0)
        def _():
            us[0] = _rms(x_ref[...], g_ref[...]).astype(BF16)

        u_ref, un_ref = us.at[i % 2], us.at[(i + 1) % 2]
    else:
        q_ref, kv0_ref, kv1_ref, kv2_ref, gq_ref, z_ref, ba_ref, ga_ref, gb_ref = rest
        u = _rms(x_ref[...], g_ref[...]).astype(BF16)

    def seg(ref, off, n):
        return jnp.dot(u_ref[...] if seq_tiles else u, ref[:, off:off + n], preferred_element_type=F32)

    def q_seg():
        q_ref[...] = seg(w_ref, OFF_Q, 3 * DIL_GW) * (DIL_HEAD_DIM ** -0.5)

    def kv_seg(g, part):
        kv_ref = (kv0_ref, kv1_ref, kv2_ref)[g]
        r = seg(w_ref, (OFF_K, OFF_V)[part] + g * DIL_GW, DIL_GW)
        kv_ref[:, part * DIL_GW:(part + 1) * DIL_GW] = r
        if seq_tiles:
            kt_refs[g][part * DIL_GW:(part + 1) * DIL_GW, :] = r.T[:, tm - kt_refs[g].shape[1]:]

    def ba_seg():
        ba_ref[...] = seg(w_ref, OFF_BA, LANES)

    def gate_seg(ref, off):
        ref[...] = seg(wg_ref, off, D_MODEL)

    others = ([(q_seg, 3 * DIL_GW)] + [(functools.partial(kv_seg, g, p), DIL_GW) for g in range(3) for p in range(2)]
              + [(functools.partial(gate_seg, ga_ref, 0), D_MODEL),
                 (functools.partial(gate_seg, gb_ref, D_MODEL), D_MODEL),
                 (ba_seg, LANES)])
    if not seq_tiles:
        for f, _ in others:
            f()
        for c in range(3):
            gq_ref[:, c * GDN_W:(c + 1) * GDN_W] = seg(w_ref, OFF_GQ + c * GDN_W, GDN_W)
        z_ref[...] = seg(w_ref, OFF_Z, GDN_W)
        return
    first = (i % seq_tiles) == 0

    @pl.when(first)
    def _():
        for s in range(GQ_SLABS):
            cs[s, CONV_PAD - 3:CONV_PAD, :] = cb_ref[:, s * LANES:(s + 1) * LANES]

    @pl.when(jnp.logical_not(first))
    def _():
        for s in range(GQ_SLABS):
            cs[s, CONV_PAD - 3:CONV_PAD, :] = cs[s, CONV_PAD + tm - 3:CONV_PAD + tm, :]

    for c in range(3):
        r = seg(w_ref, OFF_GQ + c * GDN_W, GDN_W)
        for hh in range(GDN_HEADS):
            cs[c * GDN_HEADS + hh, CONV_PAD:CONV_PAD + tm, :] = r[:, hh * LANES:(hh + 1) * LANES]
        for rs in range(c * tm // 3 // NORM_ROWS, (c + 1) * tm // 3 // NORM_ROWS):
            rows = slice(rs * NORM_ROWS, (rs + 1) * NORM_ROWS)
            un_ref[rows, :] = _rms(xn_ref[rows, :], g_ref[...]).astype(BF16)
    base = CONV_PAD - 3 + jnp.minimum(i, 0)

    def conv_slab(s):
        ls = slice(s * LANES, (s + 1) * LANES)
        y = cs[s, pl.ds(base, tm), :] * wc_ref[0:1, ls]
        for j in range(1, 4):
            y = y + cs[s, pl.ds(base + j, tm), :] * wc_ref[j:j + 1, ls]
        y = _silu(y)
        if s < GDN_HEADS:
            y = _l2n(y) * (GDN_HEAD_DIM ** -0.5)
        elif s < 2 * GDN_HEADS:
            y = _l2n(y)
        gq_ref[:, ls] = y
        tail_ref[:, ls] = cs[s, tm:tm + SUBLANES, :]

    def z_seg():
        z_ref[...] = _silu(seg(w_ref, OFF_Z, GDN_W))

    others.insert(len(others) - 1, (z_seg, GDN_W))
    total = sum(n for _, n in others)
    done, cols = 0, 0
    for f, n in others:
        f()
        cols += n
        upto = min(GQ_SLABS, (cols * GQ_SLABS + total - 1) // total)
        for s in range(done, upto):
            conv_slab(s)
        done = upto
    assert done == GQ_SLABS


def _in_proj(x2d, gain, w_arr, tm, seq=None, conv_buf=None, w_conv=None):
    rows, d = x2d.shape
    fused = seq is not None
    nt = seq // tm if fused else None
    names = [n for n, _ in IN_SEGS]
    widths = dict(IN_SEGS)
    row_spec = lambda n: pl.BlockSpec((tm, n), lambda i: (i, 0))
    out_specs, out_shape = [], []
    for n in names:
        out_specs.append(row_spec(widths[n]))
        out_shape.append(jax.ShapeDtypeStruct((rows, widths[n]), F32))
        if n == "kv2" and fused:
            for w, _ in DIL_GROUPS:
                keep = min(w, seq)
                bw = min(tm, keep)
                skip = nt - keep // bw if bw == tm else nt - 1
                out_specs.append(pl.BlockSpec((None, 2 * DIL_GW, bw),
                                              lambda i, skip=skip: (i // nt, 0, jnp.maximum(i % nt - skip, 0))))
                out_shape.append(jax.ShapeDtypeStruct((rows // seq, 2 * DIL_GW, keep), F32))
        if n == "gq" and fused:
            out_specs.append(pl.BlockSpec((None, SUBLANES, 3 * GDN_W), lambda i: (i // nt, 0, 0)))
            out_shape.append(jax.ShapeDtypeStruct((rows // seq, SUBLANES, 3 * GDN_W), F32))
    in_specs = [pl.BlockSpec((tm, d), lambda i: (0 if fused else i, 0)), _resident((1, d)),
                _resident(w_arr[0].shape), _resident(w_arr[1].shape)]
    args = [x2d, gain.reshape(1, d), *w_arr]
    scratch = []
    if fused:
        in_specs += [pl.BlockSpec((tm, d), lambda i: (jnp.minimum(i + 1, rows // tm - 1), 0)),
                     pl.BlockSpec((None, 3, 3 * GDN_W), lambda i: (i // nt, 0, 0)), _resident(w_conv.shape)]
        args += [x2d, conv_buf, w_conv]
        scratch = [pltpu.VMEM((GQ_SLABS, CONV_PAD + tm, LANES), F32), pltpu.VMEM((2, tm, d), BF16)]
    return pl.pallas_call(
        functools.partial(_in_proj_kernel, seq_tiles=nt),
        grid=(rows // tm,),
        in_specs=in_specs,
        out_specs=out_specs,
        out_shape=out_shape,
        scratch_shapes=scratch,
        compiler_params=_cparams(("arbitrary",)),
        name="in_proj",
    )(*args)


def _rel_bucket(dist):
    exact = REL_BUCKETS // 2
    d = jnp.maximum(dist, 1).astype(F32)
    large = exact + (jnp.log(d / exact) / math.log(REL_MAX_DIST / exact) * (REL_BUCKETS - exact)).astype(jnp.int32)
    return jnp.where(dist < exact, dist, jnp.minimum(large, REL_BUCKETS - 1))


def _group_bias(rel_bias, g):
    dil = DIL_GROUPS[g][1]
    dist = dil * jnp.arange(DIL_NK, dtype=jnp.int32)
    tab = rel_bias[_rel_bucket(dist)]
    return tab[:, g * DIL_HPG:(g + 1) * DIL_HPG].T.astype(F32)


def _toeplitz(v, n, width):
    h, L = v.shape
    return jnp.tile(v, (1, n))[:, :n * (L - 1)].reshape(h, n, L - 1)[:, :, :width]


def _prompt_bias_tables(rel_bias):
    cat, cur = [], []
    for g in range(3):
        bg = _group_bias(rel_bias, g)
        v = jnp.concatenate([bg[:, ::-1], jnp.full((DIL_HPG, 3 * TILE_Q - DIL_NK), NEG, F32)], axis=1)
        t = _toeplitz(v, TILE_Q, 2 * TILE_Q)
        cat.append(t)
        cur.append(t[:, :, TILE_Q:])
    return cat, cur


DIL_TIF = 2
DIL_SLABS = DIL_GW // LANES


def _dil_prompt_kernel(q0_ref, q1_ref, k0_ref, k1_ref, v0_ref, v1_ref, tcat_ref, tcur_ref,
                       o0_ref, o1_ref, l0_ref, l1_ref, *, dil):
    S = q0_ref.shape[0]
    nb = S // dil // TILE_Q
    q_refs, k_refs, v_refs = (q0_ref, q1_ref), (k0_ref, k1_ref), (v0_ref, v1_ref)
    o_refs, l_refs = (o0_ref, o1_ref), (l0_ref, l1_ref)
    even = lax.broadcasted_iota(jnp.int32, (TILE_Q, LANES), 1) < DIL_HEAD_DIM

    def rows(r, t):
        start = r + dil * TILE_Q * t
        return pl.ds(start, TILE_Q, stride=dil) if dil > 1 else pl.ds(start, TILE_Q)

    tiles = [(r, t) for r in range(dil) for t in range(nb)]
    for i0 in range(0, len(tiles), DIL_TIF):
        grp = tiles[i0:i0 + DIL_TIF]
        qm, kc, vc = {}, {}, {}
        for ti, (r, t) in enumerate(grp):
            for sl in range(DIL_SLABS):
                qf = q_refs[sl][rows(r, t), :]
                qm[ti, 2 * sl] = jnp.where(even, qf, 0.0).astype(BF16)
                qm[ti, 2 * sl + 1] = jnp.where(even, 0.0, qf).astype(BF16)
                kc[ti, sl] = k_refs[sl][rows(r, t), :].astype(BF16)
                vc[ti, sl] = v_refs[sl][rows(r, t), :].astype(BF16)
                if t > 0:
                    kc[ti, sl] = jnp.concatenate([k_refs[sl][rows(r, t - 1), :].astype(BF16), kc[ti, sl]], axis=0)
                    vc[ti, sl] = jnp.concatenate([v_refs[sl][rows(r, t - 1), :].astype(BF16), vc[ti, sl]], axis=0)
        units = [(ti, h) for ti in range(len(grp)) for h in range(DIL_HPG)]
        s = [_dot_nt(qm[ti, h], kc[ti, h // 2]) + (tcat_ref[h] if grp[ti][1] > 0 else tcur_ref[h]) for ti, h in units]
        m = [jnp.max(x, axis=-1, keepdims=True) for x in s]
        p = [jnp.exp(x - mx) for x, mx in zip(s, m)]
        l = [jnp.sum(x, axis=-1, keepdims=True) for x in p]
        pv = [jnp.dot(p[u].astype(BF16), vc[ti, h // 2], preferred_element_type=F32) for u, (ti, h) in enumerate(units)]
        o = [pv[u] / l[u] for u in range(len(units))]
        lse = [m[u] + jnp.log(l[u]) for u in range(len(units))]
        for ti, (r, t) in enumerate(grp):
            for sl in range(DIL_SLABS):
                ue, uo = ti * DIL_HPG + 2 * sl, ti * DIL_HPG + 2 * sl + 1
                o_refs[sl][rows(r, t), :] = jnp.where(even, o[ue], o[uo])
                l_refs[sl][rows(r, t), :] = jnp.where(even, lse[ue], lse[uo])


def _dil_prompt(q, kv, g, t_cat, t_cur):
    B, S, _ = q.shape
    dil = DIL_GROUPS[g][1]
    slab = lambda c: pl.BlockSpec((None, S, LANES), lambda b: (b, 0, c))
    nq, nk = g * DIL_SLABS, 0
    outs = pl.pallas_call(
        functools.partial(_dil_prompt_kernel, dil=dil),
        grid=(B,),
        in_specs=[slab(nq), slab(nq + 1), slab(nk), slab(nk + 1), slab(nk + 2), slab(nk + 3),
                  _resident(t_cat.shape), _resident(t_cur.shape)],
        out_specs=[slab(0)] * 4,
        out_shape=[jax.ShapeDtypeStruct((B, S, LANES), F32)] * 4,
        compiler_params=_cparams(("parallel",)),
        name=f"dil_prompt_g{g}",
    )(q, q, kv, kv, kv, kv, t_cat, t_cur)
    o0, o1, l0, l1 = (x.reshape(B * S, LANES) for x in outs)
    return [o0, o1], [l0, l1]


def _sample_bias_tables(rel_bias, t_real):
    R = SAMPLE_ROWS
    tabc, tabn = [], []
    t_i = np.arange(R)[:, None]
    u_i = np.arange(R)[None, :]
    for g, (w, dil) in enumerate(DIL_GROUPS):
        bg = _group_bias(rel_bias, g)
        base = bg[:, ::-1][:, :TILE_Q]
        t0 = jnp.concatenate([base[:, :, None], jnp.full((DIL_HPG, TILE_Q, dil - 1), NEG, F32)], axis=2)
        t0 = t0.reshape(DIL_HPG, w)
        tc = _toeplitz(jnp.concatenate([t0, jnp.full((DIL_HPG, R), NEG, F32)], axis=1), R, w)
        tabc.append(jnp.where((t_i < t_real)[None], tc, 0.0).reshape(DIL_HPG * R, w))
        tn = jnp.full((DIL_HPG, R, R), NEG, F32)
        for j in range(-(-t_real // dil)):
            hit = (t_i - u_i == j * dil) & (t_i < t_real)
            tn = jnp.where(hit[None], bg[:, j][:, None, None], tn)
        tn = jnp.where((t_i >= t_real)[None], 0.0, tn)
        tabn.append(tn.reshape(DIL_HPG * R, R))
    return tabc, jnp.stack(tabn)


def _dil_sample_kernel(q_ref, n0_ref, n1_ref, n2_ref, c0_ref, c1_ref, c2_ref, tc0_ref, tc1_ref, tc2_ref, tn_ref,
                       o_ref, l_ref, oc0_ref, oc1_ref, oc2_ref, *, t_real):
    R = SAMPLE_ROWS
    rows = lax.broadcasted_iota(jnp.int32, (DIL_HPG * R, DIL_GW), 0)
    lanes = lax.broadcasted_iota(jnp.int32, (DIL_HPG * R, DIL_GW), 1)
    head_mask = (lanes // DIL_HEAD_DIM) == (rows // R)
    lane_f = lax.broadcasted_iota(jnp.int32, (2 * DIL_GW, LANES), 1)
    keep = lane_f < LANES - t_real
    sel_l = lax.broadcasted_iota(jnp.int32, (LANES, R), 0)
    sel_u = lax.broadcasted_iota(jnp.int32, (LANES, R), 1)
    selT = ((sel_l == sel_u + LANES - t_real) & (sel_u < t_real)).astype(BF16)
    groups = ((n0_ref, c0_ref, tc0_ref, oc0_ref), (n1_ref, c1_ref, tc1_ref, oc1_ref), (n2_ref, c2_ref, tc2_ref, oc2_ref))

    def fold_heads(x):
        x = jnp.where(head_mask, x, 0.0)
        return x[0:R] + x[R:2 * R] + x[2 * R:3 * R] + x[3 * R:4 * R]

    G3 = range(3)
    kvn = [groups[g][0][...] for g in G3]
    q_bd = [jnp.where(head_mask, jnp.concatenate([q_ref[:, g * DIL_GW:(g + 1) * DIL_GW]] * DIL_HPG, axis=0), 0.0)
            for g in G3]
    s_c = [_dot(q_bd[g], groups[g][1][:DIL_GW, :]) + groups[g][2][...] for g in G3]
    s_n = [_dot_nt(q_bd[g], kvn[g][:, :DIL_GW]) + tn_ref[g] for g in G3]
    m = [jnp.maximum(jnp.max(s_c[g], axis=-1, keepdims=True), jnp.max(s_n[g], axis=-1, keepdims=True)) for g in G3]
    p_c = [jnp.exp(s_c[g] - m[g]) for g in G3]
    p_n = [jnp.exp(s_n[g] - m[g]) for g in G3]
    l = [jnp.sum(p_c[g], axis=-1, keepdims=True) + jnp.sum(p_n[g], axis=-1, keepdims=True) for g in G3]
    acc = [(_dot_nt(p_c[g], groups[g][1][DIL_GW:, :]) + _dot(p_n[g], kvn[g][:, DIL_GW:])) / l[g] for g in G3]
    for g in G3:
        o_ref[:, g * DIL_GW:(g + 1) * DIL_GW] = fold_heads(acc[g])
        l_ref[:, g * DIL_GW:(g + 1) * DIL_GW] = fold_heads(jnp.broadcast_to(m[g] + jnp.log(l[g]), acc[g].shape))
    for g, (n_ref, c_ref, tc_ref, oc_ref) in enumerate(groups):
        W = c_ref.shape[1]
        hi, mid, lo = _split3(kvn[g])
        tail = (jnp.dot(selT, hi, preferred_element_type=F32) + jnp.dot(selT, mid, preferred_element_type=F32)
                + jnp.dot(selT, lo, preferred_element_type=F32)).T
        nxt = pltpu.roll(c_ref[:, 0:LANES], LANES - t_real, axis=1)
        for c in range(W // LANES):
            cur = nxt
            nxt = (pltpu.roll(c_ref[:, (c + 1) * LANES:(c + 2) * LANES], LANES - t_real, axis=1)
                   if (c + 1) * LANES < W else tail)
            oc_ref[:, c * LANES:(c + 1) * LANES] = jnp.where(keep, cur, nxt)


def _dil_sample(q, kvn, caches_t, tabc, tabn, t_real):
    B = q.shape[0]
    row = lambda n: pl.BlockSpec((None, SAMPLE_ROWS, n), lambda b: (b, 0, 0))
    cspecs = [pl.BlockSpec((None,) + c.shape[1:], lambda b: (b, 0, 0)) for c in caches_t]
    for g, (w, dil) in enumerate(DIL_GROUPS):
        assert caches_t[g].shape == (B, 2 * DIL_GW, w) and w // dil == TILE_Q
    out_spec = row(3 * DIL_GW)
    o, lse, *new_caches = pl.pallas_call(
        functools.partial(_dil_sample_kernel, t_real=t_real),
        grid=(B,),
        in_specs=([row(3 * DIL_GW)] + [row(2 * DIL_GW)] * 3 + cspecs + [_resident(t.shape) for t in tabc]
                  + [_resident(tabn.shape)]),
        out_specs=[out_spec, out_spec] + cspecs,
        out_shape=([jax.ShapeDtypeStruct((B, SAMPLE_ROWS, 3 * DIL_GW), F32)] * 2
                   + [jax.ShapeDtypeStruct(c.shape, F32) for c in caches_t]),
        compiler_params=_cparams(("parallel",)),
        name="dil_sample",
    )(q, *kvn, *caches_t, *tabc, tabn)
    o = o.reshape(B * SAMPLE_ROWS, 3 * DIL_GW)
    lse = lse.reshape(B * SAMPLE_ROWS, 3 * DIL_GW)
    n = 3 * DIL_SLABS
    return ([o[:, i * LANES:(i + 1) * LANES] for i in range(n)],
            [lse[:, i * LANES:(i + 1) * LANES] for i in range(n)], new_caches)


GDN_HPS = GDN_HEADS
GDN_SW = GDN_HPS * GDN_HEAD_DIM
GDN_TPI = 4
GDN_ROWS = 1024


def _gdn_kernel(q_ref, k_ref, v_ref, ba_ref, alog_ref, dtb_ref, z_ref, gn_ref, tri_ref, s0_ref, o_ref, s_ref, carry_s,
                *, tpi):
    T = q_ref.shape[0]
    C = GDN_CHUNK
    D = GDN_HEAD_DIM
    ri = lax.broadcasted_iota(jnp.int32, (TILE_Q, TILE_Q), 0)
    ci = lax.broadcasted_iota(jnp.int32, (TILE_Q, TILE_Q), 1)
    same = (ri // C) == (ci // C)
    incl = same & (ri >= ci)
    strict = same & (ri > ci)
    eye = (ri == ci).astype(F32)
    lane, row = ci, ri
    gain = gn_ref[...]
    zpad = jnp.zeros((C, D), F32)
    HH = range(GDN_HPS)
    hsl = [slice(hh * D, (hh + 1) * D) for hh in HH]
    RW = tpi * TILE_Q
    UU = [(tt, hh) for tt in range(tpi) for hh in HH]
    UI = range(len(UU))

    def tile(i, S):
        r0 = pl.multiple_of(i * RW, RW)
        rt = [r0 + tt * TILE_Q for tt in range(tpi)]
        q = [q_ref[pl.ds(rt[tt], TILE_Q), hsl[hh]] for tt, hh in UU]
        k = [k_ref[pl.ds(rt[tt], TILE_Q), hsl[hh]] for tt, hh in UU]
        v = [v_ref[pl.ds(rt[tt], TILE_Q), hsl[hh]] for tt, hh in UU]
        tri = tri_ref[...]
        beta_all, G_all = [], []
        for tt in range(tpi):
            ba = ba_ref[pl.ds(rt[tt], TILE_Q), :]
            beta_all.append(_sigmoid(ba))
            gh, gm, gl = _split3(-jnp.exp(alog_ref[...]) * _softplus(ba + dtb_ref[...]))
            G_all.append(jnp.dot(tri, gh, preferred_element_type=F32) + jnp.dot(tri, gm, preferred_element_type=F32)
                         + jnp.dot(tri, gl, preferred_element_type=F32))
        head = [hh for _, hh in UU]
        bc = [jnp.sum(jnp.where(lane == head[u], beta_all[UU[u][0]], 0.0), axis=-1, keepdims=True) for u in UI]
        Gc = [jnp.broadcast_to(jnp.sum(jnp.where(lane == head[u] + GDN_HEADS, G_all[UU[u][0]], 0.0), axis=-1,
                                       keepdims=True), (TILE_Q, TILE_Q)) for u in UI]
        gamma = [jnp.exp(jnp.where(incl, Gc[u] - Gc[u].T, NEG)) for u in UI]
        kk = [_dot_nt(k[u], k[u]) for u in UI]
        qk = [_dot_nt(q[u], k[u]) for u in UI]
        lo = lax.broadcasted_iota(jnp.int32, (C, TILE_Q), 1) < C

        def pack(m):
            return jnp.where(lo, m[:C], m[C:])

        def blockdiag(p):
            return jnp.concatenate([jnp.where(lo, p, 0.0), jnp.where(lo, 0.0, p)], axis=0)

        eye_p = pack(eye)
        Xp = [pack(jnp.where(strict, bc[u] * kk[u] * gamma[u], 0.0)) for u in UI]
        Pp = [eye_p - Xp[u] for u in UI]
        Xb = [blockdiag(Xp[u]).astype(BF16) for u in UI]
        for _ in range(int(math.log2(C)) - 1):
            Xp = [jnp.dot(Xp[u].astype(BF16), Xb[u], preferred_element_type=F32) for u in UI]
            Xb = [blockdiag(Xp[u]).astype(BF16) for u in UI]
            Pp = [Pp[u] + jnp.dot(Pp[u].astype(BF16), Xb[u], preferred_element_type=F32) for u in UI]
        eG = [jnp.exp(Gc[u]) for u in UI]
        rhs = [jnp.concatenate([v[u] * bc[u], k[u] * (bc[u] * eG[u])], axis=-1) for u in UI]
        sol = [rhs[u] + _dot(blockdiag(Pp[u] - eye_p), rhs[u]) for u in UI]
        a_in = [qk[u] * gamma[u] for u in UI]
        q_dec = [q[u] * eG[u] for u in UI]
        kdT = [(k[u] * jnp.exp(jnp.where(row < C, Gc[u][C - 1:C, :], Gc[u][2 * C - 1:2 * C, :]) - Gc[u])).T
               for u in UI]
        S = list(S)
        for tt in range(tpi):
            us = [tt * GDN_HPS + hh for hh in HH]
            oq, vn = [[] for _ in HH], [[] for _ in HH]
            for c in range(TILE_Q // C):
                cs = slice(c * C, (c + 1) * C)
                r = [_dot(jnp.concatenate([sol[us[hh]][cs, D:], q_dec[us[hh]][cs]], axis=0), S[hh]) for hh in HH]
                for hh in HH:
                    oq[hh].append(r[hh][C:])
                    vn[hh].append(sol[us[hh]][cs, :D] - r[hh][:C])
                vpad = [jnp.concatenate([vn[hh][c], zpad] if c == 0 else [zpad, vn[hh][c]], axis=0) for hh in HH]
                S = [S[hh] * jnp.exp(Gc[us[hh]][(c + 1) * C - 1:(c + 1) * C, :]) + _dot(kdT[us[hh]], vpad[hh])
                     for hh in HH]
            o = [jnp.concatenate(oq[hh], axis=0) + _dot(a_in[us[hh]], jnp.concatenate(vn[hh], axis=0)) for hh in HH]
            outs = [_rms(o[hh], gain) * z_ref[pl.ds(rt[tt], TILE_Q), hsl[hh]] for hh in HH]
            o_ref[pl.ds(rt[tt], TILE_Q), :] = jnp.concatenate(outs, axis=-1).astype(o_ref.dtype)
        return tuple(S)

    @pl.when(pl.program_id(1) == 0)
    def _():
        carry_s[...] = s0_ref[...]

    S = lax.fori_loop(0, T // RW, tile, tuple(carry_s[hh] for hh in HH))
    for hh in HH:
        carry_s[hh] = S[hh]
        s_ref[hh] = S[hh]


def _gdn(gq, ba, z, s0, a_log, dt_bias, norm_out):
    B, T, _ = gq.shape
    H = GDN_HEADS
    pad16 = lambda x: jnp.concatenate([jnp.zeros((H,), F32), x.astype(F32), jnp.zeros((LANES - 2 * H,), F32)])
    assert GDN_HPS == H
    tb = min(GDN_ROWS, T)
    tpi = GDN_TPI if (tb // TILE_Q) % GDN_TPI == 0 else 1
    col = lambda off: pl.BlockSpec((None, tb, GDN_SW), lambda b, j: (b, j, off))
    sblk = pl.BlockSpec((None, H, GDN_HEAD_DIM, GDN_HEAD_DIM), lambda b, j: (b, 0, 0, 0))
    r = np.arange(TILE_Q)
    tri = jnp.asarray((r[:, None] >= r[None, :]) & (r[:, None] // GDN_CHUNK == r[None, :] // GDN_CHUNK), BF16)
    o, s_new = pl.pallas_call(
        functools.partial(_gdn_kernel, tpi=tpi),
        grid=(B, T // tb),
        in_specs=[col(0), col(1), col(2),
                  pl.BlockSpec((None, tb, LANES), lambda b, j: (b, j, 0)),
                  _resident((1, LANES)), _resident((1, LANES)),
                  col(0), _resident((1, LANES)), _resident((TILE_Q, TILE_Q)), sblk],
        out_specs=[col(0), sblk],
        out_shape=[jax.ShapeDtypeStruct((B, T, GDN_W), BF16),
                   jax.ShapeDtypeStruct((B, H, GDN_HEAD_DIM, GDN_HEAD_DIM), F32)],
        scratch_shapes=[pltpu.VMEM((H, GDN_HEAD_DIM, GDN_HEAD_DIM), F32)],
        compiler_params=_cparams(("parallel", "arbitrary")),
        name="gdn",
    )(gq, gq, gq, ba, pad16(a_log).reshape(1, LANES), pad16(dt_bias).reshape(1, LANES), z,
      norm_out.reshape(1, LANES), tri, s0)
    return o, s_new


def _gdn_sample_kernel(x_ref, b_ref, w_ref, ba_ref, alog_ref, dtb_ref, z_ref, gn_ref, s0_ref, o_ref, s_ref, xs,
                       *, t_real):
    R, D, H = SAMPLE_ROWS, GDN_HEAD_DIM, GDN_HEADS
    nb = x_ref.shape[0]
    ri = lax.broadcasted_iota(jnp.int32, (R, R), 0)
    ci = lax.broadcasted_iota(jnp.int32, (R, R), 1)
    incl, strict = ri >= ci, ri > ci
    eye = (ri == ci).astype(F32)
    tri = incl.astype(BF16)
    er = lax.broadcasted_iota(jnp.int32, (LANES, LANES), 0)
    ec = lax.broadcasted_iota(jnp.int32, (LANES, LANES), 1)
    eye_l = (er == ec).astype(BF16)
    live = lax.broadcasted_iota(jnp.int32, (R, LANES), 0) < t_real
    d32 = functools.partial(jnp.dot, preferred_element_type=F32)
    nt = lambda a, b: lax.dot_general(a, b, (((1,), (1,)), ((), ())), preferred_element_type=F32)
    gain = gn_ref[...]
    y, beta_all, G_all, G_allT = [], [], [], []
    for s in range(nb):
        xs[s, CONV_PAD - 3:CONV_PAD, :] = b_ref[s]
        xs[s, CONV_PAD:CONV_PAD + R, :] = x_ref[s]
        ys = xs[s, CONV_PAD - 3:CONV_PAD - 3 + R, :] * w_ref[0:1, :]
        for j in range(1, 4):
            ys = ys + xs[s, CONV_PAD - 3 + j:CONV_PAD - 3 + j + R, :] * w_ref[j:j + 1, :]
        y.append(_silu(ys))
        ba = ba_ref[s]
        beta_all.append(jnp.where(live, _sigmoid(ba), 0.0))
        gh, gm, gl = _split3(jnp.where(live, -jnp.exp(alog_ref[...]) * _softplus(ba + dtb_ref[...]), 0.0))
        G = d32(tri, gh) + d32(tri, gm) + d32(tri, gl)
        th, tm, tl = _split3(G)
        G_all.append(G)
        G_allT.append(nt(eye_l, th) + nt(eye_l, tm) + nt(eye_l, tl))
    US = [(s, h) for s in range(nb) for h in range(H)]
    UI = range(len(US))

    q = [_l2n(y[s][:, h * D:(h + 1) * D]) * (D ** -0.5) for s, h in US]
    k = [_l2n(y[s][:, GDN_W + h * D:GDN_W + (h + 1) * D]) for s, h in US]
    v = [y[s][:, 2 * GDN_W + h * D:2 * GDN_W + (h + 1) * D] for s, h in US]
    bc = [beta_all[s][:, h:h + 1] for s, h in US]
    Gc = [G_all[s][:, H + h:H + h + 1] for s, h in US]
    gamma = [jnp.exp(jnp.where(incl, Gc[u] - G_allT[s][H + h:H + h + 1, :], NEG)) for u, (s, h) in enumerate(US)]
    kk = [_dot_nt(k[u], k[u]) for u in UI]
    qk = [_dot_nt(q[u], k[u]) for u in UI]
    X = [jnp.where(strict, bc[u] * kk[u] * gamma[u], 0.0) for u in UI]
    P = [eye - X[u] for u in UI]
    for _ in range(int(math.log2(R)) - 1):
        X = [_dot(X[u], X[u]) for u in UI]
        P = [P[u] + _dot(P[u], X[u]) for u in UI]
    eG = [jnp.exp(Gc[u]) for u in UI]
    rhs = [jnp.concatenate([v[u] * bc[u], k[u] * (bc[u] * eG[u])], axis=-1) for u in UI]
    sol = [rhs[u] + _dot(P[u] - eye, rhs[u]) for u in UI]
    S = [s0_ref[s, h] for s, h in US]
    r = [_dot(jnp.concatenate([sol[u][:, D:], q[u] * eG[u]], axis=0), S[u]) for u in UI]
    v_new = [sol[u][:, :D] - r[u][:R] for u in UI]
    o = [r[u][R:] + _dot(qk[u] * gamma[u], v_new[u]) for u in UI]
    kdT = [nt(eye_l, (k[u] * jnp.exp(Gc[u][R - 1:R, :] - Gc[u])).astype(BF16)) for u in UI]
    for u, (s, h) in enumerate(US):
        s_ref[s, h] = S[u] * jnp.exp(Gc[u][R - 1:R, :]) + _dot(kdT[u], v_new[u])
    for s in range(nb):
        o_ref[s] = jnp.concatenate([_rms(o[s * H + h], gain) * _silu(z_ref[s, :, h * D:(h + 1) * D])
                                    for h in range(H)], axis=-1)


GDN_SAMPLE_SEQS = 4


def _gdn_sample(gq, ba, z, conv_buf, s0, w_conv, a_log, dt_bias, norm_out, t_real):
    B, R, _ = gq.shape
    H = GDN_HEADS
    pad16 = lambda x: jnp.concatenate([jnp.zeros((H,), F32), x.astype(F32), jnp.zeros((LANES - 2 * H,), F32)])
    nb = GDN_SAMPLE_SEQS if B % GDN_SAMPLE_SEQS == 0 else 1
    blk = lambda *s: pl.BlockSpec((nb,) + s, lambda b: (b,) + (0,) * len(s))
    return pl.pallas_call(
        functools.partial(_gdn_sample_kernel, t_real=t_real),
        grid=(B // nb,),
        in_specs=[blk(R, 3 * GDN_W), blk(3, 3 * GDN_W), _resident(w_conv.shape), blk(R, LANES),
                  _resident((1, LANES)), _resident((1, LANES)), blk(R, GDN_W), _resident((1, LANES)),
                  blk(H, GDN_HEAD_DIM, GDN_HEAD_DIM)],
        out_specs=[blk(R, GDN_W), blk(H, GDN_HEAD_DIM, GDN_HEAD_DIM)],
        out_shape=[jax.ShapeDtypeStruct((B, R, GDN_W), F32),
                   jax.ShapeDtypeStruct((B, H, GDN_HEAD_DIM, GDN_HEAD_DIM), F32)],
        scratch_shapes=[pltpu.VMEM((nb, CONV_PAD + R, 3 * GDN_W), F32)],
        compiler_params=_cparams(("parallel",)),
        name="gdn_sample",
    )(gq, conv_buf, w_conv, ba, pad16(a_log).reshape(1, LANES), pad16(dt_bias).reshape(1, LANES), z,
      norm_out.reshape(1, LANES), s0)


N_OG = 3 * DIL_SLABS


def _mix_kernel(*refs):
    h1, qm = _mix_body(*refs[:2 * N_OG + 9])
    h1_ref, qm_ref = refs[2 * N_OG + 9:]
    h1_ref[...] = h1
    qm_ref[...] = qm


def _mix_body(*refs):
    o_refs, l_refs = refs[:N_OG], refs[N_OG:2 * N_OG]
    ob_ref, ga_ref, gb_ref, h_ref, wa_ref, wb_ref, wo_ref, gq_ref, wq_ref = refs[2 * N_OG:]
    slabs = []
    for sl in range(DIL_SLABS):
        l0, l1, l2 = (l_refs[g * DIL_SLABS + sl][...] for g in range(3))
        o0, o1, o2 = (o_refs[g * DIL_SLABS + sl][...] for g in range(3))
        mx = jnp.maximum(jnp.maximum(l0, l1), l2)
        e0, e1, e2 = jnp.exp(l0 - mx), jnp.exp(l1 - mx), jnp.exp(l2 - mx)
        slabs.append((e0 * o0 + e1 * o1 + e2 * o2) / (e0 + e1 + e2))
    o_a = jnp.concatenate(slabs, axis=-1)
    a = _dot(o_a, wa_ref[...])
    b = jnp.dot(ob_ref[...], wb_ref[...], preferred_element_type=F32)
    merged = _sigmoid(ga_ref[...]) * a + _sigmoid(gb_ref[...]) * b
    h1 = h_ref[...] + _dot(merged, wo_ref[...])
    return h1, _dot(_rms(h1, gq_ref[...]), wq_ref[...]).astype(BF16)


def _mix(o_g, l_g, o_b, ga, gb, h, w_a, w_b, w_o, norm_mem_q, w_mem_q, tm):
    rows, d = h.shape
    rt = lambda n: pl.BlockSpec((tm, n), lambda i: (i, 0))
    assert len(o_g) == len(l_g) == 3 * DIL_SLABS
    return pl.pallas_call(
        _mix_kernel,
        grid=(rows // tm,),
        in_specs=[rt(LANES)] * (6 * DIL_SLABS) + [rt(GDN_W), rt(d), rt(d), rt(d),
                                     _resident(w_a.shape), _resident(w_b.shape), _resident(w_o.shape),
                                     _resident((1, d)), _resident(w_mem_q.shape)],
        out_specs=[rt(d), rt(w_mem_q.shape[1])],
        out_shape=[jax.ShapeDtypeStruct((rows, d), F32), jax.ShapeDtypeStruct((rows, w_mem_q.shape[1]), BF16)],
        compiler_params=_cparams(("parallel",)),
        name="mix",
    )(*o_g, *l_g, o_b, ga, gb, h, w_a, w_b, w_o, norm_mem_q.reshape(1, d), w_mem_q)


def _mem_kv_kernel(x_ref, g_ref, w_ref, k_ref, v_ref):
    u = _rms(x_ref[...], g_ref[...]).astype(BF16)
    tm = x_ref.shape[0]
    n = MEM_HEADS * MEM_HEAD_DIM
    for o_ref, off in ((k_ref, 0), (v_ref, n)):
        r = jnp.dot(u, w_ref[:, off:off + n], preferred_element_type=F32)
        for h in range(MEM_HEADS):
            o_ref[pl.ds(h, tm, stride=MEM_HEADS), :] = r[:, h * MEM_HEAD_DIM:(h + 1) * MEM_HEAD_DIM]


def _mem_kv(mem2d, gain, w, tm):
    rows, d = mem2d.shape
    return pl.pallas_call(
        _mem_kv_kernel,
        grid=(rows // tm,),
        in_specs=[pl.BlockSpec((tm, d), lambda i: (i, 0)), _resident((1, d)), _resident(w.shape)],
        out_specs=[pl.BlockSpec((tm * MEM_HEADS, MEM_HEAD_DIM), lambda i: (i, 0))] * 2,
        out_shape=[jax.ShapeDtypeStruct((rows * MEM_HEADS, MEM_HEAD_DIM), F32)] * 2,
        compiler_params=_cparams(("parallel",)),
        name="mem_kv",
    )(mem2d, gain.reshape(1, d), w)


def _mem_attn_body(q, k_refs, v_refs):
    nb = len(q)
    M = k_refs[0].shape[0] // MEM_HEADS
    units = [(b, h) for b in range(nb) for h in range(MEM_HEADS)]
    s = [_dot_nt(q[b][:, h * MEM_HEAD_DIM:(h + 1) * MEM_HEAD_DIM], k_refs[b][pl.ds(h, M, stride=MEM_HEADS), :])
         * (MEM_HEAD_DIM ** -0.5) for b, h in units]
    p = [jnp.exp(x - jnp.max(x, axis=-1, keepdims=True)) for x in s]
    o = [_dot(p[u], v_refs[b][pl.ds(h, M, stride=MEM_HEADS), :]) / jnp.sum(p[u], axis=-1, keepdims=True)
         for u, (b, h) in enumerate(units)]
    return [jnp.concatenate(o[b * MEM_HEADS:(b + 1) * MEM_HEADS], axis=-1) for b in range(nb)]


def _mem_attn_kernel(q_ref, k_ref, v_ref, o_ref):
    nb = q_ref.shape[0]
    o = _mem_attn_body([q_ref[b] for b in range(nb)], [k_ref.at[b] for b in range(nb)],
                       [v_ref.at[b] for b in range(nb)])
    for b in range(nb):
        o_ref[b] = o[b]


def _mem_attn(qm, mem_k, mem_v, tm, nb):
    B, T, w = qm.shape
    kv_spec = pl.BlockSpec((nb,) + mem_k.shape[1:], lambda b, j: (b, 0, 0))
    return pl.pallas_call(
        _mem_attn_kernel,
        grid=(B // nb, T // tm),
        in_specs=[pl.BlockSpec((nb, tm, w), lambda b, j: (b, j, 0)), kv_spec, kv_spec],
        out_specs=pl.BlockSpec((nb, tm, w), lambda b, j: (b, j, 0)),
        out_shape=jax.ShapeDtypeStruct((B, T, w), F32),
        compiler_params=_cparams(("parallel", "parallel")),
        name="mem_attn",
    )(qm, mem_k, mem_v)


def _ffn_kernel(*refs, inject, emit_gate):
    if inject:
        h1_ref, om_ref, init_ref, fill_ref = refs[:4]
        rest = refs[4:]
    else:
        h1_ref, om_ref, init_ref = refs[:3]
        fill_ref, rest = None, refs[3:]
    _ffn_body(h1_ref[...], om_ref[...], init_ref, fill_ref, *rest, emit_gate=emit_gate,
              first=pl.program_id(1) == 0)


def _ffn_body(h1, om, init_ref, fill_ref, wmo_ref, gf_ref, wup_ref, wc_ref, bc_ref, wd_ref, gfin_ref,
              y_ref, fc_ref, gs, *, emit_gate, first):
    tm = h1.shape[0]
    F = wd_ref.shape[0]
    PAD = SUBLANES

    @pl.when(first)
    def _():
        gs[PAD - 2:PAD, :] = init_ref[...]

    h2 = h1 + _dot(om, wmo_ref[...])
    n = _rms(h2, gf_ref[...]).astype(BF16)
    gate = jnp.dot(n, wup_ref[:, :F], preferred_element_type=F32)
    if fill_ref is not None:
        r = lax.broadcasted_iota(jnp.int32, (tm, 1), 0)
        gate = jnp.where((r % SAMPLE_ROWS) >= SAMPLE_ROWS - 2, fill_ref[...], gate)
    gs[PAD:PAD + tm, :] = gate
    conv = (gs[PAD - 2:PAD - 2 + tm, :] * wc_ref[0:1, :] + gs[PAD - 1:PAD - 1 + tm, :] * wc_ref[1:2, :]
            + gate * wc_ref[2:3, :])
    last2 = gs[PAD + tm - 2:PAD + tm, :]
    gs[PAD - 2:PAD, :] = last2
    if emit_gate:
        fc_ref[...] = gate
    else:
        fc_ref[...] = last2
    up = jnp.dot(n, wup_ref[:, F:], preferred_element_type=F32)
    act = _silu(conv + bc_ref[...]) * up
    y = h2 + _dot(act, wd_ref[...])
    y_ref[...] = _rms(y, gfin_ref[...])


def _ffn(h1, om, init, fill, w_mo, norm_ffn, w_up, w_conv, b_conv, w_down, norm_final, tm, emit_gate):
    B, T, d = h1.shape
    F = w_down.shape[0]
    inject = fill is not None
    rt = lambda n: pl.BlockSpec((None, tm, n), lambda b, j: (b, j, 0))
    in_specs = [rt(d), rt(om.shape[-1]), pl.BlockSpec((None, 2, F), lambda b, j: (b, 0, 0))]
    args = [h1, om, init]
    if inject:
        in_specs.append(rt(F))
        args.append(fill)
    in_specs += [_resident(w_mo.shape), _resident((1, d)), _resident(w_up.shape), _resident(w_conv.shape),
                 _resident((1, F)), _resident(w_down.shape), _resident((1, d))]
    args += [w_mo, norm_ffn.reshape(1, d), w_up, w_conv, b_conv.reshape(1, F), w_down, norm_final.reshape(1, d)]
    if emit_gate:
        fc_spec, fc_shape = rt(F), jax.ShapeDtypeStruct((B, T, F), F32)
    else:
        fc_spec = pl.BlockSpec((None, 2, F), lambda b, j: (b, 0, 0))
        fc_shape = jax.ShapeDtypeStruct((B, 2, F), F32)
    return pl.pallas_call(
        functools.partial(_ffn_kernel, inject=inject, emit_gate=emit_gate),
        grid=(B, T // tm),
        in_specs=in_specs,
        out_specs=[rt(d), fc_spec],
        out_shape=[jax.ShapeDtypeStruct((B, T, d), F32), fc_shape],
        scratch_shapes=[pltpu.VMEM((tm + SUBLANES, F), F32)],
        compiler_params=_cparams(("parallel", "arbitrary")),
        name="ffn",
    )(*args)


def _post_kernel(*refs, seq_tiles):
    n_mix = 2 * N_OG + 4
    mix_in, (k_ref, v_ref, init_ref), rest = refs[:n_mix], refs[n_mix:n_mix + 3], refs[n_mix + 3:]
    mix_w, ffn_rest, (h1s, oms) = rest[:5], rest[5:-2], rest[-2:]
    i = pl.program_id(0)

    @pl.when(i == 0)
    def _():
        h1s[1] = jnp.zeros(h1s.shape[1:], F32)
        oms[1] = jnp.zeros(oms.shape[1:], F32)

    o_refs, l_refs = mix_in[:N_OG], mix_in[N_OG:2 * N_OG]
    ob_ref, ga_ref, gb_ref, h_ref = mix_in[2 * N_OG:]
    wa_ref, wb_ref, wo_ref, gq_ref, wq_ref = mix_w
    wmo_ref, gf_ref, wup_ref, wc_ref, bc_ref, wd_ref, gfin_ref, y_ref, fc_ref, gs = ffn_rest
    tm, F, PAD = h_ref.shape[0], wd_ref.shape[0], SUBLANES
    done = (i + 1) % 2

    @pl.when((jnp.maximum(i - 1, 0) % seq_tiles) == 0)
    def _():
        gs[PAD - 2:PAD, :] = init_ref[...]

    slabs = []
    for sl in range(DIL_SLABS):
        l0, l1, l2 = (l_refs[g * DIL_SLABS + sl][...] for g in range(3))
        o0, o1, o2 = (o_refs[g * DIL_SLABS + sl][...] for g in range(3))
        mx = jnp.maximum(jnp.maximum(l0, l1), l2)
        e0, e1, e2 = jnp.exp(l0 - mx), jnp.exp(l1 - mx), jnp.exp(l2 - mx)
        slabs.append((e0 * o0 + e1 * o1 + e2 * o2) / (e0 + e1 + e2))
    b = jnp.dot(ob_ref[...], wb_ref[...], preferred_element_type=F32)
    h2 = h1s[done] + _dot(oms[done], wmo_ref[...])
    a = _dot(jnp.concatenate(slabs, axis=-1), wa_ref[...])
    nrm = _rms(h2, gf_ref[...]).astype(BF16)
    gate = jnp.dot(nrm, wup_ref[:, :F], preferred_element_type=F32)
    merged = _sigmoid(ga_ref[...]) * a + _sigmoid(gb_ref[...]) * b
    h1 = h_ref[...] + _dot(merged, wo_ref[...])
    gs[PAD:PAD + tm, :] = gate
    conv = (gs[PAD - 2:PAD - 2 + tm, :] * wc_ref[0:1, :] + gs[PAD - 1:PAD - 1 + tm, :] * wc_ref[1:2, :]
            + gate * wc_ref[2:3, :])
    last2 = gs[PAD + tm - 2:PAD + tm, :]
    gs[PAD - 2:PAD, :] = last2
    fc_ref[...] = last2
    up = jnp.dot(nrm, wup_ref[:, F:], preferred_element_type=F32)
    qm = _dot(_rms(h1, gq_ref[...]), wq_ref[...]).astype(BF16)
    act = _silu(conv + bc_ref[...]) * up
    M = k_ref.shape[0] // MEM_HEADS
    s = [_dot_nt(qm[:, hh * MEM_HEAD_DIM:(hh + 1) * MEM_HEAD_DIM], k_ref[pl.ds(hh, M, stride=MEM_HEADS), :])
         * (MEM_HEAD_DIM ** -0.5) for hh in range(MEM_HEADS)]
    y = h2 + _dot(act, wd_ref[...])
    p = [jnp.exp(x - jnp.max(x, axis=-1, keepdims=True)) for x in s]
    o = [_dot(p[hh], v_ref[pl.ds(hh, M, stride=MEM_HEADS), :]) / jnp.sum(p[hh], axis=-1, keepdims=True)
         for hh in range(MEM_HEADS)]
    y_ref[...] = _rms(y, gfin_ref[...])
    h1s[i % 2] = h1
    oms[i % 2] = jnp.concatenate(o, axis=-1)


def _post(o_g, l_g, o_b, ga, gb, h, mem_k, mem_v, init, w_a, w_b, w_o, norm_mem_q, w_mem_q, w_mo, norm_ffn, w_up,
          w_conv, b_conv, w_down, norm_final, tm):
    B, T, d = h.shape
    F = w_down.shape[0]
    nt = T // tm
    n = B * nt
    front = lambda i: jnp.minimum(i, n - 1)
    back = lambda i: jnp.maximum(i - 1, 0)
    rt = lambda w: pl.BlockSpec((None, tm, w), lambda i: (front(i) // nt, front(i) % nt, 0))
    per_b = lambda a, t: pl.BlockSpec((None,) + a.shape[1:], lambda i: (t(i) // nt, 0, 0))
    weights = [w_a, w_b, w_o, norm_mem_q.reshape(1, d), w_mem_q, w_mo, norm_ffn.reshape(1, d), w_up, w_conv,
               b_conv.reshape(1, F), w_down, norm_final.reshape(1, d)]
    return pl.pallas_call(
        functools.partial(_post_kernel, seq_tiles=nt),
        grid=(n + 1,),
        in_specs=([rt(LANES)] * (2 * N_OG) + [rt(GDN_W), rt(d), rt(d), rt(d), per_b(mem_k, front), per_b(mem_v, front),
                                              per_b(init, back)]
                  + [_resident(w.shape) for w in weights]),
        out_specs=[pl.BlockSpec((None, tm, d), lambda i: (back(i) // nt, back(i) % nt, 0)),
                   pl.BlockSpec((None, 2, F), lambda i: (back(i) // nt, 0, 0))],
        out_shape=[jax.ShapeDtypeStruct((B, T, d), F32), jax.ShapeDtypeStruct((B, 2, F), F32)],
        scratch_shapes=[pltpu.VMEM((tm + SUBLANES, F), F32), pltpu.VMEM((2, tm, d), F32),
                        pltpu.VMEM((2, tm, w_mem_q.shape[1]), F32)],
        compiler_params=_cparams(("arbitrary",)),
        name="post",
    )(*o_g, *l_g, o_b, ga, gb, h, mem_k, mem_v, init, *weights)


def kernel(x_prompt, x_sample, cache_dil0_kv, cache_dil1_kv, cache_dil2_kv, state_delta, state_delta_conv, cache_mem_k, cache_mem_v, state_ffn_conv, mem_prompt, rel_bias, norm_mix, w_in, w_conv_delta, a_log, dt_bias, norm_delta_out, w_branch_a, w_branch_b, w_out, norm_mem_q, norm_mem_kv, w_mem_q, w_mem_kv, w_mem_o, norm_ffn, w_ffn_up, w_ffn_conv, b_ffn_conv, w_ffn_down, norm_final):
    B, S, D = x_prompt.shape
    Bs, Ts, _ = x_sample.shape
    depth = w_in.shape[0]
    assert depth == 1 and D == D_MODEL and 3 <= Ts <= SAMPLE_ROWS - 2 and S % (16 * TILE_Q) == 0
    assert PAST_LEN >= max(w for w, _ in DIL_GROUPS)
    F = w_ffn_down.shape[1]
    M = mem_prompt.shape[1]
    l = 0
    w_arr = _arrange_w_in(w_in[l])
    w_a, w_b, w_o = (w.astype(BF16) for w in (w_branch_a[l], w_branch_b[l], w_out[l]))
    w_mq, w_mkv, w_mo = (w.astype(BF16) for w in (w_mem_q[l], w_mem_kv[l], w_mem_o[l]))
    w_up, w_dn = w_ffn_up[l].astype(BF16), w_ffn_down[l].astype(BF16)
    t_cat, t_cur = _prompt_bias_tables(rel_bias)

    xp = x_prompt.reshape(B * S, D)
    q, kv0, kv1, kv2, kt0, kt1, kt2, gq, gq_tail, z, ba, ga, gb = _in_proj(
        xp, norm_mix[l], w_arr, ROW_TILE, seq=S, conv_buf=jnp.zeros((B, 3, 3 * GDN_W), F32), w_conv=w_conv_delta[l])
    kvs = [kv.reshape(B, S, 2 * DIL_GW) for kv in (kv0, kv1, kv2)]
    q3 = q.reshape(B, S, 3 * DIL_GW)
    o_g, l_g = [], []
    for g in range(3):
        o_sl, l_sl = _dil_prompt(q3, kvs[g], g, t_cat[g], t_cur[g])
        o_g += o_sl
        l_g += l_sl
    o_b, delta_p = _gdn(gq.reshape(B, S, -1), ba.reshape(B, S, LANES), z.reshape(B, S, GDN_W),
                        jnp.zeros((B, GDN_HEADS, GDN_HEAD_DIM, GDN_HEAD_DIM), F32), a_log[l], dt_bias[l],
                        norm_delta_out[l])
    mk_p, mv_p = _mem_kv(mem_prompt.reshape(B * M, D), norm_mem_kv[l], w_mkv, ROW_TILE)
    mk_p, mv_p = (x.reshape(B, M * MEM_HEADS, MEM_HEAD_DIM) for x in (mk_p, mv_p))
    seq = lambda a: a.reshape(B, S, a.shape[-1])
    y_p, fconv_p = _post([seq(a) for a in o_g], [seq(a) for a in l_g], o_b, seq(ga), seq(gb), x_prompt, mk_p, mv_p,
                         jnp.zeros((B, 2, F), F32), w_a, w_b, w_o, norm_mem_q[l], w_mq, w_mo, norm_ffn[l], w_up,
                         w_ffn_conv[l], b_ffn_conv[l], w_dn, norm_final, ROW_TILE)
    p_out = ([kt.reshape(B, 2, DIL_HPG, DIL_HEAD_DIM, kt.shape[2]).transpose(0, 4, 1, 2, 3)[None]
              for kt in (kt0, kt1, kt2)]
             + [delta_p[None], gq_tail[:, SUBLANES - 3:][None], mk_p.reshape(1, B, M, MEM_HEADS, MEM_HEAD_DIM),
                mv_p.reshape(1, B, M, MEM_HEADS, MEM_HEAD_DIM), fconv_p[None]])

    R = SAMPLE_ROWS
    xs = jnp.pad(x_sample, ((0, 0), (0, R - Ts), (0, 0))).reshape(Bs * R, D)
    q, kv0, kv1, kv2, gq, z, ba, ga, gb = _in_proj(xs, norm_mix[l], w_arr, Bs * R)
    kvn = [kv.reshape(Bs, R, 2 * DIL_GW) for kv in (kv0, kv1, kv2)]
    caches_t = [jnp.transpose(c[l], (0, 2, 3, 4, 1)).reshape(Bs, 2 * DIL_GW, c.shape[2])
                for c in (cache_dil0_kv, cache_dil1_kv, cache_dil2_kv)]
    tabc, tabn = _sample_bias_tables(rel_bias, Ts)
    o_g, l_g, new_caches = _dil_sample(q.reshape(Bs, R, 3 * DIL_GW), kvn, caches_t, tabc, tabn, Ts)
    o_b, delta_s = _gdn_sample(gq.reshape(Bs, R, -1), ba.reshape(Bs, R, LANES), z.reshape(Bs, R, GDN_W),
                               state_delta_conv[l], state_delta[l], w_conv_delta[l], a_log[l], dt_bias[l],
                               norm_delta_out[l], Ts)
    o_b = o_b.reshape(Bs * R, GDN_W).astype(BF16)
    h1, qm = _mix(o_g, l_g, o_b, ga, gb, xs, w_a, w_b, w_o, norm_mem_q[l], w_mq, Bs * R)
    om = _mem_attn(qm.reshape(Bs, R, -1), cache_mem_k[l].reshape(Bs, M * MEM_HEADS, MEM_HEAD_DIM),
                   cache_mem_v[l].reshape(Bs, M * MEM_HEADS, MEM_HEAD_DIM), R, 8 if Bs % 8 == 0 else 1)
    fst = state_ffn_conv[l]
    fill = jnp.concatenate([jnp.zeros((Bs, R - 2, F), F32),
                            jnp.concatenate([fst[1:], jnp.zeros((1, 2, F), F32)], axis=0)], axis=1)
    y_s, gate_s = _ffn(h1.reshape(1, Bs * R, D), om.reshape(1, Bs * R, -1), fst[:1], fill.reshape(1, Bs * R, F),
                       w_mo, norm_ffn[l], w_up, w_ffn_conv[l], b_ffn_conv[l], w_dn, norm_final, Bs * R, True)
    y_s = y_s.reshape(Bs, R, D)[:, :Ts]
    gq3 = gq.reshape(Bs, R, -1)
    s_out = ([nc.reshape(Bs, 2, DIL_HPG, DIL_HEAD_DIM, nc.shape[2]).transpose(0, 4, 1, 2, 3)[None]
              for nc in new_caches]
             + [delta_s[None], gq3[:, Ts - 3:Ts][None], gate_s.reshape(Bs, R, F)[:, Ts - 2:Ts][None]])

    return (y_p.reshape(B, S, D), y_s, *p_out, *s_out)
```
